```python
import math
import jax, jax.numpy as jnp
from jax import lax
import numpy as np

D_MODEL = 1024
BATCH = 16
SEQ = 256
DEPTH = 4
DEC_BATCH = 4
DEC_SEQ = 2048
PAST_LEN = 512

GRID_W = 64
HEAD_DIM = 64
N_Q_HEADS = 8
N_KV_HEADS = 2
Q_PER_KV = N_Q_HEADS // N_KV_HEADS
ATTN_W = N_Q_HEADS * HEAD_DIM
KV_W = N_KV_HEADS * HEAD_DIM
WINDOW = 128
BLOCK = 128
ATTN_SCALE = 0.125
ROPE_THETA = 10000.0
NEG = -1e30
RWKV_HEADS = 4
RWKV_W = RWKV_HEADS * HEAD_DIM
DECAY_RANK = 64
ICLR_RANK = 64
GATE_RANK = 128
SHIFT_WIDTH = 3
DECAY_SCALE = 0.6065306597126334
GN_EPS = 64e-5
RWKV_IN_W = 3 * RWKV_W + GATE_RANK + 2 * DECAY_RANK + 2 * ICLR_RANK
RWKV_SPLITS = (RWKV_W, 2 * RWKV_W, 3 * RWKV_W, 3 * RWKV_W + GATE_RANK,
               3 * RWKV_W + GATE_RANK + 2 * DECAY_RANK)
FNET_GROUPS = 4
FNET_W = FNET_GROUPS * HEAD_DIM
MIX_W = ATTN_W + RWKV_W + FNET_W
IN_W = ATTN_W + 2 * KV_W + RWKV_IN_W + FNET_W
IN_SPLITS = (ATTN_W, ATTN_W + KV_W, ATTN_W + 2 * KV_W, ATTN_W + 2 * KV_W + RWKV_IN_W)
D_FF = -(-8 * D_MODEL // (3 * 256)) * 256
RMS_EPS = 1e-6

kernel_name = "hybrid_dit_gqa_rwkv7_fnet_step"


def rmsnorm(x, g):
    xf = x.astype(jnp.float32)
    y = xf * lax.rsqrt(jnp.mean(xf * xf, axis=-1, keepdims=True) + RMS_EPS)
    return (y * g.astype(jnp.float32)).astype(x.dtype)


def modulate(h, shift, scale):
    return h * (1.0 + scale) + shift


def axial_rope(x):
    T = x.shape[1]
    n_rows = T // GRID_W
    row = jnp.repeat(jnp.arange(n_rows), GRID_W)
    col = jnp.tile(jnp.arange(GRID_W), n_rows)
    half = HEAD_DIM // 2
    quarter = half // 2
    inv = ROPE_THETA ** (-jnp.arange(quarter, dtype=jnp.float32) / quarter)

    def rot(xh, pos):
        ang = pos.astype(jnp.float32)[:, None] * inv[None, :]
        cos = jnp.cos(ang)[None, :, None, :]
        sin = jnp.sin(ang)[None, :, None, :]
        x1, x2 = xh[..., :quarter], xh[..., quarter:]
        return jnp.concatenate([x1 * cos - x2 * sin, x2 * cos + x1 * sin], axis=-1)

    xf = x.astype(jnp.float32)
    out = jnp.concatenate([rot(xf[..., :half], row), rot(xf[..., half:], col)], axis=-1)
    return out.astype(x.dtype)


def softmax_with_sink(s, sink):
    sk = sink.astype(jnp.float32).reshape(N_KV_HEADS, Q_PER_KV, 1, 1)
    m = jnp.maximum(jnp.max(s, axis=-1, keepdims=True), sk)
    e = jnp.exp(s - m)
    return e / (jnp.sum(e, axis=-1, keepdims=True) + jnp.exp(sk - m))


def context_attention(q, k, v, sink):
    B, C = q.shape[:2]
    nqb = C // BLOCK
    qb = jnp.moveaxis(q.reshape(B, nqb, BLOCK, N_KV_HEADS, Q_PER_KV, HEAD_DIM), 1, 0)
    kf = k.astype(jnp.float32)
    vf = v.astype(jnp.float32)

    def one_block(qblk):
        s = jnp.einsum('bqkgd,bckd->bkgqc', qblk.astype(jnp.float32), kf) * ATTN_SCALE
        p = softmax_with_sink(s, sink)
        return jnp.einsum('bkgqc,bckd->bqkgd', p, vf)

    o = lax.map(one_block, qb)
    return jnp.moveaxis(o, 0, 1).reshape(B, C, ATTN_W).astype(q.dtype)


def latent_attention(q, k, v, k_ctx, v_ctx, sink):
    B, T = q.shape[:2]
    nb = T // BLOCK
    qb = q.reshape(B, nb, BLOCK, N_KV_HEADS, Q_PER_KV, HEAD_DIM).astype(jnp.float32)
    pad = ((0, 0), (BLOCK, BLOCK), (0, 0), (0, 0))
    kp = jnp.pad(k, pad).reshape(B, nb + 2, BLOCK, N_KV_HEADS, HEAD_DIM)
    vp = jnp.pad(v, pad).reshape(B, nb + 2, BLOCK, N_KV_HEADS, HEAD_DIM)
    kband = jnp.concatenate([kp[:, :-2], kp[:, 1:-1], kp[:, 2:]], axis=2).astype(jnp.float32)
    vband = jnp.concatenate([vp[:, :-2], vp[:, 1:-1], vp[:, 2:]], axis=2).astype(jnp.float32)
    s_lat = jnp.einsum('bnqkgd,bnskd->bnkgqs', qb, kband) * ATTN_SCALE
    s_ctx = jnp.einsum('bnqkgd,bckd->bnkgqc', qb, k_ctx.astype(jnp.float32)) * ATTN_SCALE
    qi = jnp.arange(BLOCK)[:, None]
    kj = jnp.arange(3 * BLOCK)[None, :]
    in_window = jnp.abs(kj - BLOCK - qi) <= WINDOW
    kpos = jnp.arange(nb)[:, None, None] * BLOCK - BLOCK + kj[None]
    mask = in_window[None] & (kpos >= 0) & (kpos < T)
    s_lat = jnp.where(mask[None, :, None, None], s_lat, NEG)
    p = softmax_with_sink(jnp.concatenate([s_lat, s_ctx], axis=-1), sink)
    o = (jnp.einsum('bnkgqs,bnskd->bnqkgd', p[..., :3 * BLOCK], vband)
         + jnp.einsum('bnkgqc,bckd->bnqkgd', p[..., 3 * BLOCK:], v_ctx.astype(jnp.float32)))
    return o.reshape(B, T, ATTN_W).astype(q.dtype)


def centred_shift(u, w):
    T = u.shape[1]
    half = SHIFT_WIDTH // 2
    up = jnp.pad(u, ((0, 0), (half, half), (0, 0)))
    return sum(up[:, j:j + T] * w[j] for j in range(SHIFT_WIDTH))


def rwkv_heads(t):
    return t.reshape(t.shape[:-1] + (RWKV_HEADS, HEAD_DIM))


def rwkv_scan(s0, r, w, k, v, kk, a, reverse):
    def step(S, inp):
        r_t, w_t, k_t, v_t, kk_t, a_t = inp
        S = (S * w_t[:, :, None, :]
             - jnp.einsum('bhvk,bhk->bhv', S, kk_t)[..., None] * (kk_t * a_t)[:, :, None, :]
             + v_t[..., None] * k_t[:, :, None, :])
        return S, jnp.einsum('bhvk,bhk->bhv', S, r_t)

    xs = tuple(jnp.moveaxis(t, 1, 0) for t in (r, w, k, v, kk, a))
    s_fin, ys = lax.scan(step, s0, xs, reverse=reverse)
    return s_fin, jnp.moveaxis(ys, 0, 1)


def rwkv_mix(u, s_fwd, s_bwd, p):
    B, T = u.shape[:2]
    u = centred_shift(u, p['rwkv_shift_w']).astype(jnp.float32)
    r, k, v, gd, wd, ad = jnp.split(u, RWKV_SPLITS, axis=-1)
    wd = wd.reshape(B, T, 2, DECAY_RANK)
    ad = ad.reshape(B, T, 2, ICLR_RANK)
    log_w = -DECAY_SCALE * jax.nn.sigmoid(
        p['rwkv_w0'] + jnp.einsum('btdr,drc->btdc', jnp.tanh(wd), p['rwkv_w_up']))
    a = jax.nn.sigmoid(p['rwkv_a0'] + jnp.einsum('btdr,drc->btdc', ad, p['rwkv_a_up']))
    g = jax.nn.sigmoid(gd) @ p['rwkv_g_up']
    r, k, v = rwkv_heads(r), rwkv_heads(k), rwkv_heads(v)
    kk = k * rwkv_heads(p['rwkv_k_k'])
    kk = kk * lax.rsqrt(jnp.sum(kk * kk, axis=-1, keepdims=True) + 1e-12)
    k_a = rwkv_heads(p['rwkv_k_a'])
    ys = []
    finals = []
    for d, (s0, rev) in enumerate(((s_fwd, False), (s_bwd, True))):
        a_d = rwkv_heads(a[:, :, d])
        k_d = k * (1.0 + (a_d - 1.0) * k_a)
        s_fin, y_d = rwkv_scan(s0.astype(jnp.float32), r, jnp.exp(rwkv_heads(log_w[:, :, d])),
                               k_d, v, kk, a_d, rev)
        ys.append(y_d)
        finals.append(s_fin)
    y = ys[0] + ys[1]
    mean = jnp.mean(y, axis=-1, keepdims=True)
    var = jnp.mean(jnp.square(y - mean), axis=-1, keepdims=True)
    y = ((y - mean) * lax.rsqrt(var + GN_EPS)).reshape(B, T, RWKV_W) * p['rwkv_ln_g'] + p['rwkv_ln_b']
    bonus = (jnp.sum(r * k * p['rwkv_r_k'], axis=-1, keepdims=True) * v).reshape(B, T, RWKV_W)
    return (y + bonus) * g, finals[0], finals[1]


def fourier_mix(u):
    B, T = u.shape[:2]
    z = u.astype(jnp.float32).reshape(B, T, FNET_GROUPS, HEAD_DIM)
    y = jnp.fft.fftn(z, axes=(1, 3), norm='ortho').real
    return y.reshape(B, T, FNET_W).astype(u.dtype)


def split_projection(h, w_in):
    B, T = h.shape[:2]
    q, k, v, u_rwkv, u_fnet = jnp.split(h @ w_in, IN_SPLITS, axis=-1)
    q = q.reshape(B, T, N_Q_HEADS, HEAD_DIM)
    k = k.reshape(B, T, N_KV_HEADS, HEAD_DIM)
    v = v.reshape(B, T, N_KV_HEADS, HEAD_DIM)
    return q, k, v, u_rwkv, u_fnet


def merge_and_ffn(x, o_attn, o_rwkv, o_fnet, mod, p):
    _, _, g1, sh2, sc2, g2 = mod
    o = jnp.concatenate([o_attn.astype(x.dtype), o_rwkv.astype(x.dtype), o_fnet.astype(x.dtype)],
                        axis=-1) @ p['w_out']
    x = x + g1 * o
    h = modulate(rmsnorm(x, p['norm_ffn_g']), sh2, sc2)
    gt, up = jnp.split(h @ p['ffn_w_in'], 2, axis=-1)
    return x + g2 * ((jax.nn.silu(gt) * up) @ p['ffn_w_out'])


def context_layer(x, mod_vec, p):
    mod = jnp.split(mod_vec, 6, axis=-1)
    h = modulate(rmsnorm(x, p['norm_mix_g']), mod[0], mod[1])
    q, k, v, u_r, u_f = split_projection(h, p['w_in'])
    o_attn = context_attention(q, k, v, p['attn_sink'])
    s0 = jnp.zeros((x.shape[0], RWKV_HEADS, HEAD_DIM, HEAD_DIM), jnp.float32)
    o_rwkv, s_f, s_b = rwkv_mix(u_r, s0, s0, p)
    o_fnet = fourier_mix(u_f)
    x = merge_and_ffn(x, o_attn, o_rwkv, o_fnet, mod, p)
    return x, k, v, jnp.stack([s_f, s_b], axis=1)


def latent_layer(x, mod_vec, k_ctx, v_ctx, s_ctx, p):
    mod = jnp.split(mod_vec[:, None, :], 6, axis=-1)
    h = modulate(rmsnorm(x, p['norm_mix_g']), mod[0], mod[1])
    q, k, v, u_r, u_f = split_projection(h, p['w_in'])
    o_attn = latent_attention(axial_rope(q), axial_rope(k), v, k_ctx, v_ctx, p['attn_sink'])
    o_rwkv, _, _ = rwkv_mix(u_r, s_ctx[:, 0], s_ctx[:, 1], p)
    o_fnet = fourier_mix(u_f)
    return merge_and_ffn(x, o_attn, o_rwkv, o_fnet, mod, p)


def setup_inputs(seed: int = 0) -> dict:
    key = jax.random.key(seed)
    ks = jax.random.split(key, 28)

    def n(k, shape, s):
        return s * jax.random.normal(k, shape, jnp.float32)

    L = DEPTH
    return {
        'x_prompt': n(ks[0], (BATCH, SEQ, D_MODEL), 1.0),
        'x_sample': n(ks[1], (DEC_BATCH, DEC_SEQ, D_MODEL), 1.0),
        'cache_k': n(ks[2], (DEC_BATCH, L, PAST_LEN, N_KV_HEADS, HEAD_DIM), 1.0),
        'cache_v': n(ks[3], (DEC_BATCH, L, PAST_LEN, N_KV_HEADS, HEAD_DIM), 1.0),
        'state_rwkv': n(ks[4], (DEC_BATCH, L, 2, RWKV_HEADS, HEAD_DIM, HEAD_DIM), 0.5),
        'c': n(ks[5], (DEC_BATCH, D_MODEL), 1.0),
        'c_ctx': n(ks[6], (D_MODEL,), 1.0),
        'w_ada': n(ks[7], (L, D_MODEL, 6 * D_MODEL), 0.3 * D_MODEL ** -0.5),
        'b_ada': n(ks[8], (L, 6 * D_MODEL), 0.02),
        'norm_mix_g': 1.0 + n(ks[9], (L, D_MODEL), 0.05),
        'norm_ffn_g': 1.0 + n(ks[10], (L, D_MODEL), 0.05),
        'w_in': n(ks[11], (L, D_MODEL, IN_W), D_MODEL ** -0.5),
        'w_out': n(ks[12], (L, MIX_W, D_MODEL), MIX_W ** -0.5),
        'attn_sink': n(ks[13], (L, N_Q_HEADS), 0.5),
        'rwkv_shift_w': jnp.array([0.25, 0.5, 0.25], jnp.float32)[None, :, None]
                        + n(ks[14], (L, SHIFT_WIDTH, RWKV_IN_W), 0.05),
        'rwkv_w0': jax.random.uniform(ks[15], (L, 2, RWKV_W), jnp.float32, -5.0, 1.0),
        'rwkv_w_up': n(ks[16], (L, 2, DECAY_RANK, RWKV_W), 0.1 * DECAY_RANK ** -0.5),
        'rwkv_a0': n(ks[17], (L, 2, RWKV_W), 0.5),
        'rwkv_a_up': n(ks[18], (L, 2, ICLR_RANK, RWKV_W), 0.1 * ICLR_RANK ** -0.5),
        'rwkv_g_up': n(ks[19], (L, GATE_RANK, RWKV_W), GATE_RANK ** -0.5),
        'rwkv_k_k': 0.85 + n(ks[20], (L, RWKV_W), 0.05),
        'rwkv_k_a': 1.0 + n(ks[21], (L, RWKV_W), 0.05),
        'rwkv_r_k': n(ks[22], (L, RWKV_HEADS, HEAD_DIM), 0.1),
        'rwkv_ln_g': 1.0 + n(ks[23], (L, RWKV_W), 0.05),
        'rwkv_ln_b': n(ks[24], (L, RWKV_W), 0.02),
        'ffn_w_in': n(ks[25], (L, D_MODEL, 2 * D_FF), D_MODEL ** -0.5),
        'ffn_w_out': n(ks[26], (L, D_FF, D_MODEL), D_FF ** -0.5),
        'norm_final_g': 1.0 + n(ks[27], (D_MODEL,), 0.05),
    }


def reference(x_prompt, x_sample, cache_k, cache_v, state_rwkv, c, c_ctx,
              w_ada, b_ada, norm_mix_g, norm_ffn_g, w_in, w_out, attn_sink,
              rwkv_shift_w, rwkv_w0, rwkv_w_up, rwkv_a0, rwkv_a_up, rwkv_g_up,
              rwkv_k_k, rwkv_k_a, rwkv_r_k, rwkv_ln_g, rwkv_ln_b,
              ffn_w_in, ffn_w_out, norm_final_g):
    y_p = x_prompt
    y_s = x_sample
    new_k, new_v, new_s = [], [], []
    for l in range(DEPTH):
        p = {
            'norm_mix_g': norm_mix_g[l], 'norm_ffn_g': norm_ffn_g[l],
            'w_in': w_in[l], 'w_out': w_out[l], 'attn_sink': attn_sink[l],
            'rwkv_shift_w': rwkv_shift_w[l], 'rwkv_w0': rwkv_w0[l], 'rwkv_w_up': rwkv_w_up[l],
            'rwkv_a0': rwkv_a0[l], 'rwkv_a_up': rwkv_a_up[l], 'rwkv_g_up': rwkv_g_up[l],
            'rwkv_k_k': rwkv_k_k[l], 'rwkv_k_a': rwkv_k_a[l], 'rwkv_r_k': rwkv_r_k[l],
            'rwkv_ln_g': rwkv_ln_g[l], 'rwkv_ln_b': rwkv_ln_b[l],
            'ffn_w_in': ffn_w_in[l], 'ffn_w_out': ffn_w_out[l],
        }
        mod_ctx = jax.nn.silu(c_ctx) @ w_ada[l] + b_ada[l]
        mod_lat = jax.nn.silu(c) @ w_ada[l] + b_ada[l]
        y_p, k_l, v_l, s_l = context_layer(y_p, mod_ctx, p)
        y_s = latent_layer(y_s, mod_lat, cache_k[:, l], cache_v[:, l], state_rwkv[:, l], p)
        new_k.append(k_l)
        new_v.append(v_l)
        new_s.append(s_l)
    y_prompt = rmsnorm(y_p, norm_final_g)
    y_sample = rmsnorm(y_s, norm_final_g)
    new_cache_k = jnp.stack(new_k, axis=1)
    new_cache_v = jnp.stack(new_v, axis=1)
    new_state_rwkv = jnp.stack(new_s, axis=1)
    return (y_prompt, y_sample, new_cache_k, new_cache_v, new_state_rwkv)
```

```python
import functools

import numpy as np
import jax
import jax.numpy as jnp
from jax import lax
from jax.experimental import pallas as pl
from jax.experimental.pallas import tpu as pltpu

F32 = jnp.float32
BF16 = jnp.bfloat16

D_MODEL = 1024
N_CTX_SEQ = 16
CTX_LEN = 256
DEPTH = 4
N_LAT_SEQ = 4
LAT_LEN = 2048
PAST_LEN = 512
GRID_W = 64
HEAD_DIM = 64
N_Q_HEADS = 8
N_KV_HEADS = 2
Q_PER_KV = N_Q_HEADS // N_KV_HEADS
ATTN_W = N_Q_HEADS * HEAD_DIM
KV_W = N_KV_HEADS * HEAD_DIM
BLOCK = 128
ATTN_SCALE = 0.125
ROPE_THETA = 10000.0
NEG = -1e30
RWKV_HEADS = 4
RWKV_W = RWKV_HEADS * HEAD_DIM
DECAY_RANK = 64
ICLR_RANK = 64
GATE_RANK = 128
DECAY_SCALE = 0.6065306597126334
GN_EPS = 64e-5
RWKV_IN_W = 3 * RWKV_W + GATE_RANK + 2 * DECAY_RANK + 2 * ICLR_RANK
FNET_W = 4 * HEAD_DIM
IN_W = ATTN_W + 2 * KV_W + RWKV_IN_W + FNET_W
D_FF = 2816
RMS_EPS = 1e-6

N_CTX = N_CTX_SEQ * CTX_LEN
N_LAT = N_LAT_SEQ * LAT_LEN
N_TOK = N_CTX + N_LAT
TM = 256
N_TILES = N_TOK // TM
CTX_TILES = N_CTX // TM
LAT_TILES_PER_SEQ = LAT_LEN // TM
N_SEQ = N_CTX_SEQ + N_LAT_SEQ
CHUNK = 64
CHUNKS_PER_TILE = TM // CHUNK
HALO = 8

VMEM_LIMIT = 56 * 1024 * 1024


def _cparams(sem):
    return pltpu.CompilerParams(dimension_semantics=sem, vmem_limit_bytes=VMEM_LIMIT)


def _mod_row(i):
    return jnp.where(i < CTX_TILES, 0, 1 + (i - CTX_TILES) // LAT_TILES_PER_SEQ)


def _split2(x):
    hi = x.astype(BF16)
    lo = (x - hi.astype(F32)).astype(BF16)
    return hi, lo


_NN = (((1,), (0,)), ((), ()))
_NT = (((1,), (1,)), ((), ()))


def _dg(a, b, dims):
    return lax.dot_general(a, b, dims, preferred_element_type=F32)


def _dot1(a, b, dims=_NN):
    return _dg(a.astype(BF16), b.astype(BF16), dims)


def _dot3(a, b, dims=_NN):
    ah, al = _split2(a)
    bh, bl = _split2(b)
    return _dg(ah, bh, dims) + (_dg(ah, bl, dims) + _dg(al, bh, dims))


def _dot_exact_rhs(a, b_bf16):
    ah, al = _split2(a)
    return _dg(ah, b_bf16, _NN) + _dg(al, b_bf16, _NN)


def _dot_exact_lhs(a_bf16, b):
    bh, bl = _split2(b)
    return _dg(a_bf16, bh, _NN) + _dg(a_bf16, bl, _NN)


def _mods_kernel(c_ref, w_ref, b_ref, o_ref):
    c = c_ref[...]
    s = c * jax.nn.sigmoid(c)
    o_ref[...] = _dot1(s, w_ref[...]) + b_ref[...]


def _mods_call(cvec, w_ada, b_ada):
    tn = 1536
    return pl.pallas_call(
        _mods_kernel,
        grid=(DEPTH, 6 * D_MODEL // tn),
        in_specs=[
            pl.BlockSpec((8, D_MODEL), lambda l, j: (0, 0)),
            pl.BlockSpec((None, D_MODEL, tn), lambda l, j: (l, 0, j)),
            pl.BlockSpec((None, 1, tn), lambda l, j: (l, 0, j)),
        ],
        out_specs=pl.BlockSpec((None, 8, tn), lambda l, j: (l, 0, j)),
        out_shape=jax.ShapeDtypeStruct((DEPTH, 8, 6 * D_MODEL), F32),
        compiler_params=_cparams(("parallel", "parallel")),
        name="adaln_mods",
    )(cvec, w_ada, b_ada.reshape(DEPTH, 1, 6 * D_MODEL))


def _rms(x, g):
    return x * lax.rsqrt(jnp.mean(x * x, axis=-1, keepdims=True) + RMS_EPS) * g


def _inproj_kernel(x_ref, mod_ref, g_ref, w_ref, q_ref, k_ref, v_ref, ur_ref, uf_ref):
    shift = mod_ref[:, 0:D_MODEL]
    scale = mod_ref[:, D_MODEL:2 * D_MODEL]
    h = _rms(x_ref[...], g_ref[...]) * (1.0 + scale) + shift
    o = _dg(h.astype(BF16), w_ref[...], _NN)
    q_ref[...] = o[:, 0:ATTN_W]
    k_ref[...] = o[:, ATTN_W:ATTN_W + KV_W]
    v_ref[...] = o[:, ATTN_W + KV_W:ATTN_W + 2 * KV_W]
    ur_ref[...] = o[:, ATTN_W + 2 * KV_W:ATTN_W + 2 * KV_W + RWKV_IN_W]
    uf_ref[...] = o[:, ATTN_W + 2 * KV_W + RWKV_IN_W:IN_W]


def _inproj_call(x, mods_l, g, w_bf16, l):
    widths = (ATTN_W, KV_W, KV_W, RWKV_IN_W, FNET_W)
    return pl.pallas_call(
        _inproj_kernel,
        grid=(N_TILES,),
        in_specs=[
            pl.BlockSpec((TM, D_MODEL), lambda i: (i, 0)),
            pl.BlockSpec((None, 1, 6 * D_MODEL), lambda i: (_mod_row(i), 0, 0)),
            pl.BlockSpec((None, 1, D_MODEL), lambda i: (l, 0, 0)),
            pl.BlockSpec((None, D_MODEL, IN_W), lambda i: (l, 0, 0)),
        ],
        out_specs=[pl.BlockSpec((TM, w), lambda i: (i, 0)) for w in widths],
        out_shape=[jax.ShapeDtypeStruct((N_TOK, w), F32) for w in widths],
        compiler_params=_cparams(("parallel",)),
        name="in_projection",
    )(x, mods_l, g, w_bf16)


def _softmax_pv(s, sink_col, v):
    m = jnp.maximum(jnp.max(s, axis=-1, keepdims=True), sink_col)
    e = jnp.exp(s - m)
    denom = jnp.sum(e, axis=-1, keepdims=True) + jnp.exp(sink_col - m)
    return _dot1(e, v) / denom


def _stack_q_heads(q, kh, rows):
    return jnp.concatenate(
        [q[:, (kh * Q_PER_KV + g) * HEAD_DIM:(kh * Q_PER_KV + g + 1) * HEAD_DIM] for g in range(Q_PER_KV)],
        axis=0)


def _sink_col(sink_ref, l, kh, rows):
    return jnp.concatenate(
        [jnp.full((rows, 1), sink_ref[l, kh * Q_PER_KV + g], F32) for g in range(Q_PER_KV)], axis=0)


def _ctx_attn_kernel(l, sink_ref, q_ref, k_ref, v_ref, o_ref):
    q = q_ref[...]
    k = k_ref[...]
    v = v_ref[...]
    for kh in range(N_KV_HEADS):
        q4 = _stack_q_heads(q, kh, CTX_LEN)
        kk = k[:, kh * HEAD_DIM:(kh + 1) * HEAD_DIM]
        vv = v[:, kh * HEAD_DIM:(kh + 1) * HEAD_DIM]
        s = _dot1(q4, kk, _NT) * ATTN_SCALE
        o4 = _softmax_pv(s, _sink_col(sink_ref, l, kh, CTX_LEN), vv)
        for g in range(Q_PER_KV):
            h = kh * Q_PER_KV + g
            o_ref[:, h * HEAD_DIM:(h + 1) * HEAD_DIM] = o4[g * CTX_LEN:(g + 1) * CTX_LEN]


def _ctx_attn_call(q, k, v, sink, l):
    return pl.pallas_call(
        functools.partial(_ctx_attn_kernel, l),
        grid=(N_CTX_SEQ,),
        in_specs=[
            pl.BlockSpec(memory_space=pltpu.SMEM),
            pl.BlockSpec((CTX_LEN, ATTN_W), lambda b: (b, 0)),
            pl.BlockSpec((CTX_LEN, KV_W), lambda b: (b, 0)),
            pl.BlockSpec((CTX_LEN, KV_W), lambda b: (b, 0)),
        ],
        out_specs=pl.BlockSpec((CTX_LEN, ATTN_W), lambda b: (b, 0)),
        out_shape=jax.ShapeDtypeStruct((N_CTX, ATTN_W), F32),
        compiler_params=_cparams(("parallel",)),
        name="context_attention",
    )(sink, q, k, v)


def _rope(x, cos, sin):
    lane = lax.broadcasted_iota(jnp.int32, x.shape, 1)
    partner = jnp.where((lane & 16) == 0, pltpu.roll(x, 128 - 16, 1), pltpu.roll(x, 16, 1))
    return x * cos + partner * sin


def _lat_attn_kernel(l, sink_ref, q_ref, k_ref, v_ref, kc_ref, vc_ref, cos_ref, sin_ref, o_ref):
    n = pl.program_id(1)
    nb = LAT_LEN // BLOCK
    r0 = pl.multiple_of(n * BLOCK, BLOCK)
    cos_q = cos_ref[pl.ds(r0, BLOCK), :]
    sin_q = sin_ref[pl.ds(r0, BLOCK), :]
    q = jnp.concatenate(
        [_rope(q_ref[:, j * 128:(j + 1) * 128], cos_q, sin_q) for j in range(ATTN_W // 128)], axis=1)

    ks, vs = [], []
    for dj in (-1, 0, 1):
        rj = pl.multiple_of(jnp.clip(n + dj, 0, nb - 1) * BLOCK, BLOCK)
        ks.append(_rope(k_ref[pl.ds(rj, BLOCK), :], cos_ref[pl.ds(rj, BLOCK), :], sin_ref[pl.ds(rj, BLOCK), :]))
        vs.append(v_ref[pl.ds(rj, BLOCK), :])
    k_all = jnp.concatenate(ks + [kc_ref[...]], axis=0)
    v_all = jnp.concatenate(vs + [vc_ref[...]], axis=0)

    n_keys = 3 * BLOCK + PAST_LEN
    qi = lax.broadcasted_iota(jnp.int32, (Q_PER_KV * BLOCK, n_keys), 0) & (BLOCK - 1)
    c = lax.broadcasted_iota(jnp.int32, (Q_PER_KV * BLOCK, n_keys), 1)
    valid4 = (((c < BLOCK) & (c >= qi) & (n > 0))
              | ((c >= BLOCK) & (c < 2 * BLOCK))
              | ((c >= 2 * BLOCK) & (c <= 2 * BLOCK + qi) & (n < nb - 1))
              | (c >= 3 * BLOCK))

    for kh in range(N_KV_HEADS):
        q4 = _stack_q_heads(q, kh, BLOCK)
        kk = k_all[:, kh * HEAD_DIM:(kh + 1) * HEAD_DIM]
        vv = v_all[:, kh * HEAD_DIM:(kh + 1) * HEAD_DIM]
        s = jnp.where(valid4, _dot1(q4, kk, _NT) * ATTN_SCALE, NEG)
        o4 = _softmax_pv(s, _sink_col(sink_ref, l, kh, BLOCK), vv)
        for g in range(Q_PER_KV):
            h = kh * Q_PER_KV + g
            o_ref[:, h * HEAD_DIM:(h + 1) * HEAD_DIM] = o4[g * BLOCK:(g + 1) * BLOCK]


def _lat_attn_call(q, k, v, kc, vc, cos, sin, sink, l):
    nb = LAT_LEN // BLOCK
    q_off = N_CTX // BLOCK
    kv_off = N_CTX // LAT_LEN
    return pl.pallas_call(
        functools.partial(_lat_attn_kernel, l),
        grid=(N_LAT_SEQ, nb),
        in_specs=[
            pl.BlockSpec(memory_space=pltpu.SMEM),
            pl.BlockSpec((BLOCK, ATTN_W), lambda b, n: (q_off + b * nb + n, 0)),
            pl.BlockSpec((LAT_LEN, KV_W), lambda b, n: (kv_off + b, 0)),
            pl.BlockSpec((LAT_LEN, KV_W), lambda b, n: (kv_off + b, 0)),
            pl.BlockSpec((None, None, PAST_LEN, KV_W), lambda b, n: (b, l, 0, 0)),
            pl.BlockSpec((None, None, PAST_LEN, KV_W), lambda b, n: (b, l, 0, 0)),
            pl.BlockSpec((LAT_LEN, 128), lambda b, n: (0, 0)),
            pl.BlockSpec((LAT_LEN, 128), lambda b, n: (0, 0)),
        ],
        out_specs=pl.BlockSpec((BLOCK, ATTN_W), lambda b, n: (b * nb + n, 0)),
        out_shape=jax.ShapeDtypeStruct((N_LAT, ATTN_W), F32),
        compiler_params=_cparams(("parallel", "arbitrary")),
        name="latent_attention",
    )(sink, q, k, v, kc, vc, cos, sin)


def _rope_tables():
    t = np.arange(LAT_LEN)
    lane = np.arange(128)
    d = lane % HEAD_DIM
    pos = np.where(d[None, :] < HEAD_DIM // 2, (t // GRID_W)[:, None], (t % GRID_W)[:, None]).astype(np.float32)
    quarter = HEAD_DIM // 4
    inv = (np.float32(ROPE_THETA) ** (-(np.arange(quarter, dtype=np.float32)) / np.float32(quarter))).astype(np.float32)
    ang = pos * inv[d % quarter][None, :]
    sign = np.where((lane & 16) == 0, -1.0, 1.0)[None, :]
    return jnp.asarray(np.cos(ang), F32), jnp.asarray(np.sin(ang) * sign, F32)


def _head_block_mask(n):
    r = np.arange(n)[:, None] // HEAD_DIM
    c = np.arange(n)[None, :] // HEAD_DIM
    return (r == c).astype(np.float32)


def _rwkv_pre_kernel(u_ref, up_ref, un_ref, sw_ref, wup_ref, aup_ref, gup_ref, w0_ref, a0_ref,
                     kk_ref, ka_ref, rk_ref, ones_ref,
                     r_out, v_out, kn_out, g_out, bonus_out, lw_out, kd_out, b_out):
    i = pl.program_id(0)
    j = (i - CTX_TILES) % LAT_TILES_PER_SEQ
    has_prev = jnp.where((i >= CTX_TILES) & (j != 0), 1.0, 0.0)
    has_next = jnp.where((i >= CTX_TILES) & (j != LAT_TILES_PER_SEQ - 1), 1.0, 0.0)
    u = u_ref[...]
    row = lax.broadcasted_iota(jnp.int32, u.shape, 0)
    prev_row = up_ref[HALO - 1:HALO, :] * has_prev
    next_row = un_ref[0:1, :] * has_next
    u_dn = jnp.where(row == 0, prev_row, pltpu.roll(u, 1, 0))
    u_up = jnp.where(row == TM - 1, next_row, pltpu.roll(u, TM - 1, 0))
    us = u_dn * sw_ref[0:1, :] + u * sw_ref[1:2, :] + u_up * sw_ref[2:3, :]

    r = us[:, 0:RWKV_W]
    k = us[:, RWKV_W:2 * RWKV_W]
    v = us[:, 2 * RWKV_W:3 * RWKV_W]
    o = 3 * RWKV_W
    gd = us[:, o:o + GATE_RANK]
    wd = us[:, o + GATE_RANK:o + GATE_RANK + 2 * DECAY_RANK]
    ad = us[:, o + GATE_RANK + 2 * DECAY_RANK:RWKV_IN_W]

    w_pre = _dot1(jnp.tanh(wd), wup_ref[...])
    a_pre = _dot1(ad, aup_ref[...])
    g_out[...] = _dot1(jax.nn.sigmoid(gd), gup_ref[...])

    ones_bd = ones_ref[...]
    kn = k * kk_ref[...]
    kn = kn * lax.rsqrt(_dot_exact_rhs(kn * kn, ones_bd) + 1e-12)
    r_out[...] = r
    v_out[...] = v
    kn_out[...] = kn
    bonus_out[...] = _dot_exact_rhs(r * k * rk_ref[...], ones_bd) * v
    ka = ka_ref[...]
    for d in range(2):
        sl = slice(d * RWKV_W, (d + 1) * RWKV_W)
        a_d = jax.nn.sigmoid(a0_ref[d:d + 1, :] + a_pre[:, sl])
        lw_out[d] = -DECAY_SCALE * jax.nn.sigmoid(w0_ref[d:d + 1, :] + w_pre[:, sl])
        kd_out[d] = k * (1.0 + (a_d - 1.0) * ka)
        b_out[d] = kn * a_d


def _rwkv_pre_call(u, sw, wup_bd, aup_bd, gup, w0, a0, k_k, k_a, r_k, ones_bd, l):
    hb = TM // HALO
    last = N_TOK // HALO - 1
    row = lambda i: (l, 0, 0)
    tok = jax.ShapeDtypeStruct((N_TOK, RWKV_W), F32)
    tok2 = jax.ShapeDtypeStruct((2, N_TOK, RWKV_W), F32)
    return pl.pallas_call(
        _rwkv_pre_kernel,
        grid=(N_TILES,),
        in_specs=[
            pl.BlockSpec((TM, RWKV_IN_W), lambda i: (i, 0)),
            pl.BlockSpec((HALO, RWKV_IN_W), lambda i: (jnp.maximum(i * hb - 1, 0), 0)),
            pl.BlockSpec((HALO, RWKV_IN_W), lambda i: (jnp.minimum((i + 1) * hb, last), 0)),
            pl.BlockSpec((None, 3, RWKV_IN_W), row),
            pl.BlockSpec((None, 2 * DECAY_RANK, 2 * RWKV_W), row),
            pl.BlockSpec((None, 2 * ICLR_RANK, 2 * RWKV_W), row),
            pl.BlockSpec((None, GATE_RANK, RWKV_W), row),
            pl.BlockSpec((None, 2, RWKV_W), row),
            pl.BlockSpec((None, 2, RWKV_W), row),
            pl.BlockSpec((None, 1, RWKV_W), row),
            pl.BlockSpec((None, 1, RWKV_W), row),
            pl.BlockSpec((None, 1, RWKV_W), row),
            pl.BlockSpec((RWKV_W, RWKV_W), lambda i: (0, 0)),
        ],
        out_specs=[pl.BlockSpec((TM, RWKV_W), lambda i: (i, 0))] * 5
        + [pl.BlockSpec((2, TM, RWKV_W), lambda i: (0, i, 0))] * 3,
        out_shape=[tok] * 5 + [tok2] * 3,
        compiler_params=_cparams(("parallel",)),
        name="rwkv_token_terms",
    )(u, u, u, sw, wup_bd, aup_bd, gup, w0, a0, k_k, k_a, r_k, ones_bd)


def _stack_heads(x, bmask):
    return jnp.concatenate([x] * RWKV_HEADS, axis=0) * bmask


def _scan_kernel(reverse, r_ref, v_ref, kn_ref, lw_ref, kd_ref, b_ref, s0_ref,
                 cs_ref, tot_ref, strict_ref, incl_ref, eye_ref, bmask_ref,
                 y_ref, sfin_ref, s_scr):
    i = pl.program_id(0)
    t = (N_TILES - 1 - i) if reverse else i
    jt = (t - CTX_TILES) % LAT_TILES_PER_SEQ
    first = (t < CTX_TILES) | (jt == (LAT_TILES_PER_SEQ - 1 if reverse else 0))
    bmask = bmask_ref[...]

    @pl.when(first)
    def _():
        s_scr[...] = jnp.concatenate([s0_ref[...]] * RWKV_HEADS, axis=1) * bmask

    lw = lw_ref[...]
    linc = _dot_exact_lhs(cs_ref[...], lw)
    ltot = _dot_exact_lhs(tot_ref[...], lw)
    kn = kn_ref[...]
    kd = kd_ref[...]
    b = b_ref[...]
    v_all = v_ref[...]
    r_t = r_ref[...] * jnp.exp(linc)
    a_t = -kn * jnp.exp(linc - lw)
    e_neg = jnp.exp(-linc)
    b_t = b * e_neg
    k_t = kd * e_neg
    e_rem = jnp.exp(ltot - linc)
    b_p = b * e_rem
    k_p = kd * e_rem
    dec = jnp.exp(ltot)

    strict = strict_ref[...]
    incl = incl_ref[...]
    eye = eye_ref[...]
    order = range(CHUNKS_PER_TILE - 1, -1, -1) if reverse else range(CHUNKS_PER_TILE)
    s = s_scr[...]
    for c in order:
        sl = slice(c * CHUNK, (c + 1) * CHUNK)
        bs = _stack_heads(b_t[sl], bmask)
        ks = _stack_heads(k_t[sl], bmask)
        vs = _stack_heads(v_all[sl], bmask)
        a_ab = _dot3(a_t[sl], bs, _NT) * strict
        a_ak = _dot3(a_t[sl], ks, _NT) * strict
        p_rb = _dot3(r_t[sl], bs, _NT) * incl
        p_rk = _dot3(r_t[sl], ks, _NT) * incl
        tinv = eye + a_ab
        p = a_ab
        for _ in range(5):
            p = _dot3(p, _stack_heads(p, bmask))
            tinv = tinv + _dot3(tinv, _stack_heads(p, bmask))
        w = _dot3(tinv, _stack_heads(a_t[sl], bmask))
        u0 = _dot3(tinv, _stack_heads(_dot3(a_ak, vs), bmask))
        qp = r_t[sl] + _dot3(p_rb, _stack_heads(w, bmask))
        y0 = _dot3(p_rb, _stack_heads(u0, bmask)) + _dot3(p_rk, vs)
        g = _dot3(v_all[sl].T, k_p[sl]) * bmask
        u = _dot3(w, s, _NT) + u0
        y_ref[sl, :] = _dot3(qp, s, _NT) + y0
        s = s * dec[c * CHUNK:c * CHUNK + 1, :] + _dot3(u.T, b_p[sl]) * bmask + g
    s_scr[...] = s
    sfin_ref[...] = s


def _scan_call(reverse, r, v, kn, lw, kd, b, s0, consts, l):
    d = 1 if reverse else 0
    cs, tot, strict, incl, eye, bmask = consts
    tile = (lambda i: (N_TILES - 1 - i, 0)) if reverse else (lambda i: (i, 0))
    tile_d = (lambda i: (d, N_TILES - 1 - i, 0)) if reverse else (lambda i: (d, i, 0))

    def seq_of(i):
        t = (N_TILES - 1 - i) if reverse else i
        return jnp.where(t < CTX_TILES, t, CTX_TILES + (t - CTX_TILES) // LAT_TILES_PER_SEQ)

    def sfin_of(i):
        t = (N_TILES - 1 - i) if reverse else i
        return (jnp.minimum(t, CTX_TILES), 0, 0)

    const2 = lambda i: (0, 0)
    tok = pl.BlockSpec((TM, RWKV_W), tile)
    tok_d = pl.BlockSpec((None, TM, RWKV_W), tile_d)
    return pl.pallas_call(
        functools.partial(_scan_kernel, reverse),
        grid=(N_TILES,),
        in_specs=[
            tok, tok, tok, tok_d, tok_d, tok_d,
            pl.BlockSpec((None, None, RWKV_W, HEAD_DIM), lambda i: (seq_of(i), d, 0, 0)),
            pl.BlockSpec((None, TM, TM), lambda i: (d, 0, 0)),
            pl.BlockSpec((TM, TM), const2),
            pl.BlockSpec((None, CHUNK, RWKV_W), lambda i: (d, 0, 0)),
            pl.BlockSpec((None, CHUNK, RWKV_W), lambda i: (d, 0, 0)),
            pl.BlockSpec((CHUNK, RWKV_W), const2),
            pl.BlockSpec((RWKV_W, RWKV_W), const2),
        ],
        out_specs=[
            tok,
            pl.BlockSpec((None, RWKV_W, RWKV_W), sfin_of),
        ],
        out_shape=[
            jax.ShapeDtypeStruct((N_TOK, RWKV_W), F32),
            jax.ShapeDtypeStruct((CTX_TILES + 1, RWKV_W, RWKV_W), F32),
        ],
        scratch_shapes=[pltpu.VMEM((RWKV_W, RWKV_W), F32)],
        compiler_params=_cparams(("arbitrary",)),
        name="rwkv_scan_bwd" if reverse else "rwkv_scan_fwd",
    )(r, v, kn, lw, kd, b, s0, cs, tot, strict, incl, eye, bmask)


def _scan_consts():
    t = np.arange(TM)
    same_chunk = (t[:, None] // CHUNK) == (t[None, :] // CHUNK)
    cs_f = same_chunk & (t[None, :] <= t[:, None])
    cs_b = same_chunk & (t[None, :] >= t[:, None])
    cs = np.stack([cs_f, cs_b]).astype(np.float32)
    tot = same_chunk.astype(np.float32)
    q = np.arange(CHUNK)[:, None]
    kcol = np.arange(RWKV_W)[None, :] % CHUNK
    strict = np.stack([kcol < q, kcol > q]).astype(np.float32)
    incl = np.stack([kcol <= q, kcol >= q]).astype(np.float32)
    eye = (kcol == q).astype(np.float32)
    return (jnp.asarray(cs, BF16), jnp.asarray(tot, BF16), jnp.asarray(strict), jnp.asarray(incl),
            jnp.asarray(eye), jnp.asarray(_head_block_mask(RWKV_W)))


def _fnet_kernel(seq_len, z_ref, ch_ref, cl_ref, th_ref, tl_ref, o_ref, zh_scr, zl_scr):
    @pl.when(pl.program_id(1) == 0)
    def _():
        zh, zl = _split2(z_ref[...])
        for half in range(2):
            ch = ch_ref[half]
            cl = cl_ref[half]
            zc = _dg(zh, ch, _NN) + (_dg(zh, cl, _NN) + _dg(zl, ch, _NN))
            h, lo = _split2(zc)
            zh_scr[half * seq_len:(half + 1) * seq_len, :] = h
            zl_scr[half * seq_len:(half + 1) * seq_len, :] = lo

    th = th_ref[...]
    o_ref[...] = _dg(th, zh_scr[...], _NN) + (_dg(th, zl_scr[...], _NN) + _dg(tl_ref[...], zh_scr[...], _NN))


def _fnet_call(u_f, seq_len, n_seq, row_off, tables, name):
    ch, cl, th, tl = tables
    tt = 256
    nt = seq_len // tt
    seq_off = row_off // seq_len
    return pl.pallas_call(
        functools.partial(_fnet_kernel, seq_len),
        grid=(n_seq, nt),
        in_specs=[
            pl.BlockSpec((seq_len, FNET_W), lambda b, j: (seq_off + b, 0)),
            pl.BlockSpec((2, FNET_W, FNET_W), lambda b, j: (0, 0, 0)),
            pl.BlockSpec((2, FNET_W, FNET_W), lambda b, j: (0, 0, 0)),
            pl.BlockSpec((tt, 2 * seq_len), lambda b, j: (j, 0)),
            pl.BlockSpec((tt, 2 * seq_len), lambda b, j: (j, 0)),
        ],
        out_specs=pl.BlockSpec((tt, FNET_W), lambda b, j: (b * nt + j, 0)),
        out_shape=jax.ShapeDtypeStruct((n_seq * seq_len, FNET_W), F32),
        scratch_shapes=[pltpu.VMEM((2 * seq_len, FNET_W), BF16), pltpu.VMEM((2 * seq_len, FNET_W), BF16)],
        compiler_params=_cparams(("parallel", "arbitrary")),
        name=name,
    )(u_f, ch, cl, th, tl)


def _dft_tables(seq_len):
    c = np.arange(FNET_W)
    ang_c = 2.0 * np.pi * ((c[:, None] % HEAD_DIM) * (c[None, :] % HEAD_DIM) % HEAD_DIM) / HEAD_DIM
    scale = 1.0 / np.sqrt(float(seq_len) * HEAD_DIM)
    blk = _head_block_mask(FNET_W)
    chan = jnp.asarray(np.stack([np.cos(ang_c) * blk * scale, np.sin(ang_c) * blk * scale]), F32)
    t = jnp.arange(seq_len, dtype=jnp.int32)
    ang_t = ((t[:, None] * t[None, :]) % seq_len).astype(F32) * F32(2.0 * np.pi / seq_len)
    pos = jnp.concatenate([jnp.cos(ang_t), -jnp.sin(ang_t)], axis=1)
    ch, cl = _split2(chan)
    th, tl = _split2(pos)
    return ch, cl, th, tl


def _merge_ffn_kernel(x_ref, mod_ref, oa_ref, yf_ref, yb_ref, bonus_ref, g_ref, of_ref,
                      lng_ref, lnb_ref, avg_ref, nfg_ref, wo_ref, wi_ref, w2_ref, o_ref):
    avg = avg_ref[...]
    y = yf_ref[...] + yb_ref[...]
    yc = y - _dot_exact_rhs(y, avg)
    var = _dot_exact_rhs(yc * yc, avg)
    yn = yc * lax.rsqrt(var + GN_EPS) * lng_ref[...] + lnb_ref[...]
    o_rwkv = (yn + bonus_ref[...]) * g_ref[...]

    o = (_dg(oa_ref[...].astype(BF16), wo_ref[0:ATTN_W, :], _NN)
         + _dg(o_rwkv.astype(BF16), wo_ref[ATTN_W:ATTN_W + RWKV_W, :], _NN)
         + _dg(of_ref[...].astype(BF16), wo_ref[ATTN_W + RWKV_W:, :], _NN))
    g1 = mod_ref[:, 2 * D_MODEL:3 * D_MODEL]
    sh2 = mod_ref[:, 3 * D_MODEL:4 * D_MODEL]
    sc2 = mod_ref[:, 4 * D_MODEL:5 * D_MODEL]
    g2 = mod_ref[:, 5 * D_MODEL:6 * D_MODEL]
    x = x_ref[...] + g1 * o
    h = _rms(x, nfg_ref[...]) * (1.0 + sc2) + sh2
    gu = _dg(h.astype(BF16), wi_ref[...], _NN)
    gt = gu[:, 0:D_FF]
    act = gt * jax.nn.sigmoid(gt) * gu[:, D_FF:]
    o_ref[...] = x + g2 * _dg(act.astype(BF16), w2_ref[...], _NN)


def _merge_ffn_call(x, mods_l, o_attn, y_f, y_b, bonus, g, o_fnet, ln_g, ln_b, avg, nfg, wo, wi, w2, l):
    row = lambda i: (l, 0, 0)
    tile = lambda i: (i, 0)
    once = pl.Buffered(1)
    return pl.pallas_call(
        _merge_ffn_kernel,
        grid=(N_TILES,),
        in_specs=[
            pl.BlockSpec((TM, D_MODEL), tile),
            pl.BlockSpec((None, 1, 6 * D_MODEL), lambda i: (_mod_row(i), 0, 0)),
            pl.BlockSpec((TM, ATTN_W), tile),
            pl.BlockSpec((TM, RWKV_W), tile),
            pl.BlockSpec((TM, RWKV_W), tile),
            pl.BlockSpec((TM, RWKV_W), tile),
            pl.BlockSpec((TM, RWKV_W), tile),
            pl.BlockSpec((TM, FNET_W), tile),
            pl.BlockSpec((None, 1, RWKV_W), row),
            pl.BlockSpec((None, 1, RWKV_W), row),
            pl.BlockSpec((RWKV_W, RWKV_W), lambda i: (0, 0)),
            pl.BlockSpec((None, 1, D_MODEL), row),
            pl.BlockSpec((None, D_MODEL, D_MODEL), row, pipeline_mode=once),
            pl.BlockSpec((None, D_MODEL, 2 * D_FF), row, pipeline_mode=once),
            pl.BlockSpec((None, D_FF, D_MODEL), row, pipeline_mode=once),
        ],
        out_specs=pl.BlockSpec((TM, D_MODEL), tile),
        out_shape=jax.ShapeDtypeStruct((N_TOK, D_MODEL), F32),
        compiler_params=_cparams(("parallel",)),
        name="merge_ffn",
    )(x, mods_l, o_attn, y_f, y_b, bonus, g, o_fnet, ln_g, ln_b, avg, nfg, wo, wi, w2)


def _final_norm_kernel(x_ref, g_ref, o_ref):
    o_ref[...] = _rms(x_ref[...], g_ref[...])


def _final_norm_call(x, g):
    return pl.pallas_call(
        _final_norm_kernel,
        grid=(N_TILES,),
        in_specs=[pl.BlockSpec((TM, D_MODEL), lambda i: (i, 0)), pl.BlockSpec((1, D_MODEL), lambda i: (0, 0))],
        out_specs=pl.BlockSpec((TM, D_MODEL), lambda i: (i, 0)),
        out_shape=jax.ShapeDtypeStruct((N_TOK, D_MODEL), F32),
        compiler_params=_cparams(("parallel",)),
        name="final_norm",
    )(x, g)


def _block_diag2(w):
    z = jnp.zeros_like(w[:, 0])
    return jnp.concatenate([jnp.concatenate([w[:, 0], z], axis=2), jnp.concatenate([z, w[:, 1]], axis=2)], axis=1)


def kernel(x_prompt, x_sample, cache_k, cache_v, state_rwkv, c, c_ctx, w_ada, b_ada, norm_mix_g, norm_ffn_g,
           w_in, w_out, attn_sink, rwkv_shift_w, rwkv_w0, rwkv_w_up, rwkv_a0, rwkv_a_up, rwkv_g_up,
           rwkv_k_k, rwkv_k_a, rwkv_r_k, rwkv_ln_g, rwkv_ln_b, ffn_w_in, ffn_w_out, norm_final_g):
    x = jnp.concatenate([x_prompt.reshape(N_CTX, D_MODEL), x_sample.reshape(N_LAT, D_MODEL)], axis=0)
    cvec = jnp.concatenate([c_ctx[None, :], c, jnp.zeros((8 - 1 - N_LAT_SEQ, D_MODEL), F32)], axis=0)
    mods = _mods_call(cvec, w_ada, b_ada).reshape(DEPTH, 8, 1, 6 * D_MODEL)

    w_in_b = w_in.astype(BF16)
    w_out_b = w_out.astype(BF16)
    ffn_in_b = ffn_w_in.astype(BF16)
    ffn_out_b = ffn_w_out.astype(BF16)
    wup_bd = _block_diag2(rwkv_w_up)
    aup_bd = _block_diag2(rwkv_a_up)
    row3 = lambda a: a.reshape(DEPTH, 1, -1)
    nmg, nfg = row3(norm_mix_g), row3(norm_ffn_g)
    k_k, k_a, r_k = row3(rwkv_k_k), row3(rwkv_k_a), row3(rwkv_r_k)
    ln_g, ln_b = row3(rwkv_ln_g), row3(rwkv_ln_b)
    kc = cache_k.reshape(N_LAT_SEQ, DEPTH, PAST_LEN, KV_W)
    vc = cache_v.reshape(N_LAT_SEQ, DEPTH, PAST_LEN, KV_W)
    s0_lat = state_rwkv.reshape(N_LAT_SEQ, DEPTH, 2, RWKV_W, HEAD_DIM)
    s0_ctx = jnp.zeros((N_CTX_SEQ, 2, RWKV_W, HEAD_DIM), F32)

    cos, sin = _rope_tables()
    consts = _scan_consts()
    bmask = consts[-1]
    ones_bd = bmask.astype(BF16)
    avg_bd = (bmask * (1.0 / HEAD_DIM)).astype(BF16)
    dft_ctx = _dft_tables(CTX_LEN)
    dft_lat = _dft_tables(LAT_LEN)

    new_k, new_v, new_s = [], [], []
    for l in range(DEPTH):
        q, k, v, u_r, u_f = _inproj_call(x, mods[l], nmg, w_in_b, l)
        new_k.append(k[:N_CTX].reshape(N_CTX_SEQ, CTX_LEN, N_KV_HEADS, HEAD_DIM))
        new_v.append(v[:N_CTX].reshape(N_CTX_SEQ, CTX_LEN, N_KV_HEADS, HEAD_DIM))

        o_attn = jnp.concatenate([
            _ctx_attn_call(q, k, v, attn_sink, l),
            _lat_attn_call(q, k, v, kc, vc, cos, sin, attn_sink, l),
        ], axis=0)

        r, vv, kn, g, bonus, lw, kd, b = _rwkv_pre_call(
            u_r, rwkv_shift_w, wup_bd, aup_bd, rwkv_g_up, rwkv_w0, rwkv_a0, k_k, k_a, r_k, ones_bd, l)
        s0 = jnp.concatenate([s0_ctx, s0_lat[:, l]], axis=0)
        y_f, sf = _scan_call(False, r, vv, kn, lw, kd, b, s0, consts, l)
        y_b, sb = _scan_call(True, r, vv, kn, lw, kd, b, s0, consts, l)
        fin = jnp.stack([sf[:N_CTX_SEQ], sb[:N_CTX_SEQ]], axis=1)
        fin = fin.reshape(N_CTX_SEQ, 2, RWKV_HEADS, HEAD_DIM, RWKV_HEADS, HEAD_DIM)
        new_s.append(jnp.stack([fin[:, :, h, :, h, :] for h in range(RWKV_HEADS)], axis=2))

        o_fnet = jnp.concatenate([
            _fnet_call(u_f, CTX_LEN, N_CTX_SEQ, 0, dft_ctx, "fnet_context"),
            _fnet_call(u_f, LAT_LEN, N_LAT_SEQ, N_CTX, dft_lat, "fnet_latent"),
        ], axis=0)

        x = _merge_ffn_call(x, mods[l], o_attn, y_f, y_b, bonus, g, o_fnet, ln_g, ln_b, avg_bd, nfg,
                            w_out_b, ffn_in_b, ffn_out_b, l)

    y = _final_norm_call(x, norm_final_g.reshape(1, D_MODEL))
    return (y[:N_CTX].reshape(N_CTX_SEQ, CTX_LEN, D_MODEL),
            y[N_CTX:].reshape(N_LAT_SEQ, LAT_LEN, D_MODEL),
            jnp.stack(new_k, axis=1), jnp.stack(new_v, axis=1), jnp.stack(new_s, axis=1))
```

```python
import functools

import numpy as np
import jax
import jax.numpy as jnp
from jax import lax
from jax.experimental import pallas as pl
from jax.experimental.pallas import tpu as pltpu

F32 = jnp.float32
BF16 = jnp.bfloat16

D_MODEL = 1024
N_CTX_SEQ = 16
CTX_LEN = 256
DEPTH = 4
N_LAT_SEQ = 4
LAT_LEN = 2048
PAST_LEN = 512
GRID_W = 64
HEAD_DIM = 64
N_Q_HEADS = 8
N_KV_HEADS = 2
Q_PER_KV = N_Q_HEADS // N_KV_HEADS
ATTN_W = N_Q_HEADS * HEAD_DIM
KV_W = N_KV_HEADS * HEAD_DIM
BLOCK = 128
ATTN_SCALE = 0.125
ROPE_THETA = 10000.0
NEG = -1e30
RWKV_HEADS = 4
RWKV_W = RWKV_HEADS * HEAD_DIM
DECAY_RANK = 64
ICLR_RANK = 64
GATE_RANK = 128
DECAY_SCALE = 0.6065306597126334
GN_EPS = 64e-5
RWKV_IN_W = 3 * RWKV_W + GATE_RANK + 2 * DECAY_RANK + 2 * ICLR_RANK
FNET_W = 4 * HEAD_DIM
IN_W = ATTN_W + 2 * KV_W + RWKV_IN_W + FNET_W
D_FF = 2816
RMS_EPS = 1e-6

N_CTX = N_CTX_SEQ * CTX_LEN
N_LAT = N_LAT_SEQ * LAT_LEN
N_TOK = N_CTX + N_LAT
TM = 256
N_TILES = N_TOK // TM
CTX_TILES = N_CTX // TM
LAT_TILES_PER_SEQ = LAT_LEN // TM
N_SEQ = N_CTX_SEQ + N_LAT_SEQ
CHUNK = 64
CHUNKS_PER_TILE = TM // CHUNK
HALO = 8

VMEM_LIMIT = 56 * 1024 * 1024


def _cparams(sem):
    return pltpu.CompilerParams(dimension_semantics=sem, vmem_limit_bytes=VMEM_LIMIT)


def _mod_row(i):
    return jnp.where(i < CTX_TILES, 0, 1 + (i - CTX_TILES) // LAT_TILES_PER_SEQ)


def _split2(x):
    hi = x.astype(BF16)
    lo = (x - hi.astype(F32)).astype(BF16)
    return hi, lo


_NN = (((1,), (0,)), ((), ()))
_NT = (((1,), (1,)), ((), ()))


def _dg(a, b, dims):
    return lax.dot_general(a, b, dims, preferred_element_type=F32)


def _dot1(a, b, dims=_NN):
    return _dg(a.astype(BF16), b.astype(BF16), dims)


def _dot3(a, b, dims=_NN):
    ah, al = _split2(a)
    bh, bl = _split2(b)
    return _dg(ah, bh, dims) + (_dg(ah, bl, dims) + _dg(al, bh, dims))


def _dot_exact_rhs(a, b_bf16):
    ah, al = _split2(a)
    return _dg(ah, b_bf16, _NN) + _dg(al, b_bf16, _NN)


def _dot_exact_lhs(a_bf16, b):
    bh, bl = _split2(b)
    return _dg(a_bf16, bh, _NN) + _dg(a_bf16, bl, _NN)


def _mods_kernel(c_ref, w_ref, b_ref, o_ref):
    c = c_ref[...]
    s = c * jax.nn.sigmoid(c)
    o_ref[...] = _dot1(s, w_ref[...]) + b_ref[...]


def _mods_call(cvec, w_ada, b_ada):
    tn = 1536
    return pl.pallas_call(
        _mods_kernel,
        grid=(DEPTH, 6 * D_MODEL // tn),
        in_specs=[
            pl.BlockSpec((8, D_MODEL), lambda l, j: (0, 0)),
            pl.BlockSpec((None, D_MODEL, tn), lambda l, j: (l, 0, j)),
            pl.BlockSpec((None, 1, tn), lambda l, j: (l, 0, j)),
        ],
        out_specs=pl.BlockSpec((None, 8, tn), lambda l, j: (l, 0, j)),
        out_shape=jax.ShapeDtypeStruct((DEPTH, 8, 6 * D_MODEL), F32),
        compiler_params=_cparams(("parallel", "parallel")),
        name="adaln_mods",
    )(cvec, w_ada, b_ada.reshape(DEPTH, 1, 6 * D_MODEL))


def _rms(x, g):
    return x * lax.rsqrt(jnp.mean(x * x, axis=-1, keepdims=True) + RMS_EPS) * g


def _inproj_kernel(x_ref, mod_ref, g_ref, w_ref, q_ref, k_ref, v_ref, ur_ref, uf_ref):
    shift = mod_ref[:, 0:D_MODEL]
    scale = mod_ref[:, D_MODEL:2 * D_MODEL]
    h = _rms(x_ref[...], g_ref[...]) * (1.0 + scale) + shift
    o = _dg(h.astype(BF16), w_ref[...], _NN)
    q_ref[...] = o[:, 0:ATTN_W]
    k_ref[...] = o[:, ATTN_W:ATTN_W + KV_W]
    v_ref[...] = o[:, ATTN_W + KV_W:ATTN_W + 2 * KV_W]
    ur_ref[...] = o[:, ATTN_W + 2 * KV_W:ATTN_W + 2 * KV_W + RWKV_IN_W]
    uf_ref[...] = o[:, ATTN_W + 2 * KV_W + RWKV_IN_W:IN_W]


def _inproj_call(x, mods_l, g, w_bf16, l):
    widths = (ATTN_W, KV_W, KV_W, RWKV_IN_W, FNET_W)
    return pl.pallas_call(
        _inproj_kernel,
        grid=(N_TILES,),
        in_specs=[
            pl.BlockSpec((TM, D_MODEL), lambda i: (i, 0)),
            pl.BlockSpec((None, 1, 6 * D_MODEL), lambda i: (_mod_row(i), 0, 0)),
            pl.BlockSpec((None, 1, D_MODEL), lambda i: (l, 0, 0)),
            pl.BlockSpec((None, D_MODEL, IN_W), lambda i: (l, 0, 0)),
        ],
        out_specs=[pl.BlockSpec((TM, w), lambda i: (i, 0)) for w in widths],
        out_shape=[jax.ShapeDtypeStruct((N_TOK, w), F32) for w in widths],
        compiler_params=_cparams(("parallel",)),
        name="in_projection",
    )(x, mods_l, g, w_bf16)


def _softmax_pv(s, sink_col, v):
    m = jnp.maximum(jnp.max(s, axis=-1, keepdims=True), sink_col)
    e = jnp.exp(s - m)
    denom = jnp.sum(e, axis=-1, keepdims=True) + jnp.exp(sink_col - m)
    return _dot1(e, v) / denom


def _stack_q_heads(q, kh, rows):
    return jnp.concatenate(
        [q[:, (kh * Q_PER_KV + g) * HEAD_DIM:(kh * Q_PER_KV + g + 1) * HEAD_DIM] for g in range(Q_PER_KV)],
        axis=0)


def _sink_col(sink_ref, l, kh, rows):
    return jnp.concatenate(
        [jnp.full((rows, 1), sink_ref[l, kh * Q_PER_KV + g], F32) for g in range(Q_PER_KV)], axis=0)


def _ctx_attn_kernel(l, sink_ref, q_ref, k_ref, v_ref, o_ref):
    q = q_ref[...]
    k = k_ref[...]
    v = v_ref[...]
    for kh in range(N_KV_HEADS):
        q4 = _stack_q_heads(q, kh, CTX_LEN)
        kk = k[:, kh * HEAD_DIM:(kh + 1) * HEAD_DIM]
        vv = v[:, kh * HEAD_DIM:(kh + 1) * HEAD_DIM]
        s = _dot1(q4, kk, _NT) * ATTN_SCALE
        o4 = _softmax_pv(s, _sink_col(sink_ref, l, kh, CTX_LEN), vv)
        for g in range(Q_PER_KV):
            h = kh * Q_PER_KV + g
            o_ref[:, h * HEAD_DIM:(h + 1) * HEAD_DIM] = o4[g * CTX_LEN:(g + 1) * CTX_LEN]


def _ctx_attn_call(q, k, v, sink, l):
    return pl.pallas_call(
        functools.partial(_ctx_attn_kernel, l),
        grid=(N_CTX_SEQ,),
        in_specs=[
            pl.BlockSpec(memory_space=pltpu.SMEM),
            pl.BlockSpec((CTX_LEN, ATTN_W), lambda b: (b, 0)),
            pl.BlockSpec((CTX_LEN, KV_W), lambda b: (b, 0)),
            pl.BlockSpec((CTX_LEN, KV_W), lambda b: (b, 0)),
        ],
        out_specs=pl.BlockSpec((CTX_LEN, ATTN_W), lambda b: (b, 0)),
        out_shape=jax.ShapeDtypeStruct((N_CTX, ATTN_W), F32),
        compiler_params=_cparams(("parallel",)),
        name="context_attention",
    )(sink, q, k, v)


def _rope(x, cos, sin):
    lane = lax.broadcasted_iota(jnp.int32, x.shape, 1)
    partner = jnp.where((lane & 16) == 0, pltpu.roll(x, 128 - 16, 1), pltpu.roll(x, 16, 1))
    return x * cos + partner * sin


def _lat_attn_kernel(l, sink_ref, q_ref, k_ref, v_ref, kc_ref, vc_ref, cos_ref, sin_ref, o_ref):
    n = pl.program_id(1)
    nb = LAT_LEN // BLOCK
    r0 = pl.multiple_of(n * BLOCK, BLOCK)
    cos_q = cos_ref[pl.ds(r0, BLOCK), :]
    sin_q = sin_ref[pl.ds(r0, BLOCK), :]
    q = jnp.concatenate(
        [_rope(q_ref[:, j * 128:(j + 1) * 128], cos_q, sin_q) for j in range(ATTN_W // 128)], axis=1)

    ks, vs = [], []
    for dj in (-1, 0, 1):
        rj = pl.multiple_of(jnp.clip(n + dj, 0, nb - 1) * BLOCK, BLOCK)
        ks.append(_rope(k_ref[pl.ds(rj, BLOCK), :], cos_ref[pl.ds(rj, BLOCK), :], sin_ref[pl.ds(rj, BLOCK), :]))
        vs.append(v_ref[pl.ds(rj, BLOCK), :])
    k_all = jnp.concatenate(ks + [kc_ref[...]], axis=0)
    v_all = jnp.concatenate(vs + [vc_ref[...]], axis=0)

    n_keys = 3 * BLOCK + PAST_LEN
    qi = lax.broadcasted_iota(jnp.int32, (Q_PER_KV * BLOCK, n_keys), 0) & (BLOCK - 1)
    c = lax.broadcasted_iota(jnp.int32, (Q_PER_KV * BLOCK, n_keys), 1)
    valid4 = (((c < BLOCK) & (c >= qi) & (n > 0))
              | ((c >= BLOCK) & (c < 2 * BLOCK))
              | ((c >= 2 * BLOCK) & (c <= 2 * BLOCK + qi) & (n < nb - 1))
              | (c >= 3 * BLOCK))

    for kh in range(N_KV_HEADS):
        q4 = _stack_q_heads(q, kh, BLOCK)
        kk = k_all[:, kh * HEAD_DIM:(kh + 1) * HEAD_DIM]
        vv = v_all[:, kh * HEAD_DIM:(kh + 1) * HEAD_DIM]
        s = jnp.where(valid4, _dot1(q4, kk, _NT) * ATTN_SCALE, NEG)
        o4 = _softmax_pv(s, _sink_col(sink_ref, l, kh, BLOCK), vv)
        for g in range(Q_PER_KV):
            h = kh * Q_PER_KV + g
            o_ref[:, h * HEAD_DIM:(h + 1) * HEAD_DIM] = o4[g * BLOCK:(g + 1) * BLOCK]


def _lat_attn_call(q, k, v, kc, vc, cos, sin, sink, l):
    nb = LAT_LEN // BLOCK
    q_off = N_CTX // BLOCK
    kv_off = N_CTX // LAT_LEN
    return pl.pallas_call(
        functools.partial(_lat_attn_kernel, l),
        grid=(N_LAT_SEQ, nb),
        in_specs=[
            pl.BlockSpec(memory_space=pltpu.SMEM),
            pl.BlockSpec((BLOCK, ATTN_W), lambda b, n: (q_off + b * nb + n, 0)),
            pl.BlockSpec((LAT_LEN, KV_W), lambda b, n: (kv_off + b, 0)),
            pl.BlockSpec((LAT_LEN, KV_W), lambda b, n: (kv_off + b, 0)),
            pl.BlockSpec((None, None, PAST_LEN, KV_W), lambda b, n: (b, l, 0, 0)),
            pl.BlockSpec((None, None, PAST_LEN, KV_W), lambda b, n: (b, l, 0, 0)),
            pl.BlockSpec((LAT_LEN, 128), lambda b, n: (0, 0)),
            pl.BlockSpec((LAT_LEN, 128), lambda b, n: (0, 0)),
        ],
        out_specs=pl.BlockSpec((BLOCK, ATTN_W), lambda b, n: (b * nb + n, 0)),
        out_shape=jax.ShapeDtypeStruct((N_LAT, ATTN_W), F32),
        compiler_params=_cparams(("parallel", "arbitrary")),
        name="latent_attention",
    )(sink, q, k, v, kc, vc, cos, sin)


def _rope_tables():
    t = np.arange(LAT_LEN)
    lane = np.arange(128)
    d = lane % HEAD_DIM
    pos = np.where(d[None, :] < HEAD_DIM // 2, (t // GRID_W)[:, None], (t % GRID_W)[:, None]).astype(np.float32)
    quarter = HEAD_DIM // 4
    inv = (np.float32(ROPE_THETA) ** (-(np.arange(quarter, dtype=np.float32)) / np.float32(quarter))).astype(np.float32)
    ang = pos * inv[d % quarter][None, :]
    sign = np.where((lane & 16) == 0, -1.0, 1.0)[None, :]
    return jnp.asarray(np.cos(ang), F32), jnp.asarray(np.sin(ang) * sign, F32)


def _head_block_mask(n):
    r = np.arange(n)[:, None] // HEAD_DIM
    c = np.arange(n)[None, :] // HEAD_DIM
    return (r == c).astype(np.float32)


def _rwkv_pre_kernel(u_ref, up_ref, un_ref, sw_ref, wup_ref, aup_ref, gup_ref, w0_ref, a0_ref,
                     kk_ref, ka_ref, rk_ref, ones_ref,
                     r_out, v_out, kn_out, g_out, bonus_out, lw_out, kd_out, b_out):
    i = pl.program_id(0)
    j = (i - CTX_TILES) % LAT_TILES_PER_SEQ
    has_prev = jnp.where((i >= CTX_TILES) & (j != 0), 1.0, 0.0)
    has_next = jnp.where((i >= CTX_TILES) & (j != LAT_TILES_PER_SEQ - 1), 1.0, 0.0)
    u = u_ref[...]
    row = lax.broadcasted_iota(jnp.int32, u.shape, 0)
    prev_row = up_ref[HALO - 1:HALO, :] * has_prev
    next_row = un_ref[0:1, :] * has_next
    u_dn = jnp.where(row == 0, prev_row, pltpu.roll(u, 1, 0))
    u_up = jnp.where(row == TM - 1, next_row, pltpu.roll(u, TM - 1, 0))
    us = u_dn * sw_ref[0:1, :] + u * sw_ref[1:2, :] + u_up * sw_ref[2:3, :]

    r = us[:, 0:RWKV_W]
    k = us[:, RWKV_W:2 * RWKV_W]
    v = us[:, 2 * RWKV_W:3 * RWKV_W]
    o = 3 * RWKV_W
    gd = us[:, o:o + GATE_RANK]
    wd = us[:, o + GATE_RANK:o + GATE_RANK + 2 * DECAY_RANK]
    ad = us[:, o + GATE_RANK + 2 * DECAY_RANK:RWKV_IN_W]

    w_pre = _dot1(jnp.tanh(wd), wup_ref[...])
    a_pre = _dot1(ad, aup_ref[...])
    g_out[...] = _dot1(jax.nn.sigmoid(gd), gup_ref[...])

    ones_bd = ones_ref[...]
    kn = k * kk_ref[...]
    kn = kn * lax.rsqrt(_dot_exact_rhs(kn * kn, ones_bd) + 1e-12)
    r_out[...] = r
    v_out[...] = v
    kn_out[...] = kn
    bonus_out[...] = _dot_exact_rhs(r * k * rk_ref[...], ones_bd) * v
    ka = ka_ref[...]
    for d in range(2):
        sl = slice(d * RWKV_W, (d + 1) * RWKV_W)
        a_d = jax.nn.sigmoid(a0_ref[d:d + 1, :] + a_pre[:, sl])
        lw_out[d] = -DECAY_SCALE * jax.nn.sigmoid(w0_ref[d:d + 1, :] + w_pre[:, sl])
        kd_out[d] = k * (1.0 + (a_d - 1.0) * ka)
        b_out[d] = kn * a_d


def _rwkv_pre_call(u, sw, wup_bd, aup_bd, gup, w0, a0, k_k, k_a, r_k, ones_bd, l):
    hb = TM // HALO
    last = N_TOK // HALO - 1
    row = lambda i: (l, 0, 0)
    tok = jax.ShapeDtypeStruct((N_TOK, RWKV_W), F32)
    tok2 = jax.ShapeDtypeStruct((2, N_TOK, RWKV_W), F32)
    return pl.pallas_call(
        _rwkv_pre_kernel,
        grid=(N_TILES,),
        in_specs=[
            pl.BlockSpec((TM, RWKV_IN_W), lambda i: (i, 0)),
            pl.BlockSpec((HALO, RWKV_IN_W), lambda i: (jnp.maximum(i * hb - 1, 0), 0)),
            pl.BlockSpec((HALO, RWKV_IN_W), lambda i: (jnp.minimum((i + 1) * hb, last), 0)),
            pl.BlockSpec((None, 3, RWKV_IN_W), row),
            pl.BlockSpec((None, 2 * DECAY_RANK, 2 * RWKV_W), row),
            pl.BlockSpec((None, 2 * ICLR_RANK, 2 * RWKV_W), row),
            pl.BlockSpec((None, GATE_RANK, RWKV_W), row),
            pl.BlockSpec((None, 2, RWKV_W), row),
            pl.BlockSpec((None, 2, RWKV_W), row),
            pl.BlockSpec((None, 1, RWKV_W), row),
            pl.BlockSpec((None, 1, RWKV_W), row),
            pl.BlockSpec((None, 1, RWKV_W), row),
            pl.BlockSpec((RWKV_W, RWKV_W), lambda i: (0, 0)),
        ],
        out_specs=[pl.BlockSpec((TM, RWKV_W), lambda i: (i, 0))] * 5
        + [pl.BlockSpec((2, TM, RWKV_W), lambda i: (0, i, 0))] * 3,
        out_shape=[tok] * 5 + [tok2] * 3,
        compiler_params=_cparams(("parallel",)),
        name="rwkv_token_terms",
    )(u, u, u, sw, wup_bd, aup_bd, gup, w0, a0, k_k, k_a, r_k, ones_bd)


def _stack_heads(x, bmask):
    return jnp.concatenate([x.astype(BF16)] * RWKV_HEADS, axis=0) * bmask


_TN = (((0,), (0,)), ((), ()))


def _scan_kernel(reverse, r_ref, v_ref, kn_ref, lw_ref, kd_ref, b_ref, s0_ref,
                 cs_ref, tot_ref, strict_ref, incl_ref, eye_ref, bmask_ref,
                 y_ref, sfin_ref, s_scr):
    i = pl.program_id(0)
    t = (N_TILES - 1 - i) if reverse else i
    jt = (t - CTX_TILES) % LAT_TILES_PER_SEQ
    first = (t < CTX_TILES) | (jt == (LAT_TILES_PER_SEQ - 1 if reverse else 0))
    bmask_f = bmask_ref[...]
    bmask = bmask_f.astype(BF16)

    @pl.when(first)
    def _():
        s_scr[...] = jnp.concatenate([s0_ref[...]] * RWKV_HEADS, axis=1) * bmask_f

    lw = lw_ref[...]
    linc = _dot_exact_lhs(cs_ref[...], lw)
    ltot = _dot_exact_lhs(tot_ref[...], lw)
    kn = kn_ref[...]
    kd = kd_ref[...]
    b = b_ref[...]
    v_all = v_ref[...].astype(BF16)
    r_t = r_ref[...] * jnp.exp(linc)
    a_t = (-kn * jnp.exp(linc - lw)).astype(BF16)
    e_neg = jnp.exp(-linc)
    b_t = b * e_neg
    k_t = kd * e_neg
    e_rem = jnp.exp(ltot - linc)
    b_p = (b * e_rem).astype(BF16)
    k_p = (kd * e_rem).astype(BF16)
    dec = jnp.exp(ltot)

    strict = strict_ref[...]
    incl = incl_ref[...]
    eye = eye_ref[...]
    order = range(CHUNKS_PER_TILE - 1, -1, -1) if reverse else range(CHUNKS_PER_TILE)
    chunks = range(CHUNKS_PER_TILE)
    sls = [slice(c * CHUNK, (c + 1) * CHUNK) for c in chunks]
    stk = lambda x: _stack_heads(x, bmask)
    ar = [jnp.concatenate([a_t[sl], r_t[sl].astype(BF16)], axis=0) for sl in sls]
    m_b = [_dg(ar[c], stk(b_t[sls[c]]), _NT) for c in chunks]
    m_k = [_dg(ar[c], stk(k_t[sls[c]]), _NT) for c in chunks]
    a_ab = [m_b[c][:CHUNK] * strict for c in chunks]
    p_rb = [(m_b[c][CHUNK:] * incl).astype(BF16) for c in chunks]
    ak_rk = [jnp.concatenate([m_k[c][:CHUNK] * strict, m_k[c][CHUNK:] * incl], axis=0).astype(BF16)
             for c in chunks]

    tinv = [eye + a_ab[c] for c in chunks]
    p = [a_ab[c].astype(BF16) for c in chunks]
    p = [_dg(p[c], stk(p[c]), _NN).astype(BF16) for c in chunks]
    for _ in range(4):
        pt = [_dg(jnp.concatenate([p[c], tinv[c].astype(BF16)], axis=0), stk(p[c]), _NN) for c in chunks]
        p = [pt[c][:CHUNK].astype(BF16) for c in chunks]
        tinv = [tinv[c] + pt[c][CHUNK:] for c in chunks]
    tinv = [(tinv[c] + _dg(tinv[c].astype(BF16), stk(p[c]), _NN)).astype(BF16) for c in chunks]

    m_v = [_dg(ak_rk[c], stk(v_all[sls[c]]), _NN) for c in chunks]
    w = [_dg(tinv[c], stk(a_t[sls[c]]), _NN) for c in chunks]
    u0 = [_dg(tinv[c], stk(m_v[c][:CHUNK]), _NN) for c in chunks]
    qp = [r_t[sls[c]] + _dg(p_rb[c], stk(w[c]), _NN) for c in chunks]
    y0 = [_dg(p_rb[c], stk(u0[c]), _NN) + m_v[c][CHUNK:] for c in chunks]
    wq = [jnp.concatenate([w[c].astype(BF16), qp[c].astype(BF16)], axis=0) for c in chunks]
    bk = [jnp.concatenate([b_p[sl], k_p[sl]], axis=0) for sl in sls]

    s = s_scr[...]
    for c in order:
        sl = sls[c]
        m_s = _dg(wq[c], s.astype(BF16), _NT)
        u = m_s[:CHUNK] + u0[c]
        y_ref[sl, :] = m_s[CHUNK:] + y0[c]
        uv = jnp.concatenate([u.astype(BF16), v_all[sl]], axis=0)
        s = s * dec[c * CHUNK:c * CHUNK + 1, :] + _dg(uv, bk[c], _TN) * bmask_f
    s_scr[...] = s
    sfin_ref[...] = s


def _scan_call(reverse, r, v, kn, lw, kd, b, s0, consts, l):
    d = 1 if reverse else 0
    cs, tot, strict, incl, eye, bmask = consts
    tile = (lambda i: (N_TILES - 1 - i, 0)) if reverse else (lambda i: (i, 0))
    tile_d = (lambda i: (d, N_TILES - 1 - i, 0)) if reverse else (lambda i: (d, i, 0))

    def seq_of(i):
        t = (N_TILES - 1 - i) if reverse else i
        return jnp.where(t < CTX_TILES, t, CTX_TILES + (t - CTX_TILES) // LAT_TILES_PER_SEQ)

    def sfin_of(i):
        t = (N_TILES - 1 - i) if reverse else i
        return (jnp.minimum(t, CTX_TILES), 0, 0)

    const2 = lambda i: (0, 0)
    tok = pl.BlockSpec((TM, RWKV_W), tile)
    tok_d = pl.BlockSpec((None, TM, RWKV_W), tile_d)
    return pl.pallas_call(
        functools.partial(_scan_kernel, reverse),
        grid=(N_TILES,),
        in_specs=[
            tok, tok, tok, tok_d, tok_d, tok_d,
            pl.BlockSpec((None, None, RWKV_W, HEAD_DIM), lambda i: (seq_of(i), d, 0, 0)),
            pl.BlockSpec((None, TM, TM), lambda i: (d, 0, 0)),
            pl.BlockSpec((TM, TM), const2),
            pl.BlockSpec((None, CHUNK, RWKV_W), lambda i: (d, 0, 0)),
            pl.BlockSpec((None, CHUNK, RWKV_W), lambda i: (d, 0, 0)),
            pl.BlockSpec((CHUNK, RWKV_W), const2),
            pl.BlockSpec((RWKV_W, RWKV_W), const2),
        ],
        out_specs=[
            tok,
            pl.BlockSpec((None, RWKV_W, RWKV_W), sfin_of),
        ],
        out_shape=[
            jax.ShapeDtypeStruct((N_TOK, RWKV_W), F32),
            jax.ShapeDtypeStruct((CTX_TILES + 1, RWKV_W, RWKV_W), F32),
        ],
        scratch_shapes=[pltpu.VMEM((RWKV_W, RWKV_W), F32)],
        compiler_params=_cparams(("arbitrary",)),
        name="rwkv_scan_bwd" if reverse else "rwkv_scan_fwd",
    )(r, v, kn, lw, kd, b, s0, cs, tot, strict, incl, eye, bmask)


def _scan_consts():
    t = np.arange(TM)
    same_chunk = (t[:, None] // CHUNK) == (t[None, :] // CHUNK)
    cs_f = same_chunk & (t[None, :] <= t[:, None])
    cs_b = same_chunk & (t[None, :] >= t[:, None])
    cs = np.stack([cs_f, cs_b]).astype(np.float32)
    tot = same_chunk.astype(np.float32)
    q = np.arange(CHUNK)[:, None]
    kcol = np.arange(RWKV_W)[None, :] % CHUNK
    strict = np.stack([kcol < q, kcol > q]).astype(np.float32)
    incl = np.stack([kcol <= q, kcol >= q]).astype(np.float32)
    eye = (kcol == q).astype(np.float32)
    return (jnp.asarray(cs, BF16), jnp.asarray(tot, BF16), jnp.asarray(strict), jnp.asarray(incl),
            jnp.asarray(eye), jnp.asarray(_head_block_mask(RWKV_W)))


def _fnet_kernel(seq_len, z_ref, ch_ref, cl_ref, th_ref, tl_ref, o_ref, zh_scr, zl_scr):
    @pl.when(pl.program_id(1) == 0)
    def _():
        zh, zl = _split2(z_ref[...])
        for half in range(2):
            ch = ch_ref[half]
            cl = cl_ref[half]
            zc = _dg(zh, ch, _NN) + (_dg(zh, cl, _NN) + _dg(zl, ch, _NN))
            h, lo = _split2(zc)
            zh_scr[half * seq_len:(half + 1) * seq_len, :] = h
            zl_scr[half * seq_len:(half + 1) * seq_len, :] = lo

    th = th_ref[...]
    o_ref[...] = _dg(th, zh_scr[...], _NN) + (_dg(th, zl_scr[...], _NN) + _dg(tl_ref[...], zh_scr[...], _NN))


def _fnet_call(u_f, seq_len, n_seq, row_off, tables, name):
    ch, cl, th, tl = tables
    tt = 256
    nt = seq_len // tt
    seq_off = row_off // seq_len
    return pl.pallas_call(
        functools.partial(_fnet_kernel, seq_len),
        grid=(n_seq, nt),
        in_specs=[
            pl.BlockSpec((seq_len, FNET_W), lambda b, j: (seq_off + b, 0)),
            pl.BlockSpec((2, FNET_W, FNET_W), lambda b, j: (0, 0, 0)),
            pl.BlockSpec((2, FNET_W, FNET_W), lambda b, j: (0, 0, 0)),
            pl.BlockSpec((tt, 2 * seq_len), lambda b, j: (j, 0)),
            pl.BlockSpec((tt, 2 * seq_len), lambda b, j: (j, 0)),
        ],
        out_specs=pl.BlockSpec((tt, FNET_W), lambda b, j: (b * nt + j, 0)),
        out_shape=jax.ShapeDtypeStruct((n_seq * seq_len, FNET_W), F32),
        scratch_shapes=[pltpu.VMEM((2 * seq_len, FNET_W), BF16), pltpu.VMEM((2 * seq_len, FNET_W), BF16)],
        compiler_params=_cparams(("parallel", "arbitrary")),
        name=name,
    )(u_f, ch, cl, th, tl)


def _dft_tables(seq_len):
    c = np.arange(FNET_W)
    ang_c = 2.0 * np.pi * ((c[:, None] % HEAD_DIM) * (c[None, :] % HEAD_DIM) % HEAD_DIM) / HEAD_DIM
    scale = 1.0 / np.sqrt(float(seq_len) * HEAD_DIM)
    blk = _head_block_mask(FNET_W)
    chan = jnp.asarray(np.stack([np.cos(ang_c) * blk * scale, np.sin(ang_c) * blk * scale]), F32)
    t = jnp.arange(seq_len, dtype=jnp.int32)
    ang_t = ((t[:, None] * t[None, :]) % seq_len).astype(F32) * F32(2.0 * np.pi / seq_len)
    pos = jnp.concatenate([jnp.cos(ang_t), -jnp.sin(ang_t)], axis=1)
    ch, cl = _split2(chan)
    th, tl = _split2(pos)
    return ch, cl, th, tl


def _merge_ffn_kernel(x_ref, mod_ref, oa_ref, yf_ref, yb_ref, bonus_ref, g_ref, of_ref,
                      lng_ref, lnb_ref, avg_ref, nfg_ref, wo_ref, wi_ref, w2_ref, o_ref):
    avg = avg_ref[...]
    y = yf_ref[...] + yb_ref[...]
    yc = y - _dot_exact_rhs(y, avg)
    var = _dot_exact_rhs(yc * yc, avg)
    yn = yc * lax.rsqrt(var + GN_EPS) * lng_ref[...] + lnb_ref[...]
    o_rwkv = (yn + bonus_ref[...]) * g_ref[...]

    o = (_dg(oa_ref[...].astype(BF16), wo_ref[0:ATTN_W, :], _NN)
         + _dg(o_rwkv.astype(BF16), wo_ref[ATTN_W:ATTN_W + RWKV_W, :], _NN)
         + _dg(of_ref[...].astype(BF16), wo_ref[ATTN_W + RWKV_W:, :], _NN))
    g1 = mod_ref[:, 2 * D_MODEL:3 * D_MODEL]
    sh2 = mod_ref[:, 3 * D_MODEL:4 * D_MODEL]
    sc2 = mod_ref[:, 4 * D_MODEL:5 * D_MODEL]
    g2 = mod_ref[:, 5 * D_MODEL:6 * D_MODEL]
    x = x_ref[...] + g1 * o
    h = _rms(x, nfg_ref[...]) * (1.0 + sc2) + sh2
    gu = _dg(h.astype(BF16), wi_ref[...], _NN)
    gt = gu[:, 0:D_FF]
    act = gt * jax.nn.sigmoid(gt) * gu[:, D_FF:]
    o_ref[...] = x + g2 * _dg(act.astype(BF16), w2_ref[...], _NN)


def _merge_ffn_call(x, mods_l, o_attn, y_f, y_b, bonus, g, o_fnet, ln_g, ln_b, avg, nfg, wo, wi, w2, l):
    row = lambda i: (l, 0, 0)
    tile = lambda i: (i, 0)
    once = pl.Buffered(1)
    return pl.pallas_call(
        _merge_ffn_kernel,
        grid=(N_TILES,),
        in_specs=[
            pl.BlockSpec((TM, D_MODEL), tile),
            pl.BlockSpec((None, 1, 6 * D_MODEL), lambda i: (_mod_row(i), 0, 0)),
            pl.BlockSpec((TM, ATTN_W), tile),
            pl.BlockSpec((TM, RWKV_W), tile),
            pl.BlockSpec((TM, RWKV_W), tile),
            pl.BlockSpec((TM, RWKV_W), tile),
            pl.BlockSpec((TM, RWKV_W), tile),
            pl.BlockSpec((TM, FNET_W), tile),
            pl.BlockSpec((None, 1, RWKV_W), row),
            pl.BlockSpec((None, 1, RWKV_W), row),
            pl.BlockSpec((RWKV_W, RWKV_W), lambda i: (0, 0)),
            pl.BlockSpec((None, 1, D_MODEL), row),
            pl.BlockSpec((None, D_MODEL, D_MODEL), row, pipeline_mode=once),
            pl.BlockSpec((None, D_MODEL, 2 * D_FF), row, pipeline_mode=once),
            pl.BlockSpec((None, D_FF, D_MODEL), row, pipeline_mode=once),
        ],
        out_specs=pl.BlockSpec((TM, D_MODEL), tile),
        out_shape=jax.ShapeDtypeStruct((N_TOK, D_MODEL), F32),
        compiler_params=_cparams(("parallel",)),
        name="merge_ffn",
    )(x, mods_l, o_attn, y_f, y_b, bonus, g, o_fnet, ln_g, ln_b, avg, nfg, wo, wi, w2)


def _final_norm_kernel(x_ref, g_ref, o_ref):
    o_ref[...] = _rms(x_ref[...], g_ref[...])


def _final_norm_call(x, g):
    return pl.pallas_call(
        _final_norm_kernel,
        grid=(N_TILES,),
        in_specs=[pl.BlockSpec((TM, D_MODEL), lambda i: (i, 0)), pl.BlockSpec((1, D_MODEL), lambda i: (0, 0))],
        out_specs=pl.BlockSpec((TM, D_MODEL), lambda i: (i, 0)),
        out_shape=jax.ShapeDtypeStruct((N_TOK, D_MODEL), F32),
        compiler_params=_cparams(("parallel",)),
        name="final_norm",
    )(x, g)


def _block_diag2(w):
    z = jnp.zeros_like(w[:, 0])
    return jnp.concatenate([jnp.concatenate([w[:, 0], z], axis=2), jnp.concatenate([z, w[:, 1]], axis=2)], axis=1)


def kernel(x_prompt, x_sample, cache_k, cache_v, state_rwkv, c, c_ctx, w_ada, b_ada, norm_mix_g, norm_ffn_g,
           w_in, w_out, attn_sink, rwkv_shift_w, rwkv_w0, rwkv_w_up, rwkv_a0, rwkv_a_up, rwkv_g_up,
           rwkv_k_k, rwkv_k_a, rwkv_r_k, rwkv_ln_g, rwkv_ln_b, ffn_w_in, ffn_w_out, norm_final_g):
    x = jnp.concatenate([x_prompt.reshape(N_CTX, D_MODEL), x_sample.reshape(N_LAT, D_MODEL)], axis=0)
    cvec = jnp.concatenate([c_ctx[None, :], c, jnp.zeros((8 - 1 - N_LAT_SEQ, D_MODEL), F32)], axis=0)
    mods = _mods_call(cvec, w_ada, b_ada).reshape(DEPTH, 8, 1, 6 * D_MODEL)

    w_in_b = w_in.astype(BF16)
    w_out_b = w_out.astype(BF16)
    ffn_in_b = ffn_w_in.astype(BF16)
    ffn_out_b = ffn_w_out.astype(BF16)
    wup_bd = _block_diag2(rwkv_w_up)
    aup_bd = _block_diag2(rwkv_a_up)
    row3 = lambda a: a.reshape(DEPTH, 1, -1)
    nmg, nfg = row3(norm_mix_g), row3(norm_ffn_g)
    k_k, k_a, r_k = row3(rwkv_k_k), row3(rwkv_k_a), row3(rwkv_r_k)
    ln_g, ln_b = row3(rwkv_ln_g), row3(rwkv_ln_b)
    kc = cache_k.reshape(N_LAT_SEQ, DEPTH, PAST_LEN, KV_W)
    vc = cache_v.reshape(N_LAT_SEQ, DEPTH, PAST_LEN, KV_W)
    s0_lat = state_rwkv.reshape(N_LAT_SEQ, DEPTH, 2, RWKV_W, HEAD_DIM)
    s0_ctx = jnp.zeros((N_CTX_SEQ, 2, RWKV_W, HEAD_DIM), F32)

    cos, sin = _rope_tables()
    consts = _scan_consts()
    bmask = consts[-1]
    ones_bd = bmask.astype(BF16)
    avg_bd = (bmask * (1.0 / HEAD_DIM)).astype(BF16)
    dft_ctx = _dft_tables(CTX_LEN)
    dft_lat = _dft_tables(LAT_LEN)

    new_k, new_v, new_s = [], [], []
    for l in range(DEPTH):
        q, k, v, u_r, u_f = _inproj_call(x, mods[l], nmg, w_in_b, l)
        new_k.append(k[:N_CTX].reshape(N_CTX_SEQ, CTX_LEN, N_KV_HEADS, HEAD_DIM))
        new_v.append(v[:N_CTX].reshape(N_CTX_SEQ, CTX_LEN, N_KV_HEADS, HEAD_DIM))

        o_attn = jnp.concatenate([
            _ctx_attn_call(q, k, v, attn_sink, l),
            _lat_attn_call(q, k, v, kc, vc, cos, sin, attn_sink, l),
        ], axis=0)

        r, vv, kn, g, bonus, lw, kd, b = _rwkv_pre_call(
            u_r, rwkv_shift_w, wup_bd, aup_bd, rwkv_g_up, rwkv_w0, rwkv_a0, k_k, k_a, r_k, ones_bd, l)
        s0 = jnp.concatenate([s0_ctx, s0_lat[:, l]], axis=0)
        y_f, sf = _scan_call(False, r, vv, kn, lw, kd, b, s0, consts, l)
        y_b, sb = _scan_call(True, r, vv, kn, lw, kd, b, s0, consts, l)
        fin = jnp.stack([sf[:N_CTX_SEQ], sb[:N_CTX_SEQ]], axis=1)
        fin = fin.reshape(N_CTX_SEQ, 2, RWKV_HEADS, HEAD_DIM, RWKV_HEADS, HEAD_DIM)
        new_s.append(jnp.stack([fin[:, :, h, :, h, :] for h in range(RWKV_HEADS)], axis=2))

        o_fnet = jnp.concatenate([
            _fnet_call(u_f, CTX_LEN, N_CTX_SEQ, 0, dft_ctx, "fnet_context"),
            _fnet_call(u_f, LAT_LEN, N_LAT_SEQ, N_CTX, dft_lat, "fnet_latent"),
        ], axis=0)

        x = _merge_ffn_call(x, mods[l], o_attn, y_f, y_b, bonus, g, o_fnet, ln_g, ln_b, avg_bd, nfg,
                            w_out_b, ffn_in_b, ffn_out_b, l)

    y = _final_norm_call(x, norm_final_g.reshape(1, D_MODEL))
    return (y[:N_CTX].reshape(N_CTX_SEQ, CTX_LEN, D_MODEL),
            y[N_CTX:].reshape(N_LAT_SEQ, LAT_LEN, D_MODEL),
            jnp.stack(new_k, axis=1), jnp.stack(new_v, axis=1), jnp.stack(new_s, axis=1))
```

```python
import functools

import numpy as np
import jax
import jax.numpy as jnp
from jax import lax
from jax.experimental import pallas as pl
from jax.experimental.pallas import tpu as pltpu

F32 = jnp.float32
BF16 = jnp.bfloat16

D_MODEL = 1024
N_CTX_SEQ = 16
CTX_LEN = 256
DEPTH = 4
N_LAT_SEQ = 4
LAT_LEN = 2048
PAST_LEN = 512
GRID_W = 64
HEAD_DIM = 64
N_Q_HEADS = 8
N_KV_HEADS = 2
Q_PER_KV = N_Q_HEADS // N_KV_HEADS
ATTN_W = N_Q_HEADS * HEAD_DIM
KV_W = N_KV_HEADS * HEAD_DIM
BLOCK = 128
ATTN_SCALE = 0.125
ROPE_THETA = 10000.0
NEG = -1e30
RWKV_HEADS = 4
RWKV_W = RWKV_HEADS * HEAD_DIM
DECAY_RANK = 64
ICLR_RANK = 64
GATE_RANK = 128
DECAY_SCALE = 0.6065306597126334
GN_EPS = 64e-5
RWKV_IN_W = 3 * RWKV_W + GATE_RANK + 2 * DECAY_RANK + 2 * ICLR_RANK
FNET_W = 4 * HEAD_DIM
IN_W = ATTN_W + 2 * KV_W + RWKV_IN_W + FNET_W
D_FF = 2816
RMS_EPS = 1e-6

N_CTX = N_CTX_SEQ * CTX_LEN
N_LAT = N_LAT_SEQ * LAT_LEN
N_TOK = N_CTX + N_LAT
TM = 256
N_TILES = N_TOK // TM
CTX_TILES = N_CTX // TM
LAT_TILES_PER_SEQ = LAT_LEN // TM
N_SEQ = N_CTX_SEQ + N_LAT_SEQ
CHUNK = 64
CHUNKS_PER_TILE = TM // CHUNK
HALO = 8

VMEM_LIMIT = 56 * 1024 * 1024


def _cparams(sem):
    return pltpu.CompilerParams(dimension_semantics=sem, vmem_limit_bytes=VMEM_LIMIT)


def _mod_row(i):
    return jnp.where(i < CTX_TILES, 0, 1 + (i - CTX_TILES) // LAT_TILES_PER_SEQ)


def _split2(x):
    hi = x.astype(BF16)
    lo = (x - hi.astype(F32)).astype(BF16)
    return hi, lo


_NN = (((1,), (0,)), ((), ()))
_NT = (((1,), (1,)), ((), ()))


def _dg(a, b, dims):
    return lax.dot_general(a, b, dims, preferred_element_type=F32)


def _dot1(a, b, dims=_NN):
    return _dg(a.astype(BF16), b.astype(BF16), dims)


def _dot3(a, b, dims=_NN):
    ah, al = _split2(a)
    bh, bl = _split2(b)
    return _dg(ah, bh, dims) + (_dg(ah, bl, dims) + _dg(al, bh, dims))


def _dot_exact_rhs(a, b_bf16):
    ah, al = _split2(a)
    return _dg(ah, b_bf16, _NN) + _dg(al, b_bf16, _NN)


def _dot_exact_lhs(a_bf16, b):
    bh, bl = _split2(b)
    return _dg(a_bf16, bh, _NN) + _dg(a_bf16, bl, _NN)


def _mods_kernel(c_ref, w_ref, b_ref, o_ref):
    c = c_ref[...]
    s = c * jax.nn.sigmoid(c)
    o_ref[...] = _dot1(s, w_ref[...]) + b_ref[...]


def _mods_call(cvec, w_ada, b_ada):
    tn = 1536
    return pl.pallas_call(
        _mods_kernel,
        grid=(DEPTH, 6 * D_MODEL // tn),
        in_specs=[
            pl.BlockSpec((8, D_MODEL), lambda l, j: (0, 0)),
            pl.BlockSpec((None, D_MODEL, tn), lambda l, j: (l, 0, j)),
            pl.BlockSpec((None, 1, tn), lambda l, j: (l, 0, j)),
        ],
        out_specs=pl.BlockSpec((None, 8, tn), lambda l, j: (l, 0, j)),
        out_shape=jax.ShapeDtypeStruct((DEPTH, 8, 6 * D_MODEL), F32),
        compiler_params=_cparams(("parallel", "parallel")),
        name="adaln_mods",
    )(cvec, w_ada, b_ada.reshape(DEPTH, 1, 6 * D_MODEL))


def _rms(x, g):
    return x * lax.rsqrt(jnp.mean(x * x, axis=-1, keepdims=True) + RMS_EPS) * g


def _inproj_kernel(x_ref, mod_ref, g_ref, w_ref, q_ref, k_ref, v_ref, ur_ref, uf_ref):
    shift = mod_ref[:, 0:D_MODEL]
    scale = mod_ref[:, D_MODEL:2 * D_MODEL]
    h = _rms(x_ref[...], g_ref[...]) * (1.0 + scale) + shift
    o = _dg(h.astype(BF16), w_ref[...], _NN)
    q_ref[...] = o[:, 0:ATTN_W]
    k_ref[...] = o[:, ATTN_W:ATTN_W + KV_W]
    v_ref[...] = o[:, ATTN_W + KV_W:ATTN_W + 2 * KV_W]
    ur_ref[...] = o[:, ATTN_W + 2 * KV_W:ATTN_W + 2 * KV_W + RWKV_IN_W]
    uf_ref[...] = o[:, ATTN_W + 2 * KV_W + RWKV_IN_W:IN_W]


def _inproj_call(x, mods_l, g, w_bf16, l):
    widths = (ATTN_W, KV_W, KV_W, RWKV_IN_W, FNET_W)
    return pl.pallas_call(
        _inproj_kernel,
        grid=(N_TILES,),
        in_specs=[
            pl.BlockSpec((TM, D_MODEL), lambda i: (i, 0)),
            pl.BlockSpec((None, 1, 6 * D_MODEL), lambda i: (_mod_row(i), 0, 0)),
            pl.BlockSpec((None, 1, D_MODEL), lambda i: (l, 0, 0)),
            pl.BlockSpec((None, D_MODEL, IN_W), lambda i: (l, 0, 0)),
        ],
        out_specs=[pl.BlockSpec((TM, w), lambda i: (i, 0)) for w in widths],
        out_shape=[jax.ShapeDtypeStruct((N_TOK, w), F32) for w in widths],
        compiler_params=_cparams(("parallel",)),
        name="in_projection",
    )(x, mods_l, g, w_bf16)


def _softmax_pv(s, sink_col, v):
    m = jnp.maximum(jnp.max(s, axis=-1, keepdims=True), sink_col)
    e = jnp.exp(s - m)
    denom = jnp.sum(e, axis=-1, keepdims=True) + jnp.exp(sink_col - m)
    return _dot1(e, v) / denom


def _stack_q_heads(q, kh, rows):
    return jnp.concatenate(
        [q[:, (kh * Q_PER_KV + g) * HEAD_DIM:(kh * Q_PER_KV + g + 1) * HEAD_DIM] for g in range(Q_PER_KV)],
        axis=0)


def _sink_col(sink_ref, l, kh, rows):
    return jnp.concatenate(
        [jnp.full((rows, 1), sink_ref[l, kh * Q_PER_KV + g], F32) for g in range(Q_PER_KV)], axis=0)


def _ctx_attn_kernel(l, sink_ref, q_ref, k_ref, v_ref, o_ref):
    q = q_ref[...]
    k = k_ref[...]
    v = v_ref[...]
    for kh in range(N_KV_HEADS):
        q4 = _stack_q_heads(q, kh, CTX_LEN)
        kk = k[:, kh * HEAD_DIM:(kh + 1) * HEAD_DIM]
        vv = v[:, kh * HEAD_DIM:(kh + 1) * HEAD_DIM]
        s = _dot1(q4 * ATTN_SCALE, kk, _NT)
        o4 = _softmax_pv(s, _sink_col(sink_ref, l, kh, CTX_LEN), vv)
        for g in range(Q_PER_KV):
            h = kh * Q_PER_KV + g
            o_ref[:, h * HEAD_DIM:(h + 1) * HEAD_DIM] = o4[g * CTX_LEN:(g + 1) * CTX_LEN]


def _ctx_attn_call(q, k, v, sink, l):
    return pl.pallas_call(
        functools.partial(_ctx_attn_kernel, l),
        grid=(N_CTX_SEQ,),
        in_specs=[
            pl.BlockSpec(memory_space=pltpu.SMEM),
            pl.BlockSpec((CTX_LEN, ATTN_W), lambda b: (b, 0)),
            pl.BlockSpec((CTX_LEN, KV_W), lambda b: (b, 0)),
            pl.BlockSpec((CTX_LEN, KV_W), lambda b: (b, 0)),
        ],
        out_specs=pl.BlockSpec((CTX_LEN, ATTN_W), lambda b: (b, 0)),
        out_shape=jax.ShapeDtypeStruct((N_CTX, ATTN_W), F32),
        compiler_params=_cparams(("parallel",)),
        name="context_attention",
    )(sink, q, k, v)


def _rope(x, cos, sin):
    lane = lax.broadcasted_iota(jnp.int32, x.shape, 1)
    partner = jnp.where((lane & 16) == 0, pltpu.roll(x, 128 - 16, 1), pltpu.roll(x, 16, 1))
    return x * cos + partner * sin


def _lat_attn_kernel(l, sink_ref, q_ref, k_ref, v_ref, kc_ref, vc_ref, cos_ref, sin_ref, o_ref):
    n = pl.program_id(1)
    nb = LAT_LEN // BLOCK
    r0 = pl.multiple_of(n * BLOCK, BLOCK)
    cos_q = cos_ref[pl.ds(r0, BLOCK), :]
    sin_q = sin_ref[pl.ds(r0, BLOCK), :]
    q = jnp.concatenate(
        [_rope(q_ref[:, j * 128:(j + 1) * 128], cos_q, sin_q) for j in range(ATTN_W // 128)], axis=1)

    ks, vs = [], []
    for dj in (-1, 0, 1):
        rj = pl.multiple_of(jnp.clip(n + dj, 0, nb - 1) * BLOCK, BLOCK)
        ks.append(_rope(k_ref[pl.ds(rj, BLOCK), :], cos_ref[pl.ds(rj, BLOCK), :], sin_ref[pl.ds(rj, BLOCK), :]))
        vs.append(v_ref[pl.ds(rj, BLOCK), :])
    k_all = jnp.concatenate(ks + [kc_ref[...]], axis=0)
    v_all = jnp.concatenate(vs + [vc_ref[...]], axis=0)

    qi = lax.broadcasted_iota(jnp.int32, (Q_PER_KV * BLOCK, BLOCK), 0) & (BLOCK - 1)
    kj = lax.broadcasted_iota(jnp.int32, (Q_PER_KV * BLOCK, BLOCK), 1)
    keep_prev = (kj >= qi) & (n > 0)
    keep_next = (kj <= qi) & (n < nb - 1)

    for kh in range(N_KV_HEADS):
        q4 = _stack_q_heads(q, kh, BLOCK) * ATTN_SCALE
        kk = k_all[:, kh * HEAD_DIM:(kh + 1) * HEAD_DIM]
        vv = v_all[:, kh * HEAD_DIM:(kh + 1) * HEAD_DIM]
        s = _dot1(q4, kk, _NT)
        s = jnp.concatenate([jnp.where(keep_prev, s[:, 0:BLOCK], NEG), s[:, BLOCK:2 * BLOCK],
                             jnp.where(keep_next, s[:, 2 * BLOCK:3 * BLOCK], NEG), s[:, 3 * BLOCK:]], axis=1)
        o4 = _softmax_pv(s, _sink_col(sink_ref, l, kh, BLOCK), vv)
        for g in range(Q_PER_KV):
            h = kh * Q_PER_KV + g
            o_ref[:, h * HEAD_DIM:(h + 1) * HEAD_DIM] = o4[g * BLOCK:(g + 1) * BLOCK]


def _lat_attn_call(q, k, v, kc, vc, cos, sin, sink, l):
    nb = LAT_LEN // BLOCK
    q_off = N_CTX // BLOCK
    kv_off = N_CTX // LAT_LEN
    return pl.pallas_call(
        functools.partial(_lat_attn_kernel, l),
        grid=(N_LAT_SEQ, nb),
        in_specs=[
            pl.BlockSpec(memory_space=pltpu.SMEM),
            pl.BlockSpec((BLOCK, ATTN_W), lambda b, n: (q_off + b * nb + n, 0)),
            pl.BlockSpec((LAT_LEN, KV_W), lambda b, n: (kv_off + b, 0)),
            pl.BlockSpec((LAT_LEN, KV_W), lambda b, n: (kv_off + b, 0)),
            pl.BlockSpec((None, None, PAST_LEN, KV_W), lambda b, n: (b, l, 0, 0)),
            pl.BlockSpec((None, None, PAST_LEN, KV_W), lambda b, n: (b, l, 0, 0)),
            pl.BlockSpec((LAT_LEN, 128), lambda b, n: (0, 0)),
            pl.BlockSpec((LAT_LEN, 128), lambda b, n: (0, 0)),
        ],
        out_specs=pl.BlockSpec((BLOCK, ATTN_W), lambda b, n: (b * nb + n, 0)),
        out_shape=jax.ShapeDtypeStruct((N_LAT, ATTN_W), F32),
        compiler_params=_cparams(("parallel", "arbitrary")),
        name="latent_attention",
    )(sink, q, k, v, kc, vc, cos, sin)


def _rope_tables():
    t = np.arange(LAT_LEN)
    lane = np.arange(128)
    d = lane % HEAD_DIM
    pos = np.where(d[None, :] < HEAD_DIM // 2, (t // GRID_W)[:, None], (t % GRID_W)[:, None]).astype(np.float32)
    quarter = HEAD_DIM // 4
    inv = (np.float32(ROPE_THETA) ** (-(np.arange(quarter, dtype=np.float32)) / np.float32(quarter))).astype(np.float32)
    ang = pos * inv[d % quarter][None, :]
    sign = np.where((lane & 16) == 0, -1.0, 1.0)[None, :]
    return jnp.asarray(np.cos(ang), F32), jnp.asarray(np.sin(ang) * sign, F32)


def _head_block_mask(n):
    r = np.arange(n)[:, None] // HEAD_DIM
    c = np.arange(n)[None, :] // HEAD_DIM
    return (r == c).astype(np.float32)


def _rwkv_pre_kernel(u_ref, up_ref, un_ref, sw_ref, wup_ref, aup_ref, gup_ref, w0_ref, a0_ref,
                     kk_ref, ka_ref, rk_ref, ones_ref,
                     r_out, v_out, kn_out, g_out, bonus_out, lw_out, kd_out, b_out):
    i = pl.program_id(0)
    j = (i - CTX_TILES) % LAT_TILES_PER_SEQ
    has_prev = jnp.where((i >= CTX_TILES) & (j != 0), 1.0, 0.0)
    has_next = jnp.where((i >= CTX_TILES) & (j != LAT_TILES_PER_SEQ - 1), 1.0, 0.0)
    u = u_ref[...]
    row = lax.broadcasted_iota(jnp.int32, u.shape, 0)
    prev_row = up_ref[HALO - 1:HALO, :] * has_prev
    next_row = un_ref[0:1, :] * has_next
    u_dn = jnp.where(row == 0, prev_row, pltpu.roll(u, 1, 0))
    u_up = jnp.where(row == TM - 1, next_row, pltpu.roll(u, TM - 1, 0))
    us = u_dn * sw_ref[0:1, :] + u * sw_ref[1:2, :] + u_up * sw_ref[2:3, :]

    r = us[:, 0:RWKV_W]
    k = us[:, RWKV_W:2 * RWKV_W]
    v = us[:, 2 * RWKV_W:3 * RWKV_W]
    o = 3 * RWKV_W
    gd = us[:, o:o + GATE_RANK]
    wd = us[:, o + GATE_RANK:o + GATE_RANK + 2 * DECAY_RANK]
    ad = us[:, o + GATE_RANK + 2 * DECAY_RANK:RWKV_IN_W]

    w_pre = _dot1(jnp.tanh(wd), wup_ref[...])
    a_pre = _dot1(ad, aup_ref[...])
    g_out[...] = _dot1(jax.nn.sigmoid(gd), gup_ref[...])

    ones_bd = ones_ref[...]
    kn = k * kk_ref[...]
    kn = kn * lax.rsqrt(_dot_exact_rhs(kn * kn, ones_bd) + 1e-12)
    r_out[...] = r
    v_out[...] = v
    kn_out[...] = kn
    bonus_out[...] = _dot_exact_rhs(r * k * rk_ref[...], ones_bd) * v
    ka = ka_ref[...]
    for d in range(2):
        sl = slice(d * RWKV_W, (d + 1) * RWKV_W)
        a_d = jax.nn.sigmoid(a0_ref[d:d + 1, :] + a_pre[:, sl])
        lw_out[d] = -DECAY_SCALE * jax.nn.sigmoid(w0_ref[d:d + 1, :] + w_pre[:, sl])
        kd_out[d] = k * (1.0 + (a_d - 1.0) * ka)
        b_out[d] = kn * a_d


def _rwkv_pre_call(u, sw, wup_bd, aup_bd, gup, w0, a0, k_k, k_a, r_k, ones_bd, l):
    hb = TM // HALO
    last = N_TOK // HALO - 1
    row = lambda i: (l, 0, 0)
    tok = jax.ShapeDtypeStruct((N_TOK, RWKV_W), F32)
    tok2 = jax.ShapeDtypeStruct((2, N_TOK, RWKV_W), F32)
    return pl.pallas_call(
        _rwkv_pre_kernel,
        grid=(N_TILES,),
        in_specs=[
            pl.BlockSpec((TM, RWKV_IN_W), lambda i: (i, 0)),
            pl.BlockSpec((HALO, RWKV_IN_W), lambda i: (jnp.maximum(i * hb - 1, 0), 0)),
            pl.BlockSpec((HALO, RWKV_IN_W), lambda i: (jnp.minimum((i + 1) * hb, last), 0)),
            pl.BlockSpec((None, 3, RWKV_IN_W), row),
            pl.BlockSpec((None, 2 * DECAY_RANK, 2 * RWKV_W), row),
            pl.BlockSpec((None, 2 * ICLR_RANK, 2 * RWKV_W), row),
            pl.BlockSpec((None, GATE_RANK, RWKV_W), row),
            pl.BlockSpec((None, 2, RWKV_W), row),
            pl.BlockSpec((None, 2, RWKV_W), row),
            pl.BlockSpec((None, 1, RWKV_W), row),
            pl.BlockSpec((None, 1, RWKV_W), row),
            pl.BlockSpec((None, 1, RWKV_W), row),
            pl.BlockSpec((RWKV_W, RWKV_W), lambda i: (0, 0)),
        ],
        out_specs=[pl.BlockSpec((TM, RWKV_W), lambda i: (i, 0))] * 5
        + [pl.BlockSpec((2, TM, RWKV_W), lambda i: (0, i, 0))] * 3,
        out_shape=[tok] * 5 + [tok2] * 3,
        compiler_params=_cparams(("parallel",)),
        name="rwkv_token_terms",
    )(u, u, u, sw, wup_bd, aup_bd, gup, w0, a0, k_k, k_a, r_k, ones_bd)


def _stack_heads(x, bmask):
    return jnp.concatenate([x.astype(BF16)] * RWKV_HEADS, axis=0) * bmask


_TN = (((0,), (0,)), ((), ()))


def _scan_prepare(r, v, kn, lw, kd, b, cs, tot, strict, incl, eye, bmask):
    linc = _dot_exact_lhs(cs, lw)
    ltot = _dot_exact_lhs(tot, lw)
    v_all = v.astype(BF16)
    r_t = r * jnp.exp(linc)
    a_t = (-kn * jnp.exp(linc - lw)).astype(BF16)
    e_neg = jnp.exp(-linc)
    b_t = b * e_neg
    k_t = kd * e_neg
    e_rem = jnp.exp(ltot - linc)
    b_p = (b * e_rem).astype(BF16)
    k_p = (kd * e_rem).astype(BF16)
    dec = jnp.exp(ltot)

    chunks = range(CHUNKS_PER_TILE)
    sls = [slice(c * CHUNK, (c + 1) * CHUNK) for c in chunks]
    stk = lambda x: _stack_heads(x, bmask)
    ar = [jnp.concatenate([a_t[sl], r_t[sl].astype(BF16)], axis=0) for sl in sls]
    m_b = [_dg(ar[c], stk(b_t[sls[c]]), _NT) for c in chunks]
    m_k = [_dg(ar[c], stk(k_t[sls[c]]), _NT) for c in chunks]
    a_ab = [m_b[c][:CHUNK] * strict for c in chunks]
    p_rb = [(m_b[c][CHUNK:] * incl).astype(BF16) for c in chunks]
    ak_rk = [jnp.concatenate([m_k[c][:CHUNK] * strict, m_k[c][CHUNK:] * incl], axis=0).astype(BF16)
             for c in chunks]

    tinv = [eye + a_ab[c] for c in chunks]
    p = [a_ab[c].astype(BF16) for c in chunks]
    p = [_dg(p[c], stk(p[c]), _NN).astype(BF16) for c in chunks]
    for _ in range(4):
        pt = [_dg(jnp.concatenate([p[c], tinv[c].astype(BF16)], axis=0), stk(p[c]), _NN) for c in chunks]
        p = [pt[c][:CHUNK].astype(BF16) for c in chunks]
        tinv = [tinv[c] + pt[c][CHUNK:] for c in chunks]
    tinv = [(tinv[c] + _dg(tinv[c].astype(BF16), stk(p[c]), _NN)).astype(BF16) for c in chunks]

    m_v = [_dg(ak_rk[c], stk(v_all[sls[c]]), _NN) for c in chunks]
    w = [_dg(tinv[c], stk(a_t[sls[c]]), _NN) for c in chunks]
    u0 = [_dg(tinv[c], stk(m_v[c][:CHUNK]), _NN) for c in chunks]
    qp = [r_t[sls[c]] + _dg(p_rb[c], stk(w[c]), _NN) for c in chunks]
    y0 = [_dg(p_rb[c], stk(u0[c]), _NN) + m_v[c][CHUNK:] for c in chunks]
    wq = [jnp.concatenate([w[c].astype(BF16), qp[c].astype(BF16)], axis=0) for c in chunks]
    bk = [jnp.concatenate([b_p[sl], k_p[sl]], axis=0) for sl in sls]
    vc = [v_all[sl] for sl in sls]
    decs = [dec[c * CHUNK:c * CHUNK + 1, :] for c in chunks]
    return wq, u0, y0, bk, vc, decs


def _scan_chain_step(c, s, prep, y_ref, bmask_f):
    wq, u0, y0, bk, vc, decs = prep
    m_s = _dg(wq[c], s.astype(BF16), _NT)
    u = m_s[:CHUNK] + u0[c]
    y_ref[c * CHUNK:(c + 1) * CHUNK, :] = m_s[CHUNK:] + y0[c]
    uv = jnp.concatenate([u.astype(BF16), vc[c]], axis=0)
    return s * decs[c] + _dg(uv, bk[c], _TN) * bmask_f


def _scan_kernel(rf_ref, vf_ref, knf_ref, lwf_ref, kdf_ref, bf_ref, s0f_ref,
                 rb_ref, vb_ref, knb_ref, lwb_ref, kdb_ref, bb_ref, s0b_ref,
                 cs_ref, tot_ref, strict_ref, incl_ref, eye_ref, bmask_ref,
                 yf_ref, yb_ref, sfinf_ref, sfinb_ref, sf_scr, sb_scr):
    i = pl.program_id(0)
    tb = N_TILES - 1 - i
    first_f = (i < CTX_TILES) | ((i - CTX_TILES) % LAT_TILES_PER_SEQ == 0)
    first_b = (tb < CTX_TILES) | ((tb - CTX_TILES) % LAT_TILES_PER_SEQ == LAT_TILES_PER_SEQ - 1)
    bmask_f = bmask_ref[...]
    bmask = bmask_f.astype(BF16)

    @pl.when(first_f)
    def _():
        sf_scr[...] = jnp.concatenate([s0f_ref[...]] * RWKV_HEADS, axis=1) * bmask_f

    @pl.when(first_b)
    def _():
        sb_scr[...] = jnp.concatenate([s0b_ref[...]] * RWKV_HEADS, axis=1) * bmask_f

    tot = tot_ref[...]
    eye = eye_ref[...]
    prep_f = _scan_prepare(rf_ref[...], vf_ref[...], knf_ref[...], lwf_ref[...], kdf_ref[...], bf_ref[...],
                           cs_ref[0], tot, strict_ref[0], incl_ref[0], eye, bmask)
    prep_b = _scan_prepare(rb_ref[...], vb_ref[...], knb_ref[...], lwb_ref[...], kdb_ref[...], bb_ref[...],
                           cs_ref[1], tot, strict_ref[1], incl_ref[1], eye, bmask)
    s_f = sf_scr[...]
    s_b = sb_scr[...]
    for c in range(CHUNKS_PER_TILE):
        s_f = _scan_chain_step(c, s_f, prep_f, yf_ref, bmask_f)
        s_b = _scan_chain_step(CHUNKS_PER_TILE - 1 - c, s_b, prep_b, yb_ref, bmask_f)
    sf_scr[...] = s_f
    sb_scr[...] = s_b
    sfinf_ref[...] = s_f
    sfinb_ref[...] = s_b


def _scan_call(r, v, kn, lw, kd, b, s0, consts, l):
    cs, tot, strict, incl, eye, bmask = consts
    rev = lambda i: N_TILES - 1 - i
    seq_of = lambda t: jnp.where(t < CTX_TILES, t, CTX_TILES + (t - CTX_TILES) // LAT_TILES_PER_SEQ)
    const2 = lambda i: (0, 0)
    const3 = lambda i: (0, 0, 0)
    tok_f = pl.BlockSpec((TM, RWKV_W), lambda i: (i, 0))
    tok_b = pl.BlockSpec((TM, RWKV_W), lambda i: (rev(i), 0))
    tokd_f = pl.BlockSpec((None, TM, RWKV_W), lambda i: (0, i, 0))
    tokd_b = pl.BlockSpec((None, TM, RWKV_W), lambda i: (1, rev(i), 0))
    s0_f = pl.BlockSpec((None, None, None, RWKV_W, HEAD_DIM), lambda i: (seq_of(i), l, 0, 0, 0))
    s0_b = pl.BlockSpec((None, None, None, RWKV_W, HEAD_DIM), lambda i: (seq_of(rev(i)), l, 1, 0, 0))
    sfin = jax.ShapeDtypeStruct((CTX_TILES + 1, RWKV_W, RWKV_W), F32)
    y = jax.ShapeDtypeStruct((N_TOK, RWKV_W), F32)
    return pl.pallas_call(
        _scan_kernel,
        grid=(N_TILES,),
        in_specs=[
            tok_f, tok_f, tok_f, tokd_f, tokd_f, tokd_f, s0_f,
            tok_b, tok_b, tok_b, tokd_b, tokd_b, tokd_b, s0_b,
            pl.BlockSpec((2, TM, TM), const3),
            pl.BlockSpec((TM, TM), const2),
            pl.BlockSpec((2, CHUNK, RWKV_W), const3),
            pl.BlockSpec((2, CHUNK, RWKV_W), const3),
            pl.BlockSpec((CHUNK, RWKV_W), const2),
            pl.BlockSpec((RWKV_W, RWKV_W), const2),
        ],
        out_specs=[
            tok_f, tok_b,
            pl.BlockSpec((None, RWKV_W, RWKV_W), lambda i: (jnp.minimum(i, CTX_TILES), 0, 0)),
            pl.BlockSpec((None, RWKV_W, RWKV_W), lambda i: (jnp.minimum(rev(i), CTX_TILES), 0, 0)),
        ],
        out_shape=[y, y, sfin, sfin],
        scratch_shapes=[pltpu.VMEM((RWKV_W, RWKV_W), F32), pltpu.VMEM((RWKV_W, RWKV_W), F32)],
        compiler_params=_cparams(("arbitrary",)),
        name="rwkv_scan",
    )(r, v, kn, lw, kd, b, s0, r, v, kn, lw, kd, b, s0, cs, tot, strict, incl, eye, bmask)


def _scan_consts():
    t = np.arange(TM)
    same_chunk = (t[:, None] // CHUNK) == (t[None, :] // CHUNK)
    cs_f = same_chunk & (t[None, :] <= t[:, None])
    cs_b = same_chunk & (t[None, :] >= t[:, None])
    cs = np.stack([cs_f, cs_b]).astype(np.float32)
    tot = same_chunk.astype(np.float32)
    q = np.arange(CHUNK)[:, None]
    kcol = np.arange(RWKV_W)[None, :] % CHUNK
    strict = np.stack([kcol < q, kcol > q]).astype(np.float32)
    incl = np.stack([kcol <= q, kcol >= q]).astype(np.float32)
    eye = (kcol == q).astype(np.float32)
    return (jnp.asarray(cs, BF16), jnp.asarray(tot, BF16), jnp.asarray(strict), jnp.asarray(incl),
            jnp.asarray(eye), jnp.asarray(_head_block_mask(RWKV_W)))


def _fnet_kernel(seq_len, z_ref, ch_ref, cl_ref, th_ref, o_ref, z_scr):
    @pl.when(pl.program_id(1) == 0)
    def _():
        zh, zl = _split2(z_ref[...])
        for half in range(2):
            ch = ch_ref[half]
            cl = cl_ref[half]
            zc = _dg(zh, ch, _NN) + (_dg(zh, cl, _NN) + _dg(zl, ch, _NN))
            z_scr[half * seq_len:(half + 1) * seq_len, :] = zc.astype(BF16)

    o_ref[...] = _dg(th_ref[...], z_scr[...], _NN)


def _fnet_call(u_f, seq_len, n_seq, row_off, tables, name):
    ch, cl, th = tables
    tt = 256
    nt = seq_len // tt
    seq_off = row_off // seq_len
    return pl.pallas_call(
        functools.partial(_fnet_kernel, seq_len),
        grid=(n_seq, nt),
        in_specs=[
            pl.BlockSpec((seq_len, FNET_W), lambda b, j: (seq_off + b, 0)),
            pl.BlockSpec((2, FNET_W, FNET_W), lambda b, j: (0, 0, 0)),
            pl.BlockSpec((2, FNET_W, FNET_W), lambda b, j: (0, 0, 0)),
            pl.BlockSpec((tt, 2 * seq_len), lambda b, j: (j, 0)),
        ],
        out_specs=pl.BlockSpec((tt, FNET_W), lambda b, j: (b * nt + j, 0)),
        out_shape=jax.ShapeDtypeStruct((n_seq * seq_len, FNET_W), F32),
        scratch_shapes=[pltpu.VMEM((2 * seq_len, FNET_W), BF16)],
        compiler_params=_cparams(("parallel", "arbitrary")),
        name=name,
    )(u_f, ch, cl, th)


DFT_SPLIT = 32


def _dft_tables(seq_len):
    c = np.arange(FNET_W)
    ang_c = 2.0 * np.pi * ((c[:, None] % HEAD_DIM) * (c[None, :] % HEAD_DIM) % HEAD_DIM) / HEAD_DIM
    scale = 1.0 / np.sqrt(float(seq_len) * HEAD_DIM)
    blk = _head_block_mask(FNET_W)
    ch, cl = _split2(jnp.asarray(np.stack([np.cos(ang_c) * blk * scale, np.sin(ang_c) * blk * scale]), F32))

    n1 = seq_len // DFT_SPLIT
    f = np.arange(seq_len)
    a1 = 2.0 * np.pi * ((np.arange(n1)[:, None] * f[None, :]) % n1) / n1
    a2 = 2.0 * np.pi * ((np.arange(DFT_SPLIT)[:, None] * f[None, :]) % seq_len) / seq_len
    c1, s1 = jnp.asarray(np.cos(a1), F32)[:, None, :], jnp.asarray(np.sin(a1), F32)[:, None, :]
    c2, s2 = jnp.asarray(np.cos(a2), F32)[None, :, :], jnp.asarray(np.sin(a2), F32)[None, :, :]
    cos_t = (c1 * c2 - s1 * s2).reshape(seq_len, seq_len)
    sin_t = (s1 * c2 + c1 * s2).reshape(seq_len, seq_len)
    th = jnp.concatenate([cos_t, -sin_t], axis=1).astype(BF16)
    return ch, cl, th


def _merge_ffn_kernel(final, x_ref, mod_ref, oac_ref, oal_ref, yf_ref, yb_ref, bonus_ref, g_ref, ofc_ref, ofl_ref,
                      lng_ref, lnb_ref, avg_ref, nfg_ref, wo_ref, wi_ref, w2_ref, *tail):
    is_ctx = pl.program_id(0) < CTX_TILES
    o_attn = jnp.where(is_ctx, oac_ref[...], oal_ref[...])
    o_fnet = jnp.where(is_ctx, ofc_ref[...], ofl_ref[...])
    avg = avg_ref[...]
    y = yf_ref[...] + yb_ref[...]
    yc = y - _dot_exact_rhs(y, avg)
    var = _dot_exact_rhs(yc * yc, avg)
    yn = yc * lax.rsqrt(var + GN_EPS) * lng_ref[...] + lnb_ref[...]
    o_rwkv = (yn + bonus_ref[...]) * g_ref[...]

    o = (_dg(o_attn.astype(BF16), wo_ref[0:ATTN_W, :], _NN)
         + _dg(o_rwkv.astype(BF16), wo_ref[ATTN_W:ATTN_W + RWKV_W, :], _NN)
         + _dg(o_fnet.astype(BF16), wo_ref[ATTN_W + RWKV_W:, :], _NN))
    g1 = mod_ref[:, 2 * D_MODEL:3 * D_MODEL]
    sh2 = mod_ref[:, 3 * D_MODEL:4 * D_MODEL]
    sc2 = mod_ref[:, 4 * D_MODEL:5 * D_MODEL]
    g2 = mod_ref[:, 5 * D_MODEL:6 * D_MODEL]
    x = x_ref[...] + g1 * o
    h = _rms(x, nfg_ref[...]) * (1.0 + sc2) + sh2
    gu = _dg(h.astype(BF16), wi_ref[...], _NN)
    gt = gu[:, 0:D_FF]
    act = gt * jax.nn.sigmoid(gt) * gu[:, D_FF:]
    x = x + g2 * _dg(act.astype(BF16), w2_ref[...], _NN)
    if not final:
        (o_ref,) = tail
        o_ref[...] = x
    else:
        fng_ref, oc_ref, ol_ref = tail
        y_out = _rms(x, fng_ref[...])

        @pl.when(is_ctx)
        def _():
            oc_ref[...] = y_out

        @pl.when(jnp.logical_not(is_ctx))
        def _():
            ol_ref[...] = y_out


def _merge_ffn_call(x, mods_l, oa_ctx, oa_lat, y_f, y_b, bonus, g, of_ctx, of_lat, ln_g, ln_b, avg, nfg,
                    wo, wi, w2, l, final_g=None):
    final = final_g is not None
    row = lambda i: (l, 0, 0)
    tile = lambda i: (i, 0)
    ctx_tile = lambda i: (jnp.minimum(i, CTX_TILES - 1), 0)
    lat_tile = lambda i: (jnp.maximum(i - CTX_TILES, 0), 0)
    once = pl.Buffered(1)
    in_specs = [
        pl.BlockSpec((TM, D_MODEL), tile),
        pl.BlockSpec((None, 1, 6 * D_MODEL), lambda i: (_mod_row(i), 0, 0)),
        pl.BlockSpec((TM, ATTN_W), ctx_tile),
        pl.BlockSpec((TM, ATTN_W), lat_tile),
        pl.BlockSpec((TM, RWKV_W), tile),
        pl.BlockSpec((TM, RWKV_W), tile),
        pl.BlockSpec((TM, RWKV_W), tile),
        pl.BlockSpec((TM, RWKV_W), tile),
        pl.BlockSpec((TM, FNET_W), ctx_tile),
        pl.BlockSpec((TM, FNET_W), lat_tile),
        pl.BlockSpec((None, 1, RWKV_W), row),
        pl.BlockSpec((None, 1, RWKV_W), row),
        pl.BlockSpec((RWKV_W, RWKV_W), lambda i: (0, 0)),
        pl.BlockSpec((None, 1, D_MODEL), row),
        pl.BlockSpec((None, D_MODEL, D_MODEL), row, pipeline_mode=once),
        pl.BlockSpec((None, D_MODEL, 2 * D_FF), row, pipeline_mode=once),
        pl.BlockSpec((None, D_FF, D_MODEL), row, pipeline_mode=once),
    ]
    args = [x, mods_l, oa_ctx, oa_lat, y_f, y_b, bonus, g, of_ctx, of_lat, ln_g, ln_b, avg, nfg, wo, wi, w2]
    if final:
        in_specs.append(pl.BlockSpec((1, D_MODEL), lambda i: (0, 0)))
        args.append(final_g)
        out_specs = [pl.BlockSpec((TM, D_MODEL), ctx_tile), pl.BlockSpec((TM, D_MODEL), lat_tile)]
        out_shape = [jax.ShapeDtypeStruct((N_CTX, D_MODEL), F32), jax.ShapeDtypeStruct((N_LAT, D_MODEL), F32)]
    else:
        out_specs = pl.BlockSpec((TM, D_MODEL), tile)
        out_shape = jax.ShapeDtypeStruct((N_TOK, D_MODEL), F32)
    return pl.pallas_call(
        functools.partial(_merge_ffn_kernel, final),
        grid=(N_TILES,),
        in_specs=in_specs,
        out_specs=out_specs,
        out_shape=out_shape,
        compiler_params=_cparams(("arbitrary",)),
        name="merge_ffn_final" if final else "merge_ffn",
    )(*args)


def _block_diag2(w):
    z = jnp.zeros_like(w[:, 0])
    return jnp.concatenate([jnp.concatenate([w[:, 0], z], axis=2), jnp.concatenate([z, w[:, 1]], axis=2)], axis=1)


def kernel(x_prompt, x_sample, cache_k, cache_v, state_rwkv, c, c_ctx, w_ada, b_ada, norm_mix_g, norm_ffn_g,
           w_in, w_out, attn_sink, rwkv_shift_w, rwkv_w0, rwkv_w_up, rwkv_a0, rwkv_a_up, rwkv_g_up,
           rwkv_k_k, rwkv_k_a, rwkv_r_k, rwkv_ln_g, rwkv_ln_b, ffn_w_in, ffn_w_out, norm_final_g):
    x = jnp.concatenate([x_prompt.reshape(N_CTX, D_MODEL), x_sample.reshape(N_LAT, D_MODEL)], axis=0)
    cvec = jnp.concatenate([c_ctx[None, :], c, jnp.zeros((8 - 1 - N_LAT_SEQ, D_MODEL), F32)], axis=0)
    mods = _mods_call(cvec, w_ada, b_ada).reshape(DEPTH, 8, 1, 6 * D_MODEL)

    w_in_b = w_in.astype(BF16)
    w_out_b = w_out.astype(BF16)
    ffn_in_b = ffn_w_in.astype(BF16)
    ffn_out_b = ffn_w_out.astype(BF16)
    wup_bd = _block_diag2(rwkv_w_up)
    aup_bd = _block_diag2(rwkv_a_up)
    row3 = lambda a: a.reshape(DEPTH, 1, -1)
    nmg, nfg = row3(norm_mix_g), row3(norm_ffn_g)
    k_k, k_a, r_k = row3(rwkv_k_k), row3(rwkv_k_a), row3(rwkv_r_k)
    ln_g, ln_b = row3(rwkv_ln_g), row3(rwkv_ln_b)
    kc = cache_k.reshape(N_LAT_SEQ, DEPTH, PAST_LEN, KV_W)
    vc = cache_v.reshape(N_LAT_SEQ, DEPTH, PAST_LEN, KV_W)
    s0 = jnp.concatenate([jnp.zeros((N_CTX_SEQ, DEPTH, 2, RWKV_W, HEAD_DIM), F32),
                          state_rwkv.reshape(N_LAT_SEQ, DEPTH, 2, RWKV_W, HEAD_DIM)], axis=0)

    cos, sin = _rope_tables()
    consts = _scan_consts()
    bmask = consts[-1]
    ones_bd = bmask.astype(BF16)
    avg_bd = (bmask * (1.0 / HEAD_DIM)).astype(BF16)
    dft_ctx = _dft_tables(CTX_LEN)
    dft_lat = _dft_tables(LAT_LEN)

    new_k, new_v, new_s = [], [], []
    for l in range(DEPTH):
        q, k, v, u_r, u_f = _inproj_call(x, mods[l], nmg, w_in_b, l)
        new_k.append(k[:N_CTX].reshape(N_CTX_SEQ, CTX_LEN, N_KV_HEADS, HEAD_DIM))
        new_v.append(v[:N_CTX].reshape(N_CTX_SEQ, CTX_LEN, N_KV_HEADS, HEAD_DIM))

        oa_ctx = _ctx_attn_call(q, k, v, attn_sink, l)
        oa_lat = _lat_attn_call(q, k, v, kc, vc, cos, sin, attn_sink, l)

        r, vv, kn, g, bonus, lw, kd, b = _rwkv_pre_call(
            u_r, rwkv_shift_w, wup_bd, aup_bd, rwkv_g_up, rwkv_w0, rwkv_a0, k_k, k_a, r_k, ones_bd, l)
        y_f, y_b, sf, sb = _scan_call(r, vv, kn, lw, kd, b, s0, consts, l)
        fin = jnp.stack([sf[:N_CTX_SEQ], sb[:N_CTX_SEQ]], axis=1)
        fin = fin.reshape(N_CTX_SEQ, 2, RWKV_HEADS, HEAD_DIM, RWKV_HEADS, HEAD_DIM)
        new_s.append(jnp.stack([fin[:, :, h, :, h, :] for h in range(RWKV_HEADS)], axis=2))

        of_ctx = _fnet_call(u_f, CTX_LEN, N_CTX_SEQ, 0, dft_ctx, "fnet_context")
        of_lat = _fnet_call(u_f, LAT_LEN, N_LAT_SEQ, N_CTX, dft_lat, "fnet_latent")

        last = l == DEPTH - 1
        x = _merge_ffn_call(x, mods[l], oa_ctx, oa_lat, y_f, y_b, bonus, g, of_ctx, of_lat, ln_g, ln_b, avg_bd, nfg,
                            w_out_b, ffn_in_b, ffn_out_b, l,
                            final_g=norm_final_g.reshape(1, D_MODEL) if last else None)

    y_ctx, y_lat = x
    return (y_ctx.reshape(N_CTX_SEQ, CTX_LEN, D_MODEL), y_lat.reshape(N_LAT_SEQ, LAT_LEN, D_MODEL),
            jnp.stack(new_k, axis=1), jnp.stack(new_v, axis=1), jnp.stack(new_s, axis=1))
```

```python
import functools

import numpy as np
import jax
import jax.numpy as jnp
from jax import lax
from jax.experimental import pallas as pl
from jax.experimental.pallas import tpu as pltpu

F32 = jnp.float32
BF16 = jnp.bfloat16

D_MODEL = 1024
N_CTX_SEQ = 16
CTX_LEN = 256
DEPTH = 4
N_LAT_SEQ = 4
LAT_LEN = 2048
PAST_LEN = 512
GRID_W = 64
HEAD_DIM = 64
N_Q_HEADS = 8
N_KV_HEADS = 2
Q_PER_KV = N_Q_HEADS // N_KV_HEADS
ATTN_W = N_Q_HEADS * HEAD_DIM
KV_W = N_KV_HEADS * HEAD_DIM
BLOCK = 128
ATTN_SCALE = 0.125
ROPE_THETA = 10000.0
NEG = -1e30
RWKV_HEADS = 4
RWKV_W = RWKV_HEADS * HEAD_DIM
DECAY_RANK = 64
ICLR_RANK = 64
GATE_RANK = 128
DECAY_SCALE = 0.6065306597126334
GN_EPS = 64e-5
RWKV_IN_W = 3 * RWKV_W + GATE_RANK + 2 * DECAY_RANK + 2 * ICLR_RANK
FNET_W = 4 * HEAD_DIM
IN_W = ATTN_W + 2 * KV_W + RWKV_IN_W + FNET_W
D_FF = 2816
RMS_EPS = 1e-6

N_CTX = N_CTX_SEQ * CTX_LEN
N_LAT = N_LAT_SEQ * LAT_LEN
N_TOK = N_CTX + N_LAT
TM = 256
N_TILES = N_TOK // TM
CTX_TILES = N_CTX // TM
LAT_TILES_PER_SEQ = LAT_LEN // TM
N_SEQ = N_CTX_SEQ + N_LAT_SEQ
CHUNK = 64
CHUNKS_PER_TILE = TM // CHUNK
HALO = 8

VMEM_LIMIT = 56 * 1024 * 1024


def _cparams(sem):
    return pltpu.CompilerParams(dimension_semantics=sem, vmem_limit_bytes=VMEM_LIMIT)


def _mod_row(i):
    return jnp.where(i < CTX_TILES, 0, 1 + (i - CTX_TILES) // LAT_TILES_PER_SEQ)


def _split2(x):
    hi = x.astype(BF16)
    lo = (x - hi.astype(F32)).astype(BF16)
    return hi, lo


_NN = (((1,), (0,)), ((), ()))
_NT = (((1,), (1,)), ((), ()))


def _dg(a, b, dims):
    return lax.dot_general(a, b, dims, preferred_element_type=F32)


def _dot1(a, b, dims=_NN):
    return _dg(a.astype(BF16), b.astype(BF16), dims)


def _dot3(a, b, dims=_NN):
    ah, al = _split2(a)
    bh, bl = _split2(b)
    return _dg(ah, bh, dims) + (_dg(ah, bl, dims) + _dg(al, bh, dims))


def _dot_exact_rhs(a, b_bf16):
    ah, al = _split2(a)
    return _dg(ah, b_bf16, _NN) + _dg(al, b_bf16, _NN)


def _dot_exact_lhs(a_bf16, b):
    bh, bl = _split2(b)
    return _dg(a_bf16, bh, _NN) + _dg(a_bf16, bl, _NN)


def _mods_kernel(c_ref, w_ref, b_ref, o_ref):
    c = c_ref[...]
    s = c * jax.nn.sigmoid(c)
    o_ref[...] = _dot1(s, w_ref[...]) + b_ref[...]


def _mods_call(cvec, w_ada, b_ada):
    tn = 1536
    return pl.pallas_call(
        _mods_kernel,
        grid=(DEPTH, 6 * D_MODEL // tn),
        in_specs=[
            pl.BlockSpec((8, D_MODEL), lambda l, j: (0, 0)),
            pl.BlockSpec((None, D_MODEL, tn), lambda l, j: (l, 0, j)),
            pl.BlockSpec((None, 1, tn), lambda l, j: (l, 0, j)),
        ],
        out_specs=pl.BlockSpec((None, 8, tn), lambda l, j: (l, 0, j)),
        out_shape=jax.ShapeDtypeStruct((DEPTH, 8, 6 * D_MODEL), F32),
        compiler_params=_cparams(("parallel", "parallel")),
        name="adaln_mods",
    )(cvec, w_ada, b_ada.reshape(DEPTH, 1, 6 * D_MODEL))


def _rms(x, g):
    return x * lax.rsqrt(jnp.mean(x * x, axis=-1, keepdims=True) + RMS_EPS) * g


def _tile_of_split(i, ctx_ref, lat_ref):
    return jnp.where(i < CTX_TILES, ctx_ref[...], lat_ref[...])


_CTX_TILE = lambda i: (jnp.minimum(i, CTX_TILES - 1), 0)
_LAT_TILE = lambda i: (jnp.maximum(i - CTX_TILES, 0), 0)


def _x_specs(split):
    if split:
        return [pl.BlockSpec((TM, D_MODEL), _CTX_TILE), pl.BlockSpec((TM, D_MODEL), _LAT_TILE)]
    return [pl.BlockSpec((TM, D_MODEL), lambda i: (i, 0))]


def _inproj_kernel(split, *refs):
    n_x = 2 if split else 1
    x = _tile_of_split(pl.program_id(0), *refs[:2]) if split else refs[0][...]
    mod_ref, g_ref, w_ref, q_ref, k_ref, v_ref, ur_ref, uf_ref = refs[n_x:]
    shift = mod_ref[:, 0:D_MODEL]
    scale = mod_ref[:, D_MODEL:2 * D_MODEL]
    h = _rms(x, g_ref[...]) * (1.0 + scale) + shift
    o = _dg(h.astype(BF16), w_ref[...], _NN)
    q_ref[...] = o[:, 0:ATTN_W]
    k_ref[...] = o[:, ATTN_W:ATTN_W + KV_W]
    v_ref[...] = o[:, ATTN_W + KV_W:ATTN_W + 2 * KV_W]
    ur_ref[...] = o[:, ATTN_W + 2 * KV_W:ATTN_W + 2 * KV_W + RWKV_IN_W]
    uf_ref[...] = o[:, ATTN_W + 2 * KV_W + RWKV_IN_W:IN_W]


def _inproj_call(xs, mods_l, g, w_bf16, l):
    split = isinstance(xs, tuple)
    xs = xs if split else (xs,)
    widths = (ATTN_W, KV_W, KV_W, RWKV_IN_W, FNET_W)
    return pl.pallas_call(
        functools.partial(_inproj_kernel, split),
        grid=(N_TILES,),
        in_specs=_x_specs(split) + [
            pl.BlockSpec((None, 1, 6 * D_MODEL), lambda i: (_mod_row(i), 0, 0)),
            pl.BlockSpec((None, 1, D_MODEL), lambda i: (l, 0, 0)),
            pl.BlockSpec((None, D_MODEL, IN_W), lambda i: (l, 0, 0)),
        ],
        out_specs=[pl.BlockSpec((TM, w), lambda i: (i, 0)) for w in widths],
        out_shape=[jax.ShapeDtypeStruct((N_TOK, w), F32) for w in widths],
        compiler_params=_cparams(("arbitrary",)),
        name="in_projection",
    )(*xs, mods_l, g, w_bf16)


def _softmax_pv(s, sink_col, v):
    m = jnp.maximum(jnp.max(s, axis=-1, keepdims=True), sink_col)
    e = jnp.exp(s - m)
    denom = jnp.sum(e, axis=-1, keepdims=True) + jnp.exp(sink_col - m)
    return _dot1(e, v) / denom


def _stack_q_heads(q, kh, rows):
    return jnp.concatenate(
        [q[:, (kh * Q_PER_KV + g) * HEAD_DIM:(kh * Q_PER_KV + g + 1) * HEAD_DIM] for g in range(Q_PER_KV)],
        axis=0)


def _sink_col(sink_ref, l, kh, rows):
    return jnp.concatenate(
        [jnp.full((rows, 1), sink_ref[l, kh * Q_PER_KV + g], F32) for g in range(Q_PER_KV)], axis=0)


def _ctx_attn_kernel(l, sink_ref, q_ref, k_ref, v_ref, o_ref):
    q = q_ref[...]
    k = k_ref[...]
    v = v_ref[...]
    for kh in range(N_KV_HEADS):
        q4 = _stack_q_heads(q, kh, CTX_LEN)
        kk = k[:, kh * HEAD_DIM:(kh + 1) * HEAD_DIM]
        vv = v[:, kh * HEAD_DIM:(kh + 1) * HEAD_DIM]
        s = _dot1(q4 * ATTN_SCALE, kk, _NT)
        o4 = _softmax_pv(s, _sink_col(sink_ref, l, kh, CTX_LEN), vv)
        for g in range(Q_PER_KV):
            h = kh * Q_PER_KV + g
            o_ref[:, h * HEAD_DIM:(h + 1) * HEAD_DIM] = o4[g * CTX_LEN:(g + 1) * CTX_LEN]


def _ctx_attn_call(q, k, v, sink, l):
    return pl.pallas_call(
        functools.partial(_ctx_attn_kernel, l),
        grid=(N_CTX_SEQ,),
        in_specs=[
            pl.BlockSpec(memory_space=pltpu.SMEM),
            pl.BlockSpec((CTX_LEN, ATTN_W), lambda b: (b, 0)),
            pl.BlockSpec((CTX_LEN, KV_W), lambda b: (b, 0)),
            pl.BlockSpec((CTX_LEN, KV_W), lambda b: (b, 0)),
        ],
        out_specs=pl.BlockSpec((CTX_LEN, ATTN_W), lambda b: (b, 0)),
        out_shape=jax.ShapeDtypeStruct((N_CTX, ATTN_W), F32),
        compiler_params=_cparams(("parallel",)),
        name="context_attention",
    )(sink, q, k, v)


def _rope(x, cos, sin):
    lane = lax.broadcasted_iota(jnp.int32, x.shape, 1)
    partner = jnp.where((lane & 16) == 0, pltpu.roll(x, 128 - 16, 1), pltpu.roll(x, 16, 1))
    return x * cos + partner * sin


def _lat_attn_kernel(l, sink_ref, q_ref, k_ref, v_ref, kc_ref, vc_ref, cos_ref, sin_ref, o_ref):
    n = pl.program_id(1)
    nb = LAT_LEN // BLOCK
    r0 = pl.multiple_of(n * BLOCK, BLOCK)
    cos_q = cos_ref[pl.ds(r0, BLOCK), :]
    sin_q = sin_ref[pl.ds(r0, BLOCK), :]
    q = jnp.concatenate(
        [_rope(q_ref[:, j * 128:(j + 1) * 128], cos_q, sin_q) for j in range(ATTN_W // 128)], axis=1)

    ks, vs = [], []
    for dj in (-1, 0, 1):
        rj = pl.multiple_of(jnp.clip(n + dj, 0, nb - 1) * BLOCK, BLOCK)
        ks.append(_rope(k_ref[pl.ds(rj, BLOCK), :], cos_ref[pl.ds(rj, BLOCK), :], sin_ref[pl.ds(rj, BLOCK), :]))
        vs.append(v_ref[pl.ds(rj, BLOCK), :])
    k_all = jnp.concatenate(ks + [kc_ref[...]], axis=0)
    v_all = jnp.concatenate(vs + [vc_ref[...]], axis=0)

    qi = lax.broadcasted_iota(jnp.int32, (Q_PER_KV * BLOCK, BLOCK), 0) & (BLOCK - 1)
    kj = lax.broadcasted_iota(jnp.int32, (Q_PER_KV * BLOCK, BLOCK), 1)
    keep_prev = (kj >= qi) & (n > 0)
    keep_next = (kj <= qi) & (n < nb - 1)

    for kh in range(N_KV_HEADS):
        q4 = _stack_q_heads(q, kh, BLOCK) * ATTN_SCALE
        kk = k_all[:, kh * HEAD_DIM:(kh + 1) * HEAD_DIM]
        vv = v_all[:, kh * HEAD_DIM:(kh + 1) * HEAD_DIM]
        s = _dot1(q4, kk, _NT)
        s = jnp.concatenate([jnp.where(keep_prev, s[:, 0:BLOCK], NEG), s[:, BLOCK:2 * BLOCK],
                             jnp.where(keep_next, s[:, 2 * BLOCK:3 * BLOCK], NEG), s[:, 3 * BLOCK:]], axis=1)
        o4 = _softmax_pv(s, _sink_col(sink_ref, l, kh, BLOCK), vv)
        for g in range(Q_PER_KV):
            h = kh * Q_PER_KV + g
            o_ref[:, h * HEAD_DIM:(h + 1) * HEAD_DIM] = o4[g * BLOCK:(g + 1) * BLOCK]


def _lat_attn_call(q, k, v, kc, vc, cos, sin, sink, l):
    nb = LAT_LEN // BLOCK
    q_off = N_CTX // BLOCK
    kv_off = N_CTX // LAT_LEN
    return pl.pallas_call(
        functools.partial(_lat_attn_kernel, l),
        grid=(N_LAT_SEQ, nb),
        in_specs=[
            pl.BlockSpec(memory_space=pltpu.SMEM),
            pl.BlockSpec((BLOCK, ATTN_W), lambda b, n: (q_off + b * nb + n, 0)),
            pl.BlockSpec((LAT_LEN, KV_W), lambda b, n: (kv_off + b, 0)),
            pl.BlockSpec((LAT_LEN, KV_W), lambda b, n: (kv_off + b, 0)),
            pl.BlockSpec((None, None, PAST_LEN, KV_W), lambda b, n: (b, l, 0, 0)),
            pl.BlockSpec((None, None, PAST_LEN, KV_W), lambda b, n: (b, l, 0, 0)),
            pl.BlockSpec((LAT_LEN, 128), lambda b, n: (0, 0)),
            pl.BlockSpec((LAT_LEN, 128), lambda b, n: (0, 0)),
        ],
        out_specs=pl.BlockSpec((BLOCK, ATTN_W), lambda b, n: (b * nb + n, 0)),
        out_shape=jax.ShapeDtypeStruct((N_LAT, ATTN_W), F32),
        compiler_params=_cparams(("parallel", "arbitrary")),
        name="latent_attention",
    )(sink, q, k, v, kc, vc, cos, sin)


def _rope_tables():
    t = np.arange(LAT_LEN)
    lane = np.arange(128)
    d = lane % HEAD_DIM
    pos = np.where(d[None, :] < HEAD_DIM // 2, (t // GRID_W)[:, None], (t % GRID_W)[:, None]).astype(np.float32)
    quarter = HEAD_DIM // 4
    inv = (np.float32(ROPE_THETA) ** (-(np.arange(quarter, dtype=np.float32)) / np.float32(quarter))).astype(np.float32)
    ang = pos * inv[d % quarter][None, :]
    sign = np.where((lane & 16) == 0, -1.0, 1.0)[None, :]
    return jnp.asarray(np.cos(ang), F32), jnp.asarray(np.sin(ang) * sign, F32)


def _head_block_mask(n):
    r = np.arange(n)[:, None] // HEAD_DIM
    c = np.arange(n)[None, :] // HEAD_DIM
    return (r == c).astype(np.float32)


def _rwkv_pre_kernel(u_ref, up_ref, un_ref, sw_ref, wup_ref, aup_ref, gup_ref, w0_ref, a0_ref,
                     kk_ref, ka_ref, rk_ref, ones_ref,
                     r_out, v_out, kn_out, g_out, bonus_out, lw_out, kd_out, b_out):
    i = pl.program_id(0)
    j = (i - CTX_TILES) % LAT_TILES_PER_SEQ
    has_prev = jnp.where((i >= CTX_TILES) & (j != 0), 1.0, 0.0)
    has_next = jnp.where((i >= CTX_TILES) & (j != LAT_TILES_PER_SEQ - 1), 1.0, 0.0)
    u = u_ref[...]
    row = lax.broadcasted_iota(jnp.int32, u.shape, 0)
    prev_row = up_ref[HALO - 1:HALO, :] * has_prev
    next_row = un_ref[0:1, :] * has_next
    u_dn = jnp.where(row == 0, prev_row, pltpu.roll(u, 1, 0))
    u_up = jnp.where(row == TM - 1, next_row, pltpu.roll(u, TM - 1, 0))
    us = u_dn * sw_ref[0:1, :] + u * sw_ref[1:2, :] + u_up * sw_ref[2:3, :]

    r = us[:, 0:RWKV_W]
    k = us[:, RWKV_W:2 * RWKV_W]
    v = us[:, 2 * RWKV_W:3 * RWKV_W]
    o = 3 * RWKV_W
    gd = us[:, o:o + GATE_RANK]
    wd = us[:, o + GATE_RANK:o + GATE_RANK + 2 * DECAY_RANK]
    ad = us[:, o + GATE_RANK + 2 * DECAY_RANK:RWKV_IN_W]

    w_pre = _dot1(jnp.tanh(wd), wup_ref[...])
    a_pre = _dot1(ad, aup_ref[...])
    g_out[...] = _dot1(jax.nn.sigmoid(gd), gup_ref[...])

    ones_bd = ones_ref[...]
    kn = k * kk_ref[...]
    kn = kn * lax.rsqrt(_dot_exact_rhs(kn * kn, ones_bd) + 1e-12)
    r_out[...] = r
    v_out[...] = v
    kn_out[...] = kn
    bonus_out[...] = _dot_exact_rhs(r * k * rk_ref[...], ones_bd) * v
    ka = ka_ref[...]
    for d in range(2):
        sl = slice(d * RWKV_W, (d + 1) * RWKV_W)
        a_d = jax.nn.sigmoid(a0_ref[d:d + 1, :] + a_pre[:, sl])
        lw_out[d] = -DECAY_SCALE * jax.nn.sigmoid(w0_ref[d:d + 1, :] + w_pre[:, sl])
        kd_out[d] = k * (1.0 + (a_d - 1.0) * ka)
        b_out[d] = kn * a_d


def _rwkv_pre_call(u, sw, wup_bd, aup_bd, gup, w0, a0, k_k, k_a, r_k, ones_bd, l):
    hb = TM // HALO
    last = N_TOK // HALO - 1
    row = lambda i: (l, 0, 0)
    tok = jax.ShapeDtypeStruct((N_TOK, RWKV_W), F32)
    tok2 = jax.ShapeDtypeStruct((2, N_TOK, RWKV_W), F32)
    return pl.pallas_call(
        _rwkv_pre_kernel,
        grid=(N_TILES,),
        in_specs=[
            pl.BlockSpec((TM, RWKV_IN_W), lambda i: (i, 0)),
            pl.BlockSpec((HALO, RWKV_IN_W), lambda i: (jnp.maximum(i * hb - 1, 0), 0)),
            pl.BlockSpec((HALO, RWKV_IN_W), lambda i: (jnp.minimum((i + 1) * hb, last), 0)),
            pl.BlockSpec((None, 3, RWKV_IN_W), row),
            pl.BlockSpec((None, 2 * DECAY_RANK, 2 * RWKV_W), row),
            pl.BlockSpec((None, 2 * ICLR_RANK, 2 * RWKV_W), row),
            pl.BlockSpec((None, GATE_RANK, RWKV_W), row),
            pl.BlockSpec((None, 2, RWKV_W), row),
            pl.BlockSpec((None, 2, RWKV_W), row),
            pl.BlockSpec((None, 1, RWKV_W), row),
            pl.BlockSpec((None, 1, RWKV_W), row),
            pl.BlockSpec((None, 1, RWKV_W), row),
            pl.BlockSpec((RWKV_W, RWKV_W), lambda i: (0, 0)),
        ],
        out_specs=[pl.BlockSpec((TM, RWKV_W), lambda i: (i, 0))] * 5
        + [pl.BlockSpec((2, TM, RWKV_W), lambda i: (0, i, 0))] * 3,
        out_shape=[tok] * 5 + [tok2] * 3,
        compiler_params=_cparams(("parallel",)),
        name="rwkv_token_terms",
    )(u, u, u, sw, wup_bd, aup_bd, gup, w0, a0, k_k, k_a, r_k, ones_bd)


PAIR_W = 2 * HEAD_DIM
N_PAIRS = RWKV_HEADS // 2
_TN = (((0,), (0,)), ((), ()))


def _pair_stack(x, pmask):
    return jnp.concatenate([x.astype(BF16)] * 2, axis=0) * pmask


def _hdot(m, x, pmask, dims):
    return jnp.concatenate(
        [_dg(m[:, p * PAIR_W:(p + 1) * PAIR_W], _pair_stack(x[:, p * PAIR_W:(p + 1) * PAIR_W], pmask), dims)
         for p in range(N_PAIRS)], axis=1)


def _scan_prepare(r, v, kn, lw, kd, b, cs, tot, strict, incl, eye, pmask):
    linc = _dot_exact_lhs(cs, lw)
    ltot = _dot_exact_lhs(tot, lw)
    v_all = v.astype(BF16)
    r_t = r * jnp.exp(linc)
    a_t = (-kn * jnp.exp(linc - lw)).astype(BF16)
    e_neg = jnp.exp(-linc)
    b_t = (b * e_neg).astype(BF16)
    k_t = (kd * e_neg).astype(BF16)
    e_rem = jnp.exp(ltot - linc)
    b_p = (b * e_rem).astype(BF16)
    k_p = (kd * e_rem).astype(BF16)
    dec = jnp.exp(ltot)

    chunks = range(CHUNKS_PER_TILE)
    sls = [slice(c * CHUNK, (c + 1) * CHUNK) for c in chunks]
    hd = lambda m, x, dims: _hdot(m, x, pmask, dims)
    ar = [jnp.concatenate([a_t[sl], r_t[sl].astype(BF16)], axis=0) for sl in sls]
    m_b = [hd(ar[c], b_t[sls[c]], _NT) for c in chunks]
    m_k = [hd(ar[c], k_t[sls[c]], _NT) for c in chunks]
    a_ab = [m_b[c][:CHUNK] * strict for c in chunks]
    p_rb = [(m_b[c][CHUNK:] * incl).astype(BF16) for c in chunks]
    ak_rk = [jnp.concatenate([m_k[c][:CHUNK] * strict, m_k[c][CHUNK:] * incl], axis=0).astype(BF16)
             for c in chunks]

    tinv = [eye + a_ab[c] for c in chunks]
    p = [a_ab[c].astype(BF16) for c in chunks]
    p = [hd(p[c], p[c], _NN).astype(BF16) for c in chunks]
    for _ in range(4):
        pt = [hd(jnp.concatenate([p[c], tinv[c].astype(BF16)], axis=0), p[c], _NN) for c in chunks]
        p = [pt[c][:CHUNK].astype(BF16) for c in chunks]
        tinv = [tinv[c] + pt[c][CHUNK:] for c in chunks]
    tinv = [(tinv[c] + hd(tinv[c].astype(BF16), p[c], _NN)).astype(BF16) for c in chunks]

    m_v = [hd(ak_rk[c], v_all[sls[c]], _NN) for c in chunks]
    w = [hd(tinv[c], a_t[sls[c]], _NN) for c in chunks]
    u0 = [hd(tinv[c], m_v[c][:CHUNK], _NN) for c in chunks]
    qp = [r_t[sls[c]] + hd(p_rb[c], w[c], _NN) for c in chunks]
    y0 = [hd(p_rb[c], u0[c], _NN) + m_v[c][CHUNK:] for c in chunks]
    pr = [slice(p * PAIR_W, (p + 1) * PAIR_W) for p in range(N_PAIRS)]
    pmask_f = pmask.astype(F32)
    bk = [jnp.concatenate([b_p[sl], k_p[sl]], axis=0) for sl in sls]
    uv = [jnp.concatenate([u0[c].astype(BF16), v_all[sls[c]]], axis=0) for c in chunks]
    wb = [[_dg(w[c][:, q].astype(BF16), b_p[sls[c]][:, q], _TN) * pmask_f for q in pr] for c in chunks]
    gs = [[_dg(uv[c][:, q], bk[c][:, q], _TN) * pmask_f for q in pr] for c in chunks]
    qpb = [qp[c].astype(BF16) for c in chunks]
    decs = [dec[c * CHUNK:c * CHUNK + 1, :] for c in chunks]
    return qpb, y0, wb, gs, decs


def _scan_chain_step(c, s, prep, y_ref):
    qpb, y0, wb, gs, decs = prep
    pr = [slice(p * PAIR_W, (p + 1) * PAIR_W) for p in range(N_PAIRS)]
    sb = [s[p].astype(BF16) for p in range(N_PAIRS)]
    y_ref[c * CHUNK:(c + 1) * CHUNK, :] = y0[c] + jnp.concatenate(
        [_dg(qpb[c][:, pr[p]], sb[p], _NT) for p in range(N_PAIRS)], axis=1)
    return [s[p] * decs[c][:, pr[p]] + (_dg(sb[p], wb[c][p].astype(BF16), _NN) + gs[c][p]) for p in range(N_PAIRS)]


def _scan_kernel(rf_ref, vf_ref, knf_ref, lwf_ref, kdf_ref, bf_ref, s0f_ref,
                 rb_ref, vb_ref, knb_ref, lwb_ref, kdb_ref, bb_ref, s0b_ref,
                 cs_ref, tot_ref, strict_ref, incl_ref, eye_ref, pmask_ref,
                 yf_ref, yb_ref, sfinf_ref, sfinb_ref, sf_scr, sb_scr):
    i = pl.program_id(0)
    tb = N_TILES - 1 - i
    first_f = (i < CTX_TILES) | ((i - CTX_TILES) % LAT_TILES_PER_SEQ == 0)
    first_b = (tb < CTX_TILES) | ((tb - CTX_TILES) % LAT_TILES_PER_SEQ == LAT_TILES_PER_SEQ - 1)
    pmask_f = pmask_ref[...]
    pmask = pmask_f.astype(BF16)

    def load_state(s0_ref, scr):
        for p in range(N_PAIRS):
            scr[p] = jnp.concatenate([s0_ref[p * PAIR_W:(p + 1) * PAIR_W, :]] * 2, axis=1) * pmask_f

    pl.when(first_f)(lambda: load_state(s0f_ref, sf_scr))
    pl.when(first_b)(lambda: load_state(s0b_ref, sb_scr))

    tot = tot_ref[...]
    eye = eye_ref[...]
    prep_f = _scan_prepare(rf_ref[...], vf_ref[...], knf_ref[...], lwf_ref[...], kdf_ref[...], bf_ref[...],
                           cs_ref[0], tot, strict_ref[0], incl_ref[0], eye, pmask)
    prep_b = _scan_prepare(rb_ref[...], vb_ref[...], knb_ref[...], lwb_ref[...], kdb_ref[...], bb_ref[...],
                           cs_ref[1], tot, strict_ref[1], incl_ref[1], eye, pmask)
    s_f = [sf_scr[p] for p in range(N_PAIRS)]
    s_b = [sb_scr[p] for p in range(N_PAIRS)]
    for c in range(CHUNKS_PER_TILE):
        s_f = _scan_chain_step(c, s_f, prep_f, yf_ref)
        s_b = _scan_chain_step(CHUNKS_PER_TILE - 1 - c, s_b, prep_b, yb_ref)
    for p in range(N_PAIRS):
        sf_scr[p] = s_f[p]
        sb_scr[p] = s_b[p]
        sfinf_ref[p] = s_f[p]
        sfinb_ref[p] = s_b[p]


def _scan_call(r, v, kn, lw, kd, b, s0, consts, l):
    cs, tot, strict, incl, eye, pmask = consts
    rev = lambda i: N_TILES - 1 - i
    seq_of = lambda t: jnp.where(t < CTX_TILES, t, CTX_TILES + (t - CTX_TILES) // LAT_TILES_PER_SEQ)
    const2 = lambda i: (0, 0)
    const3 = lambda i: (0, 0, 0)
    tok_f = pl.BlockSpec((TM, RWKV_W), lambda i: (i, 0))
    tok_b = pl.BlockSpec((TM, RWKV_W), lambda i: (rev(i), 0))
    tokd_f = pl.BlockSpec((None, TM, RWKV_W), lambda i: (0, i, 0))
    tokd_b = pl.BlockSpec((None, TM, RWKV_W), lambda i: (1, rev(i), 0))
    s0_f = pl.BlockSpec((None, None, None, RWKV_W, HEAD_DIM), lambda i: (seq_of(i), l, 0, 0, 0))
    s0_b = pl.BlockSpec((None, None, None, RWKV_W, HEAD_DIM), lambda i: (seq_of(rev(i)), l, 1, 0, 0))
    sfin = jax.ShapeDtypeStruct((CTX_TILES + 1, N_PAIRS, PAIR_W, PAIR_W), F32)
    y = jax.ShapeDtypeStruct((N_TOK, RWKV_W), F32)
    state = pltpu.VMEM((N_PAIRS, PAIR_W, PAIR_W), F32)
    return pl.pallas_call(
        _scan_kernel,
        grid=(N_TILES,),
        in_specs=[
            tok_f, tok_f, tok_f, tokd_f, tokd_f, tokd_f, s0_f,
            tok_b, tok_b, tok_b, tokd_b, tokd_b, tokd_b, s0_b,
            pl.BlockSpec((2, TM, TM), const3),
            pl.BlockSpec((TM, TM), const2),
            pl.BlockSpec((2, CHUNK, RWKV_W), const3),
            pl.BlockSpec((2, CHUNK, RWKV_W), const3),
            pl.BlockSpec((CHUNK, RWKV_W), const2),
            pl.BlockSpec((PAIR_W, PAIR_W), const2),
        ],
        out_specs=[
            tok_f, tok_b,
            pl.BlockSpec((None, N_PAIRS, PAIR_W, PAIR_W), lambda i: (jnp.minimum(i, CTX_TILES), 0, 0, 0)),
            pl.BlockSpec((None, N_PAIRS, PAIR_W, PAIR_W), lambda i: (jnp.minimum(rev(i), CTX_TILES), 0, 0, 0)),
        ],
        out_shape=[y, y, sfin, sfin],
        scratch_shapes=[state, state],
        compiler_params=_cparams(("arbitrary",)),
        name="rwkv_scan",
    )(r, v, kn, lw, kd, b, s0, r, v, kn, lw, kd, b, s0, cs, tot, strict, incl, eye, pmask)


def _scan_final_states(sf, sb):
    fin = jnp.stack([sf[:N_CTX_SEQ], sb[:N_CTX_SEQ]], axis=1)
    fin = fin.reshape(N_CTX_SEQ, 2, N_PAIRS, 2, HEAD_DIM, 2, HEAD_DIM)
    fin = jnp.stack([fin[:, :, :, h, :, h, :] for h in range(2)], axis=3)
    return fin.reshape(N_CTX_SEQ, 2, RWKV_HEADS, HEAD_DIM, HEAD_DIM)


def _scan_consts():
    t = np.arange(TM)
    same_chunk = (t[:, None] // CHUNK) == (t[None, :] // CHUNK)
    cs_f = same_chunk & (t[None, :] <= t[:, None])
    cs_b = same_chunk & (t[None, :] >= t[:, None])
    cs = np.stack([cs_f, cs_b]).astype(np.float32)
    tot = same_chunk.astype(np.float32)
    q = np.arange(CHUNK)[:, None]
    kcol = np.arange(RWKV_W)[None, :] % CHUNK
    strict = np.stack([kcol < q, kcol > q]).astype(np.float32)
    incl = np.stack([kcol <= q, kcol >= q]).astype(np.float32)
    eye = (kcol == q).astype(np.float32)
    return (jnp.asarray(cs, BF16), jnp.asarray(tot, BF16), jnp.asarray(strict), jnp.asarray(incl),
            jnp.asarray(eye), jnp.asarray(_head_block_mask(PAIR_W)))


def _fnet_kernel(seq_len, z_ref, ch_ref, cl_ref, th_ref, o_ref, z_scr):
    @pl.when(pl.program_id(1) == 0)
    def _():
        zh, zl = _split2(z_ref[...])
        for half in range(2):
            ch = ch_ref[half]
            cl = cl_ref[half]
            zc = _dg(zh, ch, _NN) + (_dg(zh, cl, _NN) + _dg(zl, ch, _NN))
            z_scr[half * seq_len:(half + 1) * seq_len, :] = zc.astype(BF16)

    o_ref[...] = _dg(th_ref[...], z_scr[...], _NN)


def _fnet_call(u_f, seq_len, n_seq, row_off, tables, name):
    ch, cl, th = tables
    tt = 256
    nt = seq_len // tt
    seq_off = row_off // seq_len
    return pl.pallas_call(
        functools.partial(_fnet_kernel, seq_len),
        grid=(n_seq, nt),
        in_specs=[
            pl.BlockSpec((seq_len, FNET_W), lambda b, j: (seq_off + b, 0)),
            pl.BlockSpec((2, FNET_W, FNET_W), lambda b, j: (0, 0, 0)),
            pl.BlockSpec((2, FNET_W, FNET_W), lambda b, j: (0, 0, 0)),
            pl.BlockSpec((tt, 2 * seq_len), lambda b, j: (j, 0)),
        ],
        out_specs=pl.BlockSpec((tt, FNET_W), lambda b, j: (b * nt + j, 0)),
        out_shape=jax.ShapeDtypeStruct((n_seq * seq_len, FNET_W), F32),
        scratch_shapes=[pltpu.VMEM((2 * seq_len, FNET_W), BF16)],
        compiler_params=_cparams(("parallel", "arbitrary")),
        name=name,
    )(u_f, ch, cl, th)


DFT_SPLIT = 32


def _dft_tables(seq_len):
    c = np.arange(FNET_W)
    ang_c = 2.0 * np.pi * ((c[:, None] % HEAD_DIM) * (c[None, :] % HEAD_DIM) % HEAD_DIM) / HEAD_DIM
    scale = 1.0 / np.sqrt(float(seq_len) * HEAD_DIM)
    blk = _head_block_mask(FNET_W)
    ch, cl = _split2(jnp.asarray(np.stack([np.cos(ang_c) * blk * scale, np.sin(ang_c) * blk * scale]), F32))

    n1 = seq_len // DFT_SPLIT
    f = np.arange(seq_len)
    a1 = 2.0 * np.pi * ((np.arange(n1)[:, None] * f[None, :]) % n1) / n1
    a2 = 2.0 * np.pi * ((np.arange(DFT_SPLIT)[:, None] * f[None, :]) % seq_len) / seq_len
    c1, s1 = jnp.asarray(np.cos(a1), F32)[:, None, :], jnp.asarray(np.sin(a1), F32)[:, None, :]
    c2, s2 = jnp.asarray(np.cos(a2), F32)[None, :, :], jnp.asarray(np.sin(a2), F32)[None, :, :]
    cos_t = (c1 * c2 - s1 * s2).reshape(seq_len, seq_len)
    sin_t = (s1 * c2 + c1 * s2).reshape(seq_len, seq_len)
    th = jnp.concatenate([cos_t, -sin_t], axis=1).astype(BF16)
    return ch, cl, th


def _merge_ffn_kernel(split, final, *refs):
    i = pl.program_id(0)
    n_x = 2 if split else 1
    x_in = _tile_of_split(i, *refs[:2]) if split else refs[0][...]
    (mod_ref, oac_ref, oal_ref, yf_ref, yb_ref, bonus_ref, g_ref, ofc_ref, ofl_ref,
     lng_ref, lnb_ref, avg_ref, nfg_ref, wo_ref, wi_ref, w2_ref) = refs[n_x:n_x + 16]
    tail = refs[n_x + 16:]
    is_ctx = i < CTX_TILES
    o_attn = _tile_of_split(i, oac_ref, oal_ref)
    o_fnet = _tile_of_split(i, ofc_ref, ofl_ref)
    avg = avg_ref[...]
    y = yf_ref[...] + yb_ref[...]
    yc = y - _dot_exact_rhs(y, avg)
    var = _dot_exact_rhs(yc * yc, avg)
    yn = yc * lax.rsqrt(var + GN_EPS) * lng_ref[...] + lnb_ref[...]
    o_rwkv = (yn + bonus_ref[...]) * g_ref[...]

    o = (_dg(o_attn.astype(BF16), wo_ref[0:ATTN_W, :], _NN)
         + _dg(o_rwkv.astype(BF16), wo_ref[ATTN_W:ATTN_W + RWKV_W, :], _NN)
         + _dg(o_fnet.astype(BF16), wo_ref[ATTN_W + RWKV_W:, :], _NN))
    g1 = mod_ref[:, 2 * D_MODEL:3 * D_MODEL]
    sh2 = mod_ref[:, 3 * D_MODEL:4 * D_MODEL]
    sc2 = mod_ref[:, 4 * D_MODEL:5 * D_MODEL]
    g2 = mod_ref[:, 5 * D_MODEL:6 * D_MODEL]
    x = x_in + g1 * o
    h = _rms(x, nfg_ref[...]) * (1.0 + sc2) + sh2
    gu = _dg(h.astype(BF16), wi_ref[...], _NN)
    gt = gu[:, 0:D_FF]
    act = gt * jax.nn.sigmoid(gt) * gu[:, D_FF:]
    x = x + g2 * _dg(act.astype(BF16), w2_ref[...], _NN)
    if not final:
        (o_ref,) = tail
        o_ref[...] = x
    else:
        fng_ref, oc_ref, ol_ref = tail
        y_out = _rms(x, fng_ref[...])

        @pl.when(is_ctx)
        def _():
            oc_ref[...] = y_out

        @pl.when(jnp.logical_not(is_ctx))
        def _():
            ol_ref[...] = y_out


def _merge_ffn_call(xs, mods_l, oa_ctx, oa_lat, y_f, y_b, bonus, g, of_ctx, of_lat, ln_g, ln_b, avg, nfg,
                    wo, wi, w2, l, final_g=None):
    final = final_g is not None
    split = isinstance(xs, tuple)
    xs = xs if split else (xs,)
    row = lambda i: (l, 0, 0)
    tile = lambda i: (i, 0)
    ctx_tile, lat_tile = _CTX_TILE, _LAT_TILE
    once = pl.Buffered(1)
    in_specs = _x_specs(split) + [
        pl.BlockSpec((None, 1, 6 * D_MODEL), lambda i: (_mod_row(i), 0, 0)),
        pl.BlockSpec((TM, ATTN_W), ctx_tile),
        pl.BlockSpec((TM, ATTN_W), lat_tile),
        pl.BlockSpec((TM, RWKV_W), tile),
        pl.BlockSpec((TM, RWKV_W), tile),
        pl.BlockSpec((TM, RWKV_W), tile),
        pl.BlockSpec((TM, RWKV_W), tile),
        pl.BlockSpec((TM, FNET_W), ctx_tile),
        pl.BlockSpec((TM, FNET_W), lat_tile),
        pl.BlockSpec((None, 1, RWKV_W), row),
        pl.BlockSpec((None, 1, RWKV_W), row),
        pl.BlockSpec((RWKV_W, RWKV_W), lambda i: (0, 0)),
        pl.BlockSpec((None, 1, D_MODEL), row),
        pl.BlockSpec((None, D_MODEL, D_MODEL), row, pipeline_mode=once),
        pl.BlockSpec((None, D_MODEL, 2 * D_FF), row, pipeline_mode=once),
        pl.BlockSpec((None, D_FF, D_MODEL), row, pipeline_mode=once),
    ]
    args = [*xs, mods_l, oa_ctx, oa_lat, y_f, y_b, bonus, g, of_ctx, of_lat, ln_g, ln_b, avg, nfg, wo, wi, w2]
    if final:
        in_specs.append(pl.BlockSpec((1, D_MODEL), lambda i: (0, 0)))
        args.append(final_g)
        out_specs = [pl.BlockSpec((TM, D_MODEL), ctx_tile), pl.BlockSpec((TM, D_MODEL), lat_tile)]
        out_shape = [jax.ShapeDtypeStruct((N_CTX, D_MODEL), F32), jax.ShapeDtypeStruct((N_LAT, D_MODEL), F32)]
    else:
        out_specs = pl.BlockSpec((TM, D_MODEL), tile)
        out_shape = jax.ShapeDtypeStruct((N_TOK, D_MODEL), F32)
    return pl.pallas_call(
        functools.partial(_merge_ffn_kernel, split, final),
        grid=(N_TILES,),
        in_specs=in_specs,
        out_specs=out_specs,
        out_shape=out_shape,
        compiler_params=_cparams(("arbitrary",)),
        name="merge_ffn_final" if final else "merge_ffn",
    )(*args)


def _block_diag2(w):
    z = jnp.zeros_like(w[:, 0])
    return jnp.concatenate([jnp.concatenate([w[:, 0], z], axis=2), jnp.concatenate([z, w[:, 1]], axis=2)], axis=1)


def kernel(x_prompt, x_sample, cache_k, cache_v, state_rwkv, c, c_ctx, w_ada, b_ada, norm_mix_g, norm_ffn_g,
           w_in, w_out, attn_sink, rwkv_shift_w, rwkv_w0, rwkv_w_up, rwkv_a0, rwkv_a_up, rwkv_g_up,
           rwkv_k_k, rwkv_k_a, rwkv_r_k, rwkv_ln_g, rwkv_ln_b, ffn_w_in, ffn_w_out, norm_final_g):
    x = (x_prompt.reshape(N_CTX, D_MODEL), x_sample.reshape(N_LAT, D_MODEL))
    cvec = jnp.concatenate([c_ctx[None, :], c, jnp.zeros((8 - 1 - N_LAT_SEQ, D_MODEL), F32)], axis=0)
    mods = _mods_call(cvec, w_ada, b_ada).reshape(DEPTH, 8, 1, 6 * D_MODEL)

    w_in_b = w_in.astype(BF16)
    w_out_b = w_out.astype(BF16)
    ffn_in_b = ffn_w_in.astype(BF16)
    ffn_out_b = ffn_w_out.astype(BF16)
    wup_bd = _block_diag2(rwkv_w_up)
    aup_bd = _block_diag2(rwkv_a_up)
    row3 = lambda a: a.reshape(DEPTH, 1, -1)
    nmg, nfg = row3(norm_mix_g), row3(norm_ffn_g)
    k_k, k_a, r_k = row3(rwkv_k_k), row3(rwkv_k_a), row3(rwkv_r_k)
    ln_g, ln_b = row3(rwkv_ln_g), row3(rwkv_ln_b)
    kc = cache_k.reshape(N_LAT_SEQ, DEPTH, PAST_LEN, KV_W)
    vc = cache_v.reshape(N_LAT_SEQ, DEPTH, PAST_LEN, KV_W)
    s0 = jnp.concatenate([jnp.zeros((N_CTX_SEQ, DEPTH, 2, RWKV_W, HEAD_DIM), F32),
                          state_rwkv.reshape(N_LAT_SEQ, DEPTH, 2, RWKV_W, HEAD_DIM)], axis=0)

    cos, sin = _rope_tables()
    consts = _scan_consts()
    bmask = jnp.asarray(_head_block_mask(RWKV_W))
    ones_bd = bmask.astype(BF16)
    avg_bd = (bmask * (1.0 / HEAD_DIM)).astype(BF16)
    dft_ctx = _dft_tables(CTX_LEN)
    dft_lat = _dft_tables(LAT_LEN)

    new_k, new_v, new_s = [], [], []
    for l in range(DEPTH):
        q, k, v, u_r, u_f = _inproj_call(x, mods[l], nmg, w_in_b, l)
        new_k.append(k[:N_CTX].reshape(N_CTX_SEQ, CTX_LEN, N_KV_HEADS, HEAD_DIM))
        new_v.append(v[:N_CTX].reshape(N_CTX_SEQ, CTX_LEN, N_KV_HEADS, HEAD_DIM))

        oa_ctx = _ctx_attn_call(q, k, v, attn_sink, l)
        oa_lat = _lat_attn_call(q, k, v, kc, vc, cos, sin, attn_sink, l)

        r, vv, kn, g, bonus, lw, kd, b = _rwkv_pre_call(
            u_r, rwkv_shift_w, wup_bd, aup_bd, rwkv_g_up, rwkv_w0, rwkv_a0, k_k, k_a, r_k, ones_bd, l)
        y_f, y_b, sf, sb = _scan_call(r, vv, kn, lw, kd, b, s0, consts, l)
        new_s.append(_scan_final_states(sf, sb))

        of_ctx = _fnet_call(u_f, CTX_LEN, N_CTX_SEQ, 0, dft_ctx, "fnet_context")
        of_lat = _fnet_call(u_f, LAT_LEN, N_LAT_SEQ, N_CTX, dft_lat, "fnet_latent")

        last = l == DEPTH - 1
        x = _merge_ffn_call(x, mods[l], oa_ctx, oa_lat, y_f, y_b, bonus, g, of_ctx, of_lat, ln_g, ln_b, avg_bd, nfg,
                            w_out_b, ffn_in_b, ffn_out_b, l,
                            final_g=norm_final_g.reshape(1, D_MODEL) if last else None)

    y_ctx, y_lat = x
    return (y_ctx.reshape(N_CTX_SEQ, CTX_LEN, D_MODEL), y_lat.reshape(N_LAT_SEQ, LAT_LEN, D_MODEL),
            jnp.stack(new_k, axis=1), jnp.stack(new_v, axis=1), jnp.stack(new_s, axis=1))
```

```python
import functools

import numpy as np
import jax
import jax.numpy as jnp
from jax import lax
from jax.experimental import pallas as pl
from jax.experimental.pallas import tpu as pltpu

F32 = jnp.float32
BF16 = jnp.bfloat16

D_MODEL = 1024
N_CTX_SEQ = 16
CTX_LEN = 256
DEPTH = 4
N_LAT_SEQ = 4
LAT_LEN = 2048
PAST_LEN = 512
GRID_W = 64
HEAD_DIM = 64
N_Q_HEADS = 8
N_KV_HEADS = 2
Q_PER_KV = N_Q_HEADS // N_KV_HEADS
ATTN_W = N_Q_HEADS * HEAD_DIM
KV_W = N_KV_HEADS * HEAD_DIM
BLOCK = 128
ATTN_SCALE = 0.125
ROPE_THETA = 10000.0
NEG = -1e30
RWKV_HEADS = 4
RWKV_W = RWKV_HEADS * HEAD_DIM
DECAY_RANK = 64
ICLR_RANK = 64
GATE_RANK = 128
DECAY_SCALE = 0.6065306597126334
GN_EPS = 64e-5
RWKV_IN_W = 3 * RWKV_W + GATE_RANK + 2 * DECAY_RANK + 2 * ICLR_RANK
FNET_W = 4 * HEAD_DIM
IN_W = ATTN_W + 2 * KV_W + RWKV_IN_W + FNET_W
D_FF = 2816
RMS_EPS = 1e-6

N_CTX = N_CTX_SEQ * CTX_LEN
N_LAT = N_LAT_SEQ * LAT_LEN
N_TOK = N_CTX + N_LAT
TM = 256
N_TILES = N_TOK // TM
CTX_TILES = N_CTX // TM
LAT_TILES_PER_SEQ = LAT_LEN // TM
N_SEQ = N_CTX_SEQ + N_LAT_SEQ
CHUNK = 64
CHUNKS_PER_TILE = TM // CHUNK
HALO = 8
LAT_ATTN_HEADS = 1

VMEM_LIMIT = 56 * 1024 * 1024


def _cparams(sem):
    return pltpu.CompilerParams(dimension_semantics=sem, vmem_limit_bytes=VMEM_LIMIT)


def _mod_row(i):
    return jnp.where(i < CTX_TILES, 0, 1 + (i - CTX_TILES) // LAT_TILES_PER_SEQ)


def _split2(x):
    hi = x.astype(BF16)
    lo = (x - hi.astype(F32)).astype(BF16)
    return hi, lo


_NN = (((1,), (0,)), ((), ()))
_NT = (((1,), (1,)), ((), ()))


def _dg(a, b, dims):
    return lax.dot_general(a, b, dims, preferred_element_type=F32)


def _dot1(a, b, dims=_NN):
    return _dg(a.astype(BF16), b.astype(BF16), dims)


def _dot3(a, b, dims=_NN):
    ah, al = _split2(a)
    bh, bl = _split2(b)
    return _dg(ah, bh, dims) + (_dg(ah, bl, dims) + _dg(al, bh, dims))


def _dot_exact_rhs(a, b_bf16):
    ah, al = _split2(a)
    return _dg(ah, b_bf16, _NN) + _dg(al, b_bf16, _NN)


def _dot_exact_lhs(a_bf16, b):
    bh, bl = _split2(b)
    return _dg(a_bf16, bh, _NN) + _dg(a_bf16, bl, _NN)


def _mods_kernel(c_ref, w_ref, b_ref, o_ref):
    c = c_ref[...]
    s = c * jax.nn.sigmoid(c)
    o_ref[...] = _dot1(s, w_ref[...]) + b_ref[...]


def _mods_call(cvec, w_ada, b_ada):
    tn = 1536
    return pl.pallas_call(
        _mods_kernel,
        grid=(DEPTH, 6 * D_MODEL // tn),
        in_specs=[
            pl.BlockSpec((8, D_MODEL), lambda l, j: (0, 0)),
            pl.BlockSpec((None, D_MODEL, tn), lambda l, j: (l, 0, j)),
            pl.BlockSpec((None, 1, tn), lambda l, j: (l, 0, j)),
        ],
        out_specs=pl.BlockSpec((None, 8, tn), lambda l, j: (l, 0, j)),
        out_shape=jax.ShapeDtypeStruct((DEPTH, 8, 6 * D_MODEL), F32),
        compiler_params=_cparams(("parallel", "parallel")),
        name="adaln_mods",
    )(cvec, w_ada, b_ada.reshape(DEPTH, 1, 6 * D_MODEL))


def _rms(x, g):
    return x * lax.rsqrt(jnp.mean(x * x, axis=-1, keepdims=True) + RMS_EPS) * g


def _tile_of_split(i, ctx_ref, lat_ref):
    return jnp.where(i < CTX_TILES, ctx_ref[...], lat_ref[...])


_CTX_TILE = lambda i: (jnp.minimum(i, CTX_TILES - 1), 0)
_LAT_TILE = lambda i: (jnp.maximum(i - CTX_TILES, 0), 0)


def _x_specs(split):
    if split:
        return [pl.BlockSpec((TM, D_MODEL), _CTX_TILE), pl.BlockSpec((TM, D_MODEL), _LAT_TILE)]
    return [pl.BlockSpec((TM, D_MODEL), lambda i: (i, 0))]


def _head_block_mask(n):
    r = np.arange(n)[:, None] // HEAD_DIM
    c = np.arange(n)[None, :] // HEAD_DIM
    return (r == c).astype(np.float32)


def _rwkv_token_terms(u, prev_row, next_row, sw_ref, wup_ref, aup_ref, gup_ref, w0_ref, a0_ref,
                      kk_ref, ka_ref, rk_ref, ones_ref, r_out, v_out, kn_out, g_out, bonus_out, lw_out, kd_out, b_out):
    row = lax.broadcasted_iota(jnp.int32, u.shape, 0)
    u_dn = jnp.where(row == 0, prev_row, pltpu.roll(u, 1, 0))
    u_up = jnp.where(row == TM - 1, next_row, pltpu.roll(u, TM - 1, 0))
    us = u_dn * sw_ref[0:1, :] + u * sw_ref[1:2, :] + u_up * sw_ref[2:3, :]

    r = us[:, 0:RWKV_W]
    k = us[:, RWKV_W:2 * RWKV_W]
    v = us[:, 2 * RWKV_W:3 * RWKV_W]
    o = 3 * RWKV_W
    gd = us[:, o:o + GATE_RANK]
    wd = us[:, o + GATE_RANK:o + GATE_RANK + 2 * DECAY_RANK]
    ad = us[:, o + GATE_RANK + 2 * DECAY_RANK:RWKV_IN_W]

    w_pre = _dot1(jnp.tanh(wd), wup_ref[...])
    a_pre = _dot1(ad, aup_ref[...])
    g_out[...] = _dot1(jax.nn.sigmoid(gd), gup_ref[...])

    ones_bd = ones_ref[...]
    kn = k * kk_ref[...]
    kn = kn * lax.rsqrt(_dot_exact_rhs(kn * kn, ones_bd) + 1e-12)
    r_out[...] = r
    v_out[...] = v
    kn_out[...] = kn
    bonus_out[...] = _dot_exact_rhs(r * k * rk_ref[...], ones_bd) * v
    ka = ka_ref[...]
    for d in range(2):
        sl = slice(d * RWKV_W, (d + 1) * RWKV_W)
        a_d = jax.nn.sigmoid(a0_ref[d:d + 1, :] + a_pre[:, sl])
        lw_out[d] = -DECAY_SCALE * jax.nn.sigmoid(w0_ref[d:d + 1, :] + w_pre[:, sl])
        kd_out[d] = k * (1.0 + (a_d - 1.0) * ka)
        b_out[d] = kn * a_d


RWKV_COL0 = ATTN_W + 2 * KV_W
N_RWKV_PARAMS = 10


def _inproj_kernel(split, *refs):
    n_x = 2 if split else 1
    i = pl.program_id(0)
    x = _tile_of_split(i, *refs[:2]) if split else refs[0][...]
    xp_ref, xn_ref, mod_ref, g_ref, w_ref = refs[n_x:n_x + 5]
    rwkv_params = refs[n_x + 5:n_x + 5 + N_RWKV_PARAMS]
    q_ref, k_ref, v_ref, uf_ref = refs[n_x + 5 + N_RWKV_PARAMS:n_x + 9 + N_RWKV_PARAMS]
    rwkv_outs = refs[n_x + 9 + N_RWKV_PARAMS:]

    shift = mod_ref[:, 0:D_MODEL]
    scale = mod_ref[:, D_MODEL:2 * D_MODEL]
    x_all = jnp.concatenate([x, xp_ref[...], xn_ref[...]], axis=0)
    h = _rms(x_all, g_ref[...]) * (1.0 + scale) + shift
    o = _dg(h.astype(BF16), w_ref[...], _NN)
    q_ref[...] = o[:TM, 0:ATTN_W]
    k_ref[...] = o[:TM, ATTN_W:ATTN_W + KV_W]
    v_ref[...] = o[:TM, ATTN_W + KV_W:RWKV_COL0]
    uf_ref[...] = o[:TM, RWKV_COL0 + RWKV_IN_W:IN_W]

    j = (i - CTX_TILES) % LAT_TILES_PER_SEQ
    has_prev = jnp.where((i >= CTX_TILES) & (j != 0), 1.0, 0.0)
    has_next = jnp.where((i >= CTX_TILES) & (j != LAT_TILES_PER_SEQ - 1), 1.0, 0.0)
    u_all = o[:, RWKV_COL0:RWKV_COL0 + RWKV_IN_W]
    prev_row = u_all[TM + HALO - 1:TM + HALO, :] * has_prev
    next_row = u_all[TM + HALO:TM + HALO + 1, :] * has_next
    _rwkv_token_terms(u_all[:TM], prev_row, next_row, *rwkv_params, *rwkv_outs)


def _inproj_call(xs, mods_l, g, w_bf16, rwkv_params, l):
    split = isinstance(xs, tuple)
    xs = xs if split else (xs,)
    hb = TM // HALO
    if split:
        halo_src = xs[1]
        last = N_LAT // HALO - 1
        prev_blk = lambda i: (jnp.clip((i - CTX_TILES) * hb - 1, 0, last), 0)
        next_blk = lambda i: (jnp.clip((i - CTX_TILES + 1) * hb, 0, last), 0)
    else:
        halo_src = xs[0]
        last = N_TOK // HALO - 1
        prev_blk = lambda i: (jnp.maximum(i * hb - 1, 0), 0)
        next_blk = lambda i: (jnp.minimum((i + 1) * hb, last), 0)
    row = lambda i: (l, 0, 0)
    widths = (ATTN_W, KV_W, KV_W, FNET_W)
    tok = jax.ShapeDtypeStruct((N_TOK, RWKV_W), F32)
    tok2 = jax.ShapeDtypeStruct((2, N_TOK, RWKV_W), F32)
    return pl.pallas_call(
        functools.partial(_inproj_kernel, split),
        grid=(N_TILES,),
        in_specs=_x_specs(split) + [
            pl.BlockSpec((HALO, D_MODEL), prev_blk),
            pl.BlockSpec((HALO, D_MODEL), next_blk),
            pl.BlockSpec((None, 1, 6 * D_MODEL), lambda i: (_mod_row(i), 0, 0)),
            pl.BlockSpec((None, 1, D_MODEL), row),
            pl.BlockSpec((None, D_MODEL, IN_W), row),
            pl.BlockSpec((None, 3, RWKV_IN_W), row),
            pl.BlockSpec((None, 2 * DECAY_RANK, 2 * RWKV_W), row),
            pl.BlockSpec((None, 2 * ICLR_RANK, 2 * RWKV_W), row),
            pl.BlockSpec((None, GATE_RANK, RWKV_W), row),
            pl.BlockSpec((None, 2, RWKV_W), row),
            pl.BlockSpec((None, 2, RWKV_W), row),
            pl.BlockSpec((None, 1, RWKV_W), row),
            pl.BlockSpec((None, 1, RWKV_W), row),
            pl.BlockSpec((None, 1, RWKV_W), row),
            pl.BlockSpec((RWKV_W, RWKV_W), lambda i: (0, 0)),
        ],
        out_specs=[pl.BlockSpec((TM, w), lambda i: (i, 0)) for w in widths]
        + [pl.BlockSpec((TM, RWKV_W), lambda i: (i, 0))] * 5
        + [pl.BlockSpec((2, TM, RWKV_W), lambda i: (0, i, 0))] * 3,
        out_shape=[jax.ShapeDtypeStruct((N_TOK, w), F32) for w in widths] + [tok] * 5 + [tok2] * 3,
        compiler_params=_cparams(("arbitrary",)),
        name="in_projection",
    )(*xs, halo_src, halo_src, mods_l, g, w_bf16, *rwkv_params)


def _softmax_pv(s, sink_col, v):
    m = jnp.maximum(jnp.max(s, axis=-1, keepdims=True), sink_col)
    e = jnp.exp(s - m)
    denom = jnp.sum(e, axis=-1, keepdims=True) + jnp.exp(sink_col - m)
    return _dot1(e, v) / denom


def _stack_q_heads(q, kh, rows):
    return jnp.concatenate(
        [q[:, (kh * Q_PER_KV + g) * HEAD_DIM:(kh * Q_PER_KV + g + 1) * HEAD_DIM] for g in range(Q_PER_KV)],
        axis=0)


def _sink_col(sink_ref, l, kh, rows):
    return jnp.concatenate(
        [jnp.full((rows, 1), sink_ref[l, kh * Q_PER_KV + g], F32) for g in range(Q_PER_KV)], axis=0)


def _ctx_attn_kernel(l, sink_ref, q_ref, k_ref, v_ref, o_ref):
    q = q_ref[...]
    k = k_ref[...]
    v = v_ref[...]
    half = CTX_LEN // 2
    pieces = [(h, r) for h in range(N_Q_HEADS) for r in range(2)]
    kb = k.astype(BF16)
    vb = v.astype(BF16)
    kv = lambda a, h: a[:, (h // Q_PER_KV) * HEAD_DIM:(h // Q_PER_KV + 1) * HEAD_DIM]
    qs = [(q[r * half:(r + 1) * half, h * HEAD_DIM:(h + 1) * HEAD_DIM] * ATTN_SCALE).astype(BF16) for h, r in pieces]
    s = [_dg(qs[n], kv(kb, h), _NT) for n, (h, r) in enumerate(pieces)]
    m = [jnp.maximum(jnp.max(s[n], axis=-1, keepdims=True), sink_ref[l, h]) for n, (h, r) in enumerate(pieces)]
    e = [jnp.exp(s[n] - m[n]) for n in range(len(pieces))]
    den = [jnp.sum(e[n], axis=-1, keepdims=True) + jnp.exp(sink_ref[l, h] - m[n]) for n, (h, r) in enumerate(pieces)]
    o = [_dg(e[n].astype(BF16), kv(vb, h), _NN) / den[n] for n, (h, r) in enumerate(pieces)]
    for n, (h, r) in enumerate(pieces):
        o_ref[r * half:(r + 1) * half, h * HEAD_DIM:(h + 1) * HEAD_DIM] = o[n]


def _ctx_attn_call(q, k, v, sink, l):
    return pl.pallas_call(
        functools.partial(_ctx_attn_kernel, l),
        grid=(N_CTX_SEQ,),
        in_specs=[
            pl.BlockSpec(memory_space=pltpu.SMEM),
            pl.BlockSpec((CTX_LEN, ATTN_W), lambda b: (b, 0)),
            pl.BlockSpec((CTX_LEN, KV_W), lambda b: (b, 0)),
            pl.BlockSpec((CTX_LEN, KV_W), lambda b: (b, 0)),
        ],
        out_specs=pl.BlockSpec((CTX_LEN, ATTN_W), lambda b: (b, 0)),
        out_shape=jax.ShapeDtypeStruct((N_CTX, ATTN_W), F32),
        compiler_params=_cparams(("parallel",)),
        name="context_attention",
    )(sink, q, k, v)


def _rope(x, cos, sin):
    lane = lax.broadcasted_iota(jnp.int32, x.shape, 1)
    partner = jnp.where((lane & 16) == 0, pltpu.roll(x, 128 - 16, 1), pltpu.roll(x, 16, 1))
    return x * cos + partner * sin


def _lat_attn_kernel(l, sink_ref, q_ref, k_ref, v_ref, kc_ref, vc_ref, cos_ref, sin_ref, o_ref):
    n = pl.program_id(1)
    nb = LAT_LEN // BLOCK
    r0 = pl.multiple_of(n * BLOCK, BLOCK)
    cos_q = cos_ref[pl.ds(r0, BLOCK), :]
    sin_q = sin_ref[pl.ds(r0, BLOCK), :]
    q = jnp.concatenate(
        [_rope(q_ref[:, j * 128:(j + 1) * 128], cos_q, sin_q) for j in range(ATTN_W // 128)], axis=1)

    ks, vs = [], []
    for dj in (-1, 0, 1):
        rj = pl.multiple_of(jnp.clip(n + dj, 0, nb - 1) * BLOCK, BLOCK)
        ks.append(_rope(k_ref[pl.ds(rj, BLOCK), :], cos_ref[pl.ds(rj, BLOCK), :], sin_ref[pl.ds(rj, BLOCK), :]))
        vs.append(v_ref[pl.ds(rj, BLOCK), :])
    k_all = jnp.concatenate(ks + [kc_ref[...]], axis=0)
    v_all = jnp.concatenate(vs + [vc_ref[...]], axis=0)

    g = LAT_ATTN_HEADS
    pieces = [tuple(range(h0, h0 + g)) for h0 in range(0, N_Q_HEADS, g)]
    kb = k_all.astype(BF16)
    vb = v_all.astype(BF16)
    kv = lambda a, hs: a[:, (hs[0] // Q_PER_KV) * HEAD_DIM:(hs[0] // Q_PER_KV + 1) * HEAD_DIM]
    kj = lax.broadcasted_iota(jnp.int32, (g * BLOCK, BLOCK), 1)
    qi = lax.broadcasted_iota(jnp.int32, (g * BLOCK, BLOCK), 0) & (BLOCK - 1)
    keep_prev = (kj >= qi) & (n > 0)
    keep_next = (kj <= qi) & (n < nb - 1)
    sinks = [jnp.concatenate([jnp.full((BLOCK, 1), sink_ref[l, h], F32) for h in hs], axis=0) for hs in pieces]

    qs = [jnp.concatenate([q[:, h * HEAD_DIM:(h + 1) * HEAD_DIM] for h in hs], axis=0) for hs in pieces]
    s = [_dg((qs[i] * ATTN_SCALE).astype(BF16), kv(kb, hs), _NT) for i, hs in enumerate(pieces)]
    s = [jnp.concatenate([jnp.where(keep_prev, s[i][:, 0:BLOCK], NEG), s[i][:, BLOCK:2 * BLOCK],
                          jnp.where(keep_next, s[i][:, 2 * BLOCK:3 * BLOCK], NEG), s[i][:, 3 * BLOCK:]], axis=1)
         for i in range(len(pieces))]
    m = [jnp.maximum(jnp.max(s[i], axis=-1, keepdims=True), sinks[i]) for i in range(len(pieces))]
    e = [jnp.exp(s[i] - m[i]) for i in range(len(pieces))]
    den = [jnp.sum(e[i], axis=-1, keepdims=True) + jnp.exp(sinks[i] - m[i]) for i in range(len(pieces))]
    o = [_dg(e[i].astype(BF16), kv(vb, hs), _NN) / den[i] for i, hs in enumerate(pieces)]
    for i, hs in enumerate(pieces):
        for j, h in enumerate(hs):
            o_ref[:, h * HEAD_DIM:(h + 1) * HEAD_DIM] = o[i][j * BLOCK:(j + 1) * BLOCK]


def _lat_attn_call(q, k, v, kc, vc, cos, sin, sink, l):
    nb = LAT_LEN // BLOCK
    q_off = N_CTX // BLOCK
    kv_off = N_CTX // LAT_LEN
    return pl.pallas_call(
        functools.partial(_lat_attn_kernel, l),
        grid=(N_LAT_SEQ, nb),
        in_specs=[
            pl.BlockSpec(memory_space=pltpu.SMEM),
            pl.BlockSpec((BLOCK, ATTN_W), lambda b, n: (q_off + b * nb + n, 0)),
            pl.BlockSpec((LAT_LEN, KV_W), lambda b, n: (kv_off + b, 0)),
            pl.BlockSpec((LAT_LEN, KV_W), lambda b, n: (kv_off + b, 0)),
            pl.BlockSpec((None, None, PAST_LEN, KV_W), lambda b, n: (b, l, 0, 0)),
            pl.BlockSpec((None, None, PAST_LEN, KV_W), lambda b, n: (b, l, 0, 0)),
            pl.BlockSpec((LAT_LEN, 128), lambda b, n: (0, 0)),
            pl.BlockSpec((LAT_LEN, 128), lambda b, n: (0, 0)),
        ],
        out_specs=pl.BlockSpec((BLOCK, ATTN_W), lambda b, n: (b * nb + n, 0)),
        out_shape=jax.ShapeDtypeStruct((N_LAT, ATTN_W), F32),
        compiler_params=_cparams(("parallel", "arbitrary")),
        name="latent_attention",
    )(sink, q, k, v, kc, vc, cos, sin)


def _rope_tables():
    t = np.arange(LAT_LEN)
    lane = np.arange(128)
    d = lane % HEAD_DIM
    pos = np.where(d[None, :] < HEAD_DIM // 2, (t // GRID_W)[:, None], (t % GRID_W)[:, None]).astype(np.float32)
    quarter = HEAD_DIM // 4
    inv = (np.float32(ROPE_THETA) ** (-(np.arange(quarter, dtype=np.float32)) / np.float32(quarter))).astype(np.float32)
    ang = pos * inv[d % quarter][None, :]
    sign = np.where((lane & 16) == 0, -1.0, 1.0)[None, :]
    return jnp.asarray(np.cos(ang), F32), jnp.asarray(np.sin(ang) * sign, F32)


PAIR_W = 2 * HEAD_DIM
N_PAIRS = RWKV_HEADS // 2
_TN = (((0,), (0,)), ((), ()))


def _pair_stack(x, pmask):
    return jnp.concatenate([x.astype(BF16)] * 2, axis=0) * pmask


def _hdot(m, x, pmask, dims):
    return jnp.concatenate(
        [_dg(m[:, p * PAIR_W:(p + 1) * PAIR_W], _pair_stack(x[:, p * PAIR_W:(p + 1) * PAIR_W], pmask), dims)
         for p in range(N_PAIRS)], axis=1)


def _scan_prepare(r, v, kn, lw, kd, b, cs, tot, strict, incl, eye, pmask):
    linc = _dot_exact_lhs(cs, lw)
    ltot = _dot_exact_lhs(tot, lw)
    v_all = v.astype(BF16)
    r_t = r * jnp.exp(linc)
    a_t = (-kn * jnp.exp(linc - lw)).astype(BF16)
    e_neg = jnp.exp(-linc)
    b_t = (b * e_neg).astype(BF16)
    k_t = (kd * e_neg).astype(BF16)
    e_rem = jnp.exp(ltot - linc)
    b_p = (b * e_rem).astype(BF16)
    k_p = (kd * e_rem).astype(BF16)
    dec = jnp.exp(ltot)

    chunks = range(CHUNKS_PER_TILE)
    sls = [slice(c * CHUNK, (c + 1) * CHUNK) for c in chunks]
    hd = lambda m, x, dims: _hdot(m, x, pmask, dims)
    ar = [jnp.concatenate([a_t[sl], r_t[sl].astype(BF16)], axis=0) for sl in sls]
    m_b = [hd(ar[c], b_t[sls[c]], _NT) for c in chunks]
    m_k = [hd(ar[c], k_t[sls[c]], _NT) for c in chunks]
    a_ab = [m_b[c][:CHUNK] * strict for c in chunks]
    p_rb = [(m_b[c][CHUNK:] * incl).astype(BF16) for c in chunks]
    ak_rk = [jnp.concatenate([m_k[c][:CHUNK] * strict, m_k[c][CHUNK:] * incl], axis=0).astype(BF16)
             for c in chunks]

    tinv = [eye + a_ab[c] for c in chunks]
    p = [a_ab[c].astype(BF16) for c in chunks]
    p = [hd(p[c], p[c], _NN).astype(BF16) for c in chunks]
    for _ in range(4):
        pt = [hd(jnp.concatenate([p[c], tinv[c].astype(BF16)], axis=0), p[c], _NN) for c in chunks]
        p = [pt[c][:CHUNK].astype(BF16) for c in chunks]
        tinv = [tinv[c] + pt[c][CHUNK:] for c in chunks]
    tinv = [(tinv[c] + hd(tinv[c].astype(BF16), p[c], _NN)).astype(BF16) for c in chunks]

    m_v = [hd(ak_rk[c], v_all[sls[c]], _NN) for c in chunks]
    w = [hd(tinv[c], a_t[sls[c]], _NN) for c in chunks]
    u0 = [hd(tinv[c], m_v[c][:CHUNK], _NN) for c in chunks]
    qp = [r_t[sls[c]] + hd(p_rb[c], w[c], _NN) for c in chunks]
    y0 = [hd(p_rb[c], u0[c], _NN) + m_v[c][CHUNK:] for c in chunks]
    pr = [slice(p * PAIR_W, (p + 1) * PAIR_W) for p in range(N_PAIRS)]
    pmask_f = pmask.astype(F32)
    bk = [jnp.concatenate([b_p[sl], k_p[sl]], axis=0) for sl in sls]
    uv = [jnp.concatenate([u0[c].astype(BF16), v_all[sls[c]]], axis=0) for c in chunks]
    wb = [[_dg(w[c][:, q].astype(BF16), b_p[sls[c]][:, q], _TN) * pmask_f for q in pr] for c in chunks]
    gs = [[_dg(uv[c][:, q], bk[c][:, q], _TN) * pmask_f for q in pr] for c in chunks]
    qpb = [qp[c].astype(BF16) for c in chunks]
    decs = [dec[c * CHUNK:c * CHUNK + 1, :] for c in chunks]
    return qpb, y0, wb, gs, decs


def _scan_chain_step(c, s, prep, y_ref):
    qpb, y0, wb, gs, decs = prep
    pr = [slice(p * PAIR_W, (p + 1) * PAIR_W) for p in range(N_PAIRS)]
    sb = [s[p].astype(BF16) for p in range(N_PAIRS)]
    y_ref[c * CHUNK:(c + 1) * CHUNK, :] = y0[c] + jnp.concatenate(
        [_dg(qpb[c][:, pr[p]], sb[p], _NT) for p in range(N_PAIRS)], axis=1)
    return [s[p] * decs[c][:, pr[p]] + (_dg(sb[p], wb[c][p].astype(BF16), _NN) + gs[c][p]) for p in range(N_PAIRS)]


def _scan_kernel(rf_ref, vf_ref, knf_ref, lwf_ref, kdf_ref, bf_ref, s0f_ref,
                 rb_ref, vb_ref, knb_ref, lwb_ref, kdb_ref, bb_ref, s0b_ref,
                 cs_ref, tot_ref, strict_ref, incl_ref, eye_ref, pmask_ref,
                 yf_ref, yb_ref, sfinf_ref, sfinb_ref, sf_scr, sb_scr):
    i = pl.program_id(0)
    tb = N_TILES - 1 - i
    first_f = (i < CTX_TILES) | ((i - CTX_TILES) % LAT_TILES_PER_SEQ == 0)
    first_b = (tb < CTX_TILES) | ((tb - CTX_TILES) % LAT_TILES_PER_SEQ == LAT_TILES_PER_SEQ - 1)
    pmask_f = pmask_ref[...]
    pmask = pmask_f.astype(BF16)

    def load_state(s0_ref, scr):
        for p in range(N_PAIRS):
            scr[p] = jnp.concatenate([s0_ref[p * PAIR_W:(p + 1) * PAIR_W, :]] * 2, axis=1) * pmask_f

    pl.when(first_f)(lambda: load_state(s0f_ref, sf_scr))
    pl.when(first_b)(lambda: load_state(s0b_ref, sb_scr))

    tot = tot_ref[...]
    eye = eye_ref[...]
    prep_f = _scan_prepare(rf_ref[...], vf_ref[...], knf_ref[...], lwf_ref[...], kdf_ref[...], bf_ref[...],
                           cs_ref[0], tot, strict_ref[0], incl_ref[0], eye, pmask)
    prep_b = _scan_prepare(rb_ref[...], vb_ref[...], knb_ref[...], lwb_ref[...], kdb_ref[...], bb_ref[...],
                           cs_ref[1], tot, strict_ref[1], incl_ref[1], eye, pmask)
    s_f = [sf_scr[p] for p in range(N_PAIRS)]
    s_b = [sb_scr[p] for p in range(N_PAIRS)]
    for c in range(CHUNKS_PER_TILE):
        s_f = _scan_chain_step(c, s_f, prep_f, yf_ref)
        s_b = _scan_chain_step(CHUNKS_PER_TILE - 1 - c, s_b, prep_b, yb_ref)
    for p in range(N_PAIRS):
        sf_scr[p] = s_f[p]
        sb_scr[p] = s_b[p]
        sfinf_ref[p] = s_f[p]
        sfinb_ref[p] = s_b[p]


def _scan_call(r, v, kn, lw, kd, b, s0, consts, l):
    cs, tot, strict, incl, eye, pmask = consts
    rev = lambda i: N_TILES - 1 - i
    seq_of = lambda t: jnp.where(t < CTX_TILES, t, CTX_TILES + (t - CTX_TILES) // LAT_TILES_PER_SEQ)
    const2 = lambda i: (0, 0)
    const3 = lambda i: (0, 0, 0)
    tok_f = pl.BlockSpec((TM, RWKV_W), lambda i: (i, 0))
    tok_b = pl.BlockSpec((TM, RWKV_W), lambda i: (rev(i), 0))
    tokd_f = pl.BlockSpec((None, TM, RWKV_W), lambda i: (0, i, 0))
    tokd_b = pl.BlockSpec((None, TM, RWKV_W), lambda i: (1, rev(i), 0))
    s0_f = pl.BlockSpec((None, None, None, RWKV_W, HEAD_DIM), lambda i: (seq_of(i), l, 0, 0, 0))
    s0_b = pl.BlockSpec((None, None, None, RWKV_W, HEAD_DIM), lambda i: (seq_of(rev(i)), l, 1, 0, 0))
    sfin = jax.ShapeDtypeStruct((CTX_TILES + 1, N_PAIRS, PAIR_W, PAIR_W), F32)
    y = jax.ShapeDtypeStruct((N_TOK, RWKV_W), F32)
    state = pltpu.VMEM((N_PAIRS, PAIR_W, PAIR_W), F32)
    return pl.pallas_call(
        _scan_kernel,
        grid=(N_TILES,),
        in_specs=[
            tok_f, tok_f, tok_f, tokd_f, tokd_f, tokd_f, s0_f,
            tok_b, tok_b, tok_b, tokd_b, tokd_b, tokd_b, s0_b,
            pl.BlockSpec((2, TM, TM), const3),
            pl.BlockSpec((TM, TM), const2),
            pl.BlockSpec((2, CHUNK, RWKV_W), const3),
            pl.BlockSpec((2, CHUNK, RWKV_W), const3),
            pl.BlockSpec((CHUNK, RWKV_W), const2),
            pl.BlockSpec((PAIR_W, PAIR_W), const2),
        ],
        out_specs=[
            tok_f, tok_b,
            pl.BlockSpec((None, N_PAIRS, PAIR_W, PAIR_W), lambda i: (jnp.minimum(i, CTX_TILES), 0, 0, 0)),
            pl.BlockSpec((None, N_PAIRS, PAIR_W, PAIR_W), lambda i: (jnp.minimum(rev(i), CTX_TILES), 0, 0, 0)),
        ],
        out_shape=[y, y, sfin, sfin],
        scratch_shapes=[state, state],
        compiler_params=_cparams(("arbitrary",)),
        name="rwkv_scan",
    )(r, v, kn, lw, kd, b, s0, r, v, kn, lw, kd, b, s0, cs, tot, strict, incl, eye, pmask)


def _scan_final_states(sf, sb):
    fin = jnp.stack([sf[:N_CTX_SEQ], sb[:N_CTX_SEQ]], axis=1)
    fin = fin.reshape(N_CTX_SEQ, 2, N_PAIRS, 2, HEAD_DIM, 2, HEAD_DIM)
    fin = jnp.stack([fin[:, :, :, h, :, h, :] for h in range(2)], axis=3)
    return fin.reshape(N_CTX_SEQ, 2, RWKV_HEADS, HEAD_DIM, HEAD_DIM)


def _scan_consts():
    t = np.arange(TM)
    same_chunk = (t[:, None] // CHUNK) == (t[None, :] // CHUNK)
    cs_f = same_chunk & (t[None, :] <= t[:, None])
    cs_b = same_chunk & (t[None, :] >= t[:, None])
    cs = np.stack([cs_f, cs_b]).astype(np.float32)
    tot = same_chunk.astype(np.float32)
    q = np.arange(CHUNK)[:, None]
    kcol = np.arange(RWKV_W)[None, :] % CHUNK
    strict = np.stack([kcol < q, kcol > q]).astype(np.float32)
    incl = np.stack([kcol <= q, kcol >= q]).astype(np.float32)
    eye = (kcol == q).astype(np.float32)
    return (jnp.asarray(cs, BF16), jnp.asarray(tot, BF16), jnp.asarray(strict), jnp.asarray(incl),
            jnp.asarray(eye), jnp.asarray(_head_block_mask(PAIR_W)))


FNET_IN_ROWS = N_CTX


def _fnet_kernel(seq_len, n_in, *refs):
    z_refs = refs[:n_in]
    ch_ref, cl_ref, th_ref, o_ref, z_scr = refs[n_in:]
    b = pl.program_id(1)

    @pl.when((pl.program_id(0) == 0) & (b == 0))
    def _():
        per_in = FNET_IN_ROWS // seq_len
        for n, z_ref in enumerate(z_refs):
            zh, zl = _split2(z_ref[...])
            for half in range(2):
                ch = ch_ref[half]
                cl = cl_ref[half]
                zc = (_dg(zh, ch, _NN) + (_dg(zh, cl, _NN) + _dg(zl, ch, _NN))).astype(BF16)
                for s in range(per_in):
                    z_scr[n * per_in + s, half * seq_len:(half + 1) * seq_len, :] = zc[s * seq_len:(s + 1) * seq_len]

    o_ref[...] = _dg(th_ref[...], z_scr[b], _NN)


def _fnet_call(u_f, seq_len, n_seq, row_off, tables, name):
    ch, cl, th = tables
    tt = 256
    nt = seq_len // tt
    n_in = n_seq * seq_len // FNET_IN_ROWS
    in_off = row_off // FNET_IN_ROWS
    z_specs = [pl.BlockSpec((FNET_IN_ROWS, FNET_W), functools.partial(lambda n, j, b: (in_off + n, 0), n))
               for n in range(n_in)]
    return pl.pallas_call(
        functools.partial(_fnet_kernel, seq_len, n_in),
        grid=(nt, n_seq),
        in_specs=z_specs + [
            pl.BlockSpec((2, FNET_W, FNET_W), lambda j, b: (0, 0, 0)),
            pl.BlockSpec((2, FNET_W, FNET_W), lambda j, b: (0, 0, 0)),
            pl.BlockSpec((tt, 2 * seq_len), lambda j, b: (j, 0)),
        ],
        out_specs=pl.BlockSpec((tt, FNET_W), lambda j, b: (b * nt + j, 0)),
        out_shape=jax.ShapeDtypeStruct((n_seq * seq_len, FNET_W), F32),
        scratch_shapes=[pltpu.VMEM((n_seq, 2 * seq_len, FNET_W), BF16)],
        compiler_params=_cparams(("arbitrary", "arbitrary")),
        name=name,
    )(*([u_f] * n_in), ch, cl, th)


DFT_SPLIT = 32


def _dft_tables(seq_len):
    c = np.arange(FNET_W)
    ang_c = 2.0 * np.pi * ((c[:, None] % HEAD_DIM) * (c[None, :] % HEAD_DIM) % HEAD_DIM) / HEAD_DIM
    scale = 1.0 / np.sqrt(float(seq_len) * HEAD_DIM)
    blk = _head_block_mask(FNET_W)
    ch, cl = _split2(jnp.asarray(np.stack([np.cos(ang_c) * blk * scale, np.sin(ang_c) * blk * scale]), F32))

    n1 = seq_len // DFT_SPLIT
    f = np.arange(seq_len)
    a1 = 2.0 * np.pi * ((np.arange(n1)[:, None] * f[None, :]) % n1) / n1
    a2 = 2.0 * np.pi * ((np.arange(DFT_SPLIT)[:, None] * f[None, :]) % seq_len) / seq_len
    c1, s1 = jnp.asarray(np.cos(a1), F32)[:, None, :], jnp.asarray(np.sin(a1), F32)[:, None, :]
    c2, s2 = jnp.asarray(np.cos(a2), F32)[None, :, :], jnp.asarray(np.sin(a2), F32)[None, :, :]
    cos_t = (c1 * c2 - s1 * s2).reshape(seq_len, seq_len)
    sin_t = (s1 * c2 + c1 * s2).reshape(seq_len, seq_len)
    th = jnp.concatenate([cos_t, -sin_t], axis=1).astype(BF16)
    return ch, cl, th


def _merge_ffn_kernel(split, final, *refs):
    i = pl.program_id(0)
    n_x = 2 if split else 1
    x_in = _tile_of_split(i, *refs[:2]) if split else refs[0][...]
    (mod_ref, oac_ref, oal_ref, yf_ref, yb_ref, bonus_ref, g_ref, ofc_ref, ofl_ref,
     lng_ref, lnb_ref, avg_ref, nfg_ref, wo_ref, wi_ref, w2_ref) = refs[n_x:n_x + 16]
    tail = refs[n_x + 16:]
    is_ctx = i < CTX_TILES
    o_attn = _tile_of_split(i, oac_ref, oal_ref)
    o_fnet = _tile_of_split(i, ofc_ref, ofl_ref)
    avg = avg_ref[...]
    y = yf_ref[...] + yb_ref[...]
    yc = y - _dot_exact_rhs(y, avg)
    var = _dot_exact_rhs(yc * yc, avg)
    yn = yc * lax.rsqrt(var + GN_EPS) * lng_ref[...] + lnb_ref[...]
    o_rwkv = (yn + bonus_ref[...]) * g_ref[...]

    o = (_dg(o_attn.astype(BF16), wo_ref[0:ATTN_W, :], _NN)
         + _dg(o_rwkv.astype(BF16), wo_ref[ATTN_W:ATTN_W + RWKV_W, :], _NN)
         + _dg(o_fnet.astype(BF16), wo_ref[ATTN_W + RWKV_W:, :], _NN))
    g1 = mod_ref[:, 2 * D_MODEL:3 * D_MODEL]
    sh2 = mod_ref[:, 3 * D_MODEL:4 * D_MODEL]
    sc2 = mod_ref[:, 4 * D_MODEL:5 * D_MODEL]
    g2 = mod_ref[:, 5 * D_MODEL:6 * D_MODEL]
    x = x_in + g1 * o
    h = _rms(x, nfg_ref[...]) * (1.0 + sc2) + sh2
    gu = _dg(h.astype(BF16), wi_ref[...], _NN)
    gt = gu[:, 0:D_FF]
    act = gt * jax.nn.sigmoid(gt) * gu[:, D_FF:]
    x = x + g2 * _dg(act.astype(BF16), w2_ref[...], _NN)
    if not final:
        (o_ref,) = tail
        o_ref[...] = x
    else:
        fng_ref, oc_ref, ol_ref = tail
        y_out = _rms(x, fng_ref[...])

        @pl.when(is_ctx)
        def _():
            oc_ref[...] = y_out

        @pl.when(jnp.logical_not(is_ctx))
        def _():
            ol_ref[...] = y_out


def _merge_ffn_call(xs, mods_l, oa_ctx, oa_lat, y_f, y_b, bonus, g, of_ctx, of_lat, ln_g, ln_b, avg, nfg,
                    wo, wi, w2, l, final_g=None):
    final = final_g is not None
    split = isinstance(xs, tuple)
    xs = xs if split else (xs,)
    row = lambda i: (l, 0, 0)
    tile = lambda i: (i, 0)
    ctx_tile, lat_tile = _CTX_TILE, _LAT_TILE
    once = pl.Buffered(1)
    in_specs = _x_specs(split) + [
        pl.BlockSpec((None, 1, 6 * D_MODEL), lambda i: (_mod_row(i), 0, 0)),
        pl.BlockSpec((TM, ATTN_W), ctx_tile),
        pl.BlockSpec((TM, ATTN_W), lat_tile),
        pl.BlockSpec((TM, RWKV_W), tile),
        pl.BlockSpec((TM, RWKV_W), tile),
        pl.BlockSpec((TM, RWKV_W), tile),
        pl.BlockSpec((TM, RWKV_W), tile),
        pl.BlockSpec((TM, FNET_W), ctx_tile),
        pl.BlockSpec((TM, FNET_W), lat_tile),
        pl.BlockSpec((None, 1, RWKV_W), row),
        pl.BlockSpec((None, 1, RWKV_W), row),
        pl.BlockSpec((RWKV_W, RWKV_W), lambda i: (0, 0)),
        pl.BlockSpec((None, 1, D_MODEL), row),
        pl.BlockSpec((None, D_MODEL, D_MODEL), row, pipeline_mode=once),
        pl.BlockSpec((None, D_MODEL, 2 * D_FF), row, pipeline_mode=once),
        pl.BlockSpec((None, D_FF, D_MODEL), row, pipeline_mode=once),
    ]
    args = [*xs, mods_l, oa_ctx, oa_lat, y_f, y_b, bonus, g, of_ctx, of_lat, ln_g, ln_b, avg, nfg, wo, wi, w2]
    if final:
        in_specs.append(pl.BlockSpec((1, D_MODEL), lambda i: (0, 0)))
        args.append(final_g)
        out_specs = [pl.BlockSpec((TM, D_MODEL), ctx_tile), pl.BlockSpec((TM, D_MODEL), lat_tile)]
        out_shape = [jax.ShapeDtypeStruct((N_CTX, D_MODEL), F32), jax.ShapeDtypeStruct((N_LAT, D_MODEL), F32)]
    else:
        out_specs = pl.BlockSpec((TM, D_MODEL), tile)
        out_shape = jax.ShapeDtypeStruct((N_TOK, D_MODEL), F32)
    return pl.pallas_call(
        functools.partial(_merge_ffn_kernel, split, final),
        grid=(N_TILES,),
        in_specs=in_specs,
        out_specs=out_specs,
        out_shape=out_shape,
        compiler_params=_cparams(("arbitrary",)),
        name="merge_ffn_final" if final else "merge_ffn",
    )(*args)


def _block_diag2(w):
    z = jnp.zeros_like(w[:, 0])
    return jnp.concatenate([jnp.concatenate([w[:, 0], z], axis=2), jnp.concatenate([z, w[:, 1]], axis=2)], axis=1)


def kernel(x_prompt, x_sample, cache_k, cache_v, state_rwkv, c, c_ctx, w_ada, b_ada, norm_mix_g, norm_ffn_g,
           w_in, w_out, attn_sink, rwkv_shift_w, rwkv_w0, rwkv_w_up, rwkv_a0, rwkv_a_up, rwkv_g_up,
           rwkv_k_k, rwkv_k_a, rwkv_r_k, rwkv_ln_g, rwkv_ln_b, ffn_w_in, ffn_w_out, norm_final_g):
    x = (x_prompt.reshape(N_CTX, D_MODEL), x_sample.reshape(N_LAT, D_MODEL))
    cvec = jnp.concatenate([c_ctx[None, :], c, jnp.zeros((8 - 1 - N_LAT_SEQ, D_MODEL), F32)], axis=0)
    mods = _mods_call(cvec, w_ada, b_ada).reshape(DEPTH, 8, 1, 6 * D_MODEL)

    w_in_b = w_in.astype(BF16)
    w_out_b = w_out.astype(BF16)
    ffn_in_b = ffn_w_in.astype(BF16)
    ffn_out_b = ffn_w_out.astype(BF16)
    wup_bd = _block_diag2(rwkv_w_up)
    aup_bd = _block_diag2(rwkv_a_up)
    row3 = lambda a: a.reshape(DEPTH, 1, -1)
    nmg, nfg = row3(norm_mix_g), row3(norm_ffn_g)
    k_k, k_a, r_k = row3(rwkv_k_k), row3(rwkv_k_a), row3(rwkv_r_k)
    ln_g, ln_b = row3(rwkv_ln_g), row3(rwkv_ln_b)
    kc = cache_k.reshape(N_LAT_SEQ, DEPTH, PAST_LEN, KV_W)
    vc = cache_v.reshape(N_LAT_SEQ, DEPTH, PAST_LEN, KV_W)
    s0 = jnp.concatenate([jnp.zeros((N_CTX_SEQ, DEPTH, 2, RWKV_W, HEAD_DIM), F32),
                          state_rwkv.reshape(N_LAT_SEQ, DEPTH, 2, RWKV_W, HEAD_DIM)], axis=0)

    cos, sin = _rope_tables()
    consts = _scan_consts()
    bmask = jnp.asarray(_head_block_mask(RWKV_W))
    ones_bd = bmask.astype(BF16)
    avg_bd = (bmask * (1.0 / HEAD_DIM)).astype(BF16)
    rwkv_params = (rwkv_shift_w, wup_bd, aup_bd, rwkv_g_up, rwkv_w0, rwkv_a0, k_k, k_a, r_k, ones_bd)
    dft_ctx = _dft_tables(CTX_LEN)
    dft_lat = _dft_tables(LAT_LEN)

    new_k, new_v, new_s = [], [], []
    for l in range(DEPTH):
        q, k, v, u_f, r, vv, kn, g, bonus, lw, kd, b = _inproj_call(x, mods[l], nmg, w_in_b, rwkv_params, l)
        new_k.append(k[:N_CTX].reshape(N_CTX_SEQ, CTX_LEN, N_KV_HEADS, HEAD_DIM))
        new_v.append(v[:N_CTX].reshape(N_CTX_SEQ, CTX_LEN, N_KV_HEADS, HEAD_DIM))

        oa_ctx = _ctx_attn_call(q, k, v, attn_sink, l)
        oa_lat = _lat_attn_call(q, k, v, kc, vc, cos, sin, attn_sink, l)

        y_f, y_b, sf, sb = _scan_call(r, vv, kn, lw, kd, b, s0, consts, l)
        new_s.append(_scan_final_states(sf, sb))

        of_ctx = _fnet_call(u_f, CTX_LEN, N_CTX_SEQ, 0, dft_ctx, "fnet_context")
        of_lat = _fnet_call(u_f, LAT_LEN, N_LAT_SEQ, N_CTX, dft_lat, "fnet_latent")

        last = l == DEPTH - 1
        x = _merge_ffn_call(x, mods[l], oa_ctx, oa_lat, y_f, y_b, bonus, g, of_ctx, of_lat, ln_g, ln_b, avg_bd, nfg,
                            w_out_b, ffn_in_b, ffn_out_b, l,
                            final_g=norm_final_g.reshape(1, D_MODEL) if last else None)

    y_ctx, y_lat = x
    return (y_ctx.reshape(N_CTX_SEQ, CTX_LEN, D_MODEL), y_lat.reshape(N_LAT_SEQ, LAT_LEN, D_MODEL),
            jnp.stack(new_k, axis=1), jnp.stack(new_v, axis=1), jnp.stack(new_s, axis=1))
```

```python
import functools

import numpy as np
import jax
import jax.numpy as jnp
from jax import lax
from jax.experimental import pallas as pl
from jax.experimental.pallas import tpu as pltpu

F32 = jnp.float32
BF16 = jnp.bfloat16

D_MODEL = 1024
N_CTX_SEQ = 16
CTX_LEN = 256
DEPTH = 4
N_LAT_SEQ = 4
LAT_LEN = 2048
PAST_LEN = 512
GRID_W = 64
HEAD_DIM = 64
N_Q_HEADS = 8
N_KV_HEADS = 2
Q_PER_KV = N_Q_HEADS // N_KV_HEADS
ATTN_W = N_Q_HEADS * HEAD_DIM
KV_W = N_KV_HEADS * HEAD_DIM
BLOCK = 128
ATTN_SCALE = 0.125
ROPE_THETA = 10000.0
NEG = -1e30
RWKV_HEADS = 4
RWKV_W = RWKV_HEADS * HEAD_DIM
DECAY_RANK = 64
ICLR_RANK = 64
GATE_RANK = 128
DECAY_SCALE = 0.6065306597126334
GN_EPS = 64e-5
RWKV_IN_W = 3 * RWKV_W + GATE_RANK + 2 * DECAY_RANK + 2 * ICLR_RANK
FNET_W = 4 * HEAD_DIM
IN_W = ATTN_W + 2 * KV_W + RWKV_IN_W + FNET_W
D_FF = 2816
RMS_EPS = 1e-6

N_CTX = N_CTX_SEQ * CTX_LEN
N_LAT = N_LAT_SEQ * LAT_LEN
N_TOK = N_CTX + N_LAT
TM = 256
N_TILES = N_TOK // TM
CTX_TILES = N_CTX // TM
LAT_TILES_PER_SEQ = LAT_LEN // TM
N_SEQ = N_CTX_SEQ + N_LAT_SEQ
CHUNK = 64
CHUNKS_PER_TILE = TM // CHUNK
HALO = 8
FFN_TM = 512
LAT_ATTN_HEADS = 1

VMEM_LIMIT = 56 * 1024 * 1024


def _cparams(sem):
    return pltpu.CompilerParams(dimension_semantics=sem, vmem_limit_bytes=VMEM_LIMIT)


def _mod_row(i, tm=TM):
    return jnp.where(i < N_CTX // tm, 0, 1 + (i - N_CTX // tm) // (LAT_LEN // tm))


def _split2(x):
    hi = x.astype(BF16)
    lo = (x - hi.astype(F32)).astype(BF16)
    return hi, lo


_NN = (((1,), (0,)), ((), ()))
_NT = (((1,), (1,)), ((), ()))


def _dg(a, b, dims):
    return lax.dot_general(a, b, dims, preferred_element_type=F32)


def _dot1(a, b, dims=_NN):
    return _dg(a.astype(BF16), b.astype(BF16), dims)


def _dot3(a, b, dims=_NN):
    ah, al = _split2(a)
    bh, bl = _split2(b)
    return _dg(ah, bh, dims) + (_dg(ah, bl, dims) + _dg(al, bh, dims))


def _dot_exact_rhs(a, b_bf16):
    ah, al = _split2(a)
    return _dg(ah, b_bf16, _NN) + _dg(al, b_bf16, _NN)


def _dot_exact_lhs(a_bf16, b):
    bh, bl = _split2(b)
    return _dg(a_bf16, bh, _NN) + _dg(a_bf16, bl, _NN)


def _mods_kernel(c_ref, w_ref, b_ref, o_ref):
    c = c_ref[...]
    s = c * jax.nn.sigmoid(c)
    o_ref[...] = _dot1(s, w_ref[...]) + b_ref[...]


def _mods_call(cvec, w_ada, b_ada):
    tn = 1536
    return pl.pallas_call(
        _mods_kernel,
        grid=(DEPTH, 6 * D_MODEL // tn),
        in_specs=[
            pl.BlockSpec((8, D_MODEL), lambda l, j: (0, 0)),
            pl.BlockSpec((None, D_MODEL, tn), lambda l, j: (l, 0, j)),
            pl.BlockSpec((None, 1, tn), lambda l, j: (l, 0, j)),
        ],
        out_specs=pl.BlockSpec((None, 8, tn), lambda l, j: (l, 0, j)),
        out_shape=jax.ShapeDtypeStruct((DEPTH, 8, 6 * D_MODEL), F32),
        compiler_params=_cparams(("parallel", "parallel")),
        name="adaln_mods",
    )(cvec, w_ada, b_ada.reshape(DEPTH, 1, 6 * D_MODEL))


def _rms(x, g):
    return x * lax.rsqrt(jnp.mean(x * x, axis=-1, keepdims=True) + RMS_EPS) * g


def _tile_of_split(i, ctx_ref, lat_ref, tm=TM):
    return jnp.where(i < N_CTX // tm, ctx_ref[...], lat_ref[...])


def _ctx_tile(tm=TM):
    return lambda i: (jnp.minimum(i, N_CTX // tm - 1), 0)


def _lat_tile(tm=TM):
    return lambda i: (jnp.maximum(i - N_CTX // tm, 0), 0)


def _x_specs(split, tm=TM):
    if split:
        return [pl.BlockSpec((tm, D_MODEL), _ctx_tile(tm)), pl.BlockSpec((tm, D_MODEL), _lat_tile(tm))]
    return [pl.BlockSpec((tm, D_MODEL), lambda i: (i, 0))]


def _head_block_mask(n):
    r = np.arange(n)[:, None] // HEAD_DIM
    c = np.arange(n)[None, :] // HEAD_DIM
    return (r == c).astype(np.float32)


def _rwkv_token_terms(u, prev_row, next_row, sw_ref, wup_ref, aup_ref, gup_ref, w0_ref, a0_ref,
                      kk_ref, ka_ref, rk_ref, ones_ref, r_out, v_out, kn_out, g_out, bonus_out, lw_out, kd_out, b_out):
    row = lax.broadcasted_iota(jnp.int32, u.shape, 0)
    u_dn = jnp.where(row == 0, prev_row, pltpu.roll(u, 1, 0))
    u_up = jnp.where(row == TM - 1, next_row, pltpu.roll(u, TM - 1, 0))
    us = u_dn * sw_ref[0:1, :] + u * sw_ref[1:2, :] + u_up * sw_ref[2:3, :]

    r = us[:, 0:RWKV_W]
    k = us[:, RWKV_W:2 * RWKV_W]
    v = us[:, 2 * RWKV_W:3 * RWKV_W]
    o = 3 * RWKV_W
    gd = us[:, o:o + GATE_RANK]
    wd = us[:, o + GATE_RANK:o + GATE_RANK + 2 * DECAY_RANK]
    ad = us[:, o + GATE_RANK + 2 * DECAY_RANK:RWKV_IN_W]

    w_pre = _dot1(jnp.tanh(wd), wup_ref[...])
    a_pre = _dot1(ad, aup_ref[...])
    g_out[...] = _dot1(jax.nn.sigmoid(gd), gup_ref[...])

    ones_bd = ones_ref[...]
    kn = k * kk_ref[...]
    kn = kn * lax.rsqrt(_dot_exact_rhs(kn * kn, ones_bd) + 1e-12)
    r_out[...] = r
    v_out[...] = v
    kn_out[...] = kn
    bonus_out[...] = _dot_exact_rhs(r * k * rk_ref[...], ones_bd) * v
    ka = ka_ref[...]
    for d in range(2):
        sl = slice(d * RWKV_W, (d + 1) * RWKV_W)
        a_d = jax.nn.sigmoid(a0_ref[d:d + 1, :] + a_pre[:, sl])
        lw_out[d] = -DECAY_SCALE * jax.nn.sigmoid(w0_ref[d:d + 1, :] + w_pre[:, sl])
        kd_out[d] = k * (1.0 + (a_d - 1.0) * ka)
        b_out[d] = kn * a_d


RWKV_COL0 = ATTN_W + 2 * KV_W
N_RWKV_PARAMS = 10


def _inproj_kernel(split, *refs):
    n_x = 2 if split else 1
    i = pl.program_id(0)
    x = _tile_of_split(i, *refs[:2]) if split else refs[0][...]
    xp_ref, xn_ref, mod_ref, g_ref, w_ref = refs[n_x:n_x + 5]
    rwkv_params = refs[n_x + 5:n_x + 5 + N_RWKV_PARAMS]
    q_ref, k_ref, v_ref, uf_ref = refs[n_x + 5 + N_RWKV_PARAMS:n_x + 9 + N_RWKV_PARAMS]
    rwkv_outs = refs[n_x + 9 + N_RWKV_PARAMS:]

    shift = mod_ref[:, 0:D_MODEL]
    scale = mod_ref[:, D_MODEL:2 * D_MODEL]
    x_all = jnp.concatenate([x, xp_ref[...], xn_ref[...]], axis=0)
    h = _rms(x_all, g_ref[...]) * (1.0 + scale) + shift
    o = _dg(h.astype(BF16), w_ref[...], _NN)
    q_ref[...] = o[:TM, 0:ATTN_W]
    k_ref[...] = o[:TM, ATTN_W:ATTN_W + KV_W]
    v_ref[...] = o[:TM, ATTN_W + KV_W:RWKV_COL0]
    uf_ref[...] = o[:TM, RWKV_COL0 + RWKV_IN_W:IN_W]

    j = (i - CTX_TILES) % LAT_TILES_PER_SEQ
    has_prev = jnp.where((i >= CTX_TILES) & (j != 0), 1.0, 0.0)
    has_next = jnp.where((i >= CTX_TILES) & (j != LAT_TILES_PER_SEQ - 1), 1.0, 0.0)
    u_all = o[:, RWKV_COL0:RWKV_COL0 + RWKV_IN_W]
    prev_row = u_all[TM + HALO - 1:TM + HALO, :] * has_prev
    next_row = u_all[TM + HALO:TM + HALO + 1, :] * has_next
    _rwkv_token_terms(u_all[:TM], prev_row, next_row, *rwkv_params, *rwkv_outs)


def _inproj_call(xs, mods_l, g, w_bf16, rwkv_params, l):
    split = isinstance(xs, tuple)
    xs = xs if split else (xs,)
    hb = TM // HALO
    if split:
        halo_src = xs[1]
        last = N_LAT // HALO - 1
        prev_blk = lambda i: (jnp.clip((i - CTX_TILES) * hb - 1, 0, last), 0)
        next_blk = lambda i: (jnp.clip((i - CTX_TILES + 1) * hb, 0, last), 0)
    else:
        halo_src = xs[0]
        last = N_TOK // HALO - 1
        prev_blk = lambda i: (jnp.maximum(i * hb - 1, 0), 0)
        next_blk = lambda i: (jnp.minimum((i + 1) * hb, last), 0)
    row = lambda i: (l, 0, 0)
    widths = (ATTN_W, KV_W, KV_W, FNET_W)
    tok = jax.ShapeDtypeStruct((N_TOK, RWKV_W), F32)
    tok2 = jax.ShapeDtypeStruct((2, N_TOK, RWKV_W), F32)
    return pl.pallas_call(
        functools.partial(_inproj_kernel, split),
        grid=(N_TILES,),
        in_specs=_x_specs(split) + [
            pl.BlockSpec((HALO, D_MODEL), prev_blk),
            pl.BlockSpec((HALO, D_MODEL), next_blk),
            pl.BlockSpec((None, 1, 6 * D_MODEL), lambda i: (_mod_row(i), 0, 0)),
            pl.BlockSpec((None, 1, D_MODEL), row),
            pl.BlockSpec((None, D_MODEL, IN_W), row),
            pl.BlockSpec((None, 3, RWKV_IN_W), row),
            pl.BlockSpec((None, 2 * DECAY_RANK, 2 * RWKV_W), row),
            pl.BlockSpec((None, 2 * ICLR_RANK, 2 * RWKV_W), row),
            pl.BlockSpec((None, GATE_RANK, RWKV_W), row),
            pl.BlockSpec((None, 2, RWKV_W), row),
            pl.BlockSpec((None, 2, RWKV_W), row),
            pl.BlockSpec((None, 1, RWKV_W), row),
            pl.BlockSpec((None, 1, RWKV_W), row),
            pl.BlockSpec((None, 1, RWKV_W), row),
            pl.BlockSpec((RWKV_W, RWKV_W), lambda i: (0, 0)),
        ],
        out_specs=[pl.BlockSpec((TM, w), lambda i: (i, 0)) for w in widths]
        + [pl.BlockSpec((TM, RWKV_W), lambda i: (i, 0))] * 5
        + [pl.BlockSpec((2, TM, RWKV_W), lambda i: (0, i, 0))] * 3,
        out_shape=[jax.ShapeDtypeStruct((N_TOK, w), F32) for w in widths] + [tok] * 5 + [tok2] * 3,
        compiler_params=_cparams(("arbitrary",)),
        name="in_projection",
    )(*xs, halo_src, halo_src, mods_l, g, w_bf16, *rwkv_params)


def _softmax_pv(s, sink_col, v):
    m = jnp.maximum(jnp.max(s, axis=-1, keepdims=True), sink_col)
    e = jnp.exp(s - m)
    denom = jnp.sum(e, axis=-1, keepdims=True) + jnp.exp(sink_col - m)
    return _dot1(e, v) / denom


def _stack_q_heads(q, kh, rows):
    return jnp.concatenate(
        [q[:, (kh * Q_PER_KV + g) * HEAD_DIM:(kh * Q_PER_KV + g + 1) * HEAD_DIM] for g in range(Q_PER_KV)],
        axis=0)


def _sink_col(sink_ref, l, kh, rows):
    return jnp.concatenate(
        [jnp.full((rows, 1), sink_ref[l, kh * Q_PER_KV + g], F32) for g in range(Q_PER_KV)], axis=0)


def _ctx_attn_kernel(l, sink_ref, q_ref, k_ref, v_ref, o_ref):
    q = q_ref[...]
    k = k_ref[...]
    v = v_ref[...]
    half = CTX_LEN // 2
    pieces = [(h, r) for h in range(N_Q_HEADS) for r in range(2)]
    kb = k.astype(BF16)
    vb = v.astype(BF16)
    kv = lambda a, h: a[:, (h // Q_PER_KV) * HEAD_DIM:(h // Q_PER_KV + 1) * HEAD_DIM]
    qs = [(q[r * half:(r + 1) * half, h * HEAD_DIM:(h + 1) * HEAD_DIM] * ATTN_SCALE).astype(BF16) for h, r in pieces]
    s = [_dg(qs[n], kv(kb, h), _NT) for n, (h, r) in enumerate(pieces)]
    m = [jnp.maximum(jnp.max(s[n], axis=-1, keepdims=True), sink_ref[l, h]) for n, (h, r) in enumerate(pieces)]
    e = [jnp.exp(s[n] - m[n]) for n in range(len(pieces))]
    den = [jnp.sum(e[n], axis=-1, keepdims=True) + jnp.exp(sink_ref[l, h] - m[n]) for n, (h, r) in enumerate(pieces)]
    o = [_dg(e[n].astype(BF16), kv(vb, h), _NN) / den[n] for n, (h, r) in enumerate(pieces)]
    for n, (h, r) in enumerate(pieces):
        o_ref[r * half:(r + 1) * half, h * HEAD_DIM:(h + 1) * HEAD_DIM] = o[n]


def _ctx_attn_call(q, k, v, sink, l):
    return pl.pallas_call(
        functools.partial(_ctx_attn_kernel, l),
        grid=(N_CTX_SEQ,),
        in_specs=[
            pl.BlockSpec(memory_space=pltpu.SMEM),
            pl.BlockSpec((CTX_LEN, ATTN_W), lambda b: (b, 0)),
            pl.BlockSpec((CTX_LEN, KV_W), lambda b: (b, 0)),
            pl.BlockSpec((CTX_LEN, KV_W), lambda b: (b, 0)),
        ],
        out_specs=pl.BlockSpec((CTX_LEN, ATTN_W), lambda b: (b, 0)),
        out_shape=jax.ShapeDtypeStruct((N_CTX, ATTN_W), F32),
        compiler_params=_cparams(("parallel",)),
        name="context_attention",
    )(sink, q, k, v)


def _rope(x, cos, sin):
    lane = lax.broadcasted_iota(jnp.int32, x.shape, 1)
    partner = jnp.where((lane & 16) == 0, pltpu.roll(x, 128 - 16, 1), pltpu.roll(x, 16, 1))
    return x * cos + partner * sin


def _lat_attn_kernel(l, sink_ref, q_ref, k_ref, v_ref, kc_ref, vc_ref, cos_ref, sin_ref, o_ref):
    n = pl.program_id(1)
    nb = LAT_LEN // BLOCK
    r0 = pl.multiple_of(n * BLOCK, BLOCK)
    cos_q = cos_ref[pl.ds(r0, BLOCK), :]
    sin_q = sin_ref[pl.ds(r0, BLOCK), :]
    q = jnp.concatenate(
        [_rope(q_ref[:, j * 128:(j + 1) * 128], cos_q, sin_q) for j in range(ATTN_W // 128)], axis=1)

    ks, vs = [], []
    for dj in (-1, 0, 1):
        rj = pl.multiple_of(jnp.clip(n + dj, 0, nb - 1) * BLOCK, BLOCK)
        ks.append(_rope(k_ref[pl.ds(rj, BLOCK), :], cos_ref[pl.ds(rj, BLOCK), :], sin_ref[pl.ds(rj, BLOCK), :]))
        vs.append(v_ref[pl.ds(rj, BLOCK), :])
    k_all = jnp.concatenate(ks + [kc_ref[...]], axis=0)
    v_all = jnp.concatenate(vs + [vc_ref[...]], axis=0)

    g = LAT_ATTN_HEADS
    pieces = [tuple(range(h0, h0 + g)) for h0 in range(0, N_Q_HEADS, g)]
    kb = k_all.astype(BF16)
    vb = v_all.astype(BF16)
    kv = lambda a, hs: a[:, (hs[0] // Q_PER_KV) * HEAD_DIM:(hs[0] // Q_PER_KV + 1) * HEAD_DIM]
    kj = lax.broadcasted_iota(jnp.int32, (g * BLOCK, BLOCK), 1)
    qi = lax.broadcasted_iota(jnp.int32, (g * BLOCK, BLOCK), 0) & (BLOCK - 1)
    keep_prev = (kj >= qi) & (n > 0)
    keep_next = (kj <= qi) & (n < nb - 1)
    sinks = [jnp.concatenate([jnp.full((BLOCK, 1), sink_ref[l, h], F32) for h in hs], axis=0) for hs in pieces]

    qs = [jnp.concatenate([q[:, h * HEAD_DIM:(h + 1) * HEAD_DIM] for h in hs], axis=0) for hs in pieces]
    s = [_dg((qs[i] * ATTN_SCALE).astype(BF16), kv(kb, hs), _NT) for i, hs in enumerate(pieces)]
    s = [jnp.concatenate([jnp.where(keep_prev, s[i][:, 0:BLOCK], NEG), s[i][:, BLOCK:2 * BLOCK],
                          jnp.where(keep_next, s[i][:, 2 * BLOCK:3 * BLOCK], NEG), s[i][:, 3 * BLOCK:]], axis=1)
         for i in range(len(pieces))]
    m = [jnp.maximum(jnp.max(s[i], axis=-1, keepdims=True), sinks[i]) for i in range(len(pieces))]
    e = [jnp.exp(s[i] - m[i]) for i in range(len(pieces))]
    den = [jnp.sum(e[i], axis=-1, keepdims=True) + jnp.exp(sinks[i] - m[i]) for i in range(len(pieces))]
    o = [_dg(e[i].astype(BF16), kv(vb, hs), _NN) / den[i] for i, hs in enumerate(pieces)]
    for i, hs in enumerate(pieces):
        for j, h in enumerate(hs):
            o_ref[:, h * HEAD_DIM:(h + 1) * HEAD_DIM] = o[i][j * BLOCK:(j + 1) * BLOCK]


def _lat_attn_call(q, k, v, kc, vc, cos, sin, sink, l):
    nb = LAT_LEN // BLOCK
    q_off = N_CTX // BLOCK
    kv_off = N_CTX // LAT_LEN
    return pl.pallas_call(
        functools.partial(_lat_attn_kernel, l),
        grid=(N_LAT_SEQ, nb),
        in_specs=[
            pl.BlockSpec(memory_space=pltpu.SMEM),
            pl.BlockSpec((BLOCK, ATTN_W), lambda b, n: (q_off + b * nb + n, 0)),
            pl.BlockSpec((LAT_LEN, KV_W), lambda b, n: (kv_off + b, 0)),
            pl.BlockSpec((LAT_LEN, KV_W), lambda b, n: (kv_off + b, 0)),
            pl.BlockSpec((None, None, PAST_LEN, KV_W), lambda b, n: (b, l, 0, 0)),
            pl.BlockSpec((None, None, PAST_LEN, KV_W), lambda b, n: (b, l, 0, 0)),
            pl.BlockSpec((LAT_LEN, 128), lambda b, n: (0, 0)),
            pl.BlockSpec((LAT_LEN, 128), lambda b, n: (0, 0)),
        ],
        out_specs=pl.BlockSpec((BLOCK, ATTN_W), lambda b, n: (b * nb + n, 0)),
        out_shape=jax.ShapeDtypeStruct((N_LAT, ATTN_W), F32),
        compiler_params=_cparams(("parallel", "arbitrary")),
        name="latent_attention",
    )(sink, q, k, v, kc, vc, cos, sin)


def _rope_tables():
    t = np.arange(LAT_LEN)
    lane = np.arange(128)
    d = lane % HEAD_DIM
    pos = np.where(d[None, :] < HEAD_DIM // 2, (t // GRID_W)[:, None], (t % GRID_W)[:, None]).astype(np.float32)
    quarter = HEAD_DIM // 4
    inv = (np.float32(ROPE_THETA) ** (-(np.arange(quarter, dtype=np.float32)) / np.float32(quarter))).astype(np.float32)
    ang = pos * inv[d % quarter][None, :]
    sign = np.where((lane & 16) == 0, -1.0, 1.0)[None, :]
    return jnp.asarray(np.cos(ang), F32), jnp.asarray(np.sin(ang) * sign, F32)


PAIR_W = 2 * HEAD_DIM
N_PAIRS = RWKV_HEADS // 2
_TN = (((0,), (0,)), ((), ()))


def _pair_stack(x, pmask):
    return jnp.concatenate([x.astype(BF16)] * 2, axis=0) * pmask


def _hdot(m, x, pmask, dims):
    return jnp.concatenate(
        [_dg(m[:, p * PAIR_W:(p + 1) * PAIR_W], _pair_stack(x[:, p * PAIR_W:(p + 1) * PAIR_W], pmask), dims)
         for p in range(N_PAIRS)], axis=1)


def _scan_prepare(reverse, r, v, kn, lw, kd, b, cs, strict, incl, eye, pmask):
    linc = _dot_exact_lhs(cs, lw)
    end = 0 if reverse else CHUNK - 1
    ltot = jnp.concatenate(
        [jnp.broadcast_to(linc[c * CHUNK + end:c * CHUNK + end + 1, :], (CHUNK, RWKV_W))
         for c in range(CHUNKS_PER_TILE)], axis=0)
    v_all = v.astype(BF16)
    r_t = r * jnp.exp(linc)
    a_t = (-kn * jnp.exp(linc - lw)).astype(BF16)
    e_neg = jnp.exp(-linc)
    b_t = (b * e_neg).astype(BF16)
    k_t = (kd * e_neg).astype(BF16)
    e_rem = jnp.exp(ltot - linc)
    b_p = (b * e_rem).astype(BF16)
    k_p = (kd * e_rem).astype(BF16)
    dec = jnp.exp(ltot)

    chunks = range(CHUNKS_PER_TILE)
    sls = [slice(c * CHUNK, (c + 1) * CHUNK) for c in chunks]
    hd = lambda m, x, dims: _hdot(m, x, pmask, dims)
    ar = [jnp.concatenate([a_t[sl], r_t[sl].astype(BF16)], axis=0) for sl in sls]
    m_b = [hd(ar[c], b_t[sls[c]], _NT) for c in chunks]
    m_k = [hd(ar[c], k_t[sls[c]], _NT) for c in chunks]
    a_ab = [m_b[c][:CHUNK] * strict for c in chunks]
    p_rb = [(m_b[c][CHUNK:] * incl).astype(BF16) for c in chunks]
    ak_rk = [jnp.concatenate([m_k[c][:CHUNK] * strict, m_k[c][CHUNK:] * incl], axis=0).astype(BF16)
             for c in chunks]

    tinv = [eye + a_ab[c] for c in chunks]
    p = [a_ab[c].astype(BF16) for c in chunks]
    p = [hd(p[c], p[c], _NN).astype(BF16) for c in chunks]
    for _ in range(4):
        pt = [hd(jnp.concatenate([p[c], tinv[c].astype(BF16)], axis=0), p[c], _NN) for c in chunks]
        p = [pt[c][:CHUNK].astype(BF16) for c in chunks]
        tinv = [tinv[c] + pt[c][CHUNK:] for c in chunks]
    tinv = [(tinv[c] + hd(tinv[c].astype(BF16), p[c], _NN)).astype(BF16) for c in chunks]

    m_v = [hd(ak_rk[c], v_all[sls[c]], _NN) for c in chunks]
    w = [hd(tinv[c], a_t[sls[c]], _NN) for c in chunks]
    u0 = [hd(tinv[c], m_v[c][:CHUNK], _NN) for c in chunks]
    qp = [r_t[sls[c]] + hd(p_rb[c], w[c], _NN) for c in chunks]
    y0 = [hd(p_rb[c], u0[c], _NN) + m_v[c][CHUNK:] for c in chunks]
    pr = [slice(p * PAIR_W, (p + 1) * PAIR_W) for p in range(N_PAIRS)]
    pmask_f = pmask.astype(F32)
    bk = [jnp.concatenate([b_p[sl], k_p[sl]], axis=0) for sl in sls]
    uv = [jnp.concatenate([u0[c].astype(BF16), v_all[sls[c]]], axis=0) for c in chunks]
    wb = [[_dg(w[c][:, q].astype(BF16), b_p[sls[c]][:, q], _TN) * pmask_f for q in pr] for c in chunks]
    gs = [[_dg(uv[c][:, q], bk[c][:, q], _TN) * pmask_f for q in pr] for c in chunks]
    qpb = [qp[c].astype(BF16) for c in chunks]
    decs = [dec[c * CHUNK:c * CHUNK + 1, :] for c in chunks]
    return qpb, y0, wb, gs, decs


def _scan_chain_step(c, s, prep, y_ref):
    qpb, y0, wb, gs, decs = prep
    pr = [slice(p * PAIR_W, (p + 1) * PAIR_W) for p in range(N_PAIRS)]
    sb = [s[p].astype(BF16) for p in range(N_PAIRS)]
    y_ref[c * CHUNK:(c + 1) * CHUNK, :] = y0[c] + jnp.concatenate(
        [_dg(qpb[c][:, pr[p]], sb[p], _NT) for p in range(N_PAIRS)], axis=1)
    return [s[p] * decs[c][:, pr[p]] + (_dg(sb[p], wb[c][p].astype(BF16), _NN) + gs[c][p]) for p in range(N_PAIRS)]


def _scan_kernel(rf_ref, vf_ref, knf_ref, lwf_ref, kdf_ref, bf_ref, s0f_ref,
                 rb_ref, vb_ref, knb_ref, lwb_ref, kdb_ref, bb_ref, s0b_ref,
                 cs_ref, strict_ref, incl_ref, eye_ref, pmask_ref,
                 yf_ref, yb_ref, sfinf_ref, sfinb_ref, sf_scr, sb_scr):
    i = pl.program_id(0)
    tb = N_TILES - 1 - i
    first_f = (i < CTX_TILES) | ((i - CTX_TILES) % LAT_TILES_PER_SEQ == 0)
    first_b = (tb < CTX_TILES) | ((tb - CTX_TILES) % LAT_TILES_PER_SEQ == LAT_TILES_PER_SEQ - 1)
    pmask_f = pmask_ref[...]
    pmask = pmask_f.astype(BF16)

    def load_state(s0_ref, scr):
        for p in range(N_PAIRS):
            scr[p] = jnp.concatenate([s0_ref[p * PAIR_W:(p + 1) * PAIR_W, :]] * 2, axis=1) * pmask_f

    pl.when(first_f)(lambda: load_state(s0f_ref, sf_scr))
    pl.when(first_b)(lambda: load_state(s0b_ref, sb_scr))

    eye = eye_ref[...]
    prep_f = _scan_prepare(False, rf_ref[...], vf_ref[...], knf_ref[...], lwf_ref[...], kdf_ref[...], bf_ref[...],
                           cs_ref[0], strict_ref[0], incl_ref[0], eye, pmask)
    prep_b = _scan_prepare(True, rb_ref[...], vb_ref[...], knb_ref[...], lwb_ref[...], kdb_ref[...], bb_ref[...],
                           cs_ref[1], strict_ref[1], incl_ref[1], eye, pmask)
    s_f = [sf_scr[p] for p in range(N_PAIRS)]
    s_b = [sb_scr[p] for p in range(N_PAIRS)]
    for c in range(CHUNKS_PER_TILE):
        s_f = _scan_chain_step(c, s_f, prep_f, yf_ref)
        s_b = _scan_chain_step(CHUNKS_PER_TILE - 1 - c, s_b, prep_b, yb_ref)
    for p in range(N_PAIRS):
        sf_scr[p] = s_f[p]
        sb_scr[p] = s_b[p]
        sfinf_ref[p] = s_f[p]
        sfinb_ref[p] = s_b[p]


def _scan_call(r, v, kn, lw, kd, b, s0, consts, l):
    cs, strict, incl, eye, pmask = consts
    rev = lambda i: N_TILES - 1 - i
    seq_of = lambda t: jnp.where(t < CTX_TILES, t, CTX_TILES + (t - CTX_TILES) // LAT_TILES_PER_SEQ)
    const2 = lambda i: (0, 0)
    const3 = lambda i: (0, 0, 0)
    tok_f = pl.BlockSpec((TM, RWKV_W), lambda i: (i, 0))
    tok_b = pl.BlockSpec((TM, RWKV_W), lambda i: (rev(i), 0))
    tokd_f = pl.BlockSpec((None, TM, RWKV_W), lambda i: (0, i, 0))
    tokd_b = pl.BlockSpec((None, TM, RWKV_W), lambda i: (1, rev(i), 0))
    s0_f = pl.BlockSpec((None, None, None, RWKV_W, HEAD_DIM), lambda i: (seq_of(i), l, 0, 0, 0))
    s0_b = pl.BlockSpec((None, None, None, RWKV_W, HEAD_DIM), lambda i: (seq_of(rev(i)), l, 1, 0, 0))
    sfin = jax.ShapeDtypeStruct((CTX_TILES + 1, N_PAIRS, PAIR_W, PAIR_W), F32)
    y = jax.ShapeDtypeStruct((N_TOK, RWKV_W), F32)
    state = pltpu.VMEM((N_PAIRS, PAIR_W, PAIR_W), F32)
    return pl.pallas_call(
        _scan_kernel,
        grid=(N_TILES,),
        in_specs=[
            tok_f, tok_f, tok_f, tokd_f, tokd_f, tokd_f, s0_f,
            tok_b, tok_b, tok_b, tokd_b, tokd_b, tokd_b, s0_b,
            pl.BlockSpec((2, TM, TM), const3),
            pl.BlockSpec((2, CHUNK, RWKV_W), const3),
            pl.BlockSpec((2, CHUNK, RWKV_W), const3),
            pl.BlockSpec((CHUNK, RWKV_W), const2),
            pl.BlockSpec((PAIR_W, PAIR_W), const2),
        ],
        out_specs=[
            tok_f, tok_b,
            pl.BlockSpec((None, N_PAIRS, PAIR_W, PAIR_W), lambda i: (jnp.minimum(i, CTX_TILES), 0, 0, 0)),
            pl.BlockSpec((None, N_PAIRS, PAIR_W, PAIR_W), lambda i: (jnp.minimum(rev(i), CTX_TILES), 0, 0, 0)),
        ],
        out_shape=[y, y, sfin, sfin],
        scratch_shapes=[state, state],
        compiler_params=_cparams(("arbitrary",)),
        name="rwkv_scan",
    )(r, v, kn, lw, kd, b, s0, r, v, kn, lw, kd, b, s0, cs, strict, incl, eye, pmask)


def _scan_final_states(sf, sb):
    fin = jnp.stack([sf[:N_CTX_SEQ], sb[:N_CTX_SEQ]], axis=1)
    fin = fin.reshape(N_CTX_SEQ, 2, N_PAIRS, 2, HEAD_DIM, 2, HEAD_DIM)
    fin = jnp.stack([fin[:, :, :, h, :, h, :] for h in range(2)], axis=3)
    return fin.reshape(N_CTX_SEQ, 2, RWKV_HEADS, HEAD_DIM, HEAD_DIM)


def _scan_consts():
    t = np.arange(TM)
    same_chunk = (t[:, None] // CHUNK) == (t[None, :] // CHUNK)
    cs_f = same_chunk & (t[None, :] <= t[:, None])
    cs_b = same_chunk & (t[None, :] >= t[:, None])
    cs = np.stack([cs_f, cs_b]).astype(np.float32)
    q = np.arange(CHUNK)[:, None]
    kcol = np.arange(RWKV_W)[None, :] % CHUNK
    strict = np.stack([kcol < q, kcol > q]).astype(np.float32)
    incl = np.stack([kcol <= q, kcol >= q]).astype(np.float32)
    eye = (kcol == q).astype(np.float32)
    return (jnp.asarray(cs, BF16), jnp.asarray(strict), jnp.asarray(incl),
            jnp.asarray(eye), jnp.asarray(_head_block_mask(PAIR_W)))


FNET_TT = 512
FNET_IN_ROWS = N_CTX


def _fnet_kernel(seq_len, n_in, *refs):
    z_refs = refs[:n_in]
    ch_ref, cl_ref, th_ref, o_ref, z_scr = refs[n_in:]
    b = pl.program_id(1)

    @pl.when((pl.program_id(0) == 0) & (b == 0))
    def _():
        per_in = FNET_IN_ROWS // seq_len
        for n, z_ref in enumerate(z_refs):
            zh, zl = _split2(z_ref[...])
            for half in range(2):
                ch = ch_ref[half]
                cl = cl_ref[half]
                zc = (_dg(zh, ch, _NN) + (_dg(zh, cl, _NN) + _dg(zl, ch, _NN))).astype(BF16)
                for s in range(per_in):
                    z_scr[n * per_in + s, half * seq_len:(half + 1) * seq_len, :] = zc[s * seq_len:(s + 1) * seq_len]

    o_ref[...] = _dg(th_ref[...], z_scr[b], _NN)


def _fnet_call(u_f, seq_len, n_seq, row_off, tables, name):
    ch, cl, th = tables
    tt = min(FNET_TT, seq_len)
    nt = seq_len // tt
    n_in = n_seq * seq_len // FNET_IN_ROWS
    in_off = row_off // FNET_IN_ROWS
    z_specs = [pl.BlockSpec((FNET_IN_ROWS, FNET_W), functools.partial(lambda n, j, b: (in_off + n, 0), n))
               for n in range(n_in)]
    return pl.pallas_call(
        functools.partial(_fnet_kernel, seq_len, n_in),
        grid=(nt, n_seq),
        in_specs=z_specs + [
            pl.BlockSpec((2, FNET_W, FNET_W), lambda j, b: (0, 0, 0)),
            pl.BlockSpec((2, FNET_W, FNET_W), lambda j, b: (0, 0, 0)),
            pl.BlockSpec((tt, 2 * seq_len), lambda j, b: (j, 0)),
        ],
        out_specs=pl.BlockSpec((tt, FNET_W), lambda j, b: (b * nt + j, 0)),
        out_shape=jax.ShapeDtypeStruct((n_seq * seq_len, FNET_W), F32),
        scratch_shapes=[pltpu.VMEM((n_seq, 2 * seq_len, FNET_W), BF16)],
        compiler_params=_cparams(("arbitrary", "arbitrary")),
        name=name,
    )(*([u_f] * n_in), ch, cl, th)


DFT_SPLIT = 32


def _dft_tables(seq_len):
    c = np.arange(FNET_W)
    ang_c = 2.0 * np.pi * ((c[:, None] % HEAD_DIM) * (c[None, :] % HEAD_DIM) % HEAD_DIM) / HEAD_DIM
    scale = 1.0 / np.sqrt(float(seq_len) * HEAD_DIM)
    blk = _head_block_mask(FNET_W)
    ch, cl = _split2(jnp.asarray(np.stack([np.cos(ang_c) * blk * scale, np.sin(ang_c) * blk * scale]), F32))

    n1 = seq_len // DFT_SPLIT
    f = np.arange(seq_len)
    a1 = 2.0 * np.pi * ((np.arange(n1)[:, None] * f[None, :]) % n1) / n1
    a2 = 2.0 * np.pi * ((np.arange(DFT_SPLIT)[:, None] * f[None, :]) % seq_len) / seq_len
    c1, s1 = jnp.asarray(np.cos(a1), F32)[:, None, :], jnp.asarray(np.sin(a1), F32)[:, None, :]
    c2, s2 = jnp.asarray(np.cos(a2), F32)[None, :, :], jnp.asarray(np.sin(a2), F32)[None, :, :]
    cos_t = (c1 * c2 - s1 * s2).reshape(seq_len, seq_len)
    sin_t = (s1 * c2 + c1 * s2).reshape(seq_len, seq_len)
    th = jnp.concatenate([cos_t, -sin_t], axis=1).astype(BF16)
    return ch, cl, th


def _merge_ffn_kernel(split, final, *refs):
    i = pl.program_id(0)
    n_x = 2 if split else 1
    x_in = _tile_of_split(i, *refs[:2], FFN_TM) if split else refs[0][...]
    (mod_ref, oac_ref, oal_ref, yf_ref, yb_ref, bonus_ref, g_ref, ofc_ref, ofl_ref,
     lng_ref, lnb_ref, avg_ref, nfg_ref, wo_ref, wi_ref, w2_ref) = refs[n_x:n_x + 16]
    tail = refs[n_x + 16:]
    is_ctx = i < N_CTX // FFN_TM
    o_attn = _tile_of_split(i, oac_ref, oal_ref, FFN_TM)
    o_fnet = _tile_of_split(i, ofc_ref, ofl_ref, FFN_TM)
    avg = avg_ref[...]
    y = yf_ref[...] + yb_ref[...]
    yc = y - _dot_exact_rhs(y, avg)
    var = _dot_exact_rhs(yc * yc, avg)
    yn = yc * lax.rsqrt(var + GN_EPS) * lng_ref[...] + lnb_ref[...]
    o_rwkv = (yn + bonus_ref[...]) * g_ref[...]

    o = (_dg(o_attn.astype(BF16), wo_ref[0:ATTN_W, :], _NN)
         + _dg(o_rwkv.astype(BF16), wo_ref[ATTN_W:ATTN_W + RWKV_W, :], _NN)
         + _dg(o_fnet.astype(BF16), wo_ref[ATTN_W + RWKV_W:, :], _NN))
    g1 = mod_ref[:, 2 * D_MODEL:3 * D_MODEL]
    sh2 = mod_ref[:, 3 * D_MODEL:4 * D_MODEL]
    sc2 = mod_ref[:, 4 * D_MODEL:5 * D_MODEL]
    g2 = mod_ref[:, 5 * D_MODEL:6 * D_MODEL]
    x = x_in + g1 * o
    h = _rms(x, nfg_ref[...]) * (1.0 + sc2) + sh2
    gu = _dg(h.astype(BF16), wi_ref[...], _NN)
    gt = gu[:, 0:D_FF]
    act = gt * jax.nn.sigmoid(gt) * gu[:, D_FF:]
    x = x + g2 * _dg(act.astype(BF16), w2_ref[...], _NN)
    if not final:
        (o_ref,) = tail
        o_ref[...] = x
    else:
        fng_ref, oc_ref, ol_ref = tail
        y_out = _rms(x, fng_ref[...])

        @pl.when(is_ctx)
        def _():
            oc_ref[...] = y_out

        @pl.when(jnp.logical_not(is_ctx))
        def _():
            ol_ref[...] = y_out


def _merge_ffn_call(xs, mods_l, oa_ctx, oa_lat, y_f, y_b, bonus, g, of_ctx, of_lat, ln_g, ln_b, avg, nfg,
                    wo, wi, w2, l, final_g=None):
    final = final_g is not None
    split = isinstance(xs, tuple)
    xs = xs if split else (xs,)
    row = lambda i: (l, 0, 0)
    tile = lambda i: (i, 0)
    ctx_tile, lat_tile = _ctx_tile(FFN_TM), _lat_tile(FFN_TM)
    once = pl.Buffered(1)
    in_specs = _x_specs(split, FFN_TM) + [
        pl.BlockSpec((None, 1, 6 * D_MODEL), lambda i: (_mod_row(i, FFN_TM), 0, 0)),
        pl.BlockSpec((FFN_TM, ATTN_W), ctx_tile),
        pl.BlockSpec((FFN_TM, ATTN_W), lat_tile),
        pl.BlockSpec((FFN_TM, RWKV_W), tile),
        pl.BlockSpec((FFN_TM, RWKV_W), tile),
        pl.BlockSpec((FFN_TM, RWKV_W), tile),
        pl.BlockSpec((FFN_TM, RWKV_W), tile),
        pl.BlockSpec((FFN_TM, FNET_W), ctx_tile),
        pl.BlockSpec((FFN_TM, FNET_W), lat_tile),
        pl.BlockSpec((None, 1, RWKV_W), row),
        pl.BlockSpec((None, 1, RWKV_W), row),
        pl.BlockSpec((RWKV_W, RWKV_W), lambda i: (0, 0)),
        pl.BlockSpec((None, 1, D_MODEL), row),
        pl.BlockSpec((None, D_MODEL, D_MODEL), row, pipeline_mode=once),
        pl.BlockSpec((None, D_MODEL, 2 * D_FF), row, pipeline_mode=once),
        pl.BlockSpec((None, D_FF, D_MODEL), row, pipeline_mode=once),
    ]
    args = [*xs, mods_l, oa_ctx, oa_lat, y_f, y_b, bonus, g, of_ctx, of_lat, ln_g, ln_b, avg, nfg, wo, wi, w2]
    if final:
        in_specs.append(pl.BlockSpec((1, D_MODEL), lambda i: (0, 0)))
        args.append(final_g)
        out_specs = [pl.BlockSpec((FFN_TM, D_MODEL), ctx_tile), pl.BlockSpec((FFN_TM, D_MODEL), lat_tile)]
        out_shape = [jax.ShapeDtypeStruct((N_CTX, D_MODEL), F32), jax.ShapeDtypeStruct((N_LAT, D_MODEL), F32)]
    else:
        out_specs = pl.BlockSpec((FFN_TM, D_MODEL), tile)
        out_shape = jax.ShapeDtypeStruct((N_TOK, D_MODEL), F32)
    return pl.pallas_call(
        functools.partial(_merge_ffn_kernel, split, final),
        grid=(N_TOK // FFN_TM,),
        in_specs=in_specs,
        out_specs=out_specs,
        out_shape=out_shape,
        compiler_params=_cparams(("arbitrary",)),
        name="merge_ffn_final" if final else "merge_ffn",
    )(*args)


def _block_diag2(w):
    z = jnp.zeros_like(w[:, 0])
    return jnp.concatenate([jnp.concatenate([w[:, 0], z], axis=2), jnp.concatenate([z, w[:, 1]], axis=2)], axis=1)


def kernel(x_prompt, x_sample, cache_k, cache_v, state_rwkv, c, c_ctx, w_ada, b_ada, norm_mix_g, norm_ffn_g,
           w_in, w_out, attn_sink, rwkv_shift_w, rwkv_w0, rwkv_w_up, rwkv_a0, rwkv_a_up, rwkv_g_up,
           rwkv_k_k, rwkv_k_a, rwkv_r_k, rwkv_ln_g, rwkv_ln_b, ffn_w_in, ffn_w_out, norm_final_g):
    x = (x_prompt.reshape(N_CTX, D_MODEL), x_sample.reshape(N_LAT, D_MODEL))
    cvec = jnp.concatenate([c_ctx[None, :], c, jnp.zeros((8 - 1 - N_LAT_SEQ, D_MODEL), F32)], axis=0)
    mods = _mods_call(cvec, w_ada, b_ada).reshape(DEPTH, 8, 1, 6 * D_MODEL)

    w_in_b = w_in.astype(BF16)
    w_out_b = w_out.astype(BF16)
    ffn_in_b = ffn_w_in.astype(BF16)
    ffn_out_b = ffn_w_out.astype(BF16)
    wup_bd = _block_diag2(rwkv_w_up)
    aup_bd = _block_diag2(rwkv_a_up)
    row3 = lambda a: a.reshape(DEPTH, 1, -1)
    nmg, nfg = row3(norm_mix_g), row3(norm_ffn_g)
    k_k, k_a, r_k = row3(rwkv_k_k), row3(rwkv_k_a), row3(rwkv_r_k)
    ln_g, ln_b = row3(rwkv_ln_g), row3(rwkv_ln_b)
    kc = cache_k.reshape(N_LAT_SEQ, DEPTH, PAST_LEN, KV_W)
    vc = cache_v.reshape(N_LAT_SEQ, DEPTH, PAST_LEN, KV_W)
    s0 = jnp.concatenate([jnp.zeros((N_CTX_SEQ, DEPTH, 2, RWKV_W, HEAD_DIM), F32),
                          state_rwkv.reshape(N_LAT_SEQ, DEPTH, 2, RWKV_W, HEAD_DIM)], axis=0)

    cos, sin = _rope_tables()
    consts = _scan_consts()
    bmask = jnp.asarray(_head_block_mask(RWKV_W))
    ones_bd = bmask.astype(BF16)
    avg_bd = (bmask * (1.0 / HEAD_DIM)).astype(BF16)
    rwkv_params = (rwkv_shift_w, wup_bd, aup_bd, rwkv_g_up, rwkv_w0, rwkv_a0, k_k, k_a, r_k, ones_bd)
    dft_ctx = _dft_tables(CTX_LEN)
    dft_lat = _dft_tables(LAT_LEN)

    new_k, new_v, new_s = [], [], []
    for l in range(DEPTH):
        q, k, v, u_f, r, vv, kn, g, bonus, lw, kd, b = _inproj_call(x, mods[l], nmg, w_in_b, rwkv_params, l)
        new_k.append(k[:N_CTX].reshape(N_CTX_SEQ, CTX_LEN, KV_W))
        new_v.append(v[:N_CTX].reshape(N_CTX_SEQ, CTX_LEN, KV_W))

        oa_ctx = _ctx_attn_call(q, k, v, attn_sink, l)
        oa_lat = _lat_attn_call(q, k, v, kc, vc, cos, sin, attn_sink, l)

        y_f, y_b, sf, sb = _scan_call(r, vv, kn, lw, kd, b, s0, consts, l)
        new_s.append(_scan_final_states(sf, sb))

        of_ctx = _fnet_call(u_f, CTX_LEN, N_CTX_SEQ, 0, dft_ctx, "fnet_context")
        of_lat = _fnet_call(u_f, LAT_LEN, N_LAT_SEQ, N_CTX, dft_lat, "fnet_latent")

        last = l == DEPTH - 1
        x = _merge_ffn_call(x, mods[l], oa_ctx, oa_lat, y_f, y_b, bonus, g, of_ctx, of_lat, ln_g, ln_b, avg_bd, nfg,
                            w_out_b, ffn_in_b, ffn_out_b, l,
                            final_g=norm_final_g.reshape(1, D_MODEL) if last else None)

    y_ctx, y_lat = x
    return (y_ctx.reshape(N_CTX_SEQ, CTX_LEN, D_MODEL), y_lat.reshape(N_LAT_SEQ, LAT_LEN, D_MODEL),
            jnp.stack(new_k, axis=1).reshape(N_CTX_SEQ, DEPTH, CTX_LEN, N_KV_HEADS, HEAD_DIM),
            jnp.stack(new_v, axis=1).reshape(N_CTX_SEQ, DEPTH, CTX_LEN, N_KV_HEADS, HEAD_DIM),
            jnp.stack(new_s, axis=1))
```

```python
import functools

import numpy as np
import jax
import jax.numpy as jnp
from jax import lax
from jax.experimental import pallas as pl
from jax.experimental.pallas import tpu as pltpu

F32 = jnp.float32
BF16 = jnp.bfloat16

D_MODEL = 1024
N_CTX_SEQ = 16
CTX_LEN = 256
DEPTH = 4
N_LAT_SEQ = 4
LAT_LEN = 2048
PAST_LEN = 512
GRID_W = 64
HEAD_DIM = 64
N_Q_HEADS = 8
N_KV_HEADS = 2
Q_PER_KV = N_Q_HEADS // N_KV_HEADS
ATTN_W = N_Q_HEADS * HEAD_DIM
KV_W = N_KV_HEADS * HEAD_DIM
BLOCK = 128
ATTN_SCALE = 0.125
ROPE_THETA = 10000.0
NEG = -1e30
RWKV_HEADS = 4
RWKV_W = RWKV_HEADS * HEAD_DIM
DECAY_RANK = 64
ICLR_RANK = 64
GATE_RANK = 128
DECAY_SCALE = 0.6065306597126334
GN_EPS = 64e-5
RWKV_IN_W = 3 * RWKV_W + GATE_RANK + 2 * DECAY_RANK + 2 * ICLR_RANK
FNET_W = 4 * HEAD_DIM
IN_W = ATTN_W + 2 * KV_W + RWKV_IN_W + FNET_W
D_FF = 2816
RMS_EPS = 1e-6

N_CTX = N_CTX_SEQ * CTX_LEN
N_LAT = N_LAT_SEQ * LAT_LEN
N_TOK = N_CTX + N_LAT
TM = 256
N_TILES = N_TOK // TM
CTX_TILES = N_CTX // TM
LAT_TILES_PER_SEQ = LAT_LEN // TM
N_SEQ = N_CTX_SEQ + N_LAT_SEQ
CHUNK = 64
CHUNKS_PER_TILE = TM // CHUNK
HALO = 8
FFN_TM = 512
LAT_ATTN_GROUP = 4
LAT_ATTN_HEADS = 1

VMEM_LIMIT = 56 * 1024 * 1024


def _cparams(sem):
    return pltpu.CompilerParams(dimension_semantics=sem, vmem_limit_bytes=VMEM_LIMIT)


def _mod_row(i, tm=TM):
    return jnp.where(i < N_CTX // tm, 0, 1 + (i - N_CTX // tm) // (LAT_LEN // tm))


def _split2(x):
    hi = x.astype(BF16)
    lo = (x - hi.astype(F32)).astype(BF16)
    return hi, lo


_NN = (((1,), (0,)), ((), ()))
_NT = (((1,), (1,)), ((), ()))


def _dg(a, b, dims):
    return lax.dot_general(a, b, dims, preferred_element_type=F32)


def _dot1(a, b, dims=_NN):
    return _dg(a.astype(BF16), b.astype(BF16), dims)


def _dot3(a, b, dims=_NN):
    ah, al = _split2(a)
    bh, bl = _split2(b)
    return _dg(ah, bh, dims) + (_dg(ah, bl, dims) + _dg(al, bh, dims))


def _dot_exact_rhs(a, b_bf16):
    ah, al = _split2(a)
    return _dg(ah, b_bf16, _NN) + _dg(al, b_bf16, _NN)


def _dot_exact_lhs(a_bf16, b):
    bh, bl = _split2(b)
    return _dg(a_bf16, bh, _NN) + _dg(a_bf16, bl, _NN)


def _mods_kernel(c_ref, w_ref, b_ref, o_ref):
    c = c_ref[...]
    s = c * jax.nn.sigmoid(c)
    o_ref[...] = _dot1(s, w_ref[...]) + b_ref[...]


def _mods_call(cvec, w_ada, b_ada):
    tn = 1536
    return pl.pallas_call(
        _mods_kernel,
        grid=(DEPTH, 6 * D_MODEL // tn),
        in_specs=[
            pl.BlockSpec((8, D_MODEL), lambda l, j: (0, 0)),
            pl.BlockSpec((None, D_MODEL, tn), lambda l, j: (l, 0, j)),
            pl.BlockSpec((None, 1, tn), lambda l, j: (l, 0, j)),
        ],
        out_specs=pl.BlockSpec((None, 8, tn), lambda l, j: (l, 0, j)),
        out_shape=jax.ShapeDtypeStruct((DEPTH, 8, 6 * D_MODEL), F32),
        compiler_params=_cparams(("parallel", "parallel")),
        name="adaln_mods",
    )(cvec, w_ada, b_ada.reshape(DEPTH, 1, 6 * D_MODEL))


def _rms(x, g):
    return x * lax.rsqrt(jnp.mean(x * x, axis=-1, keepdims=True) + RMS_EPS) * g


def _tile_of_split(i, ctx_ref, lat_ref, tm=TM):
    return jnp.where(i < N_CTX // tm, ctx_ref[...], lat_ref[...])


def _ctx_tile(tm=TM):
    return lambda i: (jnp.minimum(i, N_CTX // tm - 1), 0)


def _lat_tile(tm=TM):
    return lambda i: (jnp.maximum(i - N_CTX // tm, 0), 0)


def _x_specs(split, tm=TM):
    if split:
        return [pl.BlockSpec((tm, D_MODEL), _ctx_tile(tm)), pl.BlockSpec((tm, D_MODEL), _lat_tile(tm))]
    return [pl.BlockSpec((tm, D_MODEL), lambda i: (i, 0))]


def _head_block_mask(n):
    r = np.arange(n)[:, None] // HEAD_DIM
    c = np.arange(n)[None, :] // HEAD_DIM
    return (r == c).astype(np.float32)


def _rwkv_token_terms(u, prev_row, next_row, sw_ref, wup_ref, aup_ref, gup_ref, w0_ref, a0_ref,
                      kk_ref, ka_ref, rk_ref, ones_ref, r_out, v_out, kn_out, g_out, bonus_out, lw_out, kd_out, b_out):
    row = lax.broadcasted_iota(jnp.int32, u.shape, 0)
    u_dn = jnp.where(row == 0, prev_row, pltpu.roll(u, 1, 0))
    u_up = jnp.where(row == TM - 1, next_row, pltpu.roll(u, TM - 1, 0))
    us = u_dn * sw_ref[0:1, :] + u * sw_ref[1:2, :] + u_up * sw_ref[2:3, :]

    r = us[:, 0:RWKV_W]
    k = us[:, RWKV_W:2 * RWKV_W]
    v = us[:, 2 * RWKV_W:3 * RWKV_W]
    o = 3 * RWKV_W
    gd = us[:, o:o + GATE_RANK]
    wd = us[:, o + GATE_RANK:o + GATE_RANK + 2 * DECAY_RANK]
    ad = us[:, o + GATE_RANK + 2 * DECAY_RANK:RWKV_IN_W]

    w_pre = _dot1(jnp.tanh(wd), wup_ref[...])
    a_pre = _dot1(ad, aup_ref[...])
    g_out[...] = _dot1(jax.nn.sigmoid(gd), gup_ref[...])

    ones_bd = ones_ref[...]
    kn = k * kk_ref[...]
    kn = kn * lax.rsqrt(_dot_exact_rhs(kn * kn, ones_bd) + 1e-12)
    r_out[...] = r
    v_out[...] = v
    kn_out[...] = kn
    bonus_out[...] = _dot_exact_rhs(r * k * rk_ref[...], ones_bd) * v
    ka = ka_ref[...]
    for d in range(2):
        sl = slice(d * RWKV_W, (d + 1) * RWKV_W)
        a_d = jax.nn.sigmoid(a0_ref[d:d + 1, :] + a_pre[:, sl])
        lw_out[d] = -DECAY_SCALE * jax.nn.sigmoid(w0_ref[d:d + 1, :] + w_pre[:, sl])
        kd_out[d] = k * (1.0 + (a_d - 1.0) * ka)
        b_out[d] = kn * a_d


RWKV_COL0 = ATTN_W + 2 * KV_W
N_RWKV_PARAMS = 10


def _inproj_kernel(split, *refs):
    n_x = 2 if split else 1
    i = pl.program_id(0)
    x = _tile_of_split(i, *refs[:2]) if split else refs[0][...]
    xp_ref, xn_ref, mod_ref, g_ref, w_ref = refs[n_x:n_x + 5]
    rwkv_params = refs[n_x + 5:n_x + 5 + N_RWKV_PARAMS]
    q_ref, k_ref, v_ref, uf_ref = refs[n_x + 5 + N_RWKV_PARAMS:n_x + 9 + N_RWKV_PARAMS]
    rwkv_outs = refs[n_x + 9 + N_RWKV_PARAMS:]

    shift = mod_ref[:, 0:D_MODEL]
    scale = mod_ref[:, D_MODEL:2 * D_MODEL]
    x_all = jnp.concatenate([x, xp_ref[...], xn_ref[...]], axis=0)
    h = _rms(x_all, g_ref[...]) * (1.0 + scale) + shift
    o = _dg(h.astype(BF16), w_ref[...], _NN)
    q_ref[...] = o[:TM, 0:ATTN_W]
    k_ref[...] = o[:TM, ATTN_W:ATTN_W + KV_W]
    v_ref[...] = o[:TM, ATTN_W + KV_W:RWKV_COL0]
    uf_ref[...] = o[:TM, RWKV_COL0 + RWKV_IN_W:IN_W]

    j = (i - CTX_TILES) % LAT_TILES_PER_SEQ
    has_prev = jnp.where((i >= CTX_TILES) & (j != 0), 1.0, 0.0)
    has_next = jnp.where((i >= CTX_TILES) & (j != LAT_TILES_PER_SEQ - 1), 1.0, 0.0)
    u_all = o[:, RWKV_COL0:RWKV_COL0 + RWKV_IN_W]
    prev_row = u_all[TM + HALO - 1:TM + HALO, :] * has_prev
    next_row = u_all[TM + HALO:TM + HALO + 1, :] * has_next
    _rwkv_token_terms(u_all[:TM], prev_row, next_row, *rwkv_params, *rwkv_outs)


def _inproj_call(xs, mods_l, g, w_bf16, rwkv_params, l):
    split = isinstance(xs, tuple)
    xs = xs if split else (xs,)
    hb = TM // HALO
    if split:
        halo_src = xs[1]
        last = N_LAT // HALO - 1
        prev_blk = lambda i: (jnp.clip((i - CTX_TILES) * hb - 1, 0, last), 0)
        next_blk = lambda i: (jnp.clip((i - CTX_TILES + 1) * hb, 0, last), 0)
    else:
        halo_src = xs[0]
        last = N_TOK // HALO - 1
        prev_blk = lambda i: (jnp.maximum(i * hb - 1, 0), 0)
        next_blk = lambda i: (jnp.minimum((i + 1) * hb, last), 0)
    row = lambda i: (l, 0, 0)
    widths = (ATTN_W, KV_W, KV_W, FNET_W)
    tok = jax.ShapeDtypeStruct((N_TOK, RWKV_W), F32)
    tok2 = jax.ShapeDtypeStruct((2, N_TOK, RWKV_W), F32)
    return pl.pallas_call(
        functools.partial(_inproj_kernel, split),
        grid=(N_TILES,),
        in_specs=_x_specs(split) + [
            pl.BlockSpec((HALO, D_MODEL), prev_blk),
            pl.BlockSpec((HALO, D_MODEL), next_blk),
            pl.BlockSpec((None, 1, 6 * D_MODEL), lambda i: (_mod_row(i), 0, 0)),
            pl.BlockSpec((None, 1, D_MODEL), row),
            pl.BlockSpec((None, D_MODEL, IN_W), row),
            pl.BlockSpec((None, 3, RWKV_IN_W), row),
            pl.BlockSpec((None, 2 * DECAY_RANK, 2 * RWKV_W), row),
            pl.BlockSpec((None, 2 * ICLR_RANK, 2 * RWKV_W), row),
            pl.BlockSpec((None, GATE_RANK, RWKV_W), row),
            pl.BlockSpec((None, 2, RWKV_W), row),
            pl.BlockSpec((None, 2, RWKV_W), row),
            pl.BlockSpec((None, 1, RWKV_W), row),
            pl.BlockSpec((None, 1, RWKV_W), row),
            pl.BlockSpec((None, 1, RWKV_W), row),
            pl.BlockSpec((RWKV_W, RWKV_W), lambda i: (0, 0)),
        ],
        out_specs=[pl.BlockSpec((TM, w), lambda i: (i, 0)) for w in widths]
        + [pl.BlockSpec((TM, RWKV_W), lambda i: (i, 0))] * 5
        + [pl.BlockSpec((2, TM, RWKV_W), lambda i: (0, i, 0))] * 3,
        out_shape=[jax.ShapeDtypeStruct((N_TOK, w), F32) for w in widths] + [tok] * 5 + [tok2] * 3,
        compiler_params=_cparams(("arbitrary",)),
        name="in_projection",
    )(*xs, halo_src, halo_src, mods_l, g, w_bf16, *rwkv_params)


def _softmax_pv(s, sink_col, v):
    m = jnp.maximum(jnp.max(s, axis=-1, keepdims=True), sink_col)
    e = jnp.exp(s - m)
    denom = jnp.sum(e, axis=-1, keepdims=True) + jnp.exp(sink_col - m)
    return _dot1(e, v) / denom


def _stack_q_heads(q, kh, rows):
    return jnp.concatenate(
        [q[:, (kh * Q_PER_KV + g) * HEAD_DIM:(kh * Q_PER_KV + g + 1) * HEAD_DIM] for g in range(Q_PER_KV)],
        axis=0)


def _sink_col(sink_ref, l, kh, rows):
    return jnp.concatenate(
        [jnp.full((rows, 1), sink_ref[l, kh * Q_PER_KV + g], F32) for g in range(Q_PER_KV)], axis=0)


def _ctx_attn_kernel(l, sink_ref, q_ref, k_ref, v_ref, o_ref):
    q = q_ref[...]
    k = k_ref[...]
    v = v_ref[...]
    half = CTX_LEN // 2
    pieces = [(h, r) for h in range(N_Q_HEADS) for r in range(2)]
    kb = k.astype(BF16)
    vb = v.astype(BF16)
    kv = lambda a, h: a[:, (h // Q_PER_KV) * HEAD_DIM:(h // Q_PER_KV + 1) * HEAD_DIM]
    qs = [(q[r * half:(r + 1) * half, h * HEAD_DIM:(h + 1) * HEAD_DIM] * ATTN_SCALE).astype(BF16) for h, r in pieces]
    s = [_dg(qs[n], kv(kb, h), _NT) for n, (h, r) in enumerate(pieces)]
    m = [jnp.maximum(jnp.max(s[n], axis=-1, keepdims=True), sink_ref[l, h]) for n, (h, r) in enumerate(pieces)]
    e = [jnp.exp(s[n] - m[n]) for n in range(len(pieces))]
    den = [jnp.sum(e[n], axis=-1, keepdims=True) + jnp.exp(sink_ref[l, h] - m[n]) for n, (h, r) in enumerate(pieces)]
    o = [_dg(e[n].astype(BF16), kv(vb, h), _NN) / den[n] for n, (h, r) in enumerate(pieces)]
    for n, (h, r) in enumerate(pieces):
        o_ref[r * half:(r + 1) * half, h * HEAD_DIM:(h + 1) * HEAD_DIM] = o[n]


def _ctx_attn_call(q, k, v, sink, l):
    return pl.pallas_call(
        functools.partial(_ctx_attn_kernel, l),
        grid=(N_CTX_SEQ,),
        in_specs=[
            pl.BlockSpec(memory_space=pltpu.SMEM),
            pl.BlockSpec((CTX_LEN, ATTN_W), lambda b: (b, 0)),
            pl.BlockSpec((CTX_LEN, KV_W), lambda b: (b, 0)),
            pl.BlockSpec((CTX_LEN, KV_W), lambda b: (b, 0)),
        ],
        out_specs=pl.BlockSpec((CTX_LEN, ATTN_W), lambda b: (b, 0)),
        out_shape=jax.ShapeDtypeStruct((N_CTX, ATTN_W), F32),
        compiler_params=_cparams(("parallel",)),
        name="context_attention",
    )(sink, q, k, v)


def _rope(x, cos, sin):
    lane = lax.broadcasted_iota(jnp.int32, x.shape, 1)
    partner = jnp.where((lane & 16) == 0, pltpu.roll(x, 128 - 16, 1), pltpu.roll(x, 16, 1))
    return x * cos + partner * sin


def _lat_attn_kernel(l, sink_ref, q_ref, k_ref, v_ref, kc_ref, vc_ref, cos_ref, sin_ref, o_ref):
    n = pl.program_id(1)
    nb = LAT_LEN // BLOCK
    r0 = pl.multiple_of(n * BLOCK, BLOCK)
    cos_q = cos_ref[pl.ds(r0, BLOCK), :]
    sin_q = sin_ref[pl.ds(r0, BLOCK), :]
    q = jnp.concatenate(
        [_rope(q_ref[:, j * 128:(j + 1) * 128], cos_q, sin_q) for j in range(ATTN_W // 128)], axis=1)

    ks, vs = [], []
    for dj in (-1, 0, 1):
        rj = pl.multiple_of(jnp.clip(n + dj, 0, nb - 1) * BLOCK, BLOCK)
        ks.append(_rope(k_ref[pl.ds(rj, BLOCK), :], cos_ref[pl.ds(rj, BLOCK), :], sin_ref[pl.ds(rj, BLOCK), :]))
        vs.append(v_ref[pl.ds(rj, BLOCK), :])
    k_all = jnp.concatenate(ks + [kc_ref[...]], axis=0)
    v_all = jnp.concatenate(vs + [vc_ref[...]], axis=0)

    g = LAT_ATTN_HEADS
    pieces = [tuple(range(h0, h0 + g)) for h0 in range(0, N_Q_HEADS, g)]
    kb = k_all.astype(BF16)
    vb = v_all.astype(BF16)
    kv = lambda a, hs: a[:, (hs[0] // Q_PER_KV) * HEAD_DIM:(hs[0] // Q_PER_KV + 1) * HEAD_DIM]
    kj = lax.broadcasted_iota(jnp.int32, (g * BLOCK, BLOCK), 1)
    qi = lax.broadcasted_iota(jnp.int32, (g * BLOCK, BLOCK), 0) & (BLOCK - 1)
    keep_prev = (kj >= qi) & (n > 0)
    keep_next = (kj <= qi) & (n < nb - 1)
    all_pieces = pieces
    for g0 in range(0, len(all_pieces), LAT_ATTN_GROUP):
        pieces = all_pieces[g0:g0 + LAT_ATTN_GROUP]
        sinks = [jnp.concatenate([jnp.full((BLOCK, 1), sink_ref[l, h], F32) for h in hs], axis=0) for hs in pieces]
        qs = [jnp.concatenate([q[:, h * HEAD_DIM:(h + 1) * HEAD_DIM] for h in hs], axis=0) for hs in pieces]
        s = [_dg((qs[i] * ATTN_SCALE).astype(BF16), kv(kb, hs), _NT) for i, hs in enumerate(pieces)]
        s = [jnp.concatenate([jnp.where(keep_prev, s[i][:, 0:BLOCK], NEG), s[i][:, BLOCK:2 * BLOCK],
                              jnp.where(keep_next, s[i][:, 2 * BLOCK:3 * BLOCK], NEG), s[i][:, 3 * BLOCK:]], axis=1)
             for i in range(len(pieces))]
        m = [jnp.maximum(jnp.max(s[i], axis=-1, keepdims=True), sinks[i]) for i in range(len(pieces))]
        e = [jnp.exp(s[i] - m[i]) for i in range(len(pieces))]
        den = [jnp.sum(e[i], axis=-1, keepdims=True) + jnp.exp(sinks[i] - m[i]) for i in range(len(pieces))]
        o = [_dg(e[i].astype(BF16), kv(vb, hs), _NN) / den[i] for i, hs in enumerate(pieces)]
        for i, hs in enumerate(pieces):
            for j, h in enumerate(hs):
                o_ref[:, h * HEAD_DIM:(h + 1) * HEAD_DIM] = o[i][j * BLOCK:(j + 1) * BLOCK]


def _lat_attn_call(q, k, v, kc, vc, cos, sin, sink, l):
    nb = LAT_LEN // BLOCK
    q_off = N_CTX // BLOCK
    kv_off = N_CTX // LAT_LEN
    return pl.pallas_call(
        functools.partial(_lat_attn_kernel, l),
        grid=(N_LAT_SEQ, nb),
        in_specs=[
            pl.BlockSpec(memory_space=pltpu.SMEM),
            pl.BlockSpec((BLOCK, ATTN_W), lambda b, n: (q_off + b * nb + n, 0)),
            pl.BlockSpec((LAT_LEN, KV_W), lambda b, n: (kv_off + b, 0)),
            pl.BlockSpec((LAT_LEN, KV_W), lambda b, n: (kv_off + b, 0)),
            pl.BlockSpec((None, None, PAST_LEN, KV_W), lambda b, n: (b, l, 0, 0)),
            pl.BlockSpec((None, None, PAST_LEN, KV_W), lambda b, n: (b, l, 0, 0)),
            pl.BlockSpec((LAT_LEN, 128), lambda b, n: (0, 0)),
            pl.BlockSpec((LAT_LEN, 128), lambda b, n: (0, 0)),
        ],
        out_specs=pl.BlockSpec((BLOCK, ATTN_W), lambda b, n: (b * nb + n, 0)),
        out_shape=jax.ShapeDtypeStruct((N_LAT, ATTN_W), F32),
        compiler_params=_cparams(("parallel", "arbitrary")),
        name="latent_attention",
    )(sink, q, k, v, kc, vc, cos, sin)


def _rope_tables():
    t = np.arange(LAT_LEN)
    lane = np.arange(128)
    d = lane % HEAD_DIM
    pos = np.where(d[None, :] < HEAD_DIM // 2, (t // GRID_W)[:, None], (t % GRID_W)[:, None]).astype(np.float32)
    quarter = HEAD_DIM // 4
    inv = (np.float32(ROPE_THETA) ** (-(np.arange(quarter, dtype=np.float32)) / np.float32(quarter))).astype(np.float32)
    ang = pos * inv[d % quarter][None, :]
    sign = np.where((lane & 16) == 0, -1.0, 1.0)[None, :]
    return jnp.asarray(np.cos(ang), F32), jnp.asarray(np.sin(ang) * sign, F32)


PAIR_W = 2 * HEAD_DIM
N_PAIRS = RWKV_HEADS // 2
_TN = (((0,), (0,)), ((), ()))


def _pair_stack(x, pmask):
    return jnp.concatenate([x.astype(BF16)] * 2, axis=0) * pmask


def _hdot(m, x, pmask, dims):
    return jnp.concatenate(
        [_dg(m[:, p * PAIR_W:(p + 1) * PAIR_W], _pair_stack(x[:, p * PAIR_W:(p + 1) * PAIR_W], pmask), dims)
         for p in range(N_PAIRS)], axis=1)


def _scan_prepare(reverse, r, v, kn, lw, kd, b, cs, strict, incl, eye, pmask):
    linc = _dot_exact_lhs(cs, lw)
    end = 0 if reverse else CHUNK - 1
    ltot = jnp.concatenate(
        [jnp.broadcast_to(linc[c * CHUNK + end:c * CHUNK + end + 1, :], (CHUNK, RWKV_W))
         for c in range(CHUNKS_PER_TILE)], axis=0)
    v_all = v.astype(BF16)
    r_t = r * jnp.exp(linc)
    a_t = (-kn * jnp.exp(linc - lw)).astype(BF16)
    e_neg = jnp.exp(-linc)
    b_t = (b * e_neg).astype(BF16)
    k_t = (kd * e_neg).astype(BF16)
    e_rem = jnp.exp(ltot - linc)
    b_p = (b * e_rem).astype(BF16)
    k_p = (kd * e_rem).astype(BF16)
    dec = jnp.exp(ltot)

    chunks = range(CHUNKS_PER_TILE)
    sls = [slice(c * CHUNK, (c + 1) * CHUNK) for c in chunks]
    hd = lambda m, x, dims: _hdot(m, x, pmask, dims)
    ar = [jnp.concatenate([a_t[sl], r_t[sl].astype(BF16)], axis=0) for sl in sls]
    m_b = [hd(ar[c], b_t[sls[c]], _NT) for c in chunks]
    m_k = [hd(ar[c], k_t[sls[c]], _NT) for c in chunks]
    a_ab = [m_b[c][:CHUNK] * strict for c in chunks]
    p_rb = [(m_b[c][CHUNK:] * incl).astype(BF16) for c in chunks]
    ak_rk = [jnp.concatenate([m_k[c][:CHUNK] * strict, m_k[c][CHUNK:] * incl], axis=0).astype(BF16)
             for c in chunks]

    tinv = [eye + a_ab[c] for c in chunks]
    p = [a_ab[c].astype(BF16) for c in chunks]
    p = [hd(p[c], p[c], _NN).astype(BF16) for c in chunks]
    for _ in range(4):
        pt = [hd(jnp.concatenate([p[c], tinv[c].astype(BF16)], axis=0), p[c], _NN) for c in chunks]
        p = [pt[c][:CHUNK].astype(BF16) for c in chunks]
        tinv = [tinv[c] + pt[c][CHUNK:] for c in chunks]
    tinv = [(tinv[c] + hd(tinv[c].astype(BF16), p[c], _NN)).astype(BF16) for c in chunks]

    m_v = [hd(ak_rk[c], v_all[sls[c]], _NN) for c in chunks]
    w = [hd(tinv[c], a_t[sls[c]], _NN) for c in chunks]
    u0 = [hd(tinv[c], m_v[c][:CHUNK], _NN) for c in chunks]
    qp = [r_t[sls[c]] + hd(p_rb[c], w[c], _NN) for c in chunks]
    y0 = [hd(p_rb[c], u0[c], _NN) + m_v[c][CHUNK:] for c in chunks]
    pr = [slice(p * PAIR_W, (p + 1) * PAIR_W) for p in range(N_PAIRS)]
    pmask_f = pmask.astype(F32)
    bk = [jnp.concatenate([b_p[sl], k_p[sl]], axis=0) for sl in sls]
    uv = [jnp.concatenate([u0[c].astype(BF16), v_all[sls[c]]], axis=0) for c in chunks]
    wb = [[_dg(w[c][:, q].astype(BF16), b_p[sls[c]][:, q], _TN) * pmask_f for q in pr] for c in chunks]
    gs = [[_dg(uv[c][:, q], bk[c][:, q], _TN) * pmask_f for q in pr] for c in chunks]
    qpb = [qp[c].astype(BF16) for c in chunks]
    decs = [dec[c * CHUNK:c * CHUNK + 1, :] for c in chunks]
    return qpb, y0, wb, gs, decs


def _scan_chain_step(c, s, prep, y_ref):
    qpb, y0, wb, gs, decs = prep
    pr = [slice(p * PAIR_W, (p + 1) * PAIR_W) for p in range(N_PAIRS)]
    sb = [s[p].astype(BF16) for p in range(N_PAIRS)]
    y_ref[c * CHUNK:(c + 1) * CHUNK, :] = y0[c] + jnp.concatenate(
        [_dg(qpb[c][:, pr[p]], sb[p], _NT) for p in range(N_PAIRS)], axis=1)
    return [s[p] * decs[c][:, pr[p]] + (_dg(sb[p], wb[c][p].astype(BF16), _NN) + gs[c][p]) for p in range(N_PAIRS)]


N_STREAMS = 2
STREAM_TILES = N_TILES // N_STREAMS
STREAM_CTX = CTX_TILES // N_STREAMS
SCAN_LANES = tuple((rev, st) for rev in (False, True) for st in range(N_STREAMS))
N_LANE_IN = 7


def _stream_tile(stream, pos):
    lat0 = CTX_TILES + stream * (STREAM_TILES - STREAM_CTX)
    return jnp.where(pos < STREAM_CTX, stream * STREAM_CTX + pos, lat0 + pos - STREAM_CTX)


def _lane_pos(reverse, k):
    return (STREAM_TILES - 1 - k) if reverse else k


def _scan_kernel(*refs):
    n_lanes = len(SCAN_LANES)
    lane_in = [refs[n * N_LANE_IN:(n + 1) * N_LANE_IN] for n in range(n_lanes)]
    cs_ref, strict_ref, incl_ref, eye_ref, pmask_ref = refs[n_lanes * N_LANE_IN:n_lanes * N_LANE_IN + 5]
    outs = refs[n_lanes * N_LANE_IN + 5:n_lanes * N_LANE_IN + 5 + 2 * n_lanes]
    lane_out = [outs[2 * n:2 * n + 2] for n in range(n_lanes)]
    lane_scr = refs[n_lanes * N_LANE_IN + 5 + 2 * n_lanes:]

    k = pl.program_id(0)
    pmask_f = pmask_ref[...]
    pmask = pmask_f.astype(BF16)
    eye = eye_ref[...]

    def load_state(s0_ref, scr):
        for p in range(N_PAIRS):
            scr[p] = jnp.concatenate([s0_ref[p * PAIR_W:(p + 1) * PAIR_W, :]] * 2, axis=1) * pmask_f

    for n, (reverse, stream) in enumerate(SCAN_LANES):
        pos = _lane_pos(reverse, k)
        j = (pos - STREAM_CTX) % LAT_TILES_PER_SEQ
        first = (pos < STREAM_CTX) | (j == (LAT_TILES_PER_SEQ - 1 if reverse else 0))
        pl.when(first)(functools.partial(load_state, lane_in[n][6], lane_scr[n]))

    preps = []
    for n, (reverse, stream) in enumerate(SCAN_LANES):
        d = 1 if reverse else 0
        r_ref, v_ref, kn_ref, lw_ref, kd_ref, b_ref, _ = lane_in[n]
        preps.append(_scan_prepare(reverse, r_ref[...], v_ref[...], kn_ref[...], lw_ref[...], kd_ref[...], b_ref[...],
                                   cs_ref[d], strict_ref[d], incl_ref[d], eye, pmask))
    states = [[lane_scr[n][p] for p in range(N_PAIRS)] for n in range(n_lanes)]
    for c in range(CHUNKS_PER_TILE):
        for n, (reverse, stream) in enumerate(SCAN_LANES):
            cc = CHUNKS_PER_TILE - 1 - c if reverse else c
            states[n] = _scan_chain_step(cc, states[n], preps[n], lane_out[n][0])
    for n in range(n_lanes):
        for p in range(N_PAIRS):
            lane_scr[n][p] = states[n][p]
            lane_out[n][1][p] = states[n][p]


def _scan_call(r, v, kn, lw, kd, b, s0, consts, l):
    cs, strict, incl, eye, pmask = consts
    seq_of = lambda t: jnp.where(t < CTX_TILES, t, CTX_TILES + (t - CTX_TILES) // LAT_TILES_PER_SEQ)
    const2 = lambda k: (0, 0)
    const3 = lambda k: (0, 0, 0)
    in_specs, out_specs, args = [], [], []
    for reverse, stream in SCAN_LANES:
        d = 1 if reverse else 0
        tile = functools.partial(lambda rv, st, k: _stream_tile(st, _lane_pos(rv, k)), reverse, stream)
        pos = functools.partial(_lane_pos, reverse)
        tok = pl.BlockSpec((TM, RWKV_W), functools.partial(lambda tl, k: (tl(k), 0), tile))
        tok_d = pl.BlockSpec((None, TM, RWKV_W), functools.partial(lambda tl, dd, k: (dd, tl(k), 0), tile, d))
        s0_spec = pl.BlockSpec((None, None, None, RWKV_W, HEAD_DIM),
                               functools.partial(lambda tl, dd, k: (seq_of(tl(k)), l, dd, 0, 0), tile, d))
        in_specs += [tok, tok, tok, tok_d, tok_d, tok_d, s0_spec]
        args += [r, v, kn, lw, kd, b, s0]
        out_specs += [
            pl.BlockSpec((TM, RWKV_W), functools.partial(lambda ps, k: (ps(k), 0), pos)),
            pl.BlockSpec((None, N_PAIRS, PAIR_W, PAIR_W),
                         functools.partial(lambda ps, k: (jnp.minimum(ps(k), STREAM_CTX), 0, 0, 0), pos)),
        ]
    in_specs += [
        pl.BlockSpec((2, TM, TM), const3),
        pl.BlockSpec((2, CHUNK, RWKV_W), const3),
        pl.BlockSpec((2, CHUNK, RWKV_W), const3),
        pl.BlockSpec((CHUNK, RWKV_W), const2),
        pl.BlockSpec((PAIR_W, PAIR_W), const2),
    ]
    args += [cs, strict, incl, eye, pmask]
    y = jax.ShapeDtypeStruct((STREAM_TILES * TM, RWKV_W), F32)
    sfin = jax.ShapeDtypeStruct((STREAM_CTX + 1, N_PAIRS, PAIR_W, PAIR_W), F32)
    state = pltpu.VMEM((N_PAIRS, PAIR_W, PAIR_W), F32)
    outs = pl.pallas_call(
        _scan_kernel,
        grid=(STREAM_TILES,),
        in_specs=in_specs,
        out_specs=out_specs,
        out_shape=[y, sfin] * len(SCAN_LANES),
        scratch_shapes=[state] * len(SCAN_LANES),
        compiler_params=_cparams(("arbitrary",)),
        name="rwkv_scan",
    )(*args)
    return outs[0::2], outs[1::2]


def _scan_final_states(sfins):
    per_dir = [jnp.concatenate([sfins[d * N_STREAMS + st][:STREAM_CTX] for st in range(N_STREAMS)], axis=0)
               for d in range(2)]
    fin = jnp.stack(per_dir, axis=1)
    fin = fin.reshape(N_CTX_SEQ, 2, N_PAIRS, 2, HEAD_DIM, 2, HEAD_DIM)
    fin = jnp.stack([fin[:, :, :, h, :, h, :] for h in range(2)], axis=3)
    return fin.reshape(N_CTX_SEQ, 2, RWKV_HEADS, HEAD_DIM, HEAD_DIM)


def _scan_consts():
    t = np.arange(TM)
    same_chunk = (t[:, None] // CHUNK) == (t[None, :] // CHUNK)
    cs_f = same_chunk & (t[None, :] <= t[:, None])
    cs_b = same_chunk & (t[None, :] >= t[:, None])
    cs = np.stack([cs_f, cs_b]).astype(np.float32)
    q = np.arange(CHUNK)[:, None]
    kcol = np.arange(RWKV_W)[None, :] % CHUNK
    strict = np.stack([kcol < q, kcol > q]).astype(np.float32)
    incl = np.stack([kcol <= q, kcol >= q]).astype(np.float32)
    eye = (kcol == q).astype(np.float32)
    return (jnp.asarray(cs, BF16), jnp.asarray(strict), jnp.asarray(incl),
            jnp.asarray(eye), jnp.asarray(_head_block_mask(PAIR_W)))


FNET_TT = 512
FNET_IN_ROWS = N_CTX


def _fnet_kernel(seq_len, n_in, *refs):
    z_refs = refs[:n_in]
    ch_ref, cl_ref, th_ref, o_ref, z_scr = refs[n_in:]
    b = pl.program_id(1)

    @pl.when((pl.program_id(0) == 0) & (b == 0))
    def _():
        per_in = FNET_IN_ROWS // seq_len
        for n, z_ref in enumerate(z_refs):
            zh, zl = _split2(z_ref[...])
            for half in range(2):
                ch = ch_ref[half]
                cl = cl_ref[half]
                zc = (_dg(zh, ch, _NN) + (_dg(zh, cl, _NN) + _dg(zl, ch, _NN))).astype(BF16)
                for s in range(per_in):
                    z_scr[n * per_in + s, half * seq_len:(half + 1) * seq_len, :] = zc[s * seq_len:(s + 1) * seq_len]

    o_ref[...] = _dg(th_ref[...], z_scr[b], _NN)


def _fnet_call(u_f, seq_len, n_seq, row_off, tables, name):
    ch, cl, th = tables
    tt = min(FNET_TT, seq_len)
    nt = seq_len // tt
    n_in = n_seq * seq_len // FNET_IN_ROWS
    in_off = row_off // FNET_IN_ROWS
    z_specs = [pl.BlockSpec((FNET_IN_ROWS, FNET_W), functools.partial(lambda n, j, b: (in_off + n, 0), n))
               for n in range(n_in)]
    return pl.pallas_call(
        functools.partial(_fnet_kernel, seq_len, n_in),
        grid=(nt, n_seq),
        in_specs=z_specs + [
            pl.BlockSpec((2, FNET_W, FNET_W), lambda j, b: (0, 0, 0)),
            pl.BlockSpec((2, FNET_W, FNET_W), lambda j, b: (0, 0, 0)),
            pl.BlockSpec((tt, 2 * seq_len), lambda j, b: (j, 0)),
        ],
        out_specs=pl.BlockSpec((tt, FNET_W), lambda j, b: (b * nt + j, 0)),
        out_shape=jax.ShapeDtypeStruct((n_seq * seq_len, FNET_W), F32),
        scratch_shapes=[pltpu.VMEM((n_seq, 2 * seq_len, FNET_W), BF16)],
        compiler_params=_cparams(("arbitrary", "arbitrary")),
        name=name,
    )(*([u_f] * n_in), ch, cl, th)


DFT_SPLIT = 32


def _dft_tables(seq_len):
    c = np.arange(FNET_W)
    ang_c = 2.0 * np.pi * ((c[:, None] % HEAD_DIM) * (c[None, :] % HEAD_DIM) % HEAD_DIM) / HEAD_DIM
    scale = 1.0 / np.sqrt(float(seq_len) * HEAD_DIM)
    blk = _head_block_mask(FNET_W)
    ch, cl = _split2(jnp.asarray(np.stack([np.cos(ang_c) * blk * scale, np.sin(ang_c) * blk * scale]), F32))

    n1 = seq_len // DFT_SPLIT
    f = np.arange(seq_len)
    a1 = 2.0 * np.pi * ((np.arange(n1)[:, None] * f[None, :]) % n1) / n1
    a2 = 2.0 * np.pi * ((np.arange(DFT_SPLIT)[:, None] * f[None, :]) % seq_len) / seq_len
    c1, s1 = jnp.asarray(np.cos(a1), F32)[:, None, :], jnp.asarray(np.sin(a1), F32)[:, None, :]
    c2, s2 = jnp.asarray(np.cos(a2), F32)[None, :, :], jnp.asarray(np.sin(a2), F32)[None, :, :]
    cos_t = (c1 * c2 - s1 * s2).reshape(seq_len, seq_len)
    sin_t = (s1 * c2 + c1 * s2).reshape(seq_len, seq_len)
    th = jnp.concatenate([cos_t, -sin_t], axis=1).astype(BF16)
    return ch, cl, th


def _merge_ffn_kernel(split, final, *refs):
    i = pl.program_id(0)
    n_x = 2 if split else 1
    x_in = _tile_of_split(i, *refs[:2], FFN_TM) if split else refs[0][...]
    (mod_ref, oac_ref, oal_ref, yf0_ref, yf1_ref, yb0_ref, yb1_ref, bonus_ref, g_ref, ofc_ref, ofl_ref,
     lng_ref, lnb_ref, avg_ref, nfg_ref, wo_ref, wi_ref, w2_ref) = refs[n_x:n_x + 18]
    tail = refs[n_x + 18:]
    is_ctx = i < N_CTX // FFN_TM
    o_attn = _tile_of_split(i, oac_ref, oal_ref, FFN_TM)
    o_fnet = _tile_of_split(i, ofc_ref, ofl_ref, FFN_TM)
    avg = avg_ref[...]
    y = jnp.where(_in_stream0(i, FFN_TM), yf0_ref[...] + yb0_ref[...], yf1_ref[...] + yb1_ref[...])
    yc = y - _dot_exact_rhs(y, avg)
    var = _dot_exact_rhs(yc * yc, avg)
    yn = yc * lax.rsqrt(var + GN_EPS) * lng_ref[...] + lnb_ref[...]
    o_rwkv = (yn + bonus_ref[...]) * g_ref[...]

    o = (_dg(o_attn.astype(BF16), wo_ref[0:ATTN_W, :], _NN)
         + _dg(o_rwkv.astype(BF16), wo_ref[ATTN_W:ATTN_W + RWKV_W, :], _NN)
         + _dg(o_fnet.astype(BF16), wo_ref[ATTN_W + RWKV_W:, :], _NN))
    g1 = mod_ref[:, 2 * D_MODEL:3 * D_MODEL]
    sh2 = mod_ref[:, 3 * D_MODEL:4 * D_MODEL]
    sc2 = mod_ref[:, 4 * D_MODEL:5 * D_MODEL]
    g2 = mod_ref[:, 5 * D_MODEL:6 * D_MODEL]
    x = x_in + g1 * o
    h = _rms(x, nfg_ref[...]) * (1.0 + sc2) + sh2
    gu = _dg(h.astype(BF16), wi_ref[...], _NN)
    gt = gu[:, 0:D_FF]
    act = gt * jax.nn.sigmoid(gt) * gu[:, D_FF:]
    x = x + g2 * _dg(act.astype(BF16), w2_ref[...], _NN)
    if not final:
        (o_ref,) = tail
        o_ref[...] = x
    else:
        fng_ref, oc_ref, ol_ref = tail
        y_out = _rms(x, fng_ref[...])

        @pl.when(is_ctx)
        def _():
            oc_ref[...] = y_out

        @pl.when(jnp.logical_not(is_ctx))
        def _():
            ol_ref[...] = y_out


def _in_stream0(m, tm):
    ctx_m = N_CTX // tm
    lat_ps = (N_TOK // tm - ctx_m) // N_STREAMS
    return (m < ctx_m // N_STREAMS) | ((m >= ctx_m) & (m < ctx_m + lat_ps))


def _stream_block(stream, tm):
    ctx_m = N_CTX // tm
    half = ctx_m // N_STREAMS
    lat_ps = (N_TOK // tm - ctx_m) // N_STREAMS
    if stream == 0:
        return lambda m: (jnp.where(m < half, m, jnp.where(m < ctx_m, half - 1,
                                    jnp.where(m < ctx_m + lat_ps, m - half, half + lat_ps - 1))), 0)
    return lambda m: (jnp.where(m < half, 0, jnp.where(m < ctx_m, m - half,
                                jnp.where(m < ctx_m + lat_ps, half - 1, m - half - lat_ps))), 0)


def _merge_ffn_call(xs, mods_l, oa_ctx, oa_lat, ys, bonus, g, of_ctx, of_lat, ln_g, ln_b, avg, nfg,
                    wo, wi, w2, l, final_g=None):
    final = final_g is not None
    split = isinstance(xs, tuple)
    xs = xs if split else (xs,)
    row = lambda i: (l, 0, 0)
    tile = lambda i: (i, 0)
    ctx_tile, lat_tile = _ctx_tile(FFN_TM), _lat_tile(FFN_TM)
    once = pl.Buffered(1)
    in_specs = _x_specs(split, FFN_TM) + [
        pl.BlockSpec((None, 1, 6 * D_MODEL), lambda i: (_mod_row(i, FFN_TM), 0, 0)),
        pl.BlockSpec((FFN_TM, ATTN_W), ctx_tile),
        pl.BlockSpec((FFN_TM, ATTN_W), lat_tile),
        pl.BlockSpec((FFN_TM, RWKV_W), _stream_block(0, FFN_TM)),
        pl.BlockSpec((FFN_TM, RWKV_W), _stream_block(1, FFN_TM)),
        pl.BlockSpec((FFN_TM, RWKV_W), _stream_block(0, FFN_TM)),
        pl.BlockSpec((FFN_TM, RWKV_W), _stream_block(1, FFN_TM)),
        pl.BlockSpec((FFN_TM, RWKV_W), tile),
        pl.BlockSpec((FFN_TM, RWKV_W), tile),
        pl.BlockSpec((FFN_TM, FNET_W), ctx_tile),
        pl.BlockSpec((FFN_TM, FNET_W), lat_tile),
        pl.BlockSpec((None, 1, RWKV_W), row),
        pl.BlockSpec((None, 1, RWKV_W), row),
        pl.BlockSpec((RWKV_W, RWKV_W), lambda i: (0, 0)),
        pl.BlockSpec((None, 1, D_MODEL), row),
        pl.BlockSpec((None, D_MODEL, D_MODEL), row, pipeline_mode=once),
        pl.BlockSpec((None, D_MODEL, 2 * D_FF), row, pipeline_mode=once),
        pl.BlockSpec((None, D_FF, D_MODEL), row, pipeline_mode=once),
    ]
    args = [*xs, mods_l, oa_ctx, oa_lat, *ys, bonus, g, of_ctx, of_lat, ln_g, ln_b, avg, nfg, wo, wi, w2]
    if final:
        in_specs.append(pl.BlockSpec((1, D_MODEL), lambda i: (0, 0)))
        args.append(final_g)
        out_specs = [pl.BlockSpec((FFN_TM, D_MODEL), ctx_tile), pl.BlockSpec((FFN_TM, D_MODEL), lat_tile)]
        out_shape = [jax.ShapeDtypeStruct((N_CTX, D_MODEL), F32), jax.ShapeDtypeStruct((N_LAT, D_MODEL), F32)]
    else:
        out_specs = pl.BlockSpec((FFN_TM, D_MODEL), tile)
        out_shape = jax.ShapeDtypeStruct((N_TOK, D_MODEL), F32)
    return pl.pallas_call(
        functools.partial(_merge_ffn_kernel, split, final),
        grid=(N_TOK // FFN_TM,),
        in_specs=in_specs,
        out_specs=out_specs,
        out_shape=out_shape,
        compiler_params=_cparams(("arbitrary",)),
        name="merge_ffn_final" if final else "merge_ffn",
    )(*args)


def _block_diag2(w):
    z = jnp.zeros_like(w[:, 0])
    return jnp.concatenate([jnp.concatenate([w[:, 0], z], axis=2), jnp.concatenate([z, w[:, 1]], axis=2)], axis=1)


def kernel(x_prompt, x_sample, cache_k, cache_v, state_rwkv, c, c_ctx, w_ada, b_ada, norm_mix_g, norm_ffn_g,
           w_in, w_out, attn_sink, rwkv_shift_w, rwkv_w0, rwkv_w_up, rwkv_a0, rwkv_a_up, rwkv_g_up,
           rwkv_k_k, rwkv_k_a, rwkv_r_k, rwkv_ln_g, rwkv_ln_b, ffn_w_in, ffn_w_out, norm_final_g):
    x = (x_prompt.reshape(N_CTX, D_MODEL), x_sample.reshape(N_LAT, D_MODEL))
    cvec = jnp.concatenate([c_ctx[None, :], c, jnp.zeros((8 - 1 - N_LAT_SEQ, D_MODEL), F32)], axis=0)
    mods = _mods_call(cvec, w_ada, b_ada).reshape(DEPTH, 8, 1, 6 * D_MODEL)

    w_in_b = w_in.astype(BF16)
    w_out_b = w_out.astype(BF16)
    ffn_in_b = ffn_w_in.astype(BF16)
    ffn_out_b = ffn_w_out.astype(BF16)
    wup_bd = _block_diag2(rwkv_w_up)
    aup_bd = _block_diag2(rwkv_a_up)
    row3 = lambda a: a.reshape(DEPTH, 1, -1)
    nmg, nfg = row3(norm_mix_g), row3(norm_ffn_g)
    k_k, k_a, r_k = row3(rwkv_k_k), row3(rwkv_k_a), row3(rwkv_r_k)
    ln_g, ln_b = row3(rwkv_ln_g), row3(rwkv_ln_b)
    kc = cache_k.reshape(N_LAT_SEQ, DEPTH, PAST_LEN, KV_W)
    vc = cache_v.reshape(N_LAT_SEQ, DEPTH, PAST_LEN, KV_W)
    s0 = jnp.concatenate([jnp.zeros((N_CTX_SEQ, DEPTH, 2, RWKV_W, HEAD_DIM), F32),
                          state_rwkv.reshape(N_LAT_SEQ, DEPTH, 2, RWKV_W, HEAD_DIM)], axis=0)

    cos, sin = _rope_tables()
    consts = _scan_consts()
    bmask = jnp.asarray(_head_block_mask(RWKV_W))
    ones_bd = bmask.astype(BF16)
    avg_bd = (bmask * (1.0 / HEAD_DIM)).astype(BF16)
    rwkv_params = (rwkv_shift_w, wup_bd, aup_bd, rwkv_g_up, rwkv_w0, rwkv_a0, k_k, k_a, r_k, ones_bd)
    dft_ctx = _dft_tables(CTX_LEN)
    dft_lat = _dft_tables(LAT_LEN)

    new_k, new_v, new_s = [], [], []
    for l in range(DEPTH):
        q, k, v, u_f, r, vv, kn, g, bonus, lw, kd, b = _inproj_call(x, mods[l], nmg, w_in_b, rwkv_params, l)
        new_k.append(k[:N_CTX].reshape(N_CTX_SEQ, CTX_LEN, KV_W))
        new_v.append(v[:N_CTX].reshape(N_CTX_SEQ, CTX_LEN, KV_W))

        oa_ctx = _ctx_attn_call(q, k, v, attn_sink, l)
        oa_lat = _lat_attn_call(q, k, v, kc, vc, cos, sin, attn_sink, l)

        ys, sfins = _scan_call(r, vv, kn, lw, kd, b, s0, consts, l)
        new_s.append(_scan_final_states(sfins))

        of_ctx = _fnet_call(u_f, CTX_LEN, N_CTX_SEQ, 0, dft_ctx, "fnet_context")
        of_lat = _fnet_call(u_f, LAT_LEN, N_LAT_SEQ, N_CTX, dft_lat, "fnet_latent")

        last = l == DEPTH - 1
        x = _merge_ffn_call(x, mods[l], oa_ctx, oa_lat, ys, bonus, g, of_ctx, of_lat, ln_g, ln_b, avg_bd, nfg,
                            w_out_b, ffn_in_b, ffn_out_b, l,
                            final_g=norm_final_g.reshape(1, D_MODEL) if last else None)

    y_ctx, y_lat = x
    return (y_ctx.reshape(N_CTX_SEQ, CTX_LEN, D_MODEL), y_lat.reshape(N_LAT_SEQ, LAT_LEN, D_MODEL),
            jnp.stack(new_k, axis=1).reshape(N_CTX_SEQ, DEPTH, CTX_LEN, N_KV_HEADS, HEAD_DIM),
            jnp.stack(new_v, axis=1).reshape(N_CTX_SEQ, DEPTH, CTX_LEN, N_KV_HEADS, HEAD_DIM),
            jnp.stack(new_s, axis=1))
```

```python
import functools

import numpy as np
import jax
import jax.numpy as jnp
from jax import lax
from jax.experimental import pallas as pl
from jax.experimental.pallas import tpu as pltpu

F32 = jnp.float32
BF16 = jnp.bfloat16

D_MODEL = 1024
N_CTX_SEQ = 16
CTX_LEN = 256
DEPTH = 4
N_LAT_SEQ = 4
LAT_LEN = 2048
PAST_LEN = 512
GRID_W = 64
HEAD_DIM = 64
N_Q_HEADS = 8
N_KV_HEADS = 2
Q_PER_KV = N_Q_HEADS // N_KV_HEADS
ATTN_W = N_Q_HEADS * HEAD_DIM
KV_W = N_KV_HEADS * HEAD_DIM
BLOCK = 128
ATTN_SCALE = 0.125
ROPE_THETA = 10000.0
NEG = -1e30
RWKV_HEADS = 4
RWKV_W = RWKV_HEADS * HEAD_DIM
DECAY_RANK = 64
ICLR_RANK = 64
GATE_RANK = 128
DECAY_SCALE = 0.6065306597126334
GN_EPS = 64e-5
RWKV_IN_W = 3 * RWKV_W + GATE_RANK + 2 * DECAY_RANK + 2 * ICLR_RANK
FNET_W = 4 * HEAD_DIM
IN_W = ATTN_W + 2 * KV_W + RWKV_IN_W + FNET_W
D_FF = 2816
RMS_EPS = 1e-6

N_CTX = N_CTX_SEQ * CTX_LEN
N_LAT = N_LAT_SEQ * LAT_LEN
N_TOK = N_CTX + N_LAT
TM = 256
N_TILES = N_TOK // TM
CTX_TILES = N_CTX // TM
LAT_TILES_PER_SEQ = LAT_LEN // TM
N_SEQ = N_CTX_SEQ + N_LAT_SEQ
CHUNK = 64
CHUNKS_PER_TILE = TM // CHUNK
HALO = 8
FFN_TM = 512
LAT_ATTN_QBLOCKS = 2
LAT_ATTN_GROUP = 8

VMEM_LIMIT = 56 * 1024 * 1024


def _cparams(sem):
    return pltpu.CompilerParams(dimension_semantics=sem, vmem_limit_bytes=VMEM_LIMIT)


def _mod_row(i, tm=TM):
    return jnp.where(i < N_CTX // tm, 0, 1 + (i - N_CTX // tm) // (LAT_LEN // tm))


def _split2(x):
    hi = x.astype(BF16)
    lo = (x - hi.astype(F32)).astype(BF16)
    return hi, lo


_NN = (((1,), (0,)), ((), ()))
_NT = (((1,), (1,)), ((), ()))


def _dg(a, b, dims):
    return lax.dot_general(a, b, dims, preferred_element_type=F32)


def _dot1(a, b, dims=_NN):
    return _dg(a.astype(BF16), b.astype(BF16), dims)


def _dot3(a, b, dims=_NN):
    ah, al = _split2(a)
    bh, bl = _split2(b)
    return _dg(ah, bh, dims) + (_dg(ah, bl, dims) + _dg(al, bh, dims))


def _dot_exact_rhs(a, b_bf16):
    ah, al = _split2(a)
    return _dg(ah, b_bf16, _NN) + _dg(al, b_bf16, _NN)


def _dot_exact_lhs(a_bf16, b):
    bh, bl = _split2(b)
    return _dg(a_bf16, bh, _NN) + _dg(a_bf16, bl, _NN)


def _mods_kernel(c_ref, w_ref, b_ref, o_ref):
    c = c_ref[...]
    s = c * jax.nn.sigmoid(c)
    o_ref[...] = _dot1(s, w_ref[...]) + b_ref[...]


def _mods_call(cvec, w_ada, b_ada):
    tn = 1536
    return pl.pallas_call(
        _mods_kernel,
        grid=(DEPTH, 6 * D_MODEL // tn),
        in_specs=[
            pl.BlockSpec((8, D_MODEL), lambda l, j: (0, 0)),
            pl.BlockSpec((None, D_MODEL, tn), lambda l, j: (l, 0, j)),
            pl.BlockSpec((None, 1, tn), lambda l, j: (l, 0, j)),
        ],
        out_specs=pl.BlockSpec((None, 8, tn), lambda l, j: (l, 0, j)),
        out_shape=jax.ShapeDtypeStruct((DEPTH, 8, 6 * D_MODEL), F32),
        compiler_params=_cparams(("parallel", "parallel")),
        name="adaln_mods",
    )(cvec, w_ada, b_ada.reshape(DEPTH, 1, 6 * D_MODEL))


def _rms(x, g):
    return x * lax.rsqrt(jnp.mean(x * x, axis=-1, keepdims=True) + RMS_EPS) * g


def _tile_of_split(i, ctx_ref, lat_ref, tm=TM):
    return jnp.where(i < N_CTX // tm, ctx_ref[...], lat_ref[...])


def _ctx_tile(tm=TM):
    return lambda i: (jnp.minimum(i, N_CTX // tm - 1), 0)


def _lat_tile(tm=TM):
    return lambda i: (jnp.maximum(i - N_CTX // tm, 0), 0)


def _x_specs(split, tm=TM):
    if split:
        return [pl.BlockSpec((tm, D_MODEL), _ctx_tile(tm)), pl.BlockSpec((tm, D_MODEL), _lat_tile(tm))]
    return [pl.BlockSpec((tm, D_MODEL), lambda i: (i, 0))]


def _head_block_mask(n):
    r = np.arange(n)[:, None] // HEAD_DIM
    c = np.arange(n)[None, :] // HEAD_DIM
    return (r == c).astype(np.float32)


def _rwkv_token_terms(u, prev_row, next_row, sw_ref, wup_ref, aup_ref, gup_ref, w0_ref, a0_ref,
                      kk_ref, ka_ref, rk_ref, ones_ref, r_out, v_out, kn_out, g_out, bonus_out, lw_out, kd_out, b_out):
    row = lax.broadcasted_iota(jnp.int32, u.shape, 0)
    u_dn = jnp.where(row == 0, prev_row, pltpu.roll(u, 1, 0))
    u_up = jnp.where(row == TM - 1, next_row, pltpu.roll(u, TM - 1, 0))
    us = u_dn * sw_ref[0:1, :] + u * sw_ref[1:2, :] + u_up * sw_ref[2:3, :]

    r = us[:, 0:RWKV_W]
    k = us[:, RWKV_W:2 * RWKV_W]
    v = us[:, 2 * RWKV_W:3 * RWKV_W]
    o = 3 * RWKV_W
    gd = us[:, o:o + GATE_RANK]
    wd = us[:, o + GATE_RANK:o + GATE_RANK + 2 * DECAY_RANK]
    ad = us[:, o + GATE_RANK + 2 * DECAY_RANK:RWKV_IN_W]

    w_pre = _dot1(jnp.tanh(wd), wup_ref[...])
    a_pre = _dot1(ad, aup_ref[...])
    g_out[...] = _dot1(jax.nn.sigmoid(gd), gup_ref[...])

    ones_bd = ones_ref[...]
    kn = k * kk_ref[...]
    kn = kn * lax.rsqrt(_dot_exact_rhs(kn * kn, ones_bd) + 1e-12)
    r_out[...] = r
    v_out[...] = v
    kn_out[...] = kn
    bonus_out[...] = _dot_exact_rhs(r * k * rk_ref[...], ones_bd) * v
    ka = ka_ref[...]
    for d in range(2):
        sl = slice(d * RWKV_W, (d + 1) * RWKV_W)
        a_d = jax.nn.sigmoid(a0_ref[d:d + 1, :] + a_pre[:, sl])
        lw_out[d] = -DECAY_SCALE * jax.nn.sigmoid(w0_ref[d:d + 1, :] + w_pre[:, sl])
        kd_out[d] = k * (1.0 + (a_d - 1.0) * ka)
        b_out[d] = kn * a_d


RWKV_COL0 = ATTN_W + 2 * KV_W
N_RWKV_PARAMS = 10


def _inproj_kernel(split, *refs):
    n_x = 2 if split else 1
    i = pl.program_id(0)
    x = _tile_of_split(i, *refs[:2]) if split else refs[0][...]
    xp_ref, xn_ref, mod_ref, g_ref, w_ref = refs[n_x:n_x + 5]
    rwkv_params = refs[n_x + 5:n_x + 5 + N_RWKV_PARAMS]
    q_ref, k_ref, v_ref, uf_ref = refs[n_x + 5 + N_RWKV_PARAMS:n_x + 9 + N_RWKV_PARAMS]
    rwkv_outs = refs[n_x + 9 + N_RWKV_PARAMS:]

    shift = mod_ref[:, 0:D_MODEL]
    scale = mod_ref[:, D_MODEL:2 * D_MODEL]
    x_all = jnp.concatenate([x, xp_ref[...], xn_ref[...]], axis=0)
    h = _rms(x_all, g_ref[...]) * (1.0 + scale) + shift
    o = _dg(h.astype(BF16), w_ref[...], _NN)
    q_ref[...] = o[:TM, 0:ATTN_W]
    k_ref[...] = o[:TM, ATTN_W:ATTN_W + KV_W]
    v_ref[...] = o[:TM, ATTN_W + KV_W:RWKV_COL0]
    uf_ref[...] = o[:TM, RWKV_COL0 + RWKV_IN_W:IN_W]

    j = (i - CTX_TILES) % LAT_TILES_PER_SEQ
    has_prev = jnp.where((i >= CTX_TILES) & (j != 0), 1.0, 0.0)
    has_next = jnp.where((i >= CTX_TILES) & (j != LAT_TILES_PER_SEQ - 1), 1.0, 0.0)
    u_all = o[:, RWKV_COL0:RWKV_COL0 + RWKV_IN_W]
    prev_row = u_all[TM + HALO - 1:TM + HALO, :] * has_prev
    next_row = u_all[TM + HALO:TM + HALO + 1, :] * has_next
    _rwkv_token_terms(u_all[:TM], prev_row, next_row, *rwkv_params, *rwkv_outs)


def _inproj_call(xs, mods_l, g, w_bf16, rwkv_params, l):
    split = isinstance(xs, tuple)
    xs = xs if split else (xs,)
    hb = TM // HALO
    if split:
        halo_src = xs[1]
        last = N_LAT // HALO - 1
        prev_blk = lambda i: (jnp.clip((i - CTX_TILES) * hb - 1, 0, last), 0)
        next_blk = lambda i: (jnp.clip((i - CTX_TILES + 1) * hb, 0, last), 0)
    else:
        halo_src = xs[0]
        last = N_TOK // HALO - 1
        prev_blk = lambda i: (jnp.maximum(i * hb - 1, 0), 0)
        next_blk = lambda i: (jnp.minimum((i + 1) * hb, last), 0)
    row = lambda i: (l, 0, 0)
    widths = (ATTN_W, KV_W, KV_W, FNET_W)
    tok = jax.ShapeDtypeStruct((N_TOK, RWKV_W), F32)
    tok2 = jax.ShapeDtypeStruct((2, N_TOK, RWKV_W), F32)
    return pl.pallas_call(
        functools.partial(_inproj_kernel, split),
        grid=(N_TILES,),
        in_specs=_x_specs(split) + [
            pl.BlockSpec((HALO, D_MODEL), prev_blk),
            pl.BlockSpec((HALO, D_MODEL), next_blk),
            pl.BlockSpec((None, 1, 6 * D_MODEL), lambda i: (_mod_row(i), 0, 0)),
            pl.BlockSpec((None, 1, D_MODEL), row),
            pl.BlockSpec((None, D_MODEL, IN_W), row),
            pl.BlockSpec((None, 3, RWKV_IN_W), row),
            pl.BlockSpec((None, 2 * DECAY_RANK, 2 * RWKV_W), row),
            pl.BlockSpec((None, 2 * ICLR_RANK, 2 * RWKV_W), row),
            pl.BlockSpec((None, GATE_RANK, RWKV_W), row),
            pl.BlockSpec((None, 2, RWKV_W), row),
            pl.BlockSpec((None, 2, RWKV_W), row),
            pl.BlockSpec((None, 1, RWKV_W), row),
            pl.BlockSpec((None, 1, RWKV_W), row),
            pl.BlockSpec((None, 1, RWKV_W), row),
            pl.BlockSpec((RWKV_W, RWKV_W), lambda i: (0, 0)),
        ],
        out_specs=[pl.BlockSpec((TM, w), lambda i: (i, 0)) for w in widths]
        + [pl.BlockSpec((TM, RWKV_W), lambda i: (i, 0))] * 5
        + [pl.BlockSpec((2, TM, RWKV_W), lambda i: (0, i, 0))] * 3,
        out_shape=[jax.ShapeDtypeStruct((N_TOK, w), F32) for w in widths] + [tok] * 5 + [tok2] * 3,
        compiler_params=_cparams(("arbitrary",)),
        name="in_projection",
    )(*xs, halo_src, halo_src, mods_l, g, w_bf16, *rwkv_params)


def _softmax_pv(s, sink_col, v):
    m = jnp.maximum(jnp.max(s, axis=-1, keepdims=True), sink_col)
    e = jnp.exp(s - m)
    denom = jnp.sum(e, axis=-1, keepdims=True) + jnp.exp(sink_col - m)
    return _dot1(e, v) / denom


def _stack_q_heads(q, kh, rows):
    return jnp.concatenate(
        [q[:, (kh * Q_PER_KV + g) * HEAD_DIM:(kh * Q_PER_KV + g + 1) * HEAD_DIM] for g in range(Q_PER_KV)],
        axis=0)


def _sink_col(sink_ref, l, kh, rows):
    return jnp.concatenate(
        [jnp.full((rows, 1), sink_ref[l, kh * Q_PER_KV + g], F32) for g in range(Q_PER_KV)], axis=0)


def _ctx_attn_kernel(l, sink_ref, q_ref, k_ref, v_ref, o_ref):
    q = q_ref[...]
    k = k_ref[...]
    v = v_ref[...]
    half = CTX_LEN // 2
    pieces = [(h, r) for h in range(N_Q_HEADS) for r in range(2)]
    kb = k.astype(BF16)
    vb = v.astype(BF16)
    kv = lambda a, h: a[:, (h // Q_PER_KV) * HEAD_DIM:(h // Q_PER_KV + 1) * HEAD_DIM]
    qs = [(q[r * half:(r + 1) * half, h * HEAD_DIM:(h + 1) * HEAD_DIM] * ATTN_SCALE).astype(BF16) for h, r in pieces]
    s = [_dg(qs[n], kv(kb, h), _NT) for n, (h, r) in enumerate(pieces)]
    m = [jnp.maximum(jnp.max(s[n], axis=-1, keepdims=True), sink_ref[l, h]) for n, (h, r) in enumerate(pieces)]
    e = [jnp.exp(s[n] - m[n]) for n in range(len(pieces))]
    den = [jnp.sum(e[n], axis=-1, keepdims=True) + jnp.exp(sink_ref[l, h] - m[n]) for n, (h, r) in enumerate(pieces)]
    o = [_dg(e[n].astype(BF16), kv(vb, h), _NN) / den[n] for n, (h, r) in enumerate(pieces)]
    for n, (h, r) in enumerate(pieces):
        o_ref[r * half:(r + 1) * half, h * HEAD_DIM:(h + 1) * HEAD_DIM] = o[n]


def _ctx_attn_call(q, k, v, sink, l):
    return pl.pallas_call(
        functools.partial(_ctx_attn_kernel, l),
        grid=(N_CTX_SEQ,),
        in_specs=[
            pl.BlockSpec(memory_space=pltpu.SMEM),
            pl.BlockSpec((CTX_LEN, ATTN_W), lambda b: (b, 0)),
            pl.BlockSpec((CTX_LEN, KV_W), lambda b: (b, 0)),
            pl.BlockSpec((CTX_LEN, KV_W), lambda b: (b, 0)),
        ],
        out_specs=pl.BlockSpec((CTX_LEN, ATTN_W), lambda b: (b, 0)),
        out_shape=jax.ShapeDtypeStruct((N_CTX, ATTN_W), F32),
        compiler_params=_cparams(("parallel",)),
        name="context_attention",
    )(sink, q, k, v)


def _rope(x, cos, sin):
    lane = lax.broadcasted_iota(jnp.int32, x.shape, 1)
    partner = jnp.where((lane & 16) == 0, pltpu.roll(x, 128 - 16, 1), pltpu.roll(x, 16, 1))
    return x * cos + partner * sin


def _lat_attn_kernel(l, sink_ref, q_ref, k_ref, v_ref, kc_ref, vc_ref, cos_ref, sin_ref, o_ref):
    nq = LAT_ATTN_QBLOCKS
    nb = LAT_LEN // BLOCK
    n0 = pl.program_id(1) * nq
    r0 = pl.multiple_of(n0 * BLOCK, BLOCK)
    cos_q = cos_ref[pl.ds(r0, nq * BLOCK), :]
    sin_q = sin_ref[pl.ds(r0, nq * BLOCK), :]
    q = jnp.concatenate(
        [_rope(q_ref[:, j * 128:(j + 1) * 128], cos_q, sin_q) for j in range(ATTN_W // 128)], axis=1)

    kr, vr = [], []
    for dj in range(-1, nq + 1):
        rj = pl.multiple_of(jnp.clip(n0 + dj, 0, nb - 1) * BLOCK, BLOCK)
        kr.append(_rope(k_ref[pl.ds(rj, BLOCK), :], cos_ref[pl.ds(rj, BLOCK), :],
                        sin_ref[pl.ds(rj, BLOCK), :]).astype(BF16))
        vr.append(v_ref[pl.ds(rj, BLOCK), :].astype(BF16))
    kc = kc_ref[...].astype(BF16)
    vc = vc_ref[...].astype(BF16)
    k_all = [jnp.concatenate(kr[t:t + 3] + [kc], axis=0) for t in range(nq)]
    v_all = [jnp.concatenate(vr[t:t + 3] + [vc], axis=0) for t in range(nq)]

    kj = lax.broadcasted_iota(jnp.int32, (BLOCK, BLOCK), 1)
    qi = lax.broadcasted_iota(jnp.int32, (BLOCK, BLOCK), 0)
    keep_prev = [(kj >= qi) & (n0 + t > 0) for t in range(nq)]
    keep_next = [(kj <= qi) & (n0 + t < nb - 1) for t in range(nq)]
    kv_cols = lambda h: slice((h // Q_PER_KV) * HEAD_DIM, (h // Q_PER_KV + 1) * HEAD_DIM)

    all_pieces = [(t, h) for t in range(nq) for h in range(N_Q_HEADS)]
    for g0 in range(0, len(all_pieces), LAT_ATTN_GROUP):
        pieces = all_pieces[g0:g0 + LAT_ATTN_GROUP]
        rng = range(len(pieces))
        qs = [(q[t * BLOCK:(t + 1) * BLOCK, h * HEAD_DIM:(h + 1) * HEAD_DIM] * ATTN_SCALE).astype(BF16)
              for t, h in pieces]
        s = [_dg(qs[i], k_all[t][:, kv_cols(h)], _NT) for i, (t, h) in enumerate(pieces)]
        s = [jnp.concatenate([jnp.where(keep_prev[t], s[i][:, 0:BLOCK], NEG), s[i][:, BLOCK:2 * BLOCK],
                              jnp.where(keep_next[t], s[i][:, 2 * BLOCK:3 * BLOCK], NEG), s[i][:, 3 * BLOCK:]],
                             axis=1) for i, (t, h) in enumerate(pieces)]
        m = [jnp.maximum(jnp.max(s[i], axis=-1, keepdims=True), sink_ref[l, h]) for i, (t, h) in enumerate(pieces)]
        e = [jnp.exp(s[i] - m[i]) for i in rng]
        den = [jnp.sum(e[i], axis=-1, keepdims=True) + jnp.exp(sink_ref[l, h] - m[i])
               for i, (t, h) in enumerate(pieces)]
        o = [_dg(e[i].astype(BF16), v_all[t][:, kv_cols(h)], _NN) / den[i] for i, (t, h) in enumerate(pieces)]
        for i, (t, h) in enumerate(pieces):
            o_ref[t * BLOCK:(t + 1) * BLOCK, h * HEAD_DIM:(h + 1) * HEAD_DIM] = o[i]


def _lat_attn_call(q, k, v, kc, vc, cos, sin, sink, l):
    rows = LAT_ATTN_QBLOCKS * BLOCK
    steps = LAT_LEN // rows
    q_off = N_CTX // rows
    kv_off = N_CTX // LAT_LEN
    return pl.pallas_call(
        functools.partial(_lat_attn_kernel, l),
        grid=(N_LAT_SEQ, steps),
        in_specs=[
            pl.BlockSpec(memory_space=pltpu.SMEM),
            pl.BlockSpec((rows, ATTN_W), lambda b, n: (q_off + b * steps + n, 0)),
            pl.BlockSpec((LAT_LEN, KV_W), lambda b, n: (kv_off + b, 0)),
            pl.BlockSpec((LAT_LEN, KV_W), lambda b, n: (kv_off + b, 0)),
            pl.BlockSpec((None, None, PAST_LEN, KV_W), lambda b, n: (b, l, 0, 0)),
            pl.BlockSpec((None, None, PAST_LEN, KV_W), lambda b, n: (b, l, 0, 0)),
            pl.BlockSpec((LAT_LEN, 128), lambda b, n: (0, 0)),
            pl.BlockSpec((LAT_LEN, 128), lambda b, n: (0, 0)),
        ],
        out_specs=pl.BlockSpec((rows, ATTN_W), lambda b, n: (b * steps + n, 0)),
        out_shape=jax.ShapeDtypeStruct((N_LAT, ATTN_W), F32),
        compiler_params=_cparams(("parallel", "arbitrary")),
        name="latent_attention",
    )(sink, q, k, v, kc, vc, cos, sin)


def _rope_tables():
    t = np.arange(LAT_LEN)
    lane = np.arange(128)
    d = lane % HEAD_DIM
    pos = np.where(d[None, :] < HEAD_DIM // 2, (t // GRID_W)[:, None], (t % GRID_W)[:, None]).astype(np.float32)
    quarter = HEAD_DIM // 4
    inv = (np.float32(ROPE_THETA) ** (-(np.arange(quarter, dtype=np.float32)) / np.float32(quarter))).astype(np.float32)
    ang = pos * inv[d % quarter][None, :]
    sign = np.where((lane & 16) == 0, -1.0, 1.0)[None, :]
    return jnp.asarray(np.cos(ang), F32), jnp.asarray(np.sin(ang) * sign, F32)


PAIR_W = 2 * HEAD_DIM
N_PAIRS = RWKV_HEADS // 2
_TN = (((0,), (0,)), ((), ()))


def _pair_stack(x, pmask):
    return jnp.concatenate([x.astype(BF16)] * 2, axis=0) * pmask


def _hdot(m, x, pmask, dims):
    return jnp.concatenate(
        [_dg(m[:, p * PAIR_W:(p + 1) * PAIR_W], _pair_stack(x[:, p * PAIR_W:(p + 1) * PAIR_W], pmask), dims)
         for p in range(N_PAIRS)], axis=1)


def _scan_prepare(reverse, r, v, kn, lw, kd, b, cs, strict, incl, eye, pmask):
    linc = _dot_exact_lhs(cs, lw)
    end = 0 if reverse else CHUNK - 1
    ltot = jnp.concatenate(
        [jnp.broadcast_to(linc[c * CHUNK + end:c * CHUNK + end + 1, :], (CHUNK, RWKV_W))
         for c in range(CHUNKS_PER_TILE)], axis=0)
    v_all = v.astype(BF16)
    r_t = r * jnp.exp(linc)
    a_t = (-kn * jnp.exp(linc - lw)).astype(BF16)
    e_neg = jnp.exp(-linc)
    b_t = (b * e_neg).astype(BF16)
    k_t = (kd * e_neg).astype(BF16)
    e_rem = jnp.exp(ltot - linc)
    b_p = (b * e_rem).astype(BF16)
    k_p = (kd * e_rem).astype(BF16)
    dec = jnp.exp(ltot)

    chunks = range(CHUNKS_PER_TILE)
    sls = [slice(c * CHUNK, (c + 1) * CHUNK) for c in chunks]
    hd = lambda m, x, dims: _hdot(m, x, pmask, dims)
    ar = [jnp.concatenate([a_t[sl], r_t[sl].astype(BF16)], axis=0) for sl in sls]
    m_b = [hd(ar[c], b_t[sls[c]], _NT) for c in chunks]
    m_k = [hd(ar[c], k_t[sls[c]], _NT) for c in chunks]
    a_ab = [m_b[c][:CHUNK] * strict for c in chunks]
    p_rb = [(m_b[c][CHUNK:] * incl).astype(BF16) for c in chunks]
    ak_rk = [jnp.concatenate([m_k[c][:CHUNK] * strict, m_k[c][CHUNK:] * incl], axis=0).astype(BF16)
             for c in chunks]

    tinv = [eye + a_ab[c] for c in chunks]
    p = [a_ab[c].astype(BF16) for c in chunks]
    p = [hd(p[c], p[c], _NN).astype(BF16) for c in chunks]
    for _ in range(4):
        pt = [hd(jnp.concatenate([p[c], tinv[c].astype(BF16)], axis=0), p[c], _NN) for c in chunks]
        p = [pt[c][:CHUNK].astype(BF16) for c in chunks]
        tinv = [tinv[c] + pt[c][CHUNK:] for c in chunks]
    tinv = [(tinv[c] + hd(tinv[c].astype(BF16), p[c], _NN)).astype(BF16) for c in chunks]

    m_v = [hd(ak_rk[c], v_all[sls[c]], _NN) for c in chunks]
    w = [hd(tinv[c], a_t[sls[c]], _NN) for c in chunks]
    u0 = [hd(tinv[c], m_v[c][:CHUNK], _NN) for c in chunks]
    qp = [r_t[sls[c]] + hd(p_rb[c], w[c], _NN) for c in chunks]
    y0 = [hd(p_rb[c], u0[c], _NN) + m_v[c][CHUNK:] for c in chunks]
    pr = [slice(p * PAIR_W, (p + 1) * PAIR_W) for p in range(N_PAIRS)]
    pmask_f = pmask.astype(F32)
    bk = [jnp.concatenate([b_p[sl], k_p[sl]], axis=0) for sl in sls]
    uv = [jnp.concatenate([u0[c].astype(BF16), v_all[sls[c]]], axis=0) for c in chunks]
    wb = [[_dg(w[c][:, q].astype(BF16), b_p[sls[c]][:, q], _TN) * pmask_f for q in pr] for c in chunks]
    gs = [[_dg(uv[c][:, q], bk[c][:, q], _TN) * pmask_f for q in pr] for c in chunks]
    qpb = [qp[c].astype(BF16) for c in chunks]
    decs = [dec[c * CHUNK:c * CHUNK + 1, :] for c in chunks]
    return qpb, y0, wb, gs, decs


def _scan_chain_step(c, s, prep, y_ref):
    qpb, y0, wb, gs, decs = prep
    pr = [slice(p * PAIR_W, (p + 1) * PAIR_W) for p in range(N_PAIRS)]
    sb = [s[p].astype(BF16) for p in range(N_PAIRS)]
    y_ref[c * CHUNK:(c + 1) * CHUNK, :] = y0[c] + jnp.concatenate(
        [_dg(qpb[c][:, pr[p]], sb[p], _NT) for p in range(N_PAIRS)], axis=1)
    return [s[p] * decs[c][:, pr[p]] + (_dg(sb[p], wb[c][p].astype(BF16), _NN) + gs[c][p]) for p in range(N_PAIRS)]


N_STREAMS = 2
STREAM_TILES = N_TILES // N_STREAMS
STREAM_CTX = CTX_TILES // N_STREAMS
SCAN_LANES = tuple((rev, st) for rev in (False, True) for st in range(N_STREAMS))
N_LANE_IN = 7


def _stream_tile(stream, pos):
    lat0 = CTX_TILES + stream * (STREAM_TILES - STREAM_CTX)
    return jnp.where(pos < STREAM_CTX, stream * STREAM_CTX + pos, lat0 + pos - STREAM_CTX)


def _lane_pos(reverse, k):
    return (STREAM_TILES - 1 - k) if reverse else k


def _scan_kernel(*refs):
    n_lanes = len(SCAN_LANES)
    lane_in = [refs[n * N_LANE_IN:(n + 1) * N_LANE_IN] for n in range(n_lanes)]
    cs_ref, strict_ref, incl_ref, eye_ref, pmask_ref = refs[n_lanes * N_LANE_IN:n_lanes * N_LANE_IN + 5]
    outs = refs[n_lanes * N_LANE_IN + 5:n_lanes * N_LANE_IN + 5 + 2 * n_lanes]
    lane_out = [outs[2 * n:2 * n + 2] for n in range(n_lanes)]
    lane_scr = refs[n_lanes * N_LANE_IN + 5 + 2 * n_lanes:]

    k = pl.program_id(0)
    pmask_f = pmask_ref[...]
    pmask = pmask_f.astype(BF16)
    eye = eye_ref[...]

    def load_state(s0_ref, scr, is_ctx):
        for p in range(N_PAIRS):
            carried = jnp.concatenate([s0_ref[p * PAIR_W:(p + 1) * PAIR_W, :]] * 2, axis=1) * pmask_f
            scr[p] = jnp.where(is_ctx, 0.0, carried)

    for n, (reverse, stream) in enumerate(SCAN_LANES):
        pos = _lane_pos(reverse, k)
        j = (pos - STREAM_CTX) % LAT_TILES_PER_SEQ
        first = (pos < STREAM_CTX) | (j == (LAT_TILES_PER_SEQ - 1 if reverse else 0))
        pl.when(first)(functools.partial(load_state, lane_in[n][6], lane_scr[n], pos < STREAM_CTX))

    preps = []
    for n, (reverse, stream) in enumerate(SCAN_LANES):
        d = 1 if reverse else 0
        r_ref, v_ref, kn_ref, lw_ref, kd_ref, b_ref, _ = lane_in[n]
        preps.append(_scan_prepare(reverse, r_ref[...], v_ref[...], kn_ref[...], lw_ref[...], kd_ref[...], b_ref[...],
                                   cs_ref[d], strict_ref[d], incl_ref[d], eye, pmask))
    states = [[lane_scr[n][p] for p in range(N_PAIRS)] for n in range(n_lanes)]
    for c in range(CHUNKS_PER_TILE):
        for n, (reverse, stream) in enumerate(SCAN_LANES):
            cc = CHUNKS_PER_TILE - 1 - c if reverse else c
            states[n] = _scan_chain_step(cc, states[n], preps[n], lane_out[n][0])
    for n in range(n_lanes):
        for p in range(N_PAIRS):
            lane_scr[n][p] = states[n][p]
            for h in range(2):
                blk = slice(h * HEAD_DIM, (h + 1) * HEAD_DIM)
                lane_out[n][1][2 * p + h] = states[n][p][blk, blk]


def _scan_call(r, v, kn, lw, kd, b, s0, consts, l):
    cs, strict, incl, eye, pmask = consts
    lat_seq_of = lambda t: jnp.maximum(t - CTX_TILES, 0) // LAT_TILES_PER_SEQ
    const2 = lambda k: (0, 0)
    const3 = lambda k: (0, 0, 0)
    in_specs, out_specs, args = [], [], []
    for reverse, stream in SCAN_LANES:
        d = 1 if reverse else 0
        tile = functools.partial(lambda rv, st, k: _stream_tile(st, _lane_pos(rv, k)), reverse, stream)
        pos = functools.partial(_lane_pos, reverse)
        tok = pl.BlockSpec((TM, RWKV_W), functools.partial(lambda tl, k: (tl(k), 0), tile))
        tok_d = pl.BlockSpec((None, TM, RWKV_W), functools.partial(lambda tl, dd, k: (dd, tl(k), 0), tile, d))
        s0_spec = pl.BlockSpec((None, None, None, RWKV_W, HEAD_DIM),
                               functools.partial(lambda tl, dd, k: (lat_seq_of(tl(k)), l, dd, 0, 0), tile, d))
        in_specs += [tok, tok, tok, tok_d, tok_d, tok_d, s0_spec]
        args += [r, v, kn, lw, kd, b, s0]
        out_specs += [
            pl.BlockSpec((TM, RWKV_W), functools.partial(lambda ps, k: (ps(k), 0), pos)),
            pl.BlockSpec((None, RWKV_HEADS, HEAD_DIM, HEAD_DIM),
                         functools.partial(lambda ps, k: (jnp.minimum(ps(k), STREAM_CTX), 0, 0, 0), pos)),
        ]
    in_specs += [
        pl.BlockSpec((2, TM, TM), const3),
        pl.BlockSpec((2, CHUNK, RWKV_W), const3),
        pl.BlockSpec((2, CHUNK, RWKV_W), const3),
        pl.BlockSpec((CHUNK, RWKV_W), const2),
        pl.BlockSpec((PAIR_W, PAIR_W), const2),
    ]
    args += [cs, strict, incl, eye, pmask]
    y = jax.ShapeDtypeStruct((STREAM_TILES * TM, RWKV_W), F32)
    sfin = jax.ShapeDtypeStruct((STREAM_CTX + 1, RWKV_HEADS, HEAD_DIM, HEAD_DIM), F32)
    state = pltpu.VMEM((N_PAIRS, PAIR_W, PAIR_W), F32)
    outs = pl.pallas_call(
        _scan_kernel,
        grid=(STREAM_TILES,),
        in_specs=in_specs,
        out_specs=out_specs,
        out_shape=[y, sfin] * len(SCAN_LANES),
        scratch_shapes=[state] * len(SCAN_LANES),
        compiler_params=_cparams(("arbitrary",)),
        name="rwkv_scan",
    )(*args)
    return outs[0::2], outs[1::2]


def _scan_final_states(sfins_per_layer):
    return jnp.stack(
        [jnp.stack([jnp.concatenate([sfins[d * N_STREAMS + st][:STREAM_CTX] for st in range(N_STREAMS)], axis=0)
                    for d in range(2)], axis=1)
         for sfins in sfins_per_layer], axis=1)


def _scan_consts():
    t = np.arange(TM)
    same_chunk = (t[:, None] // CHUNK) == (t[None, :] // CHUNK)
    cs_f = same_chunk & (t[None, :] <= t[:, None])
    cs_b = same_chunk & (t[None, :] >= t[:, None])
    cs = np.stack([cs_f, cs_b]).astype(np.float32)
    q = np.arange(CHUNK)[:, None]
    kcol = np.arange(RWKV_W)[None, :] % CHUNK
    strict = np.stack([kcol < q, kcol > q]).astype(np.float32)
    incl = np.stack([kcol <= q, kcol >= q]).astype(np.float32)
    eye = (kcol == q).astype(np.float32)
    return (jnp.asarray(cs, BF16), jnp.asarray(strict), jnp.asarray(incl),
            jnp.asarray(eye), jnp.asarray(_head_block_mask(PAIR_W)))


FNET_TT = 512
FNET_IN_ROWS = N_CTX


def _fnet_kernel(seq_len, n_in, *refs):
    z_refs = refs[:n_in]
    ch_ref, cl_ref, th_ref, o_ref, z_scr = refs[n_in:]
    b = pl.program_id(1)

    @pl.when((pl.program_id(0) == 0) & (b == 0))
    def _():
        per_in = FNET_IN_ROWS // seq_len
        for n, z_ref in enumerate(z_refs):
            zh, zl = _split2(z_ref[...])
            for half in range(2):
                ch = ch_ref[half]
                cl = cl_ref[half]
                zc = (_dg(zh, ch, _NN) + (_dg(zh, cl, _NN) + _dg(zl, ch, _NN))).astype(BF16)
                for s in range(per_in):
                    z_scr[n * per_in + s, half * seq_len:(half + 1) * seq_len, :] = zc[s * seq_len:(s + 1) * seq_len]

    o_ref[...] = _dg(th_ref[...], z_scr[b], _NN)


def _fnet_call(u_f, seq_len, n_seq, row_off, tables, name):
    ch, cl, th = tables
    tt = min(FNET_TT, seq_len)
    nt = seq_len // tt
    n_in = n_seq * seq_len // FNET_IN_ROWS
    in_off = row_off // FNET_IN_ROWS
    z_specs = [pl.BlockSpec((FNET_IN_ROWS, FNET_W), functools.partial(lambda n, j, b: (in_off + n, 0), n))
               for n in range(n_in)]
    return pl.pallas_call(
        functools.partial(_fnet_kernel, seq_len, n_in),
        grid=(nt, n_seq),
        in_specs=z_specs + [
            pl.BlockSpec((2, FNET_W, FNET_W), lambda j, b: (0, 0, 0)),
            pl.BlockSpec((2, FNET_W, FNET_W), lambda j, b: (0, 0, 0)),
            pl.BlockSpec((tt, 2 * seq_len), lambda j, b: (j, 0)),
        ],
        out_specs=pl.BlockSpec((tt, FNET_W), lambda j, b: (b * nt + j, 0)),
        out_shape=jax.ShapeDtypeStruct((n_seq * seq_len, FNET_W), F32),
        scratch_shapes=[pltpu.VMEM((n_seq, 2 * seq_len, FNET_W), BF16)],
        compiler_params=_cparams(("arbitrary", "arbitrary")),
        name=name,
    )(*([u_f] * n_in), ch, cl, th)


DFT_SPLIT = 32


def _dft_tables(seq_len):
    c = np.arange(FNET_W)
    ang_c = 2.0 * np.pi * ((c[:, None] % HEAD_DIM) * (c[None, :] % HEAD_DIM) % HEAD_DIM) / HEAD_DIM
    scale = 1.0 / np.sqrt(float(seq_len) * HEAD_DIM)
    blk = _head_block_mask(FNET_W)
    ch, cl = _split2(jnp.asarray(np.stack([np.cos(ang_c) * blk * scale, np.sin(ang_c) * blk * scale]), F32))

    n1 = seq_len // DFT_SPLIT
    f = np.arange(seq_len)
    a1 = 2.0 * np.pi * ((np.arange(n1)[:, None] * f[None, :]) % n1) / n1
    a2 = 2.0 * np.pi * ((np.arange(DFT_SPLIT)[:, None] * f[None, :]) % seq_len) / seq_len
    c1, s1 = jnp.asarray(np.cos(a1), F32)[:, None, :], jnp.asarray(np.sin(a1), F32)[:, None, :]
    c2, s2 = jnp.asarray(np.cos(a2), F32)[None, :, :], jnp.asarray(np.sin(a2), F32)[None, :, :]
    cos_t = (c1 * c2 - s1 * s2).reshape(seq_len, seq_len)
    sin_t = (s1 * c2 + c1 * s2).reshape(seq_len, seq_len)
    th = jnp.concatenate([cos_t, -sin_t], axis=1).astype(BF16)
    return ch, cl, th


def _merge_ffn_kernel(split, final, *refs):
    i = pl.program_id(0)
    n_x = 2 if split else 1
    x_in = _tile_of_split(i, *refs[:2], FFN_TM) if split else refs[0][...]
    (mod_ref, oac_ref, oal_ref, yf0_ref, yf1_ref, yb0_ref, yb1_ref, bonus_ref, g_ref, ofc_ref, ofl_ref,
     lng_ref, lnb_ref, avg_ref, nfg_ref, wo_ref, wi_ref, w2_ref) = refs[n_x:n_x + 18]
    tail = refs[n_x + 18:]
    is_ctx = i < N_CTX // FFN_TM
    o_attn = _tile_of_split(i, oac_ref, oal_ref, FFN_TM)
    o_fnet = _tile_of_split(i, ofc_ref, ofl_ref, FFN_TM)
    avg = avg_ref[...]
    y = jnp.where(_in_stream0(i, FFN_TM), yf0_ref[...] + yb0_ref[...], yf1_ref[...] + yb1_ref[...])
    yc = y - _dot_exact_rhs(y, avg)
    var = _dot_exact_rhs(yc * yc, avg)
    yn = yc * lax.rsqrt(var + GN_EPS) * lng_ref[...] + lnb_ref[...]
    o_rwkv = (yn + bonus_ref[...]) * g_ref[...]

    o = (_dg(o_attn.astype(BF16), wo_ref[0:ATTN_W, :], _NN)
         + _dg(o_rwkv.astype(BF16), wo_ref[ATTN_W:ATTN_W + RWKV_W, :], _NN)
         + _dg(o_fnet.astype(BF16), wo_ref[ATTN_W + RWKV_W:, :], _NN))
    g1 = mod_ref[:, 2 * D_MODEL:3 * D_MODEL]
    sh2 = mod_ref[:, 3 * D_MODEL:4 * D_MODEL]
    sc2 = mod_ref[:, 4 * D_MODEL:5 * D_MODEL]
    g2 = mod_ref[:, 5 * D_MODEL:6 * D_MODEL]
    x = x_in + g1 * o
    h = _rms(x, nfg_ref[...]) * (1.0 + sc2) + sh2
    gu = _dg(h.astype(BF16), wi_ref[...], _NN)
    gt = gu[:, 0:D_FF]
    act = gt * jax.nn.sigmoid(gt) * gu[:, D_FF:]
    x = x + g2 * _dg(act.astype(BF16), w2_ref[...], _NN)
    if not final:
        (o_ref,) = tail
        o_ref[...] = x
    else:
        fng_ref, oc_ref, ol_ref = tail
        y_out = _rms(x, fng_ref[...])

        @pl.when(is_ctx)
        def _():
            oc_ref[...] = y_out

        @pl.when(jnp.logical_not(is_ctx))
        def _():
            ol_ref[...] = y_out


def _in_stream0(m, tm):
    ctx_m = N_CTX // tm
    lat_ps = (N_TOK // tm - ctx_m) // N_STREAMS
    return (m < ctx_m // N_STREAMS) | ((m >= ctx_m) & (m < ctx_m + lat_ps))


def _stream_block(stream, tm):
    ctx_m = N_CTX // tm
    half = ctx_m // N_STREAMS
    lat_ps = (N_TOK // tm - ctx_m) // N_STREAMS
    if stream == 0:
        return lambda m: (jnp.where(m < half, m, jnp.where(m < ctx_m, half - 1,
                                    jnp.where(m < ctx_m + lat_ps, m - half, half + lat_ps - 1))), 0)
    return lambda m: (jnp.where(m < half, 0, jnp.where(m < ctx_m, m - half,
                                jnp.where(m < ctx_m + lat_ps, half - 1, m - half - lat_ps))), 0)


def _merge_ffn_call(xs, mods_l, oa_ctx, oa_lat, ys, bonus, g, of_ctx, of_lat, ln_g, ln_b, avg, nfg,
                    wo, wi, w2, l, final_g=None):
    final = final_g is not None
    split = isinstance(xs, tuple)
    xs = xs if split else (xs,)
    row = lambda i: (l, 0, 0)
    tile = lambda i: (i, 0)
    ctx_tile, lat_tile = _ctx_tile(FFN_TM), _lat_tile(FFN_TM)
    once = pl.Buffered(1)
    in_specs = _x_specs(split, FFN_TM) + [
        pl.BlockSpec((None, 1, 6 * D_MODEL), lambda i: (_mod_row(i, FFN_TM), 0, 0)),
        pl.BlockSpec((FFN_TM, ATTN_W), ctx_tile),
        pl.BlockSpec((FFN_TM, ATTN_W), lat_tile),
        pl.BlockSpec((FFN_TM, RWKV_W), _stream_block(0, FFN_TM)),
        pl.BlockSpec((FFN_TM, RWKV_W), _stream_block(1, FFN_TM)),
        pl.BlockSpec((FFN_TM, RWKV_W), _stream_block(0, FFN_TM)),
        pl.BlockSpec((FFN_TM, RWKV_W), _stream_block(1, FFN_TM)),
        pl.BlockSpec((FFN_TM, RWKV_W), tile),
        pl.BlockSpec((FFN_TM, RWKV_W), tile),
        pl.BlockSpec((FFN_TM, FNET_W), ctx_tile),
        pl.BlockSpec((FFN_TM, FNET_W), lat_tile),
        pl.BlockSpec((None, 1, RWKV_W), row),
        pl.BlockSpec((None, 1, RWKV_W), row),
        pl.BlockSpec((RWKV_W, RWKV_W), lambda i: (0, 0)),
        pl.BlockSpec((None, 1, D_MODEL), row),
        pl.BlockSpec((None, D_MODEL, D_MODEL), row, pipeline_mode=once),
        pl.BlockSpec((None, D_MODEL, 2 * D_FF), row, pipeline_mode=once),
        pl.BlockSpec((None, D_FF, D_MODEL), row, pipeline_mode=once),
    ]
    args = [*xs, mods_l, oa_ctx, oa_lat, *ys, bonus, g, of_ctx, of_lat, ln_g, ln_b, avg, nfg, wo, wi, w2]
    if final:
        in_specs.append(pl.BlockSpec((1, D_MODEL), lambda i: (0, 0)))
        args.append(final_g)
        out_specs = [pl.BlockSpec((FFN_TM, D_MODEL), ctx_tile), pl.BlockSpec((FFN_TM, D_MODEL), lat_tile)]
        out_shape = [jax.ShapeDtypeStruct((N_CTX, D_MODEL), F32), jax.ShapeDtypeStruct((N_LAT, D_MODEL), F32)]
    else:
        out_specs = pl.BlockSpec((FFN_TM, D_MODEL), tile)
        out_shape = jax.ShapeDtypeStruct((N_TOK, D_MODEL), F32)
    return pl.pallas_call(
        functools.partial(_merge_ffn_kernel, split, final),
        grid=(N_TOK // FFN_TM,),
        in_specs=in_specs,
        out_specs=out_specs,
        out_shape=out_shape,
        compiler_params=_cparams(("arbitrary",)),
        name="merge_ffn_final" if final else "merge_ffn",
    )(*args)


def _block_diag2(w):
    z = jnp.zeros_like(w[:, 0])
    return jnp.concatenate([jnp.concatenate([w[:, 0], z], axis=2), jnp.concatenate([z, w[:, 1]], axis=2)], axis=1)


def kernel(x_prompt, x_sample, cache_k, cache_v, state_rwkv, c, c_ctx, w_ada, b_ada, norm_mix_g, norm_ffn_g,
           w_in, w_out, attn_sink, rwkv_shift_w, rwkv_w0, rwkv_w_up, rwkv_a0, rwkv_a_up, rwkv_g_up,
           rwkv_k_k, rwkv_k_a, rwkv_r_k, rwkv_ln_g, rwkv_ln_b, ffn_w_in, ffn_w_out, norm_final_g):
    x = (x_prompt.reshape(N_CTX, D_MODEL), x_sample.reshape(N_LAT, D_MODEL))
    cvec = jnp.concatenate([c_ctx[None, :], c, jnp.zeros((8 - 1 - N_LAT_SEQ, D_MODEL), F32)], axis=0)
    mods = _mods_call(cvec, w_ada, b_ada).reshape(DEPTH, 8, 1, 6 * D_MODEL)

    w_in_b = w_in.astype(BF16)
    w_out_b = w_out.astype(BF16)
    ffn_in_b = ffn_w_in.astype(BF16)
    ffn_out_b = ffn_w_out.astype(BF16)
    wup_bd = _block_diag2(rwkv_w_up)
    aup_bd = _block_diag2(rwkv_a_up)
    row3 = lambda a: a.reshape(DEPTH, 1, -1)
    nmg, nfg = row3(norm_mix_g), row3(norm_ffn_g)
    k_k, k_a, r_k = row3(rwkv_k_k), row3(rwkv_k_a), row3(rwkv_r_k)
    ln_g, ln_b = row3(rwkv_ln_g), row3(rwkv_ln_b)
    kc = cache_k.reshape(N_LAT_SEQ, DEPTH, PAST_LEN, KV_W)
    vc = cache_v.reshape(N_LAT_SEQ, DEPTH, PAST_LEN, KV_W)
    s0 = state_rwkv.reshape(N_LAT_SEQ, DEPTH, 2, RWKV_W, HEAD_DIM)

    cos, sin = _rope_tables()
    consts = _scan_consts()
    bmask = jnp.asarray(_head_block_mask(RWKV_W))
    ones_bd = bmask.astype(BF16)
    avg_bd = (bmask * (1.0 / HEAD_DIM)).astype(BF16)
    rwkv_params = (rwkv_shift_w, wup_bd, aup_bd, rwkv_g_up, rwkv_w0, rwkv_a0, k_k, k_a, r_k, ones_bd)
    dft_ctx = _dft_tables(CTX_LEN)
    dft_lat = _dft_tables(LAT_LEN)

    new_k, new_v, new_s = [], [], []
    for l in range(DEPTH):
        q, k, v, u_f, r, vv, kn, g, bonus, lw, kd, b = _inproj_call(x, mods[l], nmg, w_in_b, rwkv_params, l)
        new_k.append(k[:N_CTX].reshape(N_CTX_SEQ, CTX_LEN, KV_W))
        new_v.append(v[:N_CTX].reshape(N_CTX_SEQ, CTX_LEN, KV_W))

        oa_ctx = _ctx_attn_call(q, k, v, attn_sink, l)
        oa_lat = _lat_attn_call(q, k, v, kc, vc, cos, sin, attn_sink, l)

        ys, sfins = _scan_call(r, vv, kn, lw, kd, b, s0, consts, l)
        new_s.append(sfins)

        of_ctx = _fnet_call(u_f, CTX_LEN, N_CTX_SEQ, 0, dft_ctx, "fnet_context")
        of_lat = _fnet_call(u_f, LAT_LEN, N_LAT_SEQ, N_CTX, dft_lat, "fnet_latent")

        last = l == DEPTH - 1
        x = _merge_ffn_call(x, mods[l], oa_ctx, oa_lat, ys, bonus, g, of_ctx, of_lat, ln_g, ln_b, avg_bd, nfg,
                            w_out_b, ffn_in_b, ffn_out_b, l,
                            final_g=norm_final_g.reshape(1, D_MODEL) if last else None)

    y_ctx, y_lat = x
    return (y_ctx.reshape(N_CTX_SEQ, CTX_LEN, D_MODEL), y_lat.reshape(N_LAT_SEQ, LAT_LEN, D_MODEL),
            jnp.stack(new_k, axis=1).reshape(N_CTX_SEQ, DEPTH, CTX_LEN, N_KV_HEADS, HEAD_DIM),
            jnp.stack(new_v, axis=1).reshape(N_CTX_SEQ, DEPTH, CTX_LEN, N_KV_HEADS, HEAD_DIM),
            _scan_final_states(new_s))
```

```python
import functools

import numpy as np
import jax
import jax.numpy as jnp
from jax import lax
from jax.experimental import pallas as pl
from jax.experimental.pallas import tpu as pltpu

F32 = jnp.float32
BF16 = jnp.bfloat16

D_MODEL = 1024
N_CTX_SEQ = 16
CTX_LEN = 256
DEPTH = 4
N_LAT_SEQ = 4
LAT_LEN = 2048
PAST_LEN = 512
GRID_W = 64
HEAD_DIM = 64
N_Q_HEADS = 8
N_KV_HEADS = 2
Q_PER_KV = N_Q_HEADS // N_KV_HEADS
ATTN_W = N_Q_HEADS * HEAD_DIM
KV_W = N_KV_HEADS * HEAD_DIM
BLOCK = 128
ATTN_SCALE = 0.125
ROPE_THETA = 10000.0
NEG = -1e30
RWKV_HEADS = 4
RWKV_W = RWKV_HEADS * HEAD_DIM
DECAY_RANK = 64
ICLR_RANK = 64
GATE_RANK = 128
DECAY_SCALE = 0.6065306597126334
GN_EPS = 64e-5
RWKV_IN_W = 3 * RWKV_W + GATE_RANK + 2 * DECAY_RANK + 2 * ICLR_RANK
FNET_W = 4 * HEAD_DIM
IN_W = ATTN_W + 2 * KV_W + RWKV_IN_W + FNET_W
D_FF = 2816
RMS_EPS = 1e-6

N_CTX = N_CTX_SEQ * CTX_LEN
N_LAT = N_LAT_SEQ * LAT_LEN
N_TOK = N_CTX + N_LAT
TM = 256
N_TILES = N_TOK // TM
CTX_TILES = N_CTX // TM
LAT_TILES_PER_SEQ = LAT_LEN // TM
N_SEQ = N_CTX_SEQ + N_LAT_SEQ
CHUNK = 64
CHUNKS_PER_TILE = TM // CHUNK
HALO = 8
FFN_TM = 512
LAT_ATTN_QBLOCKS = 4
LAT_ATTN_GROUP = 8

VMEM_LIMIT = 56 * 1024 * 1024


def _cparams(sem):
    return pltpu.CompilerParams(dimension_semantics=sem, vmem_limit_bytes=VMEM_LIMIT)


def _mod_row(i, tm=TM):
    return jnp.where(i < N_CTX // tm, 0, 1 + (i - N_CTX // tm) // (LAT_LEN // tm))


def _split2(x):
    hi = x.astype(BF16)
    lo = (x - hi.astype(F32)).astype(BF16)
    return hi, lo


_NN = (((1,), (0,)), ((), ()))
_NT = (((1,), (1,)), ((), ()))


def _dg(a, b, dims):
    return lax.dot_general(a, b, dims, preferred_element_type=F32)


def _dot1(a, b, dims=_NN):
    return _dg(a.astype(BF16), b.astype(BF16), dims)


def _dot3(a, b, dims=_NN):
    ah, al = _split2(a)
    bh, bl = _split2(b)
    return _dg(ah, bh, dims) + (_dg(ah, bl, dims) + _dg(al, bh, dims))


def _dot_exact_rhs(a, b_bf16):
    ah, al = _split2(a)
    return _dg(ah, b_bf16, _NN) + _dg(al, b_bf16, _NN)


def _dot_exact_lhs(a_bf16, b):
    bh, bl = _split2(b)
    return _dg(a_bf16, bh, _NN) + _dg(a_bf16, bl, _NN)


def _mods_kernel(c_ref, w_ref, b_ref, o_ref):
    c = c_ref[...]
    s = c * jax.nn.sigmoid(c)
    o_ref[...] = _dot1(s, w_ref[...]) + b_ref[...]


def _mods_call(cvec, w_ada, b_ada):
    tn = 1536
    return pl.pallas_call(
        _mods_kernel,
        grid=(DEPTH, 6 * D_MODEL // tn),
        in_specs=[
            pl.BlockSpec((8, D_MODEL), lambda l, j: (0, 0)),
            pl.BlockSpec((None, D_MODEL, tn), lambda l, j: (l, 0, j)),
            pl.BlockSpec((None, 1, tn), lambda l, j: (l, 0, j)),
        ],
        out_specs=pl.BlockSpec((None, 8, tn), lambda l, j: (l, 0, j)),
        out_shape=jax.ShapeDtypeStruct((DEPTH, 8, 6 * D_MODEL), F32),
        compiler_params=_cparams(("parallel", "parallel")),
        name="adaln_mods",
    )(cvec, w_ada, b_ada.reshape(DEPTH, 1, 6 * D_MODEL))


def _rms(x, g):
    return x * lax.rsqrt(jnp.mean(x * x, axis=-1, keepdims=True) + RMS_EPS) * g


def _tile_of_split(i, ctx_ref, lat_ref, tm=TM):
    return jnp.where(i < N_CTX // tm, ctx_ref[...], lat_ref[...])


def _ctx_tile(tm=TM):
    return lambda i: (jnp.minimum(i, N_CTX // tm - 1), 0)


def _lat_tile(tm=TM):
    return lambda i: (jnp.maximum(i - N_CTX // tm, 0), 0)


def _x_specs(split, tm=TM):
    if split:
        return [pl.BlockSpec((tm, D_MODEL), _ctx_tile(tm)), pl.BlockSpec((tm, D_MODEL), _lat_tile(tm))]
    return [pl.BlockSpec((tm, D_MODEL), lambda i: (i, 0))]


def _head_block_mask(n):
    r = np.arange(n)[:, None] // HEAD_DIM
    c = np.arange(n)[None, :] // HEAD_DIM
    return (r == c).astype(np.float32)


def _rwkv_token_terms(u, prev_row, next_row, sw_ref, wup_ref, aup_ref, gup_ref, w0_ref, a0_ref,
                      kk_ref, ka_ref, rk_ref, ones_ref, r_out, v_out, kn_out, g_out, bonus_out, lw_out, kd_out, b_out):
    row = lax.broadcasted_iota(jnp.int32, u.shape, 0)
    u_dn = jnp.where(row == 0, prev_row, pltpu.roll(u, 1, 0))
    u_up = jnp.where(row == TM - 1, next_row, pltpu.roll(u, TM - 1, 0))
    us = u_dn * sw_ref[0:1, :] + u * sw_ref[1:2, :] + u_up * sw_ref[2:3, :]

    r = us[:, 0:RWKV_W]
    k = us[:, RWKV_W:2 * RWKV_W]
    v = us[:, 2 * RWKV_W:3 * RWKV_W]
    o = 3 * RWKV_W
    gd = us[:, o:o + GATE_RANK]
    wd = us[:, o + GATE_RANK:o + GATE_RANK + 2 * DECAY_RANK]
    ad = us[:, o + GATE_RANK + 2 * DECAY_RANK:RWKV_IN_W]

    w_pre = _dot1(jnp.tanh(wd), wup_ref[...])
    a_pre = _dot1(ad, aup_ref[...])
    g_out[...] = _dot1(jax.nn.sigmoid(gd), gup_ref[...])

    ones_bd = ones_ref[...]
    kn = k * kk_ref[...]
    kn = kn * lax.rsqrt(_dot_exact_rhs(kn * kn, ones_bd) + 1e-12)
    r_out[...] = r
    v_out[...] = v
    kn_out[...] = kn
    bonus_out[...] = _dot_exact_rhs(r * k * rk_ref[...], ones_bd) * v
    ka = ka_ref[...]
    for d in range(2):
        sl = slice(d * RWKV_W, (d + 1) * RWKV_W)
        a_d = jax.nn.sigmoid(a0_ref[d:d + 1, :] + a_pre[:, sl])
        lw_out[d] = -DECAY_SCALE * jax.nn.sigmoid(w0_ref[d:d + 1, :] + w_pre[:, sl])
        kd_out[d] = k * (1.0 + (a_d - 1.0) * ka)
        b_out[d] = kn * a_d


RWKV_COL0 = ATTN_W + 2 * KV_W
N_RWKV_PARAMS = 10


def _inproj_kernel(split, *refs):
    n_x = 2 if split else 1
    i = pl.program_id(0)
    x = _tile_of_split(i, *refs[:2]) if split else refs[0][...]
    xp_ref, xn_ref, mod_ref, g_ref, w_ref = refs[n_x:n_x + 5]
    rwkv_params = refs[n_x + 5:n_x + 5 + N_RWKV_PARAMS]
    q_ref, k_ref, v_ref, uf_ref = refs[n_x + 5 + N_RWKV_PARAMS:n_x + 9 + N_RWKV_PARAMS]
    rwkv_outs = refs[n_x + 9 + N_RWKV_PARAMS:]

    shift = mod_ref[:, 0:D_MODEL]
    scale = mod_ref[:, D_MODEL:2 * D_MODEL]
    x_all = jnp.concatenate([x, xp_ref[...], xn_ref[...]], axis=0)
    h = (_rms(x_all, g_ref[...]) * (1.0 + scale) + shift).astype(BF16)
    u_all = _dg(h, w_ref[:, RWKV_COL0:RWKV_COL0 + RWKV_IN_W], _NN)
    o_qkv = _dg(h[:TM], w_ref[:, 0:RWKV_COL0], _NN)
    q_ref[...] = o_qkv[:, 0:ATTN_W]
    k_ref[...] = o_qkv[:, ATTN_W:ATTN_W + KV_W]
    v_ref[...] = o_qkv[:, ATTN_W + KV_W:RWKV_COL0]
    uf_ref[...] = _dg(h[:TM], w_ref[:, RWKV_COL0 + RWKV_IN_W:IN_W], _NN)

    j = (i - CTX_TILES) % LAT_TILES_PER_SEQ
    has_prev = jnp.where((i >= CTX_TILES) & (j != 0), 1.0, 0.0)
    has_next = jnp.where((i >= CTX_TILES) & (j != LAT_TILES_PER_SEQ - 1), 1.0, 0.0)
    prev_row = u_all[TM + HALO - 1:TM + HALO, :] * has_prev
    next_row = u_all[TM + HALO:TM + HALO + 1, :] * has_next
    _rwkv_token_terms(u_all[:TM], prev_row, next_row, *rwkv_params, *rwkv_outs)


def _inproj_call(xs, mods_l, g, w_bf16, rwkv_params, l):
    split = isinstance(xs, tuple)
    xs = xs if split else (xs,)
    hb = TM // HALO
    if split:
        halo_src = xs[1]
        last = N_LAT // HALO - 1
        prev_blk = lambda i: (jnp.clip((i - CTX_TILES) * hb - 1, 0, last), 0)
        next_blk = lambda i: (jnp.clip((i - CTX_TILES + 1) * hb, 0, last), 0)
    else:
        halo_src = xs[0]
        last = N_TOK // HALO - 1
        prev_blk = lambda i: (jnp.maximum(i * hb - 1, 0), 0)
        next_blk = lambda i: (jnp.minimum((i + 1) * hb, last), 0)
    row = lambda i: (l, 0, 0)
    widths = (ATTN_W, KV_W, KV_W, FNET_W)
    tok = jax.ShapeDtypeStruct((N_TOK, RWKV_W), F32)
    tok2 = jax.ShapeDtypeStruct((2, N_TOK, RWKV_W), F32)
    return pl.pallas_call(
        functools.partial(_inproj_kernel, split),
        grid=(N_TILES,),
        in_specs=_x_specs(split) + [
            pl.BlockSpec((HALO, D_MODEL), prev_blk),
            pl.BlockSpec((HALO, D_MODEL), next_blk),
            pl.BlockSpec((None, 1, 6 * D_MODEL), lambda i: (_mod_row(i), 0, 0)),
            pl.BlockSpec((None, 1, D_MODEL), row),
            pl.BlockSpec((None, D_MODEL, IN_W), row),
            pl.BlockSpec((None, 3, RWKV_IN_W), row),
            pl.BlockSpec((None, 2 * DECAY_RANK, 2 * RWKV_W), row),
            pl.BlockSpec((None, 2 * ICLR_RANK, 2 * RWKV_W), row),
            pl.BlockSpec((None, GATE_RANK, RWKV_W), row),
            pl.BlockSpec((None, 2, RWKV_W), row),
            pl.BlockSpec((None, 2, RWKV_W), row),
            pl.BlockSpec((None, 1, RWKV_W), row),
            pl.BlockSpec((None, 1, RWKV_W), row),
            pl.BlockSpec((None, 1, RWKV_W), row),
            pl.BlockSpec((RWKV_W, RWKV_W), lambda i: (0, 0)),
        ],
        out_specs=[pl.BlockSpec((TM, w), lambda i: (i, 0)) for w in widths]
        + [pl.BlockSpec((TM, RWKV_W), lambda i: (i, 0))] * 5
        + [pl.BlockSpec((2, TM, RWKV_W), lambda i: (0, i, 0))] * 3,
        out_shape=[jax.ShapeDtypeStruct((N_TOK, w), F32) for w in widths] + [tok] * 5 + [tok2] * 3,
        compiler_params=_cparams(("arbitrary",)),
        name="in_projection",
    )(*xs, halo_src, halo_src, mods_l, g, w_bf16, *rwkv_params)


def _softmax_pv(s, sink_col, v):
    m = jnp.maximum(jnp.max(s, axis=-1, keepdims=True), sink_col)
    e = jnp.exp(s - m)
    denom = jnp.sum(e, axis=-1, keepdims=True) + jnp.exp(sink_col - m)
    return _dot1(e, v) / denom


def _stack_q_heads(q, kh, rows):
    return jnp.concatenate(
        [q[:, (kh * Q_PER_KV + g) * HEAD_DIM:(kh * Q_PER_KV + g + 1) * HEAD_DIM] for g in range(Q_PER_KV)],
        axis=0)


def _sink_col(sink_ref, l, kh, rows):
    return jnp.concatenate(
        [jnp.full((rows, 1), sink_ref[l, kh * Q_PER_KV + g], F32) for g in range(Q_PER_KV)], axis=0)


def _ctx_attn_kernel(l, sink_ref, q_ref, k_ref, v_ref, o_ref):
    q = q_ref[...]
    k = k_ref[...]
    v = v_ref[...]
    half = CTX_LEN // 2
    pieces = [(h, r) for h in range(N_Q_HEADS) for r in range(2)]
    kb = k.astype(BF16)
    vb = v.astype(BF16)
    kv = lambda a, h: a[:, (h // Q_PER_KV) * HEAD_DIM:(h // Q_PER_KV + 1) * HEAD_DIM]
    qs = [(q[r * half:(r + 1) * half, h * HEAD_DIM:(h + 1) * HEAD_DIM] * ATTN_SCALE).astype(BF16) for h, r in pieces]
    s = [_dg(qs[n], kv(kb, h), _NT) for n, (h, r) in enumerate(pieces)]
    m = [jnp.maximum(jnp.max(s[n], axis=-1, keepdims=True), sink_ref[l, h]) for n, (h, r) in enumerate(pieces)]
    e = [jnp.exp(s[n] - m[n]) for n in range(len(pieces))]
    den = [jnp.sum(e[n], axis=-1, keepdims=True) + jnp.exp(sink_ref[l, h] - m[n]) for n, (h, r) in enumerate(pieces)]
    o = [_dg(e[n].astype(BF16), kv(vb, h), _NN) / den[n] for n, (h, r) in enumerate(pieces)]
    for n, (h, r) in enumerate(pieces):
        o_ref[r * half:(r + 1) * half, h * HEAD_DIM:(h + 1) * HEAD_DIM] = o[n]


def _ctx_attn_call(q, k, v, sink, l):
    return pl.pallas_call(
        functools.partial(_ctx_attn_kernel, l),
        grid=(N_CTX_SEQ,),
        in_specs=[
            pl.BlockSpec(memory_space=pltpu.SMEM),
            pl.BlockSpec((CTX_LEN, ATTN_W), lambda b: (b, 0)),
            pl.BlockSpec((CTX_LEN, KV_W), lambda b: (b, 0)),
            pl.BlockSpec((CTX_LEN, KV_W), lambda b: (b, 0)),
        ],
        out_specs=pl.BlockSpec((CTX_LEN, ATTN_W), lambda b: (b, 0)),
        out_shape=jax.ShapeDtypeStruct((N_CTX, ATTN_W), F32),
        compiler_params=_cparams(("parallel",)),
        name="context_attention",
    )(sink, q, k, v)


def _rope(x, cos, sin):
    lane = lax.broadcasted_iota(jnp.int32, x.shape, 1)
    partner = jnp.where((lane & 16) == 0, pltpu.roll(x, 128 - 16, 1), pltpu.roll(x, 16, 1))
    return x * cos + partner * sin


def _lat_attn_kernel(l, sink_ref, q_ref, k_ref, v_ref, kc_ref, vc_ref, cos_ref, sin_ref, o_ref):
    nq = LAT_ATTN_QBLOCKS
    nb = LAT_LEN // BLOCK
    n0 = pl.program_id(1) * nq
    r0 = pl.multiple_of(n0 * BLOCK, BLOCK)
    cos_q = cos_ref[pl.ds(r0, nq * BLOCK), :]
    sin_q = sin_ref[pl.ds(r0, nq * BLOCK), :]
    q = jnp.concatenate(
        [_rope(q_ref[:, j * 128:(j + 1) * 128], cos_q, sin_q) for j in range(ATTN_W // 128)], axis=1)

    kr, vr = [], []
    for dj in range(-1, nq + 1):
        rj = pl.multiple_of(jnp.clip(n0 + dj, 0, nb - 1) * BLOCK, BLOCK)
        kr.append(_rope(k_ref[pl.ds(rj, BLOCK), :], cos_ref[pl.ds(rj, BLOCK), :],
                        sin_ref[pl.ds(rj, BLOCK), :]).astype(BF16))
        vr.append(v_ref[pl.ds(rj, BLOCK), :].astype(BF16))
    kc = kc_ref[...].astype(BF16)
    vc = vc_ref[...].astype(BF16)
    k_all = [jnp.concatenate(kr[t:t + 3] + [kc], axis=0) for t in range(nq)]
    v_all = [jnp.concatenate(vr[t:t + 3] + [vc], axis=0) for t in range(nq)]

    kj = lax.broadcasted_iota(jnp.int32, (BLOCK, BLOCK), 1)
    qi = lax.broadcasted_iota(jnp.int32, (BLOCK, BLOCK), 0)
    keep_prev = [(kj >= qi) & (n0 + t > 0) for t in range(nq)]
    keep_next = [(kj <= qi) & (n0 + t < nb - 1) for t in range(nq)]
    kv_cols = lambda h: slice((h // Q_PER_KV) * HEAD_DIM, (h // Q_PER_KV + 1) * HEAD_DIM)

    all_pieces = [(t, h) for t in range(nq) for h in range(N_Q_HEADS)]
    for g0 in range(0, len(all_pieces), LAT_ATTN_GROUP):
        pieces = all_pieces[g0:g0 + LAT_ATTN_GROUP]
        rng = range(len(pieces))
        qs = [(q[t * BLOCK:(t + 1) * BLOCK, h * HEAD_DIM:(h + 1) * HEAD_DIM] * ATTN_SCALE).astype(BF16)
              for t, h in pieces]
        s = [_dg(qs[i], k_all[t][:, kv_cols(h)], _NT) for i, (t, h) in enumerate(pieces)]
        s = [jnp.concatenate([jnp.where(keep_prev[t], s[i][:, 0:BLOCK], NEG), s[i][:, BLOCK:2 * BLOCK],
                              jnp.where(keep_next[t], s[i][:, 2 * BLOCK:3 * BLOCK], NEG), s[i][:, 3 * BLOCK:]],
                             axis=1) for i, (t, h) in enumerate(pieces)]
        m = [jnp.maximum(jnp.max(s[i], axis=-1, keepdims=True), sink_ref[l, h]) for i, (t, h) in enumerate(pieces)]
        e = [jnp.exp(s[i] - m[i]) for i in rng]
        den = [jnp.sum(e[i], axis=-1, keepdims=True) + jnp.exp(sink_ref[l, h] - m[i])
               for i, (t, h) in enumerate(pieces)]
        o = [_dg(e[i].astype(BF16), v_all[t][:, kv_cols(h)], _NN) / den[i] for i, (t, h) in enumerate(pieces)]
        for i, (t, h) in enumerate(pieces):
            o_ref[t * BLOCK:(t + 1) * BLOCK, h * HEAD_DIM:(h + 1) * HEAD_DIM] = o[i]


def _lat_attn_call(q, k, v, kc, vc, cos, sin, sink, l):
    rows = LAT_ATTN_QBLOCKS * BLOCK
    steps = LAT_LEN // rows
    q_off = N_CTX // rows
    kv_off = N_CTX // LAT_LEN
    return pl.pallas_call(
        functools.partial(_lat_attn_kernel, l),
        grid=(N_LAT_SEQ, steps),
        in_specs=[
            pl.BlockSpec(memory_space=pltpu.SMEM),
            pl.BlockSpec((rows, ATTN_W), lambda b, n: (q_off + b * steps + n, 0)),
            pl.BlockSpec((LAT_LEN, KV_W), lambda b, n: (kv_off + b, 0)),
            pl.BlockSpec((LAT_LEN, KV_W), lambda b, n: (kv_off + b, 0)),
            pl.BlockSpec((None, None, PAST_LEN, KV_W), lambda b, n: (b, l, 0, 0)),
            pl.BlockSpec((None, None, PAST_LEN, KV_W), lambda b, n: (b, l, 0, 0)),
            pl.BlockSpec((LAT_LEN, 128), lambda b, n: (0, 0)),
            pl.BlockSpec((LAT_LEN, 128), lambda b, n: (0, 0)),
        ],
        out_specs=pl.BlockSpec((rows, ATTN_W), lambda b, n: (b * steps + n, 0)),
        out_shape=jax.ShapeDtypeStruct((N_LAT, ATTN_W), F32),
        compiler_params=_cparams(("parallel", "arbitrary")),
        name="latent_attention",
    )(sink, q, k, v, kc, vc, cos, sin)


def _rope_tables():
    t = np.arange(LAT_LEN)
    lane = np.arange(128)
    d = lane % HEAD_DIM
    pos = np.where(d[None, :] < HEAD_DIM // 2, (t // GRID_W)[:, None], (t % GRID_W)[:, None]).astype(np.float32)
    quarter = HEAD_DIM // 4
    inv = (np.float32(ROPE_THETA) ** (-(np.arange(quarter, dtype=np.float32)) / np.float32(quarter))).astype(np.float32)
    ang = pos * inv[d % quarter][None, :]
    sign = np.where((lane & 16) == 0, -1.0, 1.0)[None, :]
    return jnp.asarray(np.cos(ang), F32), jnp.asarray(np.sin(ang) * sign, F32)


PAIR_W = 2 * HEAD_DIM
N_PAIRS = RWKV_HEADS // 2
_TN = (((0,), (0,)), ((), ()))


def _pair_stack(x, pmask):
    return jnp.concatenate([x.astype(BF16)] * 2, axis=0) * pmask


def _hdot(m, x, pmask, dims):
    return jnp.concatenate(
        [_dg(m[:, p * PAIR_W:(p + 1) * PAIR_W], _pair_stack(x[:, p * PAIR_W:(p + 1) * PAIR_W], pmask), dims)
         for p in range(N_PAIRS)], axis=1)


def _scan_prepare(reverse, r, v, kn, lw, kd, b, cs, strict, incl, eye, pmask):
    linc = _dot_exact_lhs(cs, lw)
    end = 0 if reverse else CHUNK - 1
    ltot = jnp.concatenate(
        [jnp.broadcast_to(linc[c * CHUNK + end:c * CHUNK + end + 1, :], (CHUNK, RWKV_W))
         for c in range(CHUNKS_PER_TILE)], axis=0)
    v_all = v.astype(BF16)
    r_t = r * jnp.exp(linc)
    a_t = (-kn * jnp.exp(linc - lw)).astype(BF16)
    e_neg = jnp.exp(-linc)
    b_t = (b * e_neg).astype(BF16)
    k_t = (kd * e_neg).astype(BF16)
    e_rem = jnp.exp(ltot - linc)
    b_p = (b * e_rem).astype(BF16)
    k_p = (kd * e_rem).astype(BF16)
    dec = jnp.exp(ltot)

    chunks = range(CHUNKS_PER_TILE)
    sls = [slice(c * CHUNK, (c + 1) * CHUNK) for c in chunks]
    hd = lambda m, x, dims: _hdot(m, x, pmask, dims)
    ar = [jnp.concatenate([a_t[sl], r_t[sl].astype(BF16)], axis=0) for sl in sls]
    m_b = [hd(ar[c], b_t[sls[c]], _NT) for c in chunks]
    m_k = [hd(ar[c], k_t[sls[c]], _NT) for c in chunks]
    a_ab = [m_b[c][:CHUNK] * strict for c in chunks]
    p_rb = [(m_b[c][CHUNK:] * incl).astype(BF16) for c in chunks]
    ak_rk = [jnp.concatenate([m_k[c][:CHUNK] * strict, m_k[c][CHUNK:] * incl], axis=0).astype(BF16)
             for c in chunks]

    tinv = [eye + a_ab[c] for c in chunks]
    p = [a_ab[c].astype(BF16) for c in chunks]
    p = [hd(p[c], p[c], _NN).astype(BF16) for c in chunks]
    for _ in range(4):
        pt = [hd(jnp.concatenate([p[c], tinv[c].astype(BF16)], axis=0), p[c], _NN) for c in chunks]
        p = [pt[c][:CHUNK].astype(BF16) for c in chunks]
        tinv = [tinv[c] + pt[c][CHUNK:] for c in chunks]
    tinv = [(tinv[c] + hd(tinv[c].astype(BF16), p[c], _NN)).astype(BF16) for c in chunks]

    m_v = [hd(ak_rk[c], v_all[sls[c]], _NN) for c in chunks]
    w = [hd(tinv[c], a_t[sls[c]], _NN) for c in chunks]
    u0 = [hd(tinv[c], m_v[c][:CHUNK], _NN) for c in chunks]
    qp = [r_t[sls[c]] + hd(p_rb[c], w[c], _NN) for c in chunks]
    y0 = [hd(p_rb[c], u0[c], _NN) + m_v[c][CHUNK:] for c in chunks]
    pr = [slice(p * PAIR_W, (p + 1) * PAIR_W) for p in range(N_PAIRS)]
    pmask_f = pmask.astype(F32)
    bk = [jnp.concatenate([b_p[sl], k_p[sl]], axis=0) for sl in sls]
    uv = [jnp.concatenate([u0[c].astype(BF16), v_all[sls[c]]], axis=0) for c in chunks]
    wb = [[_dg(w[c][:, q].astype(BF16), b_p[sls[c]][:, q], _TN) * pmask_f for q in pr] for c in chunks]
    gs = [[_dg(uv[c][:, q], bk[c][:, q], _TN) * pmask_f for q in pr] for c in chunks]
    qpb = [qp[c].astype(BF16) for c in chunks]
    decs = [dec[c * CHUNK:c * CHUNK + 1, :] for c in chunks]
    return qpb, y0, wb, gs, decs


def _scan_chain_step(c, s, prep, y_ref):
    qpb, y0, wb, gs, decs = prep
    pr = [slice(p * PAIR_W, (p + 1) * PAIR_W) for p in range(N_PAIRS)]
    sb = [s[p].astype(BF16) for p in range(N_PAIRS)]
    y_ref[c * CHUNK:(c + 1) * CHUNK, :] = y0[c] + jnp.concatenate(
        [_dg(qpb[c][:, pr[p]], sb[p], _NT) for p in range(N_PAIRS)], axis=1)
    return [s[p] * decs[c][:, pr[p]] + (_dg(sb[p], wb[c][p].astype(BF16), _NN) + gs[c][p]) for p in range(N_PAIRS)]


N_STREAMS = 2
STREAM_TILES = N_TILES // N_STREAMS
STREAM_CTX = CTX_TILES // N_STREAMS
SCAN_LANES = tuple((rev, st) for rev in (False, True) for st in range(N_STREAMS))
N_LANE_IN = 7


def _stream_tile(stream, pos):
    lat0 = CTX_TILES + stream * (STREAM_TILES - STREAM_CTX)
    return jnp.where(pos < STREAM_CTX, stream * STREAM_CTX + pos, lat0 + pos - STREAM_CTX)


def _lane_pos(reverse, k):
    return (STREAM_TILES - 1 - k) if reverse else k


def _scan_kernel(*refs):
    n_lanes = len(SCAN_LANES)
    lane_in = [refs[n * N_LANE_IN:(n + 1) * N_LANE_IN] for n in range(n_lanes)]
    cs_ref, strict_ref, incl_ref, eye_ref, pmask_ref = refs[n_lanes * N_LANE_IN:n_lanes * N_LANE_IN + 5]
    outs = refs[n_lanes * N_LANE_IN + 5:n_lanes * N_LANE_IN + 5 + 2 * n_lanes]
    lane_out = [outs[2 * n:2 * n + 2] for n in range(n_lanes)]
    lane_scr = refs[n_lanes * N_LANE_IN + 5 + 2 * n_lanes:]

    k = pl.program_id(0)
    pmask_f = pmask_ref[...]
    pmask = pmask_f.astype(BF16)
    eye = eye_ref[...]

    def load_state(s0_ref, scr, is_ctx):
        for p in range(N_PAIRS):
            carried = jnp.concatenate([s0_ref[p * PAIR_W:(p + 1) * PAIR_W, :]] * 2, axis=1) * pmask_f
            scr[p] = jnp.where(is_ctx, 0.0, carried)

    for n, (reverse, stream) in enumerate(SCAN_LANES):
        pos = _lane_pos(reverse, k)
        j = (pos - STREAM_CTX) % LAT_TILES_PER_SEQ
        first = (pos < STREAM_CTX) | (j == (LAT_TILES_PER_SEQ - 1 if reverse else 0))
        pl.when(first)(functools.partial(load_state, lane_in[n][6], lane_scr[n], pos < STREAM_CTX))

    preps = []
    for n, (reverse, stream) in enumerate(SCAN_LANES):
        d = 1 if reverse else 0
        r_ref, v_ref, kn_ref, lw_ref, kd_ref, b_ref, _ = lane_in[n]
        preps.append(_scan_prepare(reverse, r_ref[...], v_ref[...], kn_ref[...], lw_ref[...], kd_ref[...], b_ref[...],
                                   cs_ref[d], strict_ref[d], incl_ref[d], eye, pmask))
    states = [[lane_scr[n][p] for p in range(N_PAIRS)] for n in range(n_lanes)]
    for c in range(CHUNKS_PER_TILE):
        for n, (reverse, stream) in enumerate(SCAN_LANES):
            cc = CHUNKS_PER_TILE - 1 - c if reverse else c
            states[n] = _scan_chain_step(cc, states[n], preps[n], lane_out[n][0])
    for n in range(n_lanes):
        for p in range(N_PAIRS):
            lane_scr[n][p] = states[n][p]
            for h in range(2):
                blk = slice(h * HEAD_DIM, (h + 1) * HEAD_DIM)
                lane_out[n][1][2 * p + h] = states[n][p][blk, blk]


def _scan_call(r, v, kn, lw, kd, b, s0, consts, l):
    cs, strict, incl, eye, pmask = consts
    lat_seq_of = lambda t: jnp.maximum(t - CTX_TILES, 0) // LAT_TILES_PER_SEQ
    const2 = lambda k: (0, 0)
    const3 = lambda k: (0, 0, 0)
    in_specs, out_specs, args = [], [], []
    for reverse, stream in SCAN_LANES:
        d = 1 if reverse else 0
        tile = functools.partial(lambda rv, st, k: _stream_tile(st, _lane_pos(rv, k)), reverse, stream)
        pos = functools.partial(_lane_pos, reverse)
        tok = pl.BlockSpec((TM, RWKV_W), functools.partial(lambda tl, k: (tl(k), 0), tile))
        tok_d = pl.BlockSpec((None, TM, RWKV_W), functools.partial(lambda tl, dd, k: (dd, tl(k), 0), tile, d))
        s0_spec = pl.BlockSpec((None, None, None, RWKV_W, HEAD_DIM),
                               functools.partial(lambda tl, dd, k: (lat_seq_of(tl(k)), l, dd, 0, 0), tile, d))
        in_specs += [tok, tok, tok, tok_d, tok_d, tok_d, s0_spec]
        args += [r, v, kn, lw, kd, b, s0]
        out_specs += [
            pl.BlockSpec((TM, RWKV_W), functools.partial(lambda ps, k: (ps(k), 0), pos)),
            pl.BlockSpec((None, RWKV_HEADS, HEAD_DIM, HEAD_DIM),
                         functools.partial(lambda ps, k: (jnp.minimum(ps(k), STREAM_CTX), 0, 0, 0), pos)),
        ]
    in_specs += [
        pl.BlockSpec((2, TM, TM), const3),
        pl.BlockSpec((2, CHUNK, RWKV_W), const3),
        pl.BlockSpec((2, CHUNK, RWKV_W), const3),
        pl.BlockSpec((CHUNK, RWKV_W), const2),
        pl.BlockSpec((PAIR_W, PAIR_W), const2),
    ]
    args += [cs, strict, incl, eye, pmask]
    y = jax.ShapeDtypeStruct((STREAM_TILES * TM, RWKV_W), F32)
    sfin = jax.ShapeDtypeStruct((STREAM_CTX + 1, RWKV_HEADS, HEAD_DIM, HEAD_DIM), F32)
    state = pltpu.VMEM((N_PAIRS, PAIR_W, PAIR_W), F32)
    outs = pl.pallas_call(
        _scan_kernel,
        grid=(STREAM_TILES,),
        in_specs=in_specs,
        out_specs=out_specs,
        out_shape=[y, sfin] * len(SCAN_LANES),
        scratch_shapes=[state] * len(SCAN_LANES),
        compiler_params=_cparams(("arbitrary",)),
        name="rwkv_scan",
    )(*args)
    return outs[0::2], outs[1::2]


def _scan_final_states(sfins_per_layer):
    return jnp.stack(
        [jnp.stack([jnp.concatenate([sfins[d * N_STREAMS + st][:STREAM_CTX] for st in range(N_STREAMS)], axis=0)
                    for d in range(2)], axis=1)
         for sfins in sfins_per_layer], axis=1)


def _scan_consts():
    t = np.arange(TM)
    same_chunk = (t[:, None] // CHUNK) == (t[None, :] // CHUNK)
    cs_f = same_chunk & (t[None, :] <= t[:, None])
    cs_b = same_chunk & (t[None, :] >= t[:, None])
    cs = np.stack([cs_f, cs_b]).astype(np.float32)
    q = np.arange(CHUNK)[:, None]
    kcol = np.arange(RWKV_W)[None, :] % CHUNK
    strict = np.stack([kcol < q, kcol > q]).astype(np.float32)
    incl = np.stack([kcol <= q, kcol >= q]).astype(np.float32)
    eye = (kcol == q).astype(np.float32)
    return (jnp.asarray(cs, BF16), jnp.asarray(strict), jnp.asarray(incl),
            jnp.asarray(eye), jnp.asarray(_head_block_mask(PAIR_W)))


FNET_TT = 512
FNET_IN_ROWS = N_CTX


def _fnet_kernel(seq_len, n_in, *refs):
    z_refs = refs[:n_in]
    ch_ref, cl_ref, th_ref, o_ref, z_scr = refs[n_in:]
    b = pl.program_id(1)

    @pl.when((pl.program_id(0) == 0) & (b == 0))
    def _():
        per_in = FNET_IN_ROWS // seq_len
        for n, z_ref in enumerate(z_refs):
            zh, zl = _split2(z_ref[...])
            for half in range(2):
                ch = ch_ref[half]
                cl = cl_ref[half]
                zc = (_dg(zh, ch, _NN) + (_dg(zh, cl, _NN) + _dg(zl, ch, _NN))).astype(BF16)
                for s in range(per_in):
                    z_scr[n * per_in + s, half * seq_len:(half + 1) * seq_len, :] = zc[s * seq_len:(s + 1) * seq_len]

    o_ref[...] = _dg(th_ref[...], z_scr[b], _NN)


def _fnet_call(u_f, seq_len, n_seq, row_off, tables, name):
    ch, cl, th = tables
    tt = min(FNET_TT, seq_len)
    nt = seq_len // tt
    n_in = n_seq * seq_len // FNET_IN_ROWS
    in_off = row_off // FNET_IN_ROWS
    z_specs = [pl.BlockSpec((FNET_IN_ROWS, FNET_W), functools.partial(lambda n, j, b: (in_off + n, 0), n))
               for n in range(n_in)]
    return pl.pallas_call(
        functools.partial(_fnet_kernel, seq_len, n_in),
        grid=(nt, n_seq),
        in_specs=z_specs + [
            pl.BlockSpec((2, FNET_W, FNET_W), lambda j, b: (0, 0, 0)),
            pl.BlockSpec((2, FNET_W, FNET_W), lambda j, b: (0, 0, 0)),
            pl.BlockSpec((tt, 2 * seq_len), lambda j, b: (j, 0)),
        ],
        out_specs=pl.BlockSpec((tt, FNET_W), lambda j, b: (b * nt + j, 0)),
        out_shape=jax.ShapeDtypeStruct((n_seq * seq_len, FNET_W), F32),
        scratch_shapes=[pltpu.VMEM((n_seq, 2 * seq_len, FNET_W), BF16)],
        compiler_params=_cparams(("arbitrary", "arbitrary")),
        name=name,
    )(*([u_f] * n_in), ch, cl, th)


DFT_SPLIT = 32


def _dft_tables(seq_len):
    c = np.arange(FNET_W)
    ang_c = 2.0 * np.pi * ((c[:, None] % HEAD_DIM) * (c[None, :] % HEAD_DIM) % HEAD_DIM) / HEAD_DIM
    scale = 1.0 / np.sqrt(float(seq_len) * HEAD_DIM)
    blk = _head_block_mask(FNET_W)
    ch, cl = _split2(jnp.asarray(np.stack([np.cos(ang_c) * blk * scale, np.sin(ang_c) * blk * scale]), F32))

    n1 = seq_len // DFT_SPLIT
    f = np.arange(seq_len)
    a1 = 2.0 * np.pi * ((np.arange(n1)[:, None] * f[None, :]) % n1) / n1
    a2 = 2.0 * np.pi * ((np.arange(DFT_SPLIT)[:, None] * f[None, :]) % seq_len) / seq_len
    c1, s1 = jnp.asarray(np.cos(a1), F32)[:, None, :], jnp.asarray(np.sin(a1), F32)[:, None, :]
    c2, s2 = jnp.asarray(np.cos(a2), F32)[None, :, :], jnp.asarray(np.sin(a2), F32)[None, :, :]
    cos_t = (c1 * c2 - s1 * s2).reshape(seq_len, seq_len)
    sin_t = (s1 * c2 + c1 * s2).reshape(seq_len, seq_len)
    th = jnp.concatenate([cos_t, -sin_t], axis=1).astype(BF16)
    return ch, cl, th


def _merge_ffn_kernel(split, final, *refs):
    i = pl.program_id(0)
    n_x = 2 if split else 1
    x_in = _tile_of_split(i, *refs[:2], FFN_TM) if split else refs[0][...]
    (mod_ref, oac_ref, oal_ref, yf0_ref, yf1_ref, yb0_ref, yb1_ref, bonus_ref, g_ref, ofc_ref, ofl_ref,
     lng_ref, lnb_ref, avg_ref, nfg_ref, wo_ref, wi_ref, w2_ref) = refs[n_x:n_x + 18]
    tail = refs[n_x + 18:]
    is_ctx = i < N_CTX // FFN_TM
    o_attn = _tile_of_split(i, oac_ref, oal_ref, FFN_TM)
    o_fnet = _tile_of_split(i, ofc_ref, ofl_ref, FFN_TM)
    avg = avg_ref[...]
    y = jnp.where(_in_stream0(i, FFN_TM), yf0_ref[...] + yb0_ref[...], yf1_ref[...] + yb1_ref[...])
    yc = y - _dot_exact_rhs(y, avg)
    var = _dot_exact_rhs(yc * yc, avg)
    yn = yc * lax.rsqrt(var + GN_EPS) * lng_ref[...] + lnb_ref[...]
    o_rwkv = (yn + bonus_ref[...]) * g_ref[...]

    o = (_dg(o_attn.astype(BF16), wo_ref[0:ATTN_W, :], _NN)
         + _dg(o_rwkv.astype(BF16), wo_ref[ATTN_W:ATTN_W + RWKV_W, :], _NN)
         + _dg(o_fnet.astype(BF16), wo_ref[ATTN_W + RWKV_W:, :], _NN))
    g1 = mod_ref[:, 2 * D_MODEL:3 * D_MODEL]
    sh2 = mod_ref[:, 3 * D_MODEL:4 * D_MODEL]
    sc2 = mod_ref[:, 4 * D_MODEL:5 * D_MODEL]
    g2 = mod_ref[:, 5 * D_MODEL:6 * D_MODEL]
    x = x_in + g1 * o
    h = _rms(x, nfg_ref[...]) * (1.0 + sc2) + sh2
    gu = _dg(h.astype(BF16), wi_ref[...], _NN)
    gt = gu[:, 0:D_FF]
    act = gt * jax.nn.sigmoid(gt) * gu[:, D_FF:]
    x = x + g2 * _dg(act.astype(BF16), w2_ref[...], _NN)
    if not final:
        (o_ref,) = tail
        o_ref[...] = x
    else:
        fng_ref, oc_ref, ol_ref = tail
        y_out = _rms(x, fng_ref[...])

        @pl.when(is_ctx)
        def _():
            oc_ref[...] = y_out

        @pl.when(jnp.logical_not(is_ctx))
        def _():
            ol_ref[...] = y_out


def _in_stream0(m, tm):
    ctx_m = N_CTX // tm
    lat_ps = (N_TOK // tm - ctx_m) // N_STREAMS
    return (m < ctx_m // N_STREAMS) | ((m >= ctx_m) & (m < ctx_m + lat_ps))


def _stream_block(stream, tm):
    ctx_m = N_CTX // tm
    half = ctx_m // N_STREAMS
    lat_ps = (N_TOK // tm - ctx_m) // N_STREAMS
    if stream == 0:
        return lambda m: (jnp.where(m < half, m, jnp.where(m < ctx_m, half - 1,
                                    jnp.where(m < ctx_m + lat_ps, m - half, half + lat_ps - 1))), 0)
    return lambda m: (jnp.where(m < half, 0, jnp.where(m < ctx_m, m - half,
                                jnp.where(m < ctx_m + lat_ps, half - 1, m - half - lat_ps))), 0)


def _merge_ffn_call(xs, mods_l, oa_ctx, oa_lat, ys, bonus, g, of_ctx, of_lat, ln_g, ln_b, avg, nfg,
                    wo, wi, w2, l, final_g=None):
    final = final_g is not None
    split = isinstance(xs, tuple)
    xs = xs if split else (xs,)
    row = lambda i: (l, 0, 0)
    tile = lambda i: (i, 0)
    ctx_tile, lat_tile = _ctx_tile(FFN_TM), _lat_tile(FFN_TM)
    once = pl.Buffered(1)
    in_specs = _x_specs(split, FFN_TM) + [
        pl.BlockSpec((None, 1, 6 * D_MODEL), lambda i: (_mod_row(i, FFN_TM), 0, 0)),
        pl.BlockSpec((FFN_TM, ATTN_W), ctx_tile),
        pl.BlockSpec((FFN_TM, ATTN_W), lat_tile),
        pl.BlockSpec((FFN_TM, RWKV_W), _stream_block(0, FFN_TM)),
        pl.BlockSpec((FFN_TM, RWKV_W), _stream_block(1, FFN_TM)),
        pl.BlockSpec((FFN_TM, RWKV_W), _stream_block(0, FFN_TM)),
        pl.BlockSpec((FFN_TM, RWKV_W), _stream_block(1, FFN_TM)),
        pl.BlockSpec((FFN_TM, RWKV_W), tile),
        pl.BlockSpec((FFN_TM, RWKV_W), tile),
        pl.BlockSpec((FFN_TM, FNET_W), ctx_tile),
        pl.BlockSpec((FFN_TM, FNET_W), lat_tile),
        pl.BlockSpec((None, 1, RWKV_W), row),
        pl.BlockSpec((None, 1, RWKV_W), row),
        pl.BlockSpec((RWKV_W, RWKV_W), lambda i: (0, 0)),
        pl.BlockSpec((None, 1, D_MODEL), row),
        pl.BlockSpec((None, D_MODEL, D_MODEL), row, pipeline_mode=once),
        pl.BlockSpec((None, D_MODEL, 2 * D_FF), row, pipeline_mode=once),
        pl.BlockSpec((None, D_FF, D_MODEL), row, pipeline_mode=once),
    ]
    args = [*xs, mods_l, oa_ctx, oa_lat, *ys, bonus, g, of_ctx, of_lat, ln_g, ln_b, avg, nfg, wo, wi, w2]
    if final:
        in_specs.append(pl.BlockSpec((1, D_MODEL), lambda i: (0, 0)))
        args.append(final_g)
        out_specs = [pl.BlockSpec((FFN_TM, D_MODEL), ctx_tile), pl.BlockSpec((FFN_TM, D_MODEL), lat_tile)]
        out_shape = [jax.ShapeDtypeStruct((N_CTX, D_MODEL), F32), jax.ShapeDtypeStruct((N_LAT, D_MODEL), F32)]
    else:
        out_specs = pl.BlockSpec((FFN_TM, D_MODEL), tile)
        out_shape = jax.ShapeDtypeStruct((N_TOK, D_MODEL), F32)
    return pl.pallas_call(
        functools.partial(_merge_ffn_kernel, split, final),
        grid=(N_TOK // FFN_TM,),
        in_specs=in_specs,
        out_specs=out_specs,
        out_shape=out_shape,
        compiler_params=_cparams(("arbitrary",)),
        name="merge_ffn_final" if final else "merge_ffn",
    )(*args)


def _block_diag2(w):
    z = jnp.zeros_like(w[:, 0])
    return jnp.concatenate([jnp.concatenate([w[:, 0], z], axis=2), jnp.concatenate([z, w[:, 1]], axis=2)], axis=1)


def kernel(x_prompt, x_sample, cache_k, cache_v, state_rwkv, c, c_ctx, w_ada, b_ada, norm_mix_g, norm_ffn_g,
           w_in, w_out, attn_sink, rwkv_shift_w, rwkv_w0, rwkv_w_up, rwkv_a0, rwkv_a_up, rwkv_g_up,
           rwkv_k_k, rwkv_k_a, rwkv_r_k, rwkv_ln_g, rwkv_ln_b, ffn_w_in, ffn_w_out, norm_final_g):
    x = (x_prompt.reshape(N_CTX, D_MODEL), x_sample.reshape(N_LAT, D_MODEL))
    cvec = jnp.concatenate([c_ctx[None, :], c, jnp.zeros((8 - 1 - N_LAT_SEQ, D_MODEL), F32)], axis=0)
    mods = _mods_call(cvec, w_ada, b_ada).reshape(DEPTH, 8, 1, 6 * D_MODEL)

    w_in_b = w_in.astype(BF16)
    w_out_b = w_out.astype(BF16)
    ffn_in_b = ffn_w_in.astype(BF16)
    ffn_out_b = ffn_w_out.astype(BF16)
    wup_bd = _block_diag2(rwkv_w_up)
    aup_bd = _block_diag2(rwkv_a_up)
    row3 = lambda a: a.reshape(DEPTH, 1, -1)
    nmg, nfg = row3(norm_mix_g), row3(norm_ffn_g)
    k_k, k_a, r_k = row3(rwkv_k_k), row3(rwkv_k_a), row3(rwkv_r_k)
    ln_g, ln_b = row3(rwkv_ln_g), row3(rwkv_ln_b)
    kc = cache_k.reshape(N_LAT_SEQ, DEPTH, PAST_LEN, KV_W)
    vc = cache_v.reshape(N_LAT_SEQ, DEPTH, PAST_LEN, KV_W)
    s0 = state_rwkv.reshape(N_LAT_SEQ, DEPTH, 2, RWKV_W, HEAD_DIM)

    cos, sin = _rope_tables()
    consts = _scan_consts()
    bmask = jnp.asarray(_head_block_mask(RWKV_W))
    ones_bd = bmask.astype(BF16)
    avg_bd = (bmask * (1.0 / HEAD_DIM)).astype(BF16)
    rwkv_params = (rwkv_shift_w, wup_bd, aup_bd, rwkv_g_up, rwkv_w0, rwkv_a0, k_k, k_a, r_k, ones_bd)
    dft_ctx = _dft_tables(CTX_LEN)
    dft_lat = _dft_tables(LAT_LEN)

    new_k, new_v, new_s = [], [], []
    for l in range(DEPTH):
        q, k, v, u_f, r, vv, kn, g, bonus, lw, kd, b = _inproj_call(x, mods[l], nmg, w_in_b, rwkv_params, l)
        new_k.append(k[:N_CTX].reshape(N_CTX_SEQ, CTX_LEN, KV_W))
        new_v.append(v[:N_CTX].reshape(N_CTX_SEQ, CTX_LEN, KV_W))

        oa_ctx = _ctx_attn_call(q, k, v, attn_sink, l)
        oa_lat = _lat_attn_call(q, k, v, kc, vc, cos, sin, attn_sink, l)

        ys, sfins = _scan_call(r, vv, kn, lw, kd, b, s0, consts, l)
        new_s.append(sfins)

        of_ctx = _fnet_call(u_f, CTX_LEN, N_CTX_SEQ, 0, dft_ctx, "fnet_context")
        of_lat = _fnet_call(u_f, LAT_LEN, N_LAT_SEQ, N_CTX, dft_lat, "fnet_latent")

        last = l == DEPTH - 1
        x = _merge_ffn_call(x, mods[l], oa_ctx, oa_lat, ys, bonus, g, of_ctx, of_lat, ln_g, ln_b, avg_bd, nfg,
                            w_out_b, ffn_in_b, ffn_out_b, l,
                            final_g=norm_final_g.reshape(1, D_MODEL) if last else None)

    y_ctx, y_lat = x
    return (y_ctx.reshape(N_CTX_SEQ, CTX_LEN, D_MODEL), y_lat.reshape(N_LAT_SEQ, LAT_LEN, D_MODEL),
            jnp.stack(new_k, axis=1).reshape(N_CTX_SEQ, DEPTH, CTX_LEN, N_KV_HEADS, HEAD_DIM),
            jnp.stack(new_v, axis=1).reshape(N_CTX_SEQ, DEPTH, CTX_LEN, N_KV_HEADS, HEAD_DIM),
            _scan_final_states(new_s))
```

```python
import functools

import numpy as np
import jax
import jax.numpy as jnp
from jax import lax
from jax.experimental import pallas as pl
from jax.experimental.pallas import tpu as pltpu

F32 = jnp.float32
BF16 = jnp.bfloat16

D_MODEL = 1024
N_CTX_SEQ = 16
CTX_LEN = 256
DEPTH = 4
N_LAT_SEQ = 4
LAT_LEN = 2048
PAST_LEN = 512
GRID_W = 64
HEAD_DIM = 64
N_Q_HEADS = 8
N_KV_HEADS = 2
Q_PER_KV = N_Q_HEADS // N_KV_HEADS
ATTN_W = N_Q_HEADS * HEAD_DIM
KV_W = N_KV_HEADS * HEAD_DIM
BLOCK = 128
ATTN_SCALE = 0.125
ROPE_THETA = 10000.0
NEG = -1e30
RWKV_HEADS = 4
RWKV_W = RWKV_HEADS * HEAD_DIM
DECAY_RANK = 64
ICLR_RANK = 64
GATE_RANK = 128
DECAY_SCALE = 0.6065306597126334
GN_EPS = 64e-5
RWKV_IN_W = 3 * RWKV_W + GATE_RANK + 2 * DECAY_RANK + 2 * ICLR_RANK
FNET_W = 4 * HEAD_DIM
IN_W = ATTN_W + 2 * KV_W + RWKV_IN_W + FNET_W
D_FF = 2816
RMS_EPS = 1e-6

N_CTX = N_CTX_SEQ * CTX_LEN
N_LAT = N_LAT_SEQ * LAT_LEN
N_TOK = N_CTX + N_LAT
TM = 256
N_TILES = N_TOK // TM
CTX_TILES = N_CTX // TM
LAT_TILES_PER_SEQ = LAT_LEN // TM
N_SEQ = N_CTX_SEQ + N_LAT_SEQ
CHUNK = 64
CHUNKS_PER_TILE = TM // CHUNK
HALO = 8
FFN_TM = 512
LAT_ATTN_QBLOCKS = 4
LAT_ATTN_GROUP = 8

VMEM_LIMIT = 56 * 1024 * 1024


def _cparams(sem):
    return pltpu.CompilerParams(dimension_semantics=sem, vmem_limit_bytes=VMEM_LIMIT)


def _mod_row(i, tm=TM):
    return jnp.where(i < N_CTX // tm, 0, 1 + (i - N_CTX // tm) // (LAT_LEN // tm))


def _split2(x):
    hi = x.astype(BF16)
    lo = (x - hi.astype(F32)).astype(BF16)
    return hi, lo


_NN = (((1,), (0,)), ((), ()))
_NT = (((1,), (1,)), ((), ()))


def _dg(a, b, dims):
    return lax.dot_general(a, b, dims, preferred_element_type=F32)


def _dot1(a, b, dims=_NN):
    return _dg(a.astype(BF16), b.astype(BF16), dims)


def _dot3(a, b, dims=_NN):
    ah, al = _split2(a)
    bh, bl = _split2(b)
    return _dg(ah, bh, dims) + (_dg(ah, bl, dims) + _dg(al, bh, dims))


def _dot_exact_rhs(a, b_bf16):
    ah, al = _split2(a)
    return _dg(ah, b_bf16, _NN) + _dg(al, b_bf16, _NN)


def _dot_exact_lhs(a_bf16, b):
    bh, bl = _split2(b)
    return _dg(a_bf16, bh, _NN) + _dg(a_bf16, bl, _NN)


def _mods_kernel(c_ref, w_ref, b_ref, o_ref):
    c = c_ref[...]
    s = c * jax.nn.sigmoid(c)
    o_ref[...] = _dot1(s, w_ref[...]) + b_ref[...]


def _mods_call(cvec, w_ada, b_ada):
    tn = 1536
    return pl.pallas_call(
        _mods_kernel,
        grid=(DEPTH, 6 * D_MODEL // tn),
        in_specs=[
            pl.BlockSpec((8, D_MODEL), lambda l, j: (0, 0)),
            pl.BlockSpec((None, D_MODEL, tn), lambda l, j: (l, 0, j)),
            pl.BlockSpec((None, 1, tn), lambda l, j: (l, 0, j)),
        ],
        out_specs=pl.BlockSpec((None, 8, tn), lambda l, j: (l, 0, j)),
        out_shape=jax.ShapeDtypeStruct((DEPTH, 8, 6 * D_MODEL), F32),
        compiler_params=_cparams(("parallel", "parallel")),
        name="adaln_mods",
    )(cvec, w_ada, b_ada.reshape(DEPTH, 1, 6 * D_MODEL))


def _rms(x, g):
    return x * lax.rsqrt(jnp.mean(x * x, axis=-1, keepdims=True) + RMS_EPS) * g


def _tile_of_split(i, ctx_ref, lat_ref, tm=TM):
    return jnp.where(i < N_CTX // tm, ctx_ref[...], lat_ref[...])


def _ctx_tile(tm=TM):
    return lambda i: (jnp.minimum(i, N_CTX // tm - 1), 0)


def _lat_tile(tm=TM):
    return lambda i: (jnp.maximum(i - N_CTX // tm, 0), 0)


def _x_specs(split, tm=TM):
    if split:
        return [pl.BlockSpec((tm, D_MODEL), _ctx_tile(tm)), pl.BlockSpec((tm, D_MODEL), _lat_tile(tm))]
    return [pl.BlockSpec((tm, D_MODEL), lambda i: (i, 0))]


def _head_block_mask(n):
    r = np.arange(n)[:, None] // HEAD_DIM
    c = np.arange(n)[None, :] // HEAD_DIM
    return (r == c).astype(np.float32)


def _rwkv_token_terms(u, prev_row, next_row, sw_ref, wup_ref, aup_ref, gup_ref, w0_ref, a0_ref,
                      kk_ref, ka_ref, rk_ref, ones_ref, r_out, v_out, kn_out, g_out, bonus_out, lw_out, kd_out, b_out):
    row = lax.broadcasted_iota(jnp.int32, u.shape, 0)
    u_dn = jnp.where(row == 0, prev_row, pltpu.roll(u, 1, 0))
    u_up = jnp.where(row == TM - 1, next_row, pltpu.roll(u, TM - 1, 0))
    us = u_dn * sw_ref[0:1, :] + u * sw_ref[1:2, :] + u_up * sw_ref[2:3, :]

    r = us[:, 0:RWKV_W]
    k = us[:, RWKV_W:2 * RWKV_W]
    v = us[:, 2 * RWKV_W:3 * RWKV_W]
    o = 3 * RWKV_W
    gd = us[:, o:o + GATE_RANK]
    wd = us[:, o + GATE_RANK:o + GATE_RANK + 2 * DECAY_RANK]
    ad = us[:, o + GATE_RANK + 2 * DECAY_RANK:RWKV_IN_W]

    w_pre = _dot1(jnp.tanh(wd), wup_ref[...])
    a_pre = _dot1(ad, aup_ref[...])
    g_out[...] = _dot1(jax.nn.sigmoid(gd), gup_ref[...])

    ones_bd = ones_ref[...]
    kn = k * kk_ref[...]
    kn = kn * lax.rsqrt(_dot_exact_rhs(kn * kn, ones_bd) + 1e-12)
    r_out[...] = r
    v_out[...] = v
    kn_out[...] = kn
    bonus_out[...] = _dot_exact_rhs(r * k * rk_ref[...], ones_bd) * v
    ka = ka_ref[...]
    for d in range(2):
        sl = slice(d * RWKV_W, (d + 1) * RWKV_W)
        a_d = jax.nn.sigmoid(a0_ref[d:d + 1, :] + a_pre[:, sl])
        lw_out[d] = -DECAY_SCALE * jax.nn.sigmoid(w0_ref[d:d + 1, :] + w_pre[:, sl])
        kd_out[d] = k * (1.0 + (a_d - 1.0) * ka)
        b_out[d] = kn * a_d


RWKV_COL0 = ATTN_W + 2 * KV_W
N_RWKV_PARAMS = 10


def _inproj_kernel(split, *refs):
    n_x = 2 if split else 1
    i = pl.program_id(0)
    x = _tile_of_split(i, *refs[:2]) if split else refs[0][...]
    xp_ref, xn_ref, mod_ref, g_ref, w_ref = refs[n_x:n_x + 5]
    rwkv_params = refs[n_x + 5:n_x + 5 + N_RWKV_PARAMS]
    q_ref, k_ref, v_ref, uf_ref = refs[n_x + 5 + N_RWKV_PARAMS:n_x + 9 + N_RWKV_PARAMS]
    rwkv_outs = refs[n_x + 9 + N_RWKV_PARAMS:]

    shift = mod_ref[:, 0:D_MODEL]
    scale = mod_ref[:, D_MODEL:2 * D_MODEL]
    x_all = jnp.concatenate([x, xp_ref[...], xn_ref[...]], axis=0)
    h = (_rms(x_all, g_ref[...]) * (1.0 + scale) + shift).astype(BF16)
    u_all = _dg(h, w_ref[:, RWKV_COL0:RWKV_COL0 + RWKV_IN_W], _NN)
    o_qkv = _dg(h[:TM], w_ref[:, 0:RWKV_COL0], _NN)
    q_ref[...] = o_qkv[:, 0:ATTN_W]
    k_ref[...] = o_qkv[:, ATTN_W:ATTN_W + KV_W]
    v_ref[...] = o_qkv[:, ATTN_W + KV_W:RWKV_COL0]
    uf_ref[...] = _dg(h[:TM], w_ref[:, RWKV_COL0 + RWKV_IN_W:IN_W], _NN)

    j = (i - CTX_TILES) % LAT_TILES_PER_SEQ
    has_prev = jnp.where((i >= CTX_TILES) & (j != 0), 1.0, 0.0)
    has_next = jnp.where((i >= CTX_TILES) & (j != LAT_TILES_PER_SEQ - 1), 1.0, 0.0)
    prev_row = u_all[TM + HALO - 1:TM + HALO, :] * has_prev
    next_row = u_all[TM + HALO:TM + HALO + 1, :] * has_next
    _rwkv_token_terms(u_all[:TM], prev_row, next_row, *rwkv_params, *rwkv_outs)


def _inproj_call(xs, mods_l, g, w_bf16, rwkv_params, l):
    split = isinstance(xs, tuple)
    xs = xs if split else (xs,)
    hb = TM // HALO
    if split:
        halo_src = xs[1]
        last = N_LAT // HALO - 1
        prev_blk = lambda i: (jnp.clip((i - CTX_TILES) * hb - 1, 0, last), 0)
        next_blk = lambda i: (jnp.clip((i - CTX_TILES + 1) * hb, 0, last), 0)
    else:
        halo_src = xs[0]
        last = N_TOK // HALO - 1
        prev_blk = lambda i: (jnp.maximum(i * hb - 1, 0), 0)
        next_blk = lambda i: (jnp.minimum((i + 1) * hb, last), 0)
    row = lambda i: (l, 0, 0)
    widths = (ATTN_W, KV_W, KV_W, FNET_W)
    tok = jax.ShapeDtypeStruct((N_TOK, RWKV_W), F32)
    tok2 = jax.ShapeDtypeStruct((2, N_TOK, RWKV_W), F32)
    return pl.pallas_call(
        functools.partial(_inproj_kernel, split),
        grid=(N_TILES,),
        in_specs=_x_specs(split) + [
            pl.BlockSpec((HALO, D_MODEL), prev_blk),
            pl.BlockSpec((HALO, D_MODEL), next_blk),
            pl.BlockSpec((None, 1, 6 * D_MODEL), lambda i: (_mod_row(i), 0, 0)),
            pl.BlockSpec((None, 1, D_MODEL), row),
            pl.BlockSpec((None, D_MODEL, IN_W), row),
            pl.BlockSpec((None, 3, RWKV_IN_W), row),
            pl.BlockSpec((None, 2 * DECAY_RANK, 2 * RWKV_W), row),
            pl.BlockSpec((None, 2 * ICLR_RANK, 2 * RWKV_W), row),
            pl.BlockSpec((None, GATE_RANK, RWKV_W), row),
            pl.BlockSpec((None, 2, RWKV_W), row),
            pl.BlockSpec((None, 2, RWKV_W), row),
            pl.BlockSpec((None, 1, RWKV_W), row),
            pl.BlockSpec((None, 1, RWKV_W), row),
            pl.BlockSpec((None, 1, RWKV_W), row),
            pl.BlockSpec((RWKV_W, RWKV_W), lambda i: (0, 0)),
        ],
        out_specs=[pl.BlockSpec((TM, w), lambda i: (i, 0)) for w in widths]
        + [pl.BlockSpec((TM, RWKV_W), lambda i: (i, 0))] * 5
        + [pl.BlockSpec((2, TM, RWKV_W), lambda i: (0, i, 0))] * 3,
        out_shape=[jax.ShapeDtypeStruct((N_TOK, w), F32) for w in widths] + [tok] * 5 + [tok2] * 3,
        compiler_params=_cparams(("arbitrary",)),
        name="in_projection",
    )(*xs, halo_src, halo_src, mods_l, g, w_bf16, *rwkv_params)


def _softmax_pv(s, sink_col, v):
    m = jnp.maximum(jnp.max(s, axis=-1, keepdims=True), sink_col)
    e = jnp.exp(s - m)
    denom = jnp.sum(e, axis=-1, keepdims=True) + jnp.exp(sink_col - m)
    return _dot1(e, v) / denom


def _stack_q_heads(q, kh, rows):
    return jnp.concatenate(
        [q[:, (kh * Q_PER_KV + g) * HEAD_DIM:(kh * Q_PER_KV + g + 1) * HEAD_DIM] for g in range(Q_PER_KV)],
        axis=0)


def _sink_col(sink_ref, l, kh, rows):
    return jnp.concatenate(
        [jnp.full((rows, 1), sink_ref[l, kh * Q_PER_KV + g], F32) for g in range(Q_PER_KV)], axis=0)


def _ctx_attn_kernel(l, sink_ref, q_ref, k_ref, v_ref, o_ref):
    q = q_ref[...]
    k = k_ref[...]
    v = v_ref[...]
    half = CTX_LEN // 2
    pieces = [(h, r) for h in range(N_Q_HEADS) for r in range(2)]
    kb = k.astype(BF16)
    vb = v.astype(BF16)
    kv = lambda a, h: a[:, (h // Q_PER_KV) * HEAD_DIM:(h // Q_PER_KV + 1) * HEAD_DIM]
    qs = [(q[r * half:(r + 1) * half, h * HEAD_DIM:(h + 1) * HEAD_DIM] * ATTN_SCALE).astype(BF16) for h, r in pieces]
    s = [_dg(qs[n], kv(kb, h), _NT) for n, (h, r) in enumerate(pieces)]
    m = [jnp.maximum(jnp.max(s[n], axis=-1, keepdims=True), sink_ref[l, h]) for n, (h, r) in enumerate(pieces)]
    e = [jnp.exp(s[n] - m[n]) for n in range(len(pieces))]
    den = [jnp.sum(e[n], axis=-1, keepdims=True) + jnp.exp(sink_ref[l, h] - m[n]) for n, (h, r) in enumerate(pieces)]
    o = [_dg(e[n].astype(BF16), kv(vb, h), _NN) / den[n] for n, (h, r) in enumerate(pieces)]
    for n, (h, r) in enumerate(pieces):
        o_ref[r * half:(r + 1) * half, h * HEAD_DIM:(h + 1) * HEAD_DIM] = o[n]


def _ctx_attn_call(q, k, v, sink, l):
    return pl.pallas_call(
        functools.partial(_ctx_attn_kernel, l),
        grid=(N_CTX_SEQ,),
        in_specs=[
            pl.BlockSpec(memory_space=pltpu.SMEM),
            pl.BlockSpec((CTX_LEN, ATTN_W), lambda b: (b, 0)),
            pl.BlockSpec((CTX_LEN, KV_W), lambda b: (b, 0)),
            pl.BlockSpec((CTX_LEN, KV_W), lambda b: (b, 0)),
        ],
        out_specs=pl.BlockSpec((CTX_LEN, ATTN_W), lambda b: (b, 0)),
        out_shape=jax.ShapeDtypeStruct((N_CTX, ATTN_W), F32),
        compiler_params=_cparams(("parallel",)),
        name="context_attention",
    )(sink, q, k, v)


def _rope(x, cos, sin):
    lane = lax.broadcasted_iota(jnp.int32, x.shape, 1)
    partner = jnp.where((lane & 16) == 0, pltpu.roll(x, 128 - 16, 1), pltpu.roll(x, 16, 1))
    return x * cos + partner * sin


def _lat_attn_kernel(l, sink_ref, q_ref, k_ref, v_ref, kc_ref, vc_ref, cos_ref, sin_ref, o_ref):
    nq = LAT_ATTN_QBLOCKS
    nb = LAT_LEN // BLOCK
    n0 = pl.program_id(1) * nq
    r0 = pl.multiple_of(n0 * BLOCK, BLOCK)
    cos_q = cos_ref[pl.ds(r0, nq * BLOCK), :]
    sin_q = sin_ref[pl.ds(r0, nq * BLOCK), :]
    q = jnp.concatenate(
        [_rope(q_ref[:, j * 128:(j + 1) * 128], cos_q, sin_q) for j in range(ATTN_W // 128)], axis=1)

    kr, vr = [], []
    for dj in range(-1, nq + 1):
        rj = pl.multiple_of(jnp.clip(n0 + dj, 0, nb - 1) * BLOCK, BLOCK)
        kr.append(_rope(k_ref[pl.ds(rj, BLOCK), :], cos_ref[pl.ds(rj, BLOCK), :],
                        sin_ref[pl.ds(rj, BLOCK), :]).astype(BF16))
        vr.append(v_ref[pl.ds(rj, BLOCK), :].astype(BF16))
    kc = kc_ref[...].astype(BF16)
    vc = vc_ref[...].astype(BF16)
    k_all = [jnp.concatenate(kr[t:t + 3] + [kc], axis=0) for t in range(nq)]
    v_all = [jnp.concatenate(vr[t:t + 3] + [vc], axis=0) for t in range(nq)]

    kj = lax.broadcasted_iota(jnp.int32, (BLOCK, BLOCK), 1)
    qi = lax.broadcasted_iota(jnp.int32, (BLOCK, BLOCK), 0)
    keep_prev = [(kj >= qi) & (n0 + t > 0) for t in range(nq)]
    keep_next = [(kj <= qi) & (n0 + t < nb - 1) for t in range(nq)]
    kv_cols = lambda h: slice((h // Q_PER_KV) * HEAD_DIM, (h // Q_PER_KV + 1) * HEAD_DIM)

    all_pieces = [(t, h) for t in range(nq) for h in range(N_Q_HEADS)]
    for g0 in range(0, len(all_pieces), LAT_ATTN_GROUP):
        pieces = all_pieces[g0:g0 + LAT_ATTN_GROUP]
        rng = range(len(pieces))
        qs = [(q[t * BLOCK:(t + 1) * BLOCK, h * HEAD_DIM:(h + 1) * HEAD_DIM] * ATTN_SCALE).astype(BF16)
              for t, h in pieces]
        s = [_dg(qs[i], k_all[t][:, kv_cols(h)], _NT) for i, (t, h) in enumerate(pieces)]
        s = [jnp.concatenate([jnp.where(keep_prev[t], s[i][:, 0:BLOCK], NEG), s[i][:, BLOCK:2 * BLOCK],
                              jnp.where(keep_next[t], s[i][:, 2 * BLOCK:3 * BLOCK], NEG), s[i][:, 3 * BLOCK:]],
                             axis=1) for i, (t, h) in enumerate(pieces)]
        m = [jnp.maximum(jnp.max(s[i], axis=-1, keepdims=True), sink_ref[l, h]) for i, (t, h) in enumerate(pieces)]
        e = [jnp.exp(s[i] - m[i]) for i in rng]
        den = [jnp.sum(e[i], axis=-1, keepdims=True) + jnp.exp(sink_ref[l, h] - m[i])
               for i, (t, h) in enumerate(pieces)]
        o = [_dg(e[i].astype(BF16), v_all[t][:, kv_cols(h)], _NN) / den[i] for i, (t, h) in enumerate(pieces)]
        for i, (t, h) in enumerate(pieces):
            o_ref[t * BLOCK:(t + 1) * BLOCK, h * HEAD_DIM:(h + 1) * HEAD_DIM] = o[i]


def _lat_attn_call(q, k, v, kc, vc, cos, sin, sink, l):
    rows = LAT_ATTN_QBLOCKS * BLOCK
    steps = LAT_LEN // rows
    q_off = N_CTX // rows
    kv_off = N_CTX // LAT_LEN
    return pl.pallas_call(
        functools.partial(_lat_attn_kernel, l),
        grid=(N_LAT_SEQ, steps),
        in_specs=[
            pl.BlockSpec(memory_space=pltpu.SMEM),
            pl.BlockSpec((rows, ATTN_W), lambda b, n: (q_off + b * steps + n, 0)),
            pl.BlockSpec((LAT_LEN, KV_W), lambda b, n: (kv_off + b, 0)),
            pl.BlockSpec((LAT_LEN, KV_W), lambda b, n: (kv_off + b, 0)),
            pl.BlockSpec((None, None, PAST_LEN, KV_W), lambda b, n: (b, l, 0, 0)),
            pl.BlockSpec((None, None, PAST_LEN, KV_W), lambda b, n: (b, l, 0, 0)),
            pl.BlockSpec((LAT_LEN, 128), lambda b, n: (0, 0)),
            pl.BlockSpec((LAT_LEN, 128), lambda b, n: (0, 0)),
        ],
        out_specs=pl.BlockSpec((rows, ATTN_W), lambda b, n: (b * steps + n, 0)),
        out_shape=jax.ShapeDtypeStruct((N_LAT, ATTN_W), F32),
        compiler_params=_cparams(("parallel", "arbitrary")),
        name="latent_attention",
    )(sink, q, k, v, kc, vc, cos, sin)


def _rope_tables():
    t = np.arange(LAT_LEN)
    lane = np.arange(128)
    d = lane % HEAD_DIM
    pos = np.where(d[None, :] < HEAD_DIM // 2, (t // GRID_W)[:, None], (t % GRID_W)[:, None]).astype(np.float32)
    quarter = HEAD_DIM // 4
    inv = (np.float32(ROPE_THETA) ** (-(np.arange(quarter, dtype=np.float32)) / np.float32(quarter))).astype(np.float32)
    ang = pos * inv[d % quarter][None, :]
    sign = np.where((lane & 16) == 0, -1.0, 1.0)[None, :]
    return jnp.asarray(np.cos(ang), F32), jnp.asarray(np.sin(ang) * sign, F32)


PAIR_W = 2 * HEAD_DIM
N_PAIRS = RWKV_HEADS // 2
_TN = (((0,), (0,)), ((), ()))


def _pair_stack(x, pmask):
    return jnp.concatenate([x.astype(BF16)] * 2, axis=0) * pmask


def _hdot(m, x, pmask, dims):
    return jnp.concatenate(
        [_dg(m[:, p * PAIR_W:(p + 1) * PAIR_W], _pair_stack(x[:, p * PAIR_W:(p + 1) * PAIR_W], pmask), dims)
         for p in range(N_PAIRS)], axis=1)


def _scan_prepare(reverse, r, v, kn, lw, kd, b, cs, strict, incl, eye, pmask):
    linc = _dot_exact_lhs(cs, lw)
    end = 0 if reverse else CHUNK - 1
    ltot = jnp.concatenate(
        [jnp.broadcast_to(linc[c * CHUNK + end:c * CHUNK + end + 1, :], (CHUNK, RWKV_W))
         for c in range(CHUNKS_PER_TILE)], axis=0)
    v_all = v.astype(BF16)
    r_t = r * jnp.exp(linc)
    a_t = (-kn * jnp.exp(linc - lw)).astype(BF16)
    e_neg = jnp.exp(-linc)
    b_t = (b * e_neg).astype(BF16)
    k_t = (kd * e_neg).astype(BF16)
    e_rem = jnp.exp(ltot - linc)
    b_p = (b * e_rem).astype(BF16)
    k_p = (kd * e_rem).astype(BF16)
    dec = jnp.exp(ltot)

    chunks = range(CHUNKS_PER_TILE)
    sls = [slice(c * CHUNK, (c + 1) * CHUNK) for c in chunks]
    hd = lambda m, x, dims: _hdot(m, x, pmask, dims)
    ar = [jnp.concatenate([a_t[sl], r_t[sl].astype(BF16)], axis=0) for sl in sls]
    m_b = [hd(ar[c], b_t[sls[c]], _NT) for c in chunks]
    m_k = [hd(ar[c], k_t[sls[c]], _NT) for c in chunks]
    a_ab = [m_b[c][:CHUNK] * strict for c in chunks]
    p_rb = [(m_b[c][CHUNK:] * incl).astype(BF16) for c in chunks]
    ak_rk = [jnp.concatenate([m_k[c][:CHUNK] * strict, m_k[c][CHUNK:] * incl], axis=0).astype(BF16)
             for c in chunks]

    tinv = [eye + a_ab[c] for c in chunks]
    p = [a_ab[c].astype(BF16) for c in chunks]
    p = [hd(p[c], p[c], _NN).astype(BF16) for c in chunks]
    for _ in range(4):
        pt = [hd(jnp.concatenate([p[c], tinv[c].astype(BF16)], axis=0), p[c], _NN) for c in chunks]
        p = [pt[c][:CHUNK].astype(BF16) for c in chunks]
        tinv = [tinv[c] + pt[c][CHUNK:] for c in chunks]
    tinv = [(tinv[c] + hd(tinv[c].astype(BF16), p[c], _NN)).astype(BF16) for c in chunks]

    m_v = [hd(ak_rk[c], v_all[sls[c]], _NN) for c in chunks]
    w = [hd(tinv[c], a_t[sls[c]], _NN) for c in chunks]
    u0 = [hd(tinv[c], m_v[c][:CHUNK], _NN) for c in chunks]
    qp = [r_t[sls[c]] + hd(p_rb[c], w[c], _NN) for c in chunks]
    y0 = [hd(p_rb[c], u0[c], _NN) + m_v[c][CHUNK:] for c in chunks]
    pr = [slice(p * PAIR_W, (p + 1) * PAIR_W) for p in range(N_PAIRS)]
    pmask_f = pmask.astype(F32)
    bk = [jnp.concatenate([b_p[sl], k_p[sl]], axis=0) for sl in sls]
    uv = [jnp.concatenate([u0[c].astype(BF16), v_all[sls[c]]], axis=0) for c in chunks]
    wb = [[_dg(w[c][:, q].astype(BF16), b_p[sls[c]][:, q], _TN) * pmask_f for q in pr] for c in chunks]
    gs = [[_dg(uv[c][:, q], bk[c][:, q], _TN) * pmask_f for q in pr] for c in chunks]
    qpb = [qp[c].astype(BF16) for c in chunks]
    decs = [dec[c * CHUNK:c * CHUNK + 1, :] for c in chunks]
    return qpb, y0, wb, gs, decs


def _scan_chain_step(c, s, prep, y_ref):
    qpb, y0, wb, gs, decs = prep
    pr = [slice(p * PAIR_W, (p + 1) * PAIR_W) for p in range(N_PAIRS)]
    sb = [s[p].astype(BF16) for p in range(N_PAIRS)]
    y_ref[c * CHUNK:(c + 1) * CHUNK, :] = y0[c] + jnp.concatenate(
        [_dg(qpb[c][:, pr[p]], sb[p], _NT) for p in range(N_PAIRS)], axis=1)
    return [s[p] * decs[c][:, pr[p]] + (_dg(sb[p], wb[c][p].astype(BF16), _NN) + gs[c][p]) for p in range(N_PAIRS)]


N_STREAMS = 2
STREAM_TILES = N_TILES // N_STREAMS
STREAM_CTX = CTX_TILES // N_STREAMS
SCAN_LANES = tuple((rev, st) for rev in (False, True) for st in range(N_STREAMS))
N_LANE_IN = 7


def _stream_tile(stream, pos):
    lat0 = CTX_TILES + stream * (STREAM_TILES - STREAM_CTX)
    return jnp.where(pos < STREAM_CTX, stream * STREAM_CTX + pos, lat0 + pos - STREAM_CTX)


def _lane_pos(reverse, k):
    return (STREAM_TILES - 1 - k) if reverse else k


def _scan_kernel(*refs):
    n_lanes = len(SCAN_LANES)
    lane_in = [refs[n * N_LANE_IN:(n + 1) * N_LANE_IN] for n in range(n_lanes)]
    cs_ref, strict_ref, incl_ref, eye_ref, pmask_ref = refs[n_lanes * N_LANE_IN:n_lanes * N_LANE_IN + 5]
    outs = refs[n_lanes * N_LANE_IN + 5:n_lanes * N_LANE_IN + 5 + 2 * n_lanes]
    lane_out = [outs[2 * n:2 * n + 2] for n in range(n_lanes)]
    lane_scr = refs[n_lanes * N_LANE_IN + 5 + 2 * n_lanes:]

    k = pl.program_id(0)
    pmask_f = pmask_ref[...]
    pmask = pmask_f.astype(BF16)
    eye = eye_ref[...]

    def load_state(s0_ref, scr, is_ctx):
        for p in range(N_PAIRS):
            carried = jnp.concatenate([s0_ref[p * PAIR_W:(p + 1) * PAIR_W, :]] * 2, axis=1) * pmask_f
            scr[p] = jnp.where(is_ctx, 0.0, carried)

    for n, (reverse, stream) in enumerate(SCAN_LANES):
        pos = _lane_pos(reverse, k)
        j = (pos - STREAM_CTX) % LAT_TILES_PER_SEQ
        first = (pos < STREAM_CTX) | (j == (LAT_TILES_PER_SEQ - 1 if reverse else 0))
        pl.when(first)(functools.partial(load_state, lane_in[n][6], lane_scr[n], pos < STREAM_CTX))

    preps = []
    for n, (reverse, stream) in enumerate(SCAN_LANES):
        d = 1 if reverse else 0
        r_ref, v_ref, kn_ref, lw_ref, kd_ref, b_ref, _ = lane_in[n]
        preps.append(_scan_prepare(reverse, r_ref[...], v_ref[...], kn_ref[...], lw_ref[...], kd_ref[...], b_ref[...],
                                   cs_ref[d], strict_ref[d], incl_ref[d], eye, pmask))
    states = [[lane_scr[n][p] for p in range(N_PAIRS)] for n in range(n_lanes)]
    for c in range(CHUNKS_PER_TILE):
        for n, (reverse, stream) in enumerate(SCAN_LANES):
            cc = CHUNKS_PER_TILE - 1 - c if reverse else c
            states[n] = _scan_chain_step(cc, states[n], preps[n], lane_out[n][0])
    for n in range(n_lanes):
        for p in range(N_PAIRS):
            lane_scr[n][p] = states[n][p]
            for h in range(2):
                blk = slice(h * HEAD_DIM, (h + 1) * HEAD_DIM)
                lane_out[n][1][2 * p + h] = states[n][p][blk, blk]


def _scan_call(r, v, kn, lw, kd, b, s0, consts, l):
    cs, strict, incl, eye, pmask = consts
    lat_seq_of = lambda t: jnp.maximum(t - CTX_TILES, 0) // LAT_TILES_PER_SEQ
    const2 = lambda k: (0, 0)
    const3 = lambda k: (0, 0, 0)
    in_specs, out_specs, args = [], [], []
    for reverse, stream in SCAN_LANES:
        d = 1 if reverse else 0
        tile = functools.partial(lambda rv, st, k: _stream_tile(st, _lane_pos(rv, k)), reverse, stream)
        pos = functools.partial(_lane_pos, reverse)
        tok = pl.BlockSpec((TM, RWKV_W), functools.partial(lambda tl, k: (tl(k), 0), tile))
        tok_d = pl.BlockSpec((None, TM, RWKV_W), functools.partial(lambda tl, dd, k: (dd, tl(k), 0), tile, d))
        s0_spec = pl.BlockSpec((None, None, None, RWKV_W, HEAD_DIM),
                               functools.partial(lambda tl, dd, k: (lat_seq_of(tl(k)), l, dd, 0, 0), tile, d))
        in_specs += [tok, tok, tok, tok_d, tok_d, tok_d, s0_spec]
        args += [r, v, kn, lw, kd, b, s0]
        out_specs += [
            pl.BlockSpec((TM, RWKV_W), functools.partial(lambda ps, k: (ps(k), 0), pos)),
            pl.BlockSpec((None, RWKV_HEADS, HEAD_DIM, HEAD_DIM),
                         functools.partial(lambda ps, k: (jnp.minimum(ps(k), STREAM_CTX), 0, 0, 0), pos)),
        ]
    in_specs += [
        pl.BlockSpec((2, TM, TM), const3),
        pl.BlockSpec((2, CHUNK, RWKV_W), const3),
        pl.BlockSpec((2, CHUNK, RWKV_W), const3),
        pl.BlockSpec((CHUNK, RWKV_W), const2),
        pl.BlockSpec((PAIR_W, PAIR_W), const2),
    ]
    args += [cs, strict, incl, eye, pmask]
    y = jax.ShapeDtypeStruct((STREAM_TILES * TM, RWKV_W), F32)
    sfin = jax.ShapeDtypeStruct((STREAM_CTX + 1, RWKV_HEADS, HEAD_DIM, HEAD_DIM), F32)
    state = pltpu.VMEM((N_PAIRS, PAIR_W, PAIR_W), F32)
    outs = pl.pallas_call(
        _scan_kernel,
        grid=(STREAM_TILES,),
        in_specs=in_specs,
        out_specs=out_specs,
        out_shape=[y, sfin] * len(SCAN_LANES),
        scratch_shapes=[state] * len(SCAN_LANES),
        compiler_params=_cparams(("arbitrary",)),
        name="rwkv_scan",
    )(*args)
    return outs[0::2], outs[1::2]


def _scan_final_states(sfins_per_layer):
    return jnp.stack(
        [jnp.stack([jnp.concatenate([sfins[d * N_STREAMS + st][:STREAM_CTX] for st in range(N_STREAMS)], axis=0)
                    for d in range(2)], axis=1)
         for sfins in sfins_per_layer], axis=1)


def _scan_consts():
    t = np.arange(TM)
    same_chunk = (t[:, None] // CHUNK) == (t[None, :] // CHUNK)
    cs_f = same_chunk & (t[None, :] <= t[:, None])
    cs_b = same_chunk & (t[None, :] >= t[:, None])
    cs = np.stack([cs_f, cs_b]).astype(np.float32)
    q = np.arange(CHUNK)[:, None]
    kcol = np.arange(RWKV_W)[None, :] % CHUNK
    strict = np.stack([kcol < q, kcol > q]).astype(np.float32)
    incl = np.stack([kcol <= q, kcol >= q]).astype(np.float32)
    eye = (kcol == q).astype(np.float32)
    return (jnp.asarray(cs, BF16), jnp.asarray(strict), jnp.asarray(incl),
            jnp.asarray(eye), jnp.asarray(_head_block_mask(PAIR_W)))


FNET_TT = 512
FNET_IN_ROWS = N_CTX


def _fnet_kernel(seq_len, n_in, *refs):
    z_refs = refs[:n_in]
    ch_ref, cl_ref, th_ref, o_ref, z_scr = refs[n_in:]
    b = pl.program_id(1)

    @pl.when((pl.program_id(0) == 0) & (b == 0))
    def _():
        per_in = FNET_IN_ROWS // seq_len
        for n, z_ref in enumerate(z_refs):
            zh, zl = _split2(z_ref[...])
            for half in range(2):
                ch = ch_ref[half]
                cl = cl_ref[half]
                zc = (_dg(zh, ch, _NN) + (_dg(zh, cl, _NN) + _dg(zl, ch, _NN))).astype(BF16)
                for s in range(per_in):
                    z_scr[n * per_in + s, half * seq_len:(half + 1) * seq_len, :] = zc[s * seq_len:(s + 1) * seq_len]

    o_ref[...] = _dg(th_ref[...], z_scr[b], _NN)


def _fnet_call(u_f, seq_len, n_seq, row_off, tables, name):
    ch, cl, th = tables
    tt = min(FNET_TT, seq_len)
    nt = seq_len // tt
    n_in = n_seq * seq_len // FNET_IN_ROWS
    in_off = row_off // FNET_IN_ROWS
    z_specs = [pl.BlockSpec((FNET_IN_ROWS, FNET_W), functools.partial(lambda n, j, b: (in_off + n, 0), n))
               for n in range(n_in)]
    return pl.pallas_call(
        functools.partial(_fnet_kernel, seq_len, n_in),
        grid=(nt, n_seq),
        in_specs=z_specs + [
            pl.BlockSpec((2, FNET_W, FNET_W), lambda j, b: (0, 0, 0)),
            pl.BlockSpec((2, FNET_W, FNET_W), lambda j, b: (0, 0, 0)),
            pl.BlockSpec((tt, 2 * seq_len), lambda j, b: (j, 0)),
        ],
        out_specs=pl.BlockSpec((tt, FNET_W), lambda j, b: (b * nt + j, 0)),
        out_shape=jax.ShapeDtypeStruct((n_seq * seq_len, FNET_W), F32),
        scratch_shapes=[pltpu.VMEM((n_seq, 2 * seq_len, FNET_W), BF16)],
        compiler_params=_cparams(("arbitrary", "arbitrary")),
        name=name,
    )(*([u_f] * n_in), ch, cl, th)


DFT_SPLIT = 32


def _dft_tables(seq_len):
    c = np.arange(FNET_W)
    ang_c = 2.0 * np.pi * ((c[:, None] % HEAD_DIM) * (c[None, :] % HEAD_DIM) % HEAD_DIM) / HEAD_DIM
    scale = 1.0 / np.sqrt(float(seq_len) * HEAD_DIM)
    blk = _head_block_mask(FNET_W)
    ch, cl = _split2(jnp.asarray(np.stack([np.cos(ang_c) * blk * scale, np.sin(ang_c) * blk * scale]), F32))

    n1 = seq_len // DFT_SPLIT
    f = np.arange(seq_len)
    a1 = 2.0 * np.pi * ((np.arange(n1)[:, None] * f[None, :]) % n1) / n1
    a2 = 2.0 * np.pi * ((np.arange(DFT_SPLIT)[:, None] * f[None, :]) % seq_len) / seq_len
    c1, s1 = jnp.asarray(np.cos(a1), F32)[:, None, :], jnp.asarray(np.sin(a1), F32)[:, None, :]
    c2, s2 = jnp.asarray(np.cos(a2), F32)[None, :, :], jnp.asarray(np.sin(a2), F32)[None, :, :]
    cos_t = (c1 * c2 - s1 * s2).reshape(seq_len, seq_len)
    sin_t = (s1 * c2 + c1 * s2).reshape(seq_len, seq_len)
    th = jnp.concatenate([cos_t, -sin_t], axis=1).astype(BF16)
    return ch, cl, th


W_STAGE_ROWS = 64
N_MERGE_SCRATCH = 5


def _weight_chunk_copy(l, chunks, stage, sem, n):
    w_hbm, dst, r0 = chunks[n]
    return pltpu.make_async_copy(w_hbm.at[l, pl.ds(r0, W_STAGE_ROWS), :],
                                 stage.at[n % 2, :, 0:dst.shape[1]], sem.at[n % 2])


def _load_weights_bf16(l, srcs, dsts, stage, sem):
    chunks = [(w, dst, r0) for w, dst in zip(srcs, dsts) for r0 in range(0, dst.shape[0], W_STAGE_ROWS)]
    _weight_chunk_copy(l, chunks, stage, sem, 0).start()
    for n, (_, dst, r0) in enumerate(chunks):
        if n + 1 < len(chunks):
            _weight_chunk_copy(l, chunks, stage, sem, n + 1).start()
        _weight_chunk_copy(l, chunks, stage, sem, n).wait()
        dst[r0:r0 + W_STAGE_ROWS, :] = stage[n % 2, :, 0:dst.shape[1]].astype(BF16)


def _merge_ffn_kernel(split, final, l, *refs):
    wo_ref, wi_ref, w2_ref, stage, sem = refs[-N_MERGE_SCRATCH:]
    refs = refs[:-N_MERGE_SCRATCH]
    i = pl.program_id(0)
    n_x = 2 if split else 1
    (mod_ref, oac_ref, oal_ref, yf0_ref, yf1_ref, yb0_ref, yb1_ref, bonus_ref, g_ref, ofc_ref, ofl_ref,
     lng_ref, lnb_ref, avg_ref, nfg_ref, wo_hbm, wi_hbm, w2_hbm) = refs[n_x:n_x + 18]
    tail = refs[n_x + 18:]

    pl.when(i == 0)(functools.partial(_load_weights_bf16, l, (wo_hbm, wi_hbm, w2_hbm), (wo_ref, wi_ref, w2_ref),
                                      stage, sem))
    x_in = _tile_of_split(i, *refs[:2], FFN_TM) if split else refs[0][...]
    is_ctx = i < N_CTX // FFN_TM
    o_attn = _tile_of_split(i, oac_ref, oal_ref, FFN_TM)
    o_fnet = _tile_of_split(i, ofc_ref, ofl_ref, FFN_TM)
    avg = avg_ref[...]
    y = jnp.where(_in_stream0(i, FFN_TM), yf0_ref[...] + yb0_ref[...], yf1_ref[...] + yb1_ref[...])
    yc = y - _dot_exact_rhs(y, avg)
    var = _dot_exact_rhs(yc * yc, avg)
    yn = yc * lax.rsqrt(var + GN_EPS) * lng_ref[...] + lnb_ref[...]
    o_rwkv = (yn + bonus_ref[...]) * g_ref[...]

    o = (_dg(o_attn.astype(BF16), wo_ref[0:ATTN_W, :], _NN)
         + _dg(o_rwkv.astype(BF16), wo_ref[ATTN_W:ATTN_W + RWKV_W, :], _NN)
         + _dg(o_fnet.astype(BF16), wo_ref[ATTN_W + RWKV_W:, :], _NN))
    g1 = mod_ref[:, 2 * D_MODEL:3 * D_MODEL]
    sh2 = mod_ref[:, 3 * D_MODEL:4 * D_MODEL]
    sc2 = mod_ref[:, 4 * D_MODEL:5 * D_MODEL]
    g2 = mod_ref[:, 5 * D_MODEL:6 * D_MODEL]
    x = x_in + g1 * o
    h = _rms(x, nfg_ref[...]) * (1.0 + sc2) + sh2
    gu = _dg(h.astype(BF16), wi_ref[...], _NN)
    gt = gu[:, 0:D_FF]
    act = gt * jax.nn.sigmoid(gt) * gu[:, D_FF:]
    x = x + g2 * _dg(act.astype(BF16), w2_ref[...], _NN)
    if not final:
        (o_ref,) = tail
        o_ref[...] = x
    else:
        fng_ref, oc_ref, ol_ref = tail
        y_out = _rms(x, fng_ref[...])

        @pl.when(is_ctx)
        def _():
            oc_ref[...] = y_out

        @pl.when(jnp.logical_not(is_ctx))
        def _():
            ol_ref[...] = y_out


def _in_stream0(m, tm):
    ctx_m = N_CTX // tm
    lat_ps = (N_TOK // tm - ctx_m) // N_STREAMS
    return (m < ctx_m // N_STREAMS) | ((m >= ctx_m) & (m < ctx_m + lat_ps))


def _stream_block(stream, tm):
    ctx_m = N_CTX // tm
    half = ctx_m // N_STREAMS
    lat_ps = (N_TOK // tm - ctx_m) // N_STREAMS
    if stream == 0:
        return lambda m: (jnp.where(m < half, m, jnp.where(m < ctx_m, half - 1,
                                    jnp.where(m < ctx_m + lat_ps, m - half, half + lat_ps - 1))), 0)
    return lambda m: (jnp.where(m < half, 0, jnp.where(m < ctx_m, m - half,
                                jnp.where(m < ctx_m + lat_ps, half - 1, m - half - lat_ps))), 0)


def _merge_ffn_call(xs, mods_l, oa_ctx, oa_lat, ys, bonus, g, of_ctx, of_lat, ln_g, ln_b, avg, nfg,
                    wo, wi, w2, l, final_g=None):
    final = final_g is not None
    split = isinstance(xs, tuple)
    xs = xs if split else (xs,)
    row = lambda i: (l, 0, 0)
    tile = lambda i: (i, 0)
    ctx_tile, lat_tile = _ctx_tile(FFN_TM), _lat_tile(FFN_TM)
    in_specs = _x_specs(split, FFN_TM) + [
        pl.BlockSpec((None, 1, 6 * D_MODEL), lambda i: (_mod_row(i, FFN_TM), 0, 0)),
        pl.BlockSpec((FFN_TM, ATTN_W), ctx_tile),
        pl.BlockSpec((FFN_TM, ATTN_W), lat_tile),
        pl.BlockSpec((FFN_TM, RWKV_W), _stream_block(0, FFN_TM)),
        pl.BlockSpec((FFN_TM, RWKV_W), _stream_block(1, FFN_TM)),
        pl.BlockSpec((FFN_TM, RWKV_W), _stream_block(0, FFN_TM)),
        pl.BlockSpec((FFN_TM, RWKV_W), _stream_block(1, FFN_TM)),
        pl.BlockSpec((FFN_TM, RWKV_W), tile),
        pl.BlockSpec((FFN_TM, RWKV_W), tile),
        pl.BlockSpec((FFN_TM, FNET_W), ctx_tile),
        pl.BlockSpec((FFN_TM, FNET_W), lat_tile),
        pl.BlockSpec((None, 1, RWKV_W), row),
        pl.BlockSpec((None, 1, RWKV_W), row),
        pl.BlockSpec((RWKV_W, RWKV_W), lambda i: (0, 0)),
        pl.BlockSpec((None, 1, D_MODEL), row),
        pl.BlockSpec(memory_space=pl.ANY),
        pl.BlockSpec(memory_space=pl.ANY),
        pl.BlockSpec(memory_space=pl.ANY),
    ]
    args = [*xs, mods_l, oa_ctx, oa_lat, *ys, bonus, g, of_ctx, of_lat, ln_g, ln_b, avg, nfg, wo, wi, w2]
    if final:
        in_specs.append(pl.BlockSpec((1, D_MODEL), lambda i: (0, 0)))
        args.append(final_g)
        out_specs = [pl.BlockSpec((FFN_TM, D_MODEL), ctx_tile), pl.BlockSpec((FFN_TM, D_MODEL), lat_tile)]
        out_shape = [jax.ShapeDtypeStruct((N_CTX, D_MODEL), F32), jax.ShapeDtypeStruct((N_LAT, D_MODEL), F32)]
    else:
        out_specs = pl.BlockSpec((FFN_TM, D_MODEL), tile)
        out_shape = jax.ShapeDtypeStruct((N_TOK, D_MODEL), F32)
    return pl.pallas_call(
        functools.partial(_merge_ffn_kernel, split, final, l),
        grid=(N_TOK // FFN_TM,),
        in_specs=in_specs,
        out_specs=out_specs,
        out_shape=out_shape,
        scratch_shapes=[pltpu.VMEM((D_MODEL, D_MODEL), BF16), pltpu.VMEM((D_MODEL, 2 * D_FF), BF16),
                        pltpu.VMEM((D_FF, D_MODEL), BF16), pltpu.VMEM((2, W_STAGE_ROWS, 2 * D_FF), F32),
                        pltpu.SemaphoreType.DMA((2,))],
        compiler_params=_cparams(("arbitrary",)),
        name="merge_ffn_final" if final else "merge_ffn",
    )(*args)


def _block_diag2(w):
    z = jnp.zeros_like(w[:, 0])
    return jnp.concatenate([jnp.concatenate([w[:, 0], z], axis=2), jnp.concatenate([z, w[:, 1]], axis=2)], axis=1)


def kernel(x_prompt, x_sample, cache_k, cache_v, state_rwkv, c, c_ctx, w_ada, b_ada, norm_mix_g, norm_ffn_g,
           w_in, w_out, attn_sink, rwkv_shift_w, rwkv_w0, rwkv_w_up, rwkv_a0, rwkv_a_up, rwkv_g_up,
           rwkv_k_k, rwkv_k_a, rwkv_r_k, rwkv_ln_g, rwkv_ln_b, ffn_w_in, ffn_w_out, norm_final_g):
    x = (x_prompt.reshape(N_CTX, D_MODEL), x_sample.reshape(N_LAT, D_MODEL))
    cvec = jnp.concatenate([c_ctx[None, :], c, jnp.zeros((8 - 1 - N_LAT_SEQ, D_MODEL), F32)], axis=0)
    mods = _mods_call(cvec, w_ada, b_ada).reshape(DEPTH, 8, 1, 6 * D_MODEL)

    w_in_b = w_in.astype(BF16)
    wup_bd = _block_diag2(rwkv_w_up)
    aup_bd = _block_diag2(rwkv_a_up)
    row3 = lambda a: a.reshape(DEPTH, 1, -1)
    nmg, nfg = row3(norm_mix_g), row3(norm_ffn_g)
    k_k, k_a, r_k = row3(rwkv_k_k), row3(rwkv_k_a), row3(rwkv_r_k)
    ln_g, ln_b = row3(rwkv_ln_g), row3(rwkv_ln_b)
    kc = cache_k.reshape(N_LAT_SEQ, DEPTH, PAST_LEN, KV_W)
    vc = cache_v.reshape(N_LAT_SEQ, DEPTH, PAST_LEN, KV_W)
    s0 = state_rwkv.reshape(N_LAT_SEQ, DEPTH, 2, RWKV_W, HEAD_DIM)

    cos, sin = _rope_tables()
    consts = _scan_consts()
    bmask = jnp.asarray(_head_block_mask(RWKV_W))
    ones_bd = bmask.astype(BF16)
    avg_bd = (bmask * (1.0 / HEAD_DIM)).astype(BF16)
    rwkv_params = (rwkv_shift_w, wup_bd, aup_bd, rwkv_g_up, rwkv_w0, rwkv_a0, k_k, k_a, r_k, ones_bd)
    dft_ctx = _dft_tables(CTX_LEN)
    dft_lat = _dft_tables(LAT_LEN)

    new_k, new_v, new_s = [], [], []
    for l in range(DEPTH):
        q, k, v, u_f, r, vv, kn, g, bonus, lw, kd, b = _inproj_call(x, mods[l], nmg, w_in_b, rwkv_params, l)
        new_k.append(k[:N_CTX].reshape(N_CTX_SEQ, CTX_LEN, KV_W))
        new_v.append(v[:N_CTX].reshape(N_CTX_SEQ, CTX_LEN, KV_W))

        oa_ctx = _ctx_attn_call(q, k, v, attn_sink, l)
        oa_lat = _lat_attn_call(q, k, v, kc, vc, cos, sin, attn_sink, l)

        ys, sfins = _scan_call(r, vv, kn, lw, kd, b, s0, consts, l)
        new_s.append(sfins)

        of_ctx = _fnet_call(u_f, CTX_LEN, N_CTX_SEQ, 0, dft_ctx, "fnet_context")
        of_lat = _fnet_call(u_f, LAT_LEN, N_LAT_SEQ, N_CTX, dft_lat, "fnet_latent")

        last = l == DEPTH - 1
        x = _merge_ffn_call(x, mods[l], oa_ctx, oa_lat, ys, bonus, g, of_ctx, of_lat, ln_g, ln_b, avg_bd, nfg,
                            w_out, ffn_w_in, ffn_w_out, l,
                            final_g=norm_final_g.reshape(1, D_MODEL) if last else None)

    y_ctx, y_lat = x
    return (y_ctx.reshape(N_CTX_SEQ, CTX_LEN, D_MODEL), y_lat.reshape(N_LAT_SEQ, LAT_LEN, D_MODEL),
            jnp.stack(new_k, axis=1).reshape(N_CTX_SEQ, DEPTH, CTX_LEN, N_KV_HEADS, HEAD_DIM),
            jnp.stack(new_v, axis=1).reshape(N_CTX_SEQ, DEPTH, CTX_LEN, N_KV_HEADS, HEAD_DIM),
            _scan_final_states(new_s))
```

```python
import functools

import numpy as np
import jax
import jax.numpy as jnp
from jax import lax
from jax.experimental import pallas as pl
from jax.experimental.pallas import tpu as pltpu

F32 = jnp.float32
BF16 = jnp.bfloat16

D_MODEL = 1024
N_CTX_SEQ = 16
CTX_LEN = 256
DEPTH = 4
N_LAT_SEQ = 4
LAT_LEN = 2048
PAST_LEN = 512
GRID_W = 64
HEAD_DIM = 64
N_Q_HEADS = 8
N_KV_HEADS = 2
Q_PER_KV = N_Q_HEADS // N_KV_HEADS
ATTN_W = N_Q_HEADS * HEAD_DIM
KV_W = N_KV_HEADS * HEAD_DIM
BLOCK = 128
ATTN_SCALE = 0.125
ROPE_THETA = 10000.0
NEG = -1e30
RWKV_HEADS = 4
RWKV_W = RWKV_HEADS * HEAD_DIM
DECAY_RANK = 64
ICLR_RANK = 64
GATE_RANK = 128
DECAY_SCALE = 0.6065306597126334
GN_EPS = 64e-5
RWKV_IN_W = 3 * RWKV_W + GATE_RANK + 2 * DECAY_RANK + 2 * ICLR_RANK
FNET_W = 4 * HEAD_DIM
IN_W = ATTN_W + 2 * KV_W + RWKV_IN_W + FNET_W
D_FF = 2816
RMS_EPS = 1e-6

N_CTX = N_CTX_SEQ * CTX_LEN
N_LAT = N_LAT_SEQ * LAT_LEN
N_TOK = N_CTX + N_LAT
TM = 256
N_TILES = N_TOK // TM
CTX_TILES = N_CTX // TM
LAT_TILES_PER_SEQ = LAT_LEN // TM
N_SEQ = N_CTX_SEQ + N_LAT_SEQ
CHUNK = 64
CHUNKS_PER_TILE = TM // CHUNK
HALO = 8
FFN_TM = 512
INPROJ_TM = 512
LAT_ATTN_QBLOCKS = 4
LAT_ATTN_GROUP = 8

VMEM_LIMIT = 56 * 1024 * 1024


def _cparams(sem):
    return pltpu.CompilerParams(dimension_semantics=sem, vmem_limit_bytes=VMEM_LIMIT)


def _mod_row(i, tm=TM):
    return jnp.where(i < N_CTX // tm, 0, 1 + (i - N_CTX // tm) // (LAT_LEN // tm))


def _split2(x):
    hi = x.astype(BF16)
    lo = (x - hi.astype(F32)).astype(BF16)
    return hi, lo


_NN = (((1,), (0,)), ((), ()))
_NT = (((1,), (1,)), ((), ()))


def _dg(a, b, dims):
    return lax.dot_general(a, b, dims, preferred_element_type=F32)


def _dot1(a, b, dims=_NN):
    return _dg(a.astype(BF16), b.astype(BF16), dims)


def _dot3(a, b, dims=_NN):
    ah, al = _split2(a)
    bh, bl = _split2(b)
    return _dg(ah, bh, dims) + (_dg(ah, bl, dims) + _dg(al, bh, dims))


def _dot_exact_rhs(a, b_bf16):
    ah, al = _split2(a)
    return _dg(ah, b_bf16, _NN) + _dg(al, b_bf16, _NN)


def _dot_exact_lhs(a_bf16, b):
    bh, bl = _split2(b)
    return _dg(a_bf16, bh, _NN) + _dg(a_bf16, bl, _NN)


def _mods_kernel(c_ref, w_ref, b_ref, o_ref):
    c = c_ref[...]
    s = c * jax.nn.sigmoid(c)
    o_ref[...] = _dot1(s, w_ref[...]) + b_ref[...]


def _mods_call(cvec, w_ada, b_ada):
    tn = 1536
    return pl.pallas_call(
        _mods_kernel,
        grid=(DEPTH, 6 * D_MODEL // tn),
        in_specs=[
            pl.BlockSpec((8, D_MODEL), lambda l, j: (0, 0)),
            pl.BlockSpec((None, D_MODEL, tn), lambda l, j: (l, 0, j)),
            pl.BlockSpec((None, 1, tn), lambda l, j: (l, 0, j)),
        ],
        out_specs=pl.BlockSpec((None, 8, tn), lambda l, j: (l, 0, j)),
        out_shape=jax.ShapeDtypeStruct((DEPTH, 8, 6 * D_MODEL), F32),
        compiler_params=_cparams(("parallel", "parallel")),
        name="adaln_mods",
    )(cvec, w_ada, b_ada.reshape(DEPTH, 1, 6 * D_MODEL))


def _rms(x, g):
    return x * lax.rsqrt(jnp.mean(x * x, axis=-1, keepdims=True) + RMS_EPS) * g


def _tile_of_split(i, ctx_ref, lat_ref, tm=TM):
    return jnp.where(i < N_CTX // tm, ctx_ref[...], lat_ref[...])


def _ctx_tile(tm=TM):
    return lambda i: (jnp.minimum(i, N_CTX // tm - 1), 0)


def _lat_tile(tm=TM):
    return lambda i: (jnp.maximum(i - N_CTX // tm, 0), 0)


def _x_specs(split, tm=TM):
    if split:
        return [pl.BlockSpec((tm, D_MODEL), _ctx_tile(tm)), pl.BlockSpec((tm, D_MODEL), _lat_tile(tm))]
    return [pl.BlockSpec((tm, D_MODEL), lambda i: (i, 0))]


def _head_block_mask(n):
    r = np.arange(n)[:, None] // HEAD_DIM
    c = np.arange(n)[None, :] // HEAD_DIM
    return (r == c).astype(np.float32)


def _rwkv_token_terms(u, prev_row, next_row, seam, sw_ref, wup_ref, aup_ref, gup_ref, w0_ref, a0_ref,
                      kk_ref, ka_ref, rk_ref, ones_ref, r_out, v_out, kn_out, g_out, bonus_out, lw_out, kd_out, b_out):
    n = u.shape[0]
    row = lax.broadcasted_iota(jnp.int32, u.shape, 0)
    u_dn = jnp.where(row == 0, prev_row, pltpu.roll(u, 1, 0))
    u_up = jnp.where(row == n - 1, next_row, pltpu.roll(u, n - 1, 0))
    u_dn = jnp.where(seam & (row == n // 2), 0.0, u_dn)
    u_up = jnp.where(seam & (row == n // 2 - 1), 0.0, u_up)
    us = u_dn * sw_ref[0:1, :] + u * sw_ref[1:2, :] + u_up * sw_ref[2:3, :]

    r = us[:, 0:RWKV_W]
    k = us[:, RWKV_W:2 * RWKV_W]
    v = us[:, 2 * RWKV_W:3 * RWKV_W]
    o = 3 * RWKV_W
    gd = us[:, o:o + GATE_RANK]
    wd = us[:, o + GATE_RANK:o + GATE_RANK + 2 * DECAY_RANK]
    ad = us[:, o + GATE_RANK + 2 * DECAY_RANK:RWKV_IN_W]

    w_pre = _dot1(jnp.tanh(wd), wup_ref[...])
    a_pre = _dot1(ad, aup_ref[...])
    g_out[...] = _dot1(jax.nn.sigmoid(gd), gup_ref[...])

    ones_bd = ones_ref[...]
    kn = k * kk_ref[...]
    kn = kn * lax.rsqrt(_dot_exact_rhs(kn * kn, ones_bd) + 1e-12)
    r_out[...] = r
    v_out[...] = v
    kn_out[...] = kn
    bonus_out[...] = _dot_exact_rhs(r * k * rk_ref[...], ones_bd) * v
    ka = ka_ref[...]
    for d in range(2):
        sl = slice(d * RWKV_W, (d + 1) * RWKV_W)
        a_d = jax.nn.sigmoid(a0_ref[d:d + 1, :] + a_pre[:, sl])
        lw_out[d] = -DECAY_SCALE * jax.nn.sigmoid(w0_ref[d:d + 1, :] + w_pre[:, sl])
        kd_out[d] = k * (1.0 + (a_d - 1.0) * ka)
        b_out[d] = kn * a_d


RWKV_COL0 = ATTN_W + 2 * KV_W
N_RWKV_PARAMS = 10


def _inproj_kernel(split, *refs):
    n_x = 2 if split else 1
    i = pl.program_id(0)
    x = _tile_of_split(i, *refs[:2], INPROJ_TM) if split else refs[0][...]
    xp_ref, xn_ref, mod_ref, g_ref, w_ref = refs[n_x:n_x + 5]
    rwkv_params = refs[n_x + 5:n_x + 5 + N_RWKV_PARAMS]
    q_ref, k_ref, v_ref, uf_ref = refs[n_x + 5 + N_RWKV_PARAMS:n_x + 9 + N_RWKV_PARAMS]
    rwkv_outs = refs[n_x + 9 + N_RWKV_PARAMS:]

    shift = mod_ref[:, 0:D_MODEL]
    scale = mod_ref[:, D_MODEL:2 * D_MODEL]
    x_all = jnp.concatenate([x, xp_ref[...], xn_ref[...]], axis=0)
    h = (_rms(x_all, g_ref[...]) * (1.0 + scale) + shift).astype(BF16)
    u_all = _dg(h, w_ref[:, RWKV_COL0:RWKV_COL0 + RWKV_IN_W], _NN)
    o_qkv = _dg(h[:INPROJ_TM], w_ref[:, 0:RWKV_COL0], _NN)
    q_ref[...] = o_qkv[:, 0:ATTN_W]
    k_ref[...] = o_qkv[:, ATTN_W:ATTN_W + KV_W]
    v_ref[...] = o_qkv[:, ATTN_W + KV_W:RWKV_COL0]
    uf_ref[...] = _dg(h[:INPROJ_TM], w_ref[:, RWKV_COL0 + RWKV_IN_W:IN_W], _NN)

    ctx_tiles = N_CTX // INPROJ_TM
    per_seq = LAT_LEN // INPROJ_TM
    is_lat = i >= ctx_tiles
    j = (i - ctx_tiles) % per_seq
    has_prev = jnp.where(is_lat & (j != 0), 1.0, 0.0)
    has_next = jnp.where(is_lat & (j != per_seq - 1), 1.0, 0.0)
    prev_row = u_all[INPROJ_TM + HALO - 1:INPROJ_TM + HALO, :] * has_prev
    next_row = u_all[INPROJ_TM + HALO:INPROJ_TM + HALO + 1, :] * has_next
    seam = jnp.logical_not(is_lat) if INPROJ_TM == 2 * CTX_LEN else False
    _rwkv_token_terms(u_all[:INPROJ_TM], prev_row, next_row, seam, *rwkv_params, *rwkv_outs)


def _inproj_call(xs, mods_l, g, w_bf16, rwkv_params, l):
    split = isinstance(xs, tuple)
    xs = xs if split else (xs,)
    hb = INPROJ_TM // HALO
    ctx_tiles = N_CTX // INPROJ_TM
    if split:
        halo_src = xs[1]
        last = N_LAT // HALO - 1
        prev_blk = lambda i: (jnp.clip((i - ctx_tiles) * hb - 1, 0, last), 0)
        next_blk = lambda i: (jnp.clip((i - ctx_tiles + 1) * hb, 0, last), 0)
    else:
        halo_src = xs[0]
        last = N_TOK // HALO - 1
        prev_blk = lambda i: (jnp.maximum(i * hb - 1, 0), 0)
        next_blk = lambda i: (jnp.minimum((i + 1) * hb, last), 0)
    row = lambda i: (l, 0, 0)
    widths = (ATTN_W, KV_W, KV_W, FNET_W)
    tok = jax.ShapeDtypeStruct((N_TOK, RWKV_W), F32)
    tok2 = jax.ShapeDtypeStruct((2, N_TOK, RWKV_W), F32)
    return pl.pallas_call(
        functools.partial(_inproj_kernel, split),
        grid=(N_TOK // INPROJ_TM,),
        in_specs=_x_specs(split, INPROJ_TM) + [
            pl.BlockSpec((HALO, D_MODEL), prev_blk),
            pl.BlockSpec((HALO, D_MODEL), next_blk),
            pl.BlockSpec((None, 1, 6 * D_MODEL), lambda i: (_mod_row(i, INPROJ_TM), 0, 0)),
            pl.BlockSpec((None, 1, D_MODEL), row),
            pl.BlockSpec((None, D_MODEL, IN_W), row),
            pl.BlockSpec((None, 3, RWKV_IN_W), row),
            pl.BlockSpec((None, 2 * DECAY_RANK, 2 * RWKV_W), row),
            pl.BlockSpec((None, 2 * ICLR_RANK, 2 * RWKV_W), row),
            pl.BlockSpec((None, GATE_RANK, RWKV_W), row),
            pl.BlockSpec((None, 2, RWKV_W), row),
            pl.BlockSpec((None, 2, RWKV_W), row),
            pl.BlockSpec((None, 1, RWKV_W), row),
            pl.BlockSpec((None, 1, RWKV_W), row),
            pl.BlockSpec((None, 1, RWKV_W), row),
            pl.BlockSpec((RWKV_W, RWKV_W), lambda i: (0, 0)),
        ],
        out_specs=[pl.BlockSpec((INPROJ_TM, w), lambda i: (i, 0)) for w in widths]
        + [pl.BlockSpec((INPROJ_TM, RWKV_W), lambda i: (i, 0))] * 5
        + [pl.BlockSpec((2, INPROJ_TM, RWKV_W), lambda i: (0, i, 0))] * 3,
        out_shape=[jax.ShapeDtypeStruct((N_TOK, w), F32) for w in widths] + [tok] * 5 + [tok2] * 3,
        compiler_params=_cparams(("arbitrary",)),
        name="in_projection",
    )(*xs, halo_src, halo_src, mods_l, g, w_bf16, *rwkv_params)


def _softmax_pv(s, sink_col, v):
    m = jnp.maximum(jnp.max(s, axis=-1, keepdims=True), sink_col)
    e = jnp.exp(s - m)
    denom = jnp.sum(e, axis=-1, keepdims=True) + jnp.exp(sink_col - m)
    return _dot1(e, v) / denom


def _stack_q_heads(q, kh, rows):
    return jnp.concatenate(
        [q[:, (kh * Q_PER_KV + g) * HEAD_DIM:(kh * Q_PER_KV + g + 1) * HEAD_DIM] for g in range(Q_PER_KV)],
        axis=0)


def _sink_col(sink_ref, l, kh, rows):
    return jnp.concatenate(
        [jnp.full((rows, 1), sink_ref[l, kh * Q_PER_KV + g], F32) for g in range(Q_PER_KV)], axis=0)


def _ctx_attn_kernel(l, sink_ref, q_ref, k_ref, v_ref, o_ref):
    q = q_ref[...]
    k = k_ref[...]
    v = v_ref[...]
    half = CTX_LEN // 2
    pieces = [(h, r) for h in range(N_Q_HEADS) for r in range(2)]
    kb = k.astype(BF16)
    vb = v.astype(BF16)
    kv = lambda a, h: a[:, (h // Q_PER_KV) * HEAD_DIM:(h // Q_PER_KV + 1) * HEAD_DIM]
    qs = [(q[r * half:(r + 1) * half, h * HEAD_DIM:(h + 1) * HEAD_DIM] * ATTN_SCALE).astype(BF16) for h, r in pieces]
    s = [_dg(qs[n], kv(kb, h), _NT) for n, (h, r) in enumerate(pieces)]
    m = [jnp.maximum(jnp.max(s[n], axis=-1, keepdims=True), sink_ref[l, h]) for n, (h, r) in enumerate(pieces)]
    e = [jnp.exp(s[n] - m[n]) for n in range(len(pieces))]
    den = [jnp.sum(e[n], axis=-1, keepdims=True) + jnp.exp(sink_ref[l, h] - m[n]) for n, (h, r) in enumerate(pieces)]
    o = [_dg(e[n].astype(BF16), kv(vb, h), _NN) / den[n] for n, (h, r) in enumerate(pieces)]
    for n, (h, r) in enumerate(pieces):
        o_ref[r * half:(r + 1) * half, h * HEAD_DIM:(h + 1) * HEAD_DIM] = o[n]


def _ctx_attn_call(q, k, v, sink, l):
    return pl.pallas_call(
        functools.partial(_ctx_attn_kernel, l),
        grid=(N_CTX_SEQ,),
        in_specs=[
            pl.BlockSpec(memory_space=pltpu.SMEM),
            pl.BlockSpec((CTX_LEN, ATTN_W), lambda b: (b, 0)),
            pl.BlockSpec((CTX_LEN, KV_W), lambda b: (b, 0)),
            pl.BlockSpec((CTX_LEN, KV_W), lambda b: (b, 0)),
        ],
        out_specs=pl.BlockSpec((CTX_LEN, ATTN_W), lambda b: (b, 0)),
        out_shape=jax.ShapeDtypeStruct((N_CTX, ATTN_W), F32),
        compiler_params=_cparams(("parallel",)),
        name="context_attention",
    )(sink, q, k, v)


def _rope(x, cos, sin):
    lane = lax.broadcasted_iota(jnp.int32, x.shape, 1)
    partner = jnp.where((lane & 16) == 0, pltpu.roll(x, 128 - 16, 1), pltpu.roll(x, 16, 1))
    return x * cos + partner * sin


def _lat_attn_kernel(l, sink_ref, q_ref, k_ref, v_ref, kc_ref, vc_ref, cos_ref, sin_ref, o_ref):
    nq = LAT_ATTN_QBLOCKS
    nb = LAT_LEN // BLOCK
    n0 = pl.program_id(1) * nq
    r0 = pl.multiple_of(n0 * BLOCK, BLOCK)
    cos_q = cos_ref[pl.ds(r0, nq * BLOCK), :]
    sin_q = sin_ref[pl.ds(r0, nq * BLOCK), :]
    q = jnp.concatenate(
        [_rope(q_ref[:, j * 128:(j + 1) * 128], cos_q, sin_q) for j in range(ATTN_W // 128)], axis=1)

    kr, vr = [], []
    for dj in range(-1, nq + 1):
        rj = pl.multiple_of(jnp.clip(n0 + dj, 0, nb - 1) * BLOCK, BLOCK)
        kr.append(_rope(k_ref[pl.ds(rj, BLOCK), :], cos_ref[pl.ds(rj, BLOCK), :],
                        sin_ref[pl.ds(rj, BLOCK), :]).astype(BF16))
        vr.append(v_ref[pl.ds(rj, BLOCK), :].astype(BF16))
    kc = kc_ref[...].astype(BF16)
    vc = vc_ref[...].astype(BF16)
    k_all = [jnp.concatenate(kr[t:t + 3] + [kc], axis=0) for t in range(nq)]
    v_all = [jnp.concatenate(vr[t:t + 3] + [vc], axis=0) for t in range(nq)]

    kj = lax.broadcasted_iota(jnp.int32, (BLOCK, BLOCK), 1)
    qi = lax.broadcasted_iota(jnp.int32, (BLOCK, BLOCK), 0)
    keep_prev = [(kj >= qi) & (n0 + t > 0) for t in range(nq)]
    keep_next = [(kj <= qi) & (n0 + t < nb - 1) for t in range(nq)]
    kv_cols = lambda h: slice((h // Q_PER_KV) * HEAD_DIM, (h // Q_PER_KV + 1) * HEAD_DIM)

    all_pieces = [(t, h) for t in range(nq) for h in range(N_Q_HEADS)]
    for g0 in range(0, len(all_pieces), LAT_ATTN_GROUP):
        pieces = all_pieces[g0:g0 + LAT_ATTN_GROUP]
        rng = range(len(pieces))
        qs = [(q[t * BLOCK:(t + 1) * BLOCK, h * HEAD_DIM:(h + 1) * HEAD_DIM] * ATTN_SCALE).astype(BF16)
              for t, h in pieces]
        s = [_dg(qs[i], k_all[t][:, kv_cols(h)], _NT) for i, (t, h) in enumerate(pieces)]
        s = [jnp.concatenate([jnp.where(keep_prev[t], s[i][:, 0:BLOCK], NEG), s[i][:, BLOCK:2 * BLOCK],
                              jnp.where(keep_next[t], s[i][:, 2 * BLOCK:3 * BLOCK], NEG), s[i][:, 3 * BLOCK:]],
                             axis=1) for i, (t, h) in enumerate(pieces)]
        m = [jnp.maximum(jnp.max(s[i], axis=-1, keepdims=True), sink_ref[l, h]) for i, (t, h) in enumerate(pieces)]
        e = [jnp.exp(s[i] - m[i]) for i in rng]
        den = [jnp.sum(e[i], axis=-1, keepdims=True) + jnp.exp(sink_ref[l, h] - m[i])
               for i, (t, h) in enumerate(pieces)]
        o = [_dg(e[i].astype(BF16), v_all[t][:, kv_cols(h)], _NN) / den[i] for i, (t, h) in enumerate(pieces)]
        for i, (t, h) in enumerate(pieces):
            o_ref[t * BLOCK:(t + 1) * BLOCK, h * HEAD_DIM:(h + 1) * HEAD_DIM] = o[i]


def _lat_attn_call(q, k, v, kc, vc, cos, sin, sink, l):
    rows = LAT_ATTN_QBLOCKS * BLOCK
    steps = LAT_LEN // rows
    q_off = N_CTX // rows
    kv_off = N_CTX // LAT_LEN
    return pl.pallas_call(
        functools.partial(_lat_attn_kernel, l),
        grid=(N_LAT_SEQ, steps),
        in_specs=[
            pl.BlockSpec(memory_space=pltpu.SMEM),
            pl.BlockSpec((rows, ATTN_W), lambda b, n: (q_off + b * steps + n, 0)),
            pl.BlockSpec((LAT_LEN, KV_W), lambda b, n: (kv_off + b, 0)),
            pl.BlockSpec((LAT_LEN, KV_W), lambda b, n: (kv_off + b, 0)),
            pl.BlockSpec((None, None, PAST_LEN, KV_W), lambda b, n: (b, l, 0, 0)),
            pl.BlockSpec((None, None, PAST_LEN, KV_W), lambda b, n: (b, l, 0, 0)),
            pl.BlockSpec((LAT_LEN, 128), lambda b, n: (0, 0)),
            pl.BlockSpec((LAT_LEN, 128), lambda b, n: (0, 0)),
        ],
        out_specs=pl.BlockSpec((rows, ATTN_W), lambda b, n: (b * steps + n, 0)),
        out_shape=jax.ShapeDtypeStruct((N_LAT, ATTN_W), F32),
        compiler_params=_cparams(("parallel", "arbitrary")),
        name="latent_attention",
    )(sink, q, k, v, kc, vc, cos, sin)


def _rope_tables():
    t = np.arange(LAT_LEN)
    lane = np.arange(128)
    d = lane % HEAD_DIM
    pos = np.where(d[None, :] < HEAD_DIM // 2, (t // GRID_W)[:, None], (t % GRID_W)[:, None]).astype(np.float32)
    quarter = HEAD_DIM // 4
    inv = (np.float32(ROPE_THETA) ** (-(np.arange(quarter, dtype=np.float32)) / np.float32(quarter))).astype(np.float32)
    ang = pos * inv[d % quarter][None, :]
    sign = np.where((lane & 16) == 0, -1.0, 1.0)[None, :]
    return jnp.asarray(np.cos(ang), F32), jnp.asarray(np.sin(ang) * sign, F32)


PAIR_W = 2 * HEAD_DIM
N_PAIRS = RWKV_HEADS // 2
_TN = (((0,), (0,)), ((), ()))


def _pair_stack(x, pmask):
    return jnp.concatenate([x.astype(BF16)] * 2, axis=0) * pmask


def _hdot(m, x, pmask, dims):
    return jnp.concatenate(
        [_dg(m[:, p * PAIR_W:(p + 1) * PAIR_W], _pair_stack(x[:, p * PAIR_W:(p + 1) * PAIR_W], pmask), dims)
         for p in range(N_PAIRS)], axis=1)


def _scan_prepare(reverse, r, v, kn, lw, kd, b, cs, strict, incl, eye, pmask):
    linc = _dot_exact_lhs(cs, lw)
    end = 0 if reverse else CHUNK - 1
    ltot = jnp.concatenate(
        [jnp.broadcast_to(linc[c * CHUNK + end:c * CHUNK + end + 1, :], (CHUNK, RWKV_W))
         for c in range(CHUNKS_PER_TILE)], axis=0)
    v_all = v.astype(BF16)
    r_t = r * jnp.exp(linc)
    a_t = (-kn * jnp.exp(linc - lw)).astype(BF16)
    e_neg = jnp.exp(-linc)
    b_t = (b * e_neg).astype(BF16)
    k_t = (kd * e_neg).astype(BF16)
    e_rem = jnp.exp(ltot - linc)
    b_p = (b * e_rem).astype(BF16)
    k_p = (kd * e_rem).astype(BF16)
    dec = jnp.exp(ltot)

    chunks = range(CHUNKS_PER_TILE)
    sls = [slice(c * CHUNK, (c + 1) * CHUNK) for c in chunks]
    hd = lambda m, x, dims: _hdot(m, x, pmask, dims)
    ar = [jnp.concatenate([a_t[sl], r_t[sl].astype(BF16)], axis=0) for sl in sls]
    m_b = [hd(ar[c], b_t[sls[c]], _NT) for c in chunks]
    m_k = [hd(ar[c], k_t[sls[c]], _NT) for c in chunks]
    a_ab = [m_b[c][:CHUNK] * strict for c in chunks]
    p_rb = [(m_b[c][CHUNK:] * incl).astype(BF16) for c in chunks]
    ak_rk = [jnp.concatenate([m_k[c][:CHUNK] * strict, m_k[c][CHUNK:] * incl], axis=0).astype(BF16)
             for c in chunks]

    tinv = [eye + a_ab[c] for c in chunks]
    p = [a_ab[c].astype(BF16) for c in chunks]
    p = [hd(p[c], p[c], _NN).astype(BF16) for c in chunks]
    for _ in range(4):
        pt = [hd(jnp.concatenate([p[c], tinv[c].astype(BF16)], axis=0), p[c], _NN) for c in chunks]
        p = [pt[c][:CHUNK].astype(BF16) for c in chunks]
        tinv = [tinv[c] + pt[c][CHUNK:] for c in chunks]
    tinv = [(tinv[c] + hd(tinv[c].astype(BF16), p[c], _NN)).astype(BF16) for c in chunks]

    m_v = [hd(ak_rk[c], v_all[sls[c]], _NN) for c in chunks]
    w = [hd(tinv[c], a_t[sls[c]], _NN) for c in chunks]
    u0 = [hd(tinv[c], m_v[c][:CHUNK], _NN) for c in chunks]
    qp = [r_t[sls[c]] + hd(p_rb[c], w[c], _NN) for c in chunks]
    y0 = [hd(p_rb[c], u0[c], _NN) + m_v[c][CHUNK:] for c in chunks]
    pr = [slice(p * PAIR_W, (p + 1) * PAIR_W) for p in range(N_PAIRS)]
    pmask_f = pmask.astype(F32)
    bk = [jnp.concatenate([b_p[sl], k_p[sl]], axis=0) for sl in sls]
    uv = [jnp.concatenate([u0[c].astype(BF16), v_all[sls[c]]], axis=0) for c in chunks]
    wb = [[_dg(w[c][:, q].astype(BF16), b_p[sls[c]][:, q], _TN) * pmask_f for q in pr] for c in chunks]
    gs = [[_dg(uv[c][:, q], bk[c][:, q], _TN) * pmask_f for q in pr] for c in chunks]
    qpb = [qp[c].astype(BF16) for c in chunks]
    decs = [dec[c * CHUNK:c * CHUNK + 1, :] for c in chunks]
    return qpb, y0, wb, gs, decs


def _scan_chain_step(c, s, prep, y_ref):
    qpb, y0, wb, gs, decs = prep
    pr = [slice(p * PAIR_W, (p + 1) * PAIR_W) for p in range(N_PAIRS)]
    sb = [s[p].astype(BF16) for p in range(N_PAIRS)]
    y_ref[c * CHUNK:(c + 1) * CHUNK, :] = y0[c] + jnp.concatenate(
        [_dg(qpb[c][:, pr[p]], sb[p], _NT) for p in range(N_PAIRS)], axis=1)
    return [s[p] * decs[c][:, pr[p]] + (_dg(sb[p], wb[c][p].astype(BF16), _NN) + gs[c][p]) for p in range(N_PAIRS)]


N_STREAMS = 2
STREAM_TILES = N_TILES // N_STREAMS
STREAM_CTX = CTX_TILES // N_STREAMS
SCAN_LANES = tuple((rev, st) for rev in (False, True) for st in range(N_STREAMS))
N_LANE_IN = 7


def _stream_tile(stream, pos):
    lat0 = CTX_TILES + stream * (STREAM_TILES - STREAM_CTX)
    return jnp.where(pos < STREAM_CTX, stream * STREAM_CTX + pos, lat0 + pos - STREAM_CTX)


def _lane_pos(reverse, k):
    return (STREAM_TILES - 1 - k) if reverse else k


def _scan_kernel(*refs):
    n_lanes = len(SCAN_LANES)
    lane_in = [refs[n * N_LANE_IN:(n + 1) * N_LANE_IN] for n in range(n_lanes)]
    cs_ref, strict_ref, incl_ref, eye_ref, pmask_ref = refs[n_lanes * N_LANE_IN:n_lanes * N_LANE_IN + 5]
    outs = refs[n_lanes * N_LANE_IN + 5:n_lanes * N_LANE_IN + 5 + 2 * n_lanes]
    lane_out = [outs[2 * n:2 * n + 2] for n in range(n_lanes)]
    lane_scr = refs[n_lanes * N_LANE_IN + 5 + 2 * n_lanes:]

    k = pl.program_id(0)
    pmask_f = pmask_ref[...]
    pmask = pmask_f.astype(BF16)
    eye = eye_ref[...]

    def load_state(s0_ref, scr, is_ctx):
        for p in range(N_PAIRS):
            carried = jnp.concatenate([s0_ref[p * PAIR_W:(p + 1) * PAIR_W, :]] * 2, axis=1) * pmask_f
            scr[p] = jnp.where(is_ctx, 0.0, carried)

    for n, (reverse, stream) in enumerate(SCAN_LANES):
        pos = _lane_pos(reverse, k)
        j = (pos - STREAM_CTX) % LAT_TILES_PER_SEQ
        first = (pos < STREAM_CTX) | (j == (LAT_TILES_PER_SEQ - 1 if reverse else 0))
        pl.when(first)(functools.partial(load_state, lane_in[n][6], lane_scr[n], pos < STREAM_CTX))

    preps = []
    for n, (reverse, stream) in enumerate(SCAN_LANES):
        d = 1 if reverse else 0
        r_ref, v_ref, kn_ref, lw_ref, kd_ref, b_ref, _ = lane_in[n]
        preps.append(_scan_prepare(reverse, r_ref[...], v_ref[...], kn_ref[...], lw_ref[...], kd_ref[...], b_ref[...],
                                   cs_ref[d], strict_ref[d], incl_ref[d], eye, pmask))
    states = [[lane_scr[n][p] for p in range(N_PAIRS)] for n in range(n_lanes)]
    for c in range(CHUNKS_PER_TILE):
        for n, (reverse, stream) in enumerate(SCAN_LANES):
            cc = CHUNKS_PER_TILE - 1 - c if reverse else c
            states[n] = _scan_chain_step(cc, states[n], preps[n], lane_out[n][0])
    for n in range(n_lanes):
        for p in range(N_PAIRS):
            lane_scr[n][p] = states[n][p]
            for h in range(2):
                blk = slice(h * HEAD_DIM, (h + 1) * HEAD_DIM)
                lane_out[n][1][2 * p + h] = states[n][p][blk, blk]


def _scan_call(r, v, kn, lw, kd, b, s0, consts, l):
    cs, strict, incl, eye, pmask = consts
    lat_seq_of = lambda t: jnp.maximum(t - CTX_TILES, 0) // LAT_TILES_PER_SEQ
    const2 = lambda k: (0, 0)
    const3 = lambda k: (0, 0, 0)
    in_specs, out_specs, args = [], [], []
    for reverse, stream in SCAN_LANES:
        d = 1 if reverse else 0
        tile = functools.partial(lambda rv, st, k: _stream_tile(st, _lane_pos(rv, k)), reverse, stream)
        pos = functools.partial(_lane_pos, reverse)
        tok = pl.BlockSpec((TM, RWKV_W), functools.partial(lambda tl, k: (tl(k), 0), tile))
        tok_d = pl.BlockSpec((None, TM, RWKV_W), functools.partial(lambda tl, dd, k: (dd, tl(k), 0), tile, d))
        s0_spec = pl.BlockSpec((None, None, None, RWKV_W, HEAD_DIM),
                               functools.partial(lambda tl, dd, k: (lat_seq_of(tl(k)), l, dd, 0, 0), tile, d))
        in_specs += [tok, tok, tok, tok_d, tok_d, tok_d, s0_spec]
        args += [r, v, kn, lw, kd, b, s0]
        out_specs += [
            pl.BlockSpec((TM, RWKV_W), functools.partial(lambda ps, k: (ps(k), 0), pos)),
            pl.BlockSpec((None, RWKV_HEADS, HEAD_DIM, HEAD_DIM),
                         functools.partial(lambda ps, k: (jnp.minimum(ps(k), STREAM_CTX), 0, 0, 0), pos)),
        ]
    in_specs += [
        pl.BlockSpec((2, TM, TM), const3),
        pl.BlockSpec((2, CHUNK, RWKV_W), const3),
        pl.BlockSpec((2, CHUNK, RWKV_W), const3),
        pl.BlockSpec((CHUNK, RWKV_W), const2),
        pl.BlockSpec((PAIR_W, PAIR_W), const2),
    ]
    args += [cs, strict, incl, eye, pmask]
    y = jax.ShapeDtypeStruct((STREAM_TILES * TM, RWKV_W), F32)
    sfin = jax.ShapeDtypeStruct((STREAM_CTX + 1, RWKV_HEADS, HEAD_DIM, HEAD_DIM), F32)
    state = pltpu.VMEM((N_PAIRS, PAIR_W, PAIR_W), F32)
    outs = pl.pallas_call(
        _scan_kernel,
        grid=(STREAM_TILES,),
        in_specs=in_specs,
        out_specs=out_specs,
        out_shape=[y, sfin] * len(SCAN_LANES),
        scratch_shapes=[state] * len(SCAN_LANES),
        compiler_params=_cparams(("arbitrary",)),
        name="rwkv_scan",
    )(*args)
    return outs[0::2], outs[1::2]


def _scan_final_states(sfins_per_layer):
    return jnp.stack(
        [jnp.stack([jnp.concatenate([sfins[d * N_STREAMS + st][:STREAM_CTX] for st in range(N_STREAMS)], axis=0)
                    for d in range(2)], axis=1)
         for sfins in sfins_per_layer], axis=1)


def _scan_consts():
    t = np.arange(TM)
    same_chunk = (t[:, None] // CHUNK) == (t[None, :] // CHUNK)
    cs_f = same_chunk & (t[None, :] <= t[:, None])
    cs_b = same_chunk & (t[None, :] >= t[:, None])
    cs = np.stack([cs_f, cs_b]).astype(np.float32)
    q = np.arange(CHUNK)[:, None]
    kcol = np.arange(RWKV_W)[None, :] % CHUNK
    strict = np.stack([kcol < q, kcol > q]).astype(np.float32)
    incl = np.stack([kcol <= q, kcol >= q]).astype(np.float32)
    eye = (kcol == q).astype(np.float32)
    return (jnp.asarray(cs, BF16), jnp.asarray(strict), jnp.asarray(incl),
            jnp.asarray(eye), jnp.asarray(_head_block_mask(PAIR_W)))


FNET_TT = 512
FNET_IN_ROWS = N_CTX


def _fnet_kernel(seq_len, n_in, *refs):
    z_refs = refs[:n_in]
    ch_ref, cl_ref, th_ref, o_ref, z_scr = refs[n_in:]
    b = pl.program_id(1)

    @pl.when((pl.program_id(0) == 0) & (b == 0))
    def _():
        per_in = FNET_IN_ROWS // seq_len
        for n, z_ref in enumerate(z_refs):
            zh, zl = _split2(z_ref[...])
            for half in range(2):
                ch = ch_ref[half]
                cl = cl_ref[half]
                zc = (_dg(zh, ch, _NN) + (_dg(zh, cl, _NN) + _dg(zl, ch, _NN))).astype(BF16)
                for s in range(per_in):
                    z_scr[n * per_in + s, half * seq_len:(half + 1) * seq_len, :] = zc[s * seq_len:(s + 1) * seq_len]

    o_ref[...] = _dg(th_ref[...], z_scr[b], _NN)


def _fnet_call(u_f, seq_len, n_seq, row_off, tables, name):
    ch, cl, th = tables
    tt = min(FNET_TT, seq_len)
    nt = seq_len // tt
    n_in = n_seq * seq_len // FNET_IN_ROWS
    in_off = row_off // FNET_IN_ROWS
    z_specs = [pl.BlockSpec((FNET_IN_ROWS, FNET_W), functools.partial(lambda n, j, b: (in_off + n, 0), n))
               for n in range(n_in)]
    return pl.pallas_call(
        functools.partial(_fnet_kernel, seq_len, n_in),
        grid=(nt, n_seq),
        in_specs=z_specs + [
            pl.BlockSpec((2, FNET_W, FNET_W), lambda j, b: (0, 0, 0)),
            pl.BlockSpec((2, FNET_W, FNET_W), lambda j, b: (0, 0, 0)),
            pl.BlockSpec((tt, 2 * seq_len), lambda j, b: (j, 0)),
        ],
        out_specs=pl.BlockSpec((tt, FNET_W), lambda j, b: (b * nt + j, 0)),
        out_shape=jax.ShapeDtypeStruct((n_seq * seq_len, FNET_W), F32),
        scratch_shapes=[pltpu.VMEM((n_seq, 2 * seq_len, FNET_W), BF16)],
        compiler_params=_cparams(("arbitrary", "arbitrary")),
        name=name,
    )(*([u_f] * n_in), ch, cl, th)


DFT_SPLIT = 32


def _dft_tables(seq_len):
    c = np.arange(FNET_W)
    ang_c = 2.0 * np.pi * ((c[:, None] % HEAD_DIM) * (c[None, :] % HEAD_DIM) % HEAD_DIM) / HEAD_DIM
    scale = 1.0 / np.sqrt(float(seq_len) * HEAD_DIM)
    blk = _head_block_mask(FNET_W)
    ch, cl = _split2(jnp.asarray(np.stack([np.cos(ang_c) * blk * scale, np.sin(ang_c) * blk * scale]), F32))

    n1 = seq_len // DFT_SPLIT
    f = np.arange(seq_len)
    a1 = 2.0 * np.pi * ((np.arange(n1)[:, None] * f[None, :]) % n1) / n1
    a2 = 2.0 * np.pi * ((np.arange(DFT_SPLIT)[:, None] * f[None, :]) % seq_len) / seq_len
    c1, s1 = jnp.asarray(np.cos(a1), F32)[:, None, :], jnp.asarray(np.sin(a1), F32)[:, None, :]
    c2, s2 = jnp.asarray(np.cos(a2), F32)[None, :, :], jnp.asarray(np.sin(a2), F32)[None, :, :]
    cos_t = (c1 * c2 - s1 * s2).reshape(seq_len, seq_len)
    sin_t = (s1 * c2 + c1 * s2).reshape(seq_len, seq_len)
    th = jnp.concatenate([cos_t, -sin_t], axis=1).astype(BF16)
    return ch, cl, th


def _merge_ffn_kernel(split, final, *refs):
    i = pl.program_id(0)
    n_x = 2 if split else 1
    x_in = _tile_of_split(i, *refs[:2], FFN_TM) if split else refs[0][...]
    (mod_ref, oac_ref, oal_ref, yf0_ref, yf1_ref, yb0_ref, yb1_ref, bonus_ref, g_ref, ofc_ref, ofl_ref,
     lng_ref, lnb_ref, avg_ref, nfg_ref, wo_ref, wi_ref, w2_ref) = refs[n_x:n_x + 18]
    tail = refs[n_x + 18:]
    is_ctx = i < N_CTX // FFN_TM
    o_attn = _tile_of_split(i, oac_ref, oal_ref, FFN_TM)
    o_fnet = _tile_of_split(i, ofc_ref, ofl_ref, FFN_TM)
    avg = avg_ref[...]
    y = jnp.where(_in_stream0(i, FFN_TM), yf0_ref[...] + yb0_ref[...], yf1_ref[...] + yb1_ref[...])
    yc = y - _dot_exact_rhs(y, avg)
    var = _dot_exact_rhs(yc * yc, avg)
    yn = yc * lax.rsqrt(var + GN_EPS) * lng_ref[...] + lnb_ref[...]
    o_rwkv = (yn + bonus_ref[...]) * g_ref[...]

    o = (_dg(o_attn.astype(BF16), wo_ref[0:ATTN_W, :], _NN)
         + _dg(o_rwkv.astype(BF16), wo_ref[ATTN_W:ATTN_W + RWKV_W, :], _NN)
         + _dg(o_fnet.astype(BF16), wo_ref[ATTN_W + RWKV_W:, :], _NN))
    g1 = mod_ref[:, 2 * D_MODEL:3 * D_MODEL]
    sh2 = mod_ref[:, 3 * D_MODEL:4 * D_MODEL]
    sc2 = mod_ref[:, 4 * D_MODEL:5 * D_MODEL]
    g2 = mod_ref[:, 5 * D_MODEL:6 * D_MODEL]
    x = x_in + g1 * o
    h = _rms(x, nfg_ref[...]) * (1.0 + sc2) + sh2
    gu = _dg(h.astype(BF16), wi_ref[...], _NN)
    gt = gu[:, 0:D_FF]
    act = gt * jax.nn.sigmoid(gt) * gu[:, D_FF:]
    x = x + g2 * _dg(act.astype(BF16), w2_ref[...], _NN)
    if not final:
        (o_ref,) = tail
        o_ref[...] = x
    else:
        fng_ref, oc_ref, ol_ref = tail
        y_out = _rms(x, fng_ref[...])

        @pl.when(is_ctx)
        def _():
            oc_ref[...] = y_out

        @pl.when(jnp.logical_not(is_ctx))
        def _():
            ol_ref[...] = y_out


def _in_stream0(m, tm):
    ctx_m = N_CTX // tm
    lat_ps = (N_TOK // tm - ctx_m) // N_STREAMS
    return (m < ctx_m // N_STREAMS) | ((m >= ctx_m) & (m < ctx_m + lat_ps))


def _stream_block(stream, tm):
    ctx_m = N_CTX // tm
    half = ctx_m // N_STREAMS
    lat_ps = (N_TOK // tm - ctx_m) // N_STREAMS
    if stream == 0:
        return lambda m: (jnp.where(m < half, m, jnp.where(m < ctx_m, half - 1,
                                    jnp.where(m < ctx_m + lat_ps, m - half, half + lat_ps - 1))), 0)
    return lambda m: (jnp.where(m < half, 0, jnp.where(m < ctx_m, m - half,
                                jnp.where(m < ctx_m + lat_ps, half - 1, m - half - lat_ps))), 0)


def _merge_ffn_call(xs, mods_l, oa_ctx, oa_lat, ys, bonus, g, of_ctx, of_lat, ln_g, ln_b, avg, nfg,
                    wo, wi, w2, l, final_g=None):
    final = final_g is not None
    split = isinstance(xs, tuple)
    xs = xs if split else (xs,)
    row = lambda i: (l, 0, 0)
    tile = lambda i: (i, 0)
    ctx_tile, lat_tile = _ctx_tile(FFN_TM), _lat_tile(FFN_TM)
    once = pl.Buffered(1)
    in_specs = _x_specs(split, FFN_TM) + [
        pl.BlockSpec((None, 1, 6 * D_MODEL), lambda i: (_mod_row(i, FFN_TM), 0, 0)),
        pl.BlockSpec((FFN_TM, ATTN_W), ctx_tile),
        pl.BlockSpec((FFN_TM, ATTN_W), lat_tile),
        pl.BlockSpec((FFN_TM, RWKV_W), _stream_block(0, FFN_TM)),
        pl.BlockSpec((FFN_TM, RWKV_W), _stream_block(1, FFN_TM)),
        pl.BlockSpec((FFN_TM, RWKV_W), _stream_block(0, FFN_TM)),
        pl.BlockSpec((FFN_TM, RWKV_W), _stream_block(1, FFN_TM)),
        pl.BlockSpec((FFN_TM, RWKV_W), tile),
        pl.BlockSpec((FFN_TM, RWKV_W), tile),
        pl.BlockSpec((FFN_TM, FNET_W), ctx_tile),
        pl.BlockSpec((FFN_TM, FNET_W), lat_tile),
        pl.BlockSpec((None, 1, RWKV_W), row),
        pl.BlockSpec((None, 1, RWKV_W), row),
        pl.BlockSpec((RWKV_W, RWKV_W), lambda i: (0, 0)),
        pl.BlockSpec((None, 1, D_MODEL), row),
        pl.BlockSpec((None, D_MODEL, D_MODEL), row, pipeline_mode=once),
        pl.BlockSpec((None, D_MODEL, 2 * D_FF), row, pipeline_mode=once),
        pl.BlockSpec((None, D_FF, D_MODEL), row, pipeline_mode=once),
    ]
    args = [*xs, mods_l, oa_ctx, oa_lat, *ys, bonus, g, of_ctx, of_lat, ln_g, ln_b, avg, nfg, wo, wi, w2]
    if final:
        in_specs.append(pl.BlockSpec((1, D_MODEL), lambda i: (0, 0)))
        args.append(final_g)
        out_specs = [pl.BlockSpec((FFN_TM, D_MODEL), ctx_tile), pl.BlockSpec((FFN_TM, D_MODEL), lat_tile)]
        out_shape = [jax.ShapeDtypeStruct((N_CTX, D_MODEL), F32), jax.ShapeDtypeStruct((N_LAT, D_MODEL), F32)]
    else:
        out_specs = pl.BlockSpec((FFN_TM, D_MODEL), tile)
        out_shape = jax.ShapeDtypeStruct((N_TOK, D_MODEL), F32)
    return pl.pallas_call(
        functools.partial(_merge_ffn_kernel, split, final),
        grid=(N_TOK // FFN_TM,),
        in_specs=in_specs,
        out_specs=out_specs,
        out_shape=out_shape,
        compiler_params=_cparams(("arbitrary",)),
        name="merge_ffn_final" if final else "merge_ffn",
    )(*args)


def _block_diag2(w):
    z = jnp.zeros_like(w[:, 0])
    return jnp.concatenate([jnp.concatenate([w[:, 0], z], axis=2), jnp.concatenate([z, w[:, 1]], axis=2)], axis=1)


def kernel(x_prompt, x_sample, cache_k, cache_v, state_rwkv, c, c_ctx, w_ada, b_ada, norm_mix_g, norm_ffn_g,
           w_in, w_out, attn_sink, rwkv_shift_w, rwkv_w0, rwkv_w_up, rwkv_a0, rwkv_a_up, rwkv_g_up,
           rwkv_k_k, rwkv_k_a, rwkv_r_k, rwkv_ln_g, rwkv_ln_b, ffn_w_in, ffn_w_out, norm_final_g):
    x = (x_prompt.reshape(N_CTX, D_MODEL), x_sample.reshape(N_LAT, D_MODEL))
    cvec = jnp.concatenate([c_ctx[None, :], c, jnp.zeros((8 - 1 - N_LAT_SEQ, D_MODEL), F32)], axis=0)
    mods = _mods_call(cvec, w_ada, b_ada).reshape(DEPTH, 8, 1, 6 * D_MODEL)

    w_in_b = w_in.astype(BF16)
    w_out_b = w_out.astype(BF16)
    ffn_in_b = ffn_w_in.astype(BF16)
    ffn_out_b = ffn_w_out.astype(BF16)
    wup_bd = _block_diag2(rwkv_w_up)
    aup_bd = _block_diag2(rwkv_a_up)
    row3 = lambda a: a.reshape(DEPTH, 1, -1)
    nmg, nfg = row3(norm_mix_g), row3(norm_ffn_g)
    k_k, k_a, r_k = row3(rwkv_k_k), row3(rwkv_k_a), row3(rwkv_r_k)
    ln_g, ln_b = row3(rwkv_ln_g), row3(rwkv_ln_b)
    kc = cache_k.reshape(N_LAT_SEQ, DEPTH, PAST_LEN, KV_W)
    vc = cache_v.reshape(N_LAT_SEQ, DEPTH, PAST_LEN, KV_W)
    s0 = state_rwkv.reshape(N_LAT_SEQ, DEPTH, 2, RWKV_W, HEAD_DIM)

    cos, sin = _rope_tables()
    consts = _scan_consts()
    bmask = jnp.asarray(_head_block_mask(RWKV_W))
    ones_bd = bmask.astype(BF16)
    avg_bd = (bmask * (1.0 / HEAD_DIM)).astype(BF16)
    rwkv_params = (rwkv_shift_w, wup_bd, aup_bd, rwkv_g_up, rwkv_w0, rwkv_a0, k_k, k_a, r_k, ones_bd)
    dft_ctx = _dft_tables(CTX_LEN)
    dft_lat = _dft_tables(LAT_LEN)

    new_k, new_v, new_s = [], [], []
    for l in range(DEPTH):
        q, k, v, u_f, r, vv, kn, g, bonus, lw, kd, b = _inproj_call(x, mods[l], nmg, w_in_b, rwkv_params, l)
        new_k.append(k[:N_CTX].reshape(N_CTX_SEQ, CTX_LEN, KV_W))
        new_v.append(v[:N_CTX].reshape(N_CTX_SEQ, CTX_LEN, KV_W))

        oa_ctx = _ctx_attn_call(q, k, v, attn_sink, l)
        oa_lat = _lat_attn_call(q, k, v, kc, vc, cos, sin, attn_sink, l)

        ys, sfins = _scan_call(r, vv, kn, lw, kd, b, s0, consts, l)
        new_s.append(sfins)

        of_ctx = _fnet_call(u_f, CTX_LEN, N_CTX_SEQ, 0, dft_ctx, "fnet_context")
        of_lat = _fnet_call(u_f, LAT_LEN, N_LAT_SEQ, N_CTX, dft_lat, "fnet_latent")

        last = l == DEPTH - 1
        x = _merge_ffn_call(x, mods[l], oa_ctx, oa_lat, ys, bonus, g, of_ctx, of_lat, ln_g, ln_b, avg_bd, nfg,
                            w_out_b, ffn_in_b, ffn_out_b, l,
                            final_g=norm_final_g.reshape(1, D_MODEL) if last else None)

    y_ctx, y_lat = x
    return (y_ctx.reshape(N_CTX_SEQ, CTX_LEN, D_MODEL), y_lat.reshape(N_LAT_SEQ, LAT_LEN, D_MODEL),
            jnp.stack(new_k, axis=1).reshape(N_CTX_SEQ, DEPTH, CTX_LEN, N_KV_HEADS, HEAD_DIM),
            jnp.stack(new_v, axis=1).reshape(N_CTX_SEQ, DEPTH, CTX_LEN, N_KV_HEADS, HEAD_DIM),
            _scan_final_states(new_s))
```

```python
import functools

import numpy as np
import jax
import jax.numpy as jnp
from jax import lax
from jax.experimental import pallas as pl
from jax.experimental.pallas import tpu as pltpu

F32 = jnp.float32
BF16 = jnp.bfloat16

D_MODEL = 1024
N_CTX_SEQ = 16
CTX_LEN = 256
DEPTH = 4
N_LAT_SEQ = 4
LAT_LEN = 2048
PAST_LEN = 512
GRID_W = 64
HEAD_DIM = 64
N_Q_HEADS = 8
N_KV_HEADS = 2
Q_PER_KV = N_Q_HEADS // N_KV_HEADS
ATTN_W = N_Q_HEADS * HEAD_DIM
KV_W = N_KV_HEADS * HEAD_DIM
BLOCK = 128
ATTN_SCALE = 0.125
ROPE_THETA = 10000.0
NEG = -1e30
RWKV_HEADS = 4
RWKV_W = RWKV_HEADS * HEAD_DIM
DECAY_RANK = 64
ICLR_RANK = 64
GATE_RANK = 128
DECAY_SCALE = 0.6065306597126334
GN_EPS = 64e-5
RWKV_IN_W = 3 * RWKV_W + GATE_RANK + 2 * DECAY_RANK + 2 * ICLR_RANK
FNET_W = 4 * HEAD_DIM
IN_W = ATTN_W + 2 * KV_W + RWKV_IN_W + FNET_W
D_FF = 2816
RMS_EPS = 1e-6

N_CTX = N_CTX_SEQ * CTX_LEN
N_LAT = N_LAT_SEQ * LAT_LEN
N_TOK = N_CTX + N_LAT
TM = 256
N_TILES = N_TOK // TM
CTX_TILES = N_CTX // TM
LAT_TILES_PER_SEQ = LAT_LEN // TM
N_SEQ = N_CTX_SEQ + N_LAT_SEQ
CHUNK = 64
CHUNKS_PER_TILE = TM // CHUNK
HALO = 8
FFN_TM = 512
INPROJ_TM = 512
LAT_ATTN_QBLOCKS = 4
LAT_ATTN_GROUP = 8

VMEM_LIMIT = 56 * 1024 * 1024


def _cparams(sem):
    return pltpu.CompilerParams(dimension_semantics=sem, vmem_limit_bytes=VMEM_LIMIT)


def _mod_row(i, tm=TM):
    return jnp.where(i < N_CTX // tm, 0, 1 + (i - N_CTX // tm) // (LAT_LEN // tm))


def _split2(x):
    hi = x.astype(BF16)
    lo = (x - hi.astype(F32)).astype(BF16)
    return hi, lo


_NN = (((1,), (0,)), ((), ()))
_NT = (((1,), (1,)), ((), ()))


def _dg(a, b, dims):
    return lax.dot_general(a, b, dims, preferred_element_type=F32)


def _dot1(a, b, dims=_NN):
    return _dg(a.astype(BF16), b.astype(BF16), dims)


def _dot3(a, b, dims=_NN):
    ah, al = _split2(a)
    bh, bl = _split2(b)
    return _dg(ah, bh, dims) + (_dg(ah, bl, dims) + _dg(al, bh, dims))


def _dot_exact_rhs(a, b_bf16):
    ah, al = _split2(a)
    return _dg(ah, b_bf16, _NN) + _dg(al, b_bf16, _NN)


def _dot_exact_lhs(a_bf16, b):
    bh, bl = _split2(b)
    return _dg(a_bf16, bh, _NN) + _dg(a_bf16, bl, _NN)


def _mods_kernel(c_ref, w_ref, b_ref, o_ref):
    c = c_ref[...]
    s = c * jax.nn.sigmoid(c)
    o_ref[...] = _dot1(s, w_ref[...]) + b_ref[...]


def _mods_call(cvec, w_ada, b_ada):
    tn = 1536
    return pl.pallas_call(
        _mods_kernel,
        grid=(DEPTH, 6 * D_MODEL // tn),
        in_specs=[
            pl.BlockSpec((8, D_MODEL), lambda l, j: (0, 0)),
            pl.BlockSpec((None, D_MODEL, tn), lambda l, j: (l, 0, j)),
            pl.BlockSpec((None, 1, tn), lambda l, j: (l, 0, j)),
        ],
        out_specs=pl.BlockSpec((None, 8, tn), lambda l, j: (l, 0, j)),
        out_shape=jax.ShapeDtypeStruct((DEPTH, 8, 6 * D_MODEL), F32),
        compiler_params=_cparams(("parallel", "parallel")),
        name="adaln_mods",
    )(cvec, w_ada, b_ada.reshape(DEPTH, 1, 6 * D_MODEL))


def _rms(x, g):
    return x * lax.rsqrt(jnp.mean(x * x, axis=-1, keepdims=True) + RMS_EPS) * g


def _tile_of_split(i, ctx_ref, lat_ref, tm=TM):
    return jnp.where(i < N_CTX // tm, ctx_ref[...], lat_ref[...])


def _ctx_tile(tm=TM):
    return lambda i: (jnp.minimum(i, N_CTX // tm - 1), 0)


def _lat_tile(tm=TM):
    return lambda i: (jnp.maximum(i - N_CTX // tm, 0), 0)


def _x_specs(split, tm=TM):
    if split:
        return [pl.BlockSpec((tm, D_MODEL), _ctx_tile(tm)), pl.BlockSpec((tm, D_MODEL), _lat_tile(tm))]
    return [pl.BlockSpec((tm, D_MODEL), lambda i: (i, 0))]


def _head_block_mask(n):
    r = np.arange(n)[:, None] // HEAD_DIM
    c = np.arange(n)[None, :] // HEAD_DIM
    return (r == c).astype(np.float32)


def _rwkv_token_terms(u, prev_row, next_row, seam, sw_ref, wup_ref, aup_ref, gup_ref, w0_ref, a0_ref,
                      kk_ref, ka_ref, rk_ref, ones_ref, r_out, v_out, kn_out, g_out, bonus_out, lw_out, kd_out, b_out):
    n = u.shape[0]
    row = lax.broadcasted_iota(jnp.int32, u.shape, 0)
    u_dn = jnp.where(row == 0, prev_row, pltpu.roll(u, 1, 0))
    u_up = jnp.where(row == n - 1, next_row, pltpu.roll(u, n - 1, 0))
    u_dn = jnp.where(seam & (row == n // 2), 0.0, u_dn)
    u_up = jnp.where(seam & (row == n // 2 - 1), 0.0, u_up)
    us = u_dn * sw_ref[0:1, :] + u * sw_ref[1:2, :] + u_up * sw_ref[2:3, :]

    r = us[:, 0:RWKV_W]
    k = us[:, RWKV_W:2 * RWKV_W]
    v = us[:, 2 * RWKV_W:3 * RWKV_W]
    o = 3 * RWKV_W
    gd = us[:, o:o + GATE_RANK]
    wd = us[:, o + GATE_RANK:o + GATE_RANK + 2 * DECAY_RANK]
    ad = us[:, o + GATE_RANK + 2 * DECAY_RANK:RWKV_IN_W]

    w_pre = _dot1(jnp.tanh(wd), wup_ref[...])
    a_pre = _dot1(ad, aup_ref[...])
    g_out[...] = _dot1(jax.nn.sigmoid(gd), gup_ref[...])

    ones_bd = ones_ref[...]
    kn = k * kk_ref[...]
    kn = kn * lax.rsqrt(_dot_exact_rhs(kn * kn, ones_bd) + 1e-12)
    r_out[...] = r
    v_out[...] = v
    kn_out[...] = kn
    bonus_out[...] = _dot_exact_rhs(r * k * rk_ref[...], ones_bd) * v
    ka = ka_ref[...]
    for d in range(2):
        sl = slice(d * RWKV_W, (d + 1) * RWKV_W)
        a_d = jax.nn.sigmoid(a0_ref[d:d + 1, :] + a_pre[:, sl])
        lw_out[d] = -DECAY_SCALE * jax.nn.sigmoid(w0_ref[d:d + 1, :] + w_pre[:, sl])
        kd_out[d] = k * (1.0 + (a_d - 1.0) * ka)
        b_out[d] = kn * a_d


RWKV_COL0 = ATTN_W + 2 * KV_W
N_RWKV_PARAMS = 10


def _inproj_kernel(split, *refs):
    n_x = 2 if split else 1
    i = pl.program_id(0)
    x = _tile_of_split(i, *refs[:2], INPROJ_TM) if split else refs[0][...]
    xp_ref, xn_ref, mod_ref, g_ref, w_ref = refs[n_x:n_x + 5]
    rwkv_params = refs[n_x + 5:n_x + 5 + N_RWKV_PARAMS]
    q_ref, k_ref, v_ref, uf_ref = refs[n_x + 5 + N_RWKV_PARAMS:n_x + 9 + N_RWKV_PARAMS]
    rwkv_outs = refs[n_x + 9 + N_RWKV_PARAMS:]

    shift = mod_ref[:, 0:D_MODEL]
    scale = mod_ref[:, D_MODEL:2 * D_MODEL]
    x_all = jnp.concatenate([x, xp_ref[...], xn_ref[...]], axis=0)
    h = (_rms(x_all, g_ref[...]) * (1.0 + scale) + shift).astype(BF16)
    u_all = _dg(h, w_ref[:, RWKV_COL0:RWKV_COL0 + RWKV_IN_W], _NN)
    o_qkv = _dg(h[:INPROJ_TM], w_ref[:, 0:RWKV_COL0], _NN)
    q_ref[...] = o_qkv[:, 0:ATTN_W]
    k_ref[...] = o_qkv[:, ATTN_W:ATTN_W + KV_W]
    v_ref[...] = o_qkv[:, ATTN_W + KV_W:RWKV_COL0]
    uf_ref[...] = _dg(h[:INPROJ_TM], w_ref[:, RWKV_COL0 + RWKV_IN_W:IN_W], _NN)

    ctx_tiles = N_CTX // INPROJ_TM
    per_seq = LAT_LEN // INPROJ_TM
    is_lat = i >= ctx_tiles
    j = (i - ctx_tiles) % per_seq
    has_prev = jnp.where(is_lat & (j != 0), 1.0, 0.0)
    has_next = jnp.where(is_lat & (j != per_seq - 1), 1.0, 0.0)
    prev_row = u_all[INPROJ_TM + HALO - 1:INPROJ_TM + HALO, :] * has_prev
    next_row = u_all[INPROJ_TM + HALO:INPROJ_TM + HALO + 1, :] * has_next
    seam = jnp.logical_not(is_lat) if INPROJ_TM == 2 * CTX_LEN else False
    _rwkv_token_terms(u_all[:INPROJ_TM], prev_row, next_row, seam, *rwkv_params, *rwkv_outs)


def _inproj_call(xs, mods_l, g, w_bf16, rwkv_params, l):
    split = isinstance(xs, tuple)
    xs = xs if split else (xs,)
    hb = INPROJ_TM // HALO
    ctx_tiles = N_CTX // INPROJ_TM
    if split:
        halo_src = xs[1]
        last = N_LAT // HALO - 1
        prev_blk = lambda i: (jnp.clip((i - ctx_tiles) * hb - 1, 0, last), 0)
        next_blk = lambda i: (jnp.clip((i - ctx_tiles + 1) * hb, 0, last), 0)
    else:
        halo_src = xs[0]
        last = N_TOK // HALO - 1
        prev_blk = lambda i: (jnp.maximum(i * hb - 1, 0), 0)
        next_blk = lambda i: (jnp.minimum((i + 1) * hb, last), 0)
    row = lambda i: (l, 0, 0)
    widths = (ATTN_W, KV_W, KV_W, FNET_W)
    tok = jax.ShapeDtypeStruct((N_TOK, RWKV_W), F32)
    tok2 = jax.ShapeDtypeStruct((2, N_TOK, RWKV_W), F32)
    return pl.pallas_call(
        functools.partial(_inproj_kernel, split),
        grid=(N_TOK // INPROJ_TM,),
        in_specs=_x_specs(split, INPROJ_TM) + [
            pl.BlockSpec((HALO, D_MODEL), prev_blk),
            pl.BlockSpec((HALO, D_MODEL), next_blk),
            pl.BlockSpec((None, 1, 6 * D_MODEL), lambda i: (_mod_row(i, INPROJ_TM), 0, 0)),
            pl.BlockSpec((None, 1, D_MODEL), row),
            pl.BlockSpec((None, D_MODEL, IN_W), row),
            pl.BlockSpec((None, 3, RWKV_IN_W), row),
            pl.BlockSpec((None, 2 * DECAY_RANK, 2 * RWKV_W), row),
            pl.BlockSpec((None, 2 * ICLR_RANK, 2 * RWKV_W), row),
            pl.BlockSpec((None, GATE_RANK, RWKV_W), row),
            pl.BlockSpec((None, 2, RWKV_W), row),
            pl.BlockSpec((None, 2, RWKV_W), row),
            pl.BlockSpec((None, 1, RWKV_W), row),
            pl.BlockSpec((None, 1, RWKV_W), row),
            pl.BlockSpec((None, 1, RWKV_W), row),
            pl.BlockSpec((RWKV_W, RWKV_W), lambda i: (0, 0)),
        ],
        out_specs=[pl.BlockSpec((INPROJ_TM, w), lambda i: (i, 0)) for w in widths]
        + [pl.BlockSpec((INPROJ_TM, RWKV_W), lambda i: (i, 0))] * 5
        + [pl.BlockSpec((2, INPROJ_TM, RWKV_W), lambda i: (0, i, 0))] * 3,
        out_shape=[jax.ShapeDtypeStruct((N_TOK, w), F32) for w in widths] + [tok] * 5 + [tok2] * 3,
        compiler_params=_cparams(("arbitrary",)),
        name="in_projection",
    )(*xs, halo_src, halo_src, mods_l, g, w_bf16, *rwkv_params)


def _softmax_pv(s, sink_col, v):
    m = jnp.maximum(jnp.max(s, axis=-1, keepdims=True), sink_col)
    e = jnp.exp(s - m)
    denom = jnp.sum(e, axis=-1, keepdims=True) + jnp.exp(sink_col - m)
    return _dot1(e, v) / denom


def _stack_q_heads(q, kh, rows):
    return jnp.concatenate(
        [q[:, (kh * Q_PER_KV + g) * HEAD_DIM:(kh * Q_PER_KV + g + 1) * HEAD_DIM] for g in range(Q_PER_KV)],
        axis=0)


def _sink_col(sink_ref, l, kh, rows):
    return jnp.concatenate(
        [jnp.full((rows, 1), sink_ref[l, kh * Q_PER_KV + g], F32) for g in range(Q_PER_KV)], axis=0)


def _ctx_attn_kernel(l, sink_ref, q_ref, k_ref, v_ref, o_ref):
    q = q_ref[...]
    k = k_ref[...]
    v = v_ref[...]
    half = CTX_LEN // 2
    pieces = [(h, r) for h in range(N_Q_HEADS) for r in range(2)]
    kb = k.astype(BF16)
    vb = v.astype(BF16)
    kv = lambda a, h: a[:, (h // Q_PER_KV) * HEAD_DIM:(h // Q_PER_KV + 1) * HEAD_DIM]
    qs = [(q[r * half:(r + 1) * half, h * HEAD_DIM:(h + 1) * HEAD_DIM] * ATTN_SCALE).astype(BF16) for h, r in pieces]
    s = [_dg(qs[n], kv(kb, h), _NT) for n, (h, r) in enumerate(pieces)]
    m = [jnp.maximum(jnp.max(s[n], axis=-1, keepdims=True), sink_ref[l, h]) for n, (h, r) in enumerate(pieces)]
    e = [jnp.exp(s[n] - m[n]) for n in range(len(pieces))]
    den = [jnp.sum(e[n], axis=-1, keepdims=True) + jnp.exp(sink_ref[l, h] - m[n]) for n, (h, r) in enumerate(pieces)]
    o = [_dg(e[n].astype(BF16), kv(vb, h), _NN) / den[n] for n, (h, r) in enumerate(pieces)]
    for n, (h, r) in enumerate(pieces):
        o_ref[r * half:(r + 1) * half, h * HEAD_DIM:(h + 1) * HEAD_DIM] = o[n]


def _ctx_attn_call(q, k, v, sink, l):
    return pl.pallas_call(
        functools.partial(_ctx_attn_kernel, l),
        grid=(N_CTX_SEQ,),
        in_specs=[
            pl.BlockSpec(memory_space=pltpu.SMEM),
            pl.BlockSpec((CTX_LEN, ATTN_W), lambda b: (b, 0)),
            pl.BlockSpec((CTX_LEN, KV_W), lambda b: (b, 0)),
            pl.BlockSpec((CTX_LEN, KV_W), lambda b: (b, 0)),
        ],
        out_specs=pl.BlockSpec((CTX_LEN, ATTN_W), lambda b: (b, 0)),
        out_shape=jax.ShapeDtypeStruct((N_CTX, ATTN_W), F32),
        compiler_params=_cparams(("parallel",)),
        name="context_attention",
    )(sink, q, k, v)


def _rope(x, cos, sin):
    lane = lax.broadcasted_iota(jnp.int32, x.shape, 1)
    partner = jnp.where((lane & 16) == 0, pltpu.roll(x, 128 - 16, 1), pltpu.roll(x, 16, 1))
    return x * cos + partner * sin


def _lat_attn_kernel(l, sink_ref, q_ref, k_ref, v_ref, kc_ref, vc_ref, cos_ref, sin_ref, o_ref):
    nq = LAT_ATTN_QBLOCKS
    nb = LAT_LEN // BLOCK
    n0 = pl.program_id(1) * nq
    r0 = pl.multiple_of(n0 * BLOCK, BLOCK)
    cos_q = cos_ref[pl.ds(r0, nq * BLOCK), :]
    sin_q = sin_ref[pl.ds(r0, nq * BLOCK), :]
    q = jnp.concatenate(
        [_rope(q_ref[:, j * 128:(j + 1) * 128], cos_q, sin_q) for j in range(ATTN_W // 128)], axis=1)

    kr, vr = [], []
    for dj in range(-1, nq + 1):
        rj = pl.multiple_of(jnp.clip(n0 + dj, 0, nb - 1) * BLOCK, BLOCK)
        kr.append(_rope(k_ref[pl.ds(rj, BLOCK), :], cos_ref[pl.ds(rj, BLOCK), :],
                        sin_ref[pl.ds(rj, BLOCK), :]).astype(BF16))
        vr.append(v_ref[pl.ds(rj, BLOCK), :].astype(BF16))
    kc = kc_ref[...].astype(BF16)
    vc = vc_ref[...].astype(BF16)
    k_all = [jnp.concatenate(kr[t:t + 3] + [kc], axis=0) for t in range(nq)]
    v_all = [jnp.concatenate(vr[t:t + 3] + [vc], axis=0) for t in range(nq)]

    kj = lax.broadcasted_iota(jnp.int32, (BLOCK, BLOCK), 1)
    qi = lax.broadcasted_iota(jnp.int32, (BLOCK, BLOCK), 0)
    keep_prev = [(kj >= qi) & (n0 + t > 0) for t in range(nq)]
    keep_next = [(kj <= qi) & (n0 + t < nb - 1) for t in range(nq)]
    kv_cols = lambda h: slice((h // Q_PER_KV) * HEAD_DIM, (h // Q_PER_KV + 1) * HEAD_DIM)

    all_pieces = [(t, h) for t in range(nq) for h in range(N_Q_HEADS)]
    for g0 in range(0, len(all_pieces), LAT_ATTN_GROUP):
        pieces = all_pieces[g0:g0 + LAT_ATTN_GROUP]
        rng = range(len(pieces))
        qs = [(q[t * BLOCK:(t + 1) * BLOCK, h * HEAD_DIM:(h + 1) * HEAD_DIM] * ATTN_SCALE).astype(BF16)
              for t, h in pieces]
        s = [_dg(qs[i], k_all[t][:, kv_cols(h)], _NT) for i, (t, h) in enumerate(pieces)]
        s = [jnp.concatenate([jnp.where(keep_prev[t], s[i][:, 0:BLOCK], NEG), s[i][:, BLOCK:2 * BLOCK],
                              jnp.where(keep_next[t], s[i][:, 2 * BLOCK:3 * BLOCK], NEG), s[i][:, 3 * BLOCK:]],
                             axis=1) for i, (t, h) in enumerate(pieces)]
        m = [jnp.maximum(jnp.max(s[i], axis=-1, keepdims=True), sink_ref[l, h]) for i, (t, h) in enumerate(pieces)]
        e = [jnp.exp(s[i] - m[i]) for i in rng]
        den = [jnp.sum(e[i], axis=-1, keepdims=True) + jnp.exp(sink_ref[l, h] - m[i])
               for i, (t, h) in enumerate(pieces)]
        o = [_dg(e[i].astype(BF16), v_all[t][:, kv_cols(h)], _NN) / den[i] for i, (t, h) in enumerate(pieces)]
        for i, (t, h) in enumerate(pieces):
            o_ref[t * BLOCK:(t + 1) * BLOCK, h * HEAD_DIM:(h + 1) * HEAD_DIM] = o[i]


def _lat_attn_call(q, k, v, kc, vc, cos, sin, sink, l):
    rows = LAT_ATTN_QBLOCKS * BLOCK
    steps = LAT_LEN // rows
    q_off = N_CTX // rows
    kv_off = N_CTX // LAT_LEN
    return pl.pallas_call(
        functools.partial(_lat_attn_kernel, l),
        grid=(N_LAT_SEQ, steps),
        in_specs=[
            pl.BlockSpec(memory_space=pltpu.SMEM),
            pl.BlockSpec((rows, ATTN_W), lambda b, n: (q_off + b * steps + n, 0)),
            pl.BlockSpec((LAT_LEN, KV_W), lambda b, n: (kv_off + b, 0)),
            pl.BlockSpec((LAT_LEN, KV_W), lambda b, n: (kv_off + b, 0)),
            pl.BlockSpec((None, None, PAST_LEN, KV_W), lambda b, n: (b, l, 0, 0)),
            pl.BlockSpec((None, None, PAST_LEN, KV_W), lambda b, n: (b, l, 0, 0)),
            pl.BlockSpec((LAT_LEN, 128), lambda b, n: (0, 0)),
            pl.BlockSpec((LAT_LEN, 128), lambda b, n: (0, 0)),
        ],
        out_specs=pl.BlockSpec((rows, ATTN_W), lambda b, n: (b * steps + n, 0)),
        out_shape=jax.ShapeDtypeStruct((N_LAT, ATTN_W), F32),
        compiler_params=_cparams(("parallel", "arbitrary")),
        name="latent_attention",
    )(sink, q, k, v, kc, vc, cos, sin)


def _rope_tables():
    t = np.arange(LAT_LEN)
    lane = np.arange(128)
    d = lane % HEAD_DIM
    pos = np.where(d[None, :] < HEAD_DIM // 2, (t // GRID_W)[:, None], (t % GRID_W)[:, None]).astype(np.float32)
    quarter = HEAD_DIM // 4
    inv = (np.float32(ROPE_THETA) ** (-(np.arange(quarter, dtype=np.float32)) / np.float32(quarter))).astype(np.float32)
    ang = pos * inv[d % quarter][None, :]
    sign = np.where((lane & 16) == 0, -1.0, 1.0)[None, :]
    return jnp.asarray(np.cos(ang), F32), jnp.asarray(np.sin(ang) * sign, F32)


PAIR_W = 2 * HEAD_DIM
INV_BASE = 8
INV_LEVELS = 3
N_PAIRS = RWKV_HEADS // 2
_TN = (((0,), (0,)), ((), ()))


def _pair_stack(x, pmask):
    return jnp.concatenate([x.astype(BF16)] * 2, axis=0) * pmask


def _hdot(m, x, pmask, dims):
    return jnp.concatenate(
        [_dg(m[:, p * PAIR_W:(p + 1) * PAIR_W], _pair_stack(x[:, p * PAIR_W:(p + 1) * PAIR_W], pmask), dims)
         for p in range(N_PAIRS)], axis=1)


def _scan_prepare(reverse, r, v, kn, lw, kd, b, cs, strict, incl, lvl, eye, pmask):
    linc = _dot_exact_lhs(cs, lw)
    end = 0 if reverse else CHUNK - 1
    ltot = jnp.concatenate(
        [jnp.broadcast_to(linc[c * CHUNK + end:c * CHUNK + end + 1, :], (CHUNK, RWKV_W))
         for c in range(CHUNKS_PER_TILE)], axis=0)
    v_all = v.astype(BF16)
    r_t = r * jnp.exp(linc)
    a_t = (-kn * jnp.exp(linc - lw)).astype(BF16)
    e_neg = jnp.exp(-linc)
    b_t = (b * e_neg).astype(BF16)
    k_t = (kd * e_neg).astype(BF16)
    e_rem = jnp.exp(ltot - linc)
    b_p = (b * e_rem).astype(BF16)
    k_p = (kd * e_rem).astype(BF16)
    dec = jnp.exp(ltot)

    chunks = range(CHUNKS_PER_TILE)
    sls = [slice(c * CHUNK, (c + 1) * CHUNK) for c in chunks]
    hd = lambda m, x, dims: _hdot(m, x, pmask, dims)
    ar = [jnp.concatenate([a_t[sl], r_t[sl].astype(BF16)], axis=0) for sl in sls]
    m_b = [hd(ar[c], b_t[sls[c]], _NT) for c in chunks]
    m_k = [hd(ar[c], k_t[sls[c]], _NT) for c in chunks]
    a_ab = [m_b[c][:CHUNK] * strict for c in chunks]
    p_rb = [(m_b[c][CHUNK:] * incl).astype(BF16) for c in chunks]
    ak_rk = [jnp.concatenate([m_k[c][:CHUNK] * strict, m_k[c][CHUNK:] * incl], axis=0).astype(BF16)
             for c in chunks]

    d8 = [a_ab[c] * lvl[0] for c in chunks]
    tinv = [eye + d8[c] for c in chunks]
    p = [d8[c].astype(BF16) for c in chunks]
    p = [hd(p[c], p[c], _NN).astype(BF16) for c in chunks]
    pt = [hd(jnp.concatenate([p[c], tinv[c].astype(BF16)], axis=0), p[c], _NN) for c in chunks]
    p = [pt[c][:CHUNK].astype(BF16) for c in chunks]
    tinv = [tinv[c] + pt[c][CHUNK:] for c in chunks]
    tinv = [tinv[c] + hd(tinv[c].astype(BF16), p[c], _NN) for c in chunks]
    for s in range(1, INV_LEVELS + 1):
        tb = [tinv[c].astype(BF16) for c in chunks]
        tn = [hd(tb[c], (a_ab[c] * lvl[s]).astype(BF16), _NN).astype(BF16) for c in chunks]
        tinv = [tinv[c] + hd(tn[c], tb[c], _NN) for c in chunks]
    tinv = [tinv[c].astype(BF16) for c in chunks]

    m_v = [hd(ak_rk[c], v_all[sls[c]], _NN) for c in chunks]
    w = [hd(tinv[c], a_t[sls[c]], _NN) for c in chunks]
    u0 = [hd(tinv[c], m_v[c][:CHUNK], _NN) for c in chunks]
    qp = [r_t[sls[c]] + hd(p_rb[c], w[c], _NN) for c in chunks]
    y0 = [hd(p_rb[c], u0[c], _NN) + m_v[c][CHUNK:] for c in chunks]
    pr = [slice(p * PAIR_W, (p + 1) * PAIR_W) for p in range(N_PAIRS)]
    pmask_f = pmask.astype(F32)
    bk = [jnp.concatenate([b_p[sl], k_p[sl]], axis=0) for sl in sls]
    uv = [jnp.concatenate([u0[c].astype(BF16), v_all[sls[c]]], axis=0) for c in chunks]
    wb = [[_dg(w[c][:, q].astype(BF16), b_p[sls[c]][:, q], _TN) * pmask_f for q in pr] for c in chunks]
    gs = [[_dg(uv[c][:, q], bk[c][:, q], _TN) * pmask_f for q in pr] for c in chunks]
    qpb = [qp[c].astype(BF16) for c in chunks]
    decs = [dec[c * CHUNK:c * CHUNK + 1, :] for c in chunks]
    return qpb, y0, wb, gs, decs


def _scan_chain_step(c, s, prep, y_ref):
    qpb, y0, wb, gs, decs = prep
    pr = [slice(p * PAIR_W, (p + 1) * PAIR_W) for p in range(N_PAIRS)]
    sb = [s[p].astype(BF16) for p in range(N_PAIRS)]
    y_ref[c * CHUNK:(c + 1) * CHUNK, :] = y0[c] + jnp.concatenate(
        [_dg(qpb[c][:, pr[p]], sb[p], _NT) for p in range(N_PAIRS)], axis=1)
    return [s[p] * decs[c][:, pr[p]] + (_dg(sb[p], wb[c][p].astype(BF16), _NN) + gs[c][p]) for p in range(N_PAIRS)]


N_STREAMS = 2
STREAM_TILES = N_TILES // N_STREAMS
STREAM_CTX = CTX_TILES // N_STREAMS
SCAN_LANES = tuple((rev, st) for rev in (False, True) for st in range(N_STREAMS))
N_LANE_IN = 7
N_SCAN_CONSTS = 6


def _stream_tile(stream, pos):
    lat0 = CTX_TILES + stream * (STREAM_TILES - STREAM_CTX)
    return jnp.where(pos < STREAM_CTX, stream * STREAM_CTX + pos, lat0 + pos - STREAM_CTX)


def _lane_pos(reverse, k):
    return (STREAM_TILES - 1 - k) if reverse else k


def _scan_kernel(*refs):
    n_lanes = len(SCAN_LANES)
    lane_in = [refs[n * N_LANE_IN:(n + 1) * N_LANE_IN] for n in range(n_lanes)]
    cs_ref, strict_ref, incl_ref, lvl_ref, eye_ref, pmask_ref = refs[n_lanes * N_LANE_IN:n_lanes * N_LANE_IN + N_SCAN_CONSTS]
    outs = refs[n_lanes * N_LANE_IN + N_SCAN_CONSTS:n_lanes * N_LANE_IN + N_SCAN_CONSTS + 2 * n_lanes]
    lane_out = [outs[2 * n:2 * n + 2] for n in range(n_lanes)]
    lane_scr = refs[n_lanes * N_LANE_IN + N_SCAN_CONSTS + 2 * n_lanes:]

    k = pl.program_id(0)
    pmask_f = pmask_ref[...]
    pmask = pmask_f.astype(BF16)
    eye = eye_ref[...]

    def load_state(s0_ref, scr, is_ctx):
        for p in range(N_PAIRS):
            carried = jnp.concatenate([s0_ref[p * PAIR_W:(p + 1) * PAIR_W, :]] * 2, axis=1) * pmask_f
            scr[p] = jnp.where(is_ctx, 0.0, carried)

    for n, (reverse, stream) in enumerate(SCAN_LANES):
        pos = _lane_pos(reverse, k)
        j = (pos - STREAM_CTX) % LAT_TILES_PER_SEQ
        first = (pos < STREAM_CTX) | (j == (LAT_TILES_PER_SEQ - 1 if reverse else 0))
        pl.when(first)(functools.partial(load_state, lane_in[n][6], lane_scr[n], pos < STREAM_CTX))

    preps = []
    for n, (reverse, stream) in enumerate(SCAN_LANES):
        d = 1 if reverse else 0
        r_ref, v_ref, kn_ref, lw_ref, kd_ref, b_ref, _ = lane_in[n]
        preps.append(_scan_prepare(reverse, r_ref[...], v_ref[...], kn_ref[...], lw_ref[...], kd_ref[...], b_ref[...],
                                   cs_ref[d], strict_ref[d], incl_ref[d], lvl_ref[d], eye, pmask))
    states = [[lane_scr[n][p] for p in range(N_PAIRS)] for n in range(n_lanes)]
    for c in range(CHUNKS_PER_TILE):
        for n, (reverse, stream) in enumerate(SCAN_LANES):
            cc = CHUNKS_PER_TILE - 1 - c if reverse else c
            states[n] = _scan_chain_step(cc, states[n], preps[n], lane_out[n][0])
    for n in range(n_lanes):
        for p in range(N_PAIRS):
            lane_scr[n][p] = states[n][p]
            for h in range(2):
                blk = slice(h * HEAD_DIM, (h + 1) * HEAD_DIM)
                lane_out[n][1][2 * p + h] = states[n][p][blk, blk]


def _scan_call(r, v, kn, lw, kd, b, s0, consts, l):
    cs, strict, incl, lvl, eye, pmask = consts
    lat_seq_of = lambda t: jnp.maximum(t - CTX_TILES, 0) // LAT_TILES_PER_SEQ
    const2 = lambda k: (0, 0)
    const3 = lambda k: (0, 0, 0)
    in_specs, out_specs, args = [], [], []
    for reverse, stream in SCAN_LANES:
        d = 1 if reverse else 0
        tile = functools.partial(lambda rv, st, k: _stream_tile(st, _lane_pos(rv, k)), reverse, stream)
        pos = functools.partial(_lane_pos, reverse)
        tok = pl.BlockSpec((TM, RWKV_W), functools.partial(lambda tl, k: (tl(k), 0), tile))
        tok_d = pl.BlockSpec((None, TM, RWKV_W), functools.partial(lambda tl, dd, k: (dd, tl(k), 0), tile, d))
        s0_spec = pl.BlockSpec((None, None, None, RWKV_W, HEAD_DIM),
                               functools.partial(lambda tl, dd, k: (lat_seq_of(tl(k)), l, dd, 0, 0), tile, d))
        in_specs += [tok, tok, tok, tok_d, tok_d, tok_d, s0_spec]
        args += [r, v, kn, lw, kd, b, s0]
        out_specs += [
            pl.BlockSpec((TM, RWKV_W), functools.partial(lambda ps, k: (ps(k), 0), pos)),
            pl.BlockSpec((None, RWKV_HEADS, HEAD_DIM, HEAD_DIM),
                         functools.partial(lambda ps, k: (jnp.minimum(ps(k), STREAM_CTX), 0, 0, 0), pos)),
        ]
    in_specs += [
        pl.BlockSpec((2, TM, TM), const3),
        pl.BlockSpec((2, CHUNK, RWKV_W), const3),
        pl.BlockSpec((2, CHUNK, RWKV_W), const3),
        pl.BlockSpec((2, INV_LEVELS + 1, CHUNK, RWKV_W), lambda k: (0, 0, 0, 0)),
        pl.BlockSpec((CHUNK, RWKV_W), const2),
        pl.BlockSpec((PAIR_W, PAIR_W), const2),
    ]
    args += [cs, strict, incl, lvl, eye, pmask]
    y = jax.ShapeDtypeStruct((STREAM_TILES * TM, RWKV_W), F32)
    sfin = jax.ShapeDtypeStruct((STREAM_CTX + 1, RWKV_HEADS, HEAD_DIM, HEAD_DIM), F32)
    state = pltpu.VMEM((N_PAIRS, PAIR_W, PAIR_W), F32)
    outs = pl.pallas_call(
        _scan_kernel,
        grid=(STREAM_TILES,),
        in_specs=in_specs,
        out_specs=out_specs,
        out_shape=[y, sfin] * len(SCAN_LANES),
        scratch_shapes=[state] * len(SCAN_LANES),
        compiler_params=_cparams(("arbitrary",)),
        name="rwkv_scan",
    )(*args)
    return outs[0::2], outs[1::2]


def _scan_final_states(sfins_per_layer):
    return jnp.stack(
        [jnp.stack([jnp.concatenate([sfins[d * N_STREAMS + st][:STREAM_CTX] for st in range(N_STREAMS)], axis=0)
                    for d in range(2)], axis=1)
         for sfins in sfins_per_layer], axis=1)


def _scan_consts():
    t = np.arange(TM)
    same_chunk = (t[:, None] // CHUNK) == (t[None, :] // CHUNK)
    cs_f = same_chunk & (t[None, :] <= t[:, None])
    cs_b = same_chunk & (t[None, :] >= t[:, None])
    cs = np.stack([cs_f, cs_b]).astype(np.float32)
    q = np.arange(CHUNK)[:, None]
    kcol = np.arange(RWKV_W)[None, :] % CHUNK
    strict = np.stack([kcol < q, kcol > q]).astype(np.float32)
    incl = np.stack([kcol <= q, kcol >= q]).astype(np.float32)
    eye = (kcol == q).astype(np.float32)
    lvl = []
    for lower in (True, False):
        row_blk, col_blk = (q, kcol) if lower else (kcol, q)
        masks = [(q // INV_BASE == kcol // INV_BASE) & ((kcol < q) if lower else (kcol > q))]
        for s in range(INV_LEVELS):
            size = INV_BASE << s
            masks.append((q // (2 * size) == kcol // (2 * size)) & ((row_blk // size) % 2 == 1)
                         & ((col_blk // size) % 2 == 0))
        lvl.append(np.stack(masks))
    lvl = np.stack(lvl).astype(np.float32)
    return (jnp.asarray(cs, BF16), jnp.asarray(strict), jnp.asarray(incl), jnp.asarray(lvl),
            jnp.asarray(eye), jnp.asarray(_head_block_mask(PAIR_W)))


FNET_TT = 512
FNET_IN_ROWS = N_CTX


def _fnet_kernel(seq_len, n_in, *refs):
    z_refs = refs[:n_in]
    ch_ref, cl_ref, th_ref, o_ref, z_scr = refs[n_in:]
    b = pl.program_id(1)

    @pl.when((pl.program_id(0) == 0) & (b == 0))
    def _():
        per_in = FNET_IN_ROWS // seq_len
        for n, z_ref in enumerate(z_refs):
            zh, zl = _split2(z_ref[...])
            for half in range(2):
                ch = ch_ref[half]
                cl = cl_ref[half]
                zc = (_dg(zh, ch, _NN) + (_dg(zh, cl, _NN) + _dg(zl, ch, _NN))).astype(BF16)
                for s in range(per_in):
                    z_scr[n * per_in + s, half * seq_len:(half + 1) * seq_len, :] = zc[s * seq_len:(s + 1) * seq_len]

    o_ref[...] = _dg(th_ref[...], z_scr[b], _NN)


def _fnet_call(u_f, seq_len, n_seq, row_off, tables, name):
    ch, cl, th = tables
    tt = min(FNET_TT, seq_len)
    nt = seq_len // tt
    n_in = n_seq * seq_len // FNET_IN_ROWS
    in_off = row_off // FNET_IN_ROWS
    z_specs = [pl.BlockSpec((FNET_IN_ROWS, FNET_W), functools.partial(lambda n, j, b: (in_off + n, 0), n))
               for n in range(n_in)]
    return pl.pallas_call(
        functools.partial(_fnet_kernel, seq_len, n_in),
        grid=(nt, n_seq),
        in_specs=z_specs + [
            pl.BlockSpec((2, FNET_W, FNET_W), lambda j, b: (0, 0, 0)),
            pl.BlockSpec((2, FNET_W, FNET_W), lambda j, b: (0, 0, 0)),
            pl.BlockSpec((tt, 2 * seq_len), lambda j, b: (j, 0)),
        ],
        out_specs=pl.BlockSpec((tt, FNET_W), lambda j, b: (b * nt + j, 0)),
        out_shape=jax.ShapeDtypeStruct((n_seq * seq_len, FNET_W), F32),
        scratch_shapes=[pltpu.VMEM((n_seq, 2 * seq_len, FNET_W), BF16)],
        compiler_params=_cparams(("arbitrary", "arbitrary")),
        name=name,
    )(*([u_f] * n_in), ch, cl, th)


DFT_SPLIT = 32


def _dft_tables(seq_len):
    c = np.arange(FNET_W)
    ang_c = 2.0 * np.pi * ((c[:, None] % HEAD_DIM) * (c[None, :] % HEAD_DIM) % HEAD_DIM) / HEAD_DIM
    scale = 1.0 / np.sqrt(float(seq_len) * HEAD_DIM)
    blk = _head_block_mask(FNET_W)
    ch, cl = _split2(jnp.asarray(np.stack([np.cos(ang_c) * blk * scale, np.sin(ang_c) * blk * scale]), F32))

    n1 = seq_len // DFT_SPLIT
    f = np.arange(seq_len)
    a1 = 2.0 * np.pi * ((np.arange(n1)[:, None] * f[None, :]) % n1) / n1
    a2 = 2.0 * np.pi * ((np.arange(DFT_SPLIT)[:, None] * f[None, :]) % seq_len) / seq_len
    c1, s1 = jnp.asarray(np.cos(a1), F32)[:, None, :], jnp.asarray(np.sin(a1), F32)[:, None, :]
    c2, s2 = jnp.asarray(np.cos(a2), F32)[None, :, :], jnp.asarray(np.sin(a2), F32)[None, :, :]
    cos_t = (c1 * c2 - s1 * s2).reshape(seq_len, seq_len)
    sin_t = (s1 * c2 + c1 * s2).reshape(seq_len, seq_len)
    th = jnp.concatenate([cos_t, -sin_t], axis=1).astype(BF16)
    return ch, cl, th


def _merge_ffn_kernel(split, final, *refs):
    i = pl.program_id(0)
    n_x = 2 if split else 1
    x_in = _tile_of_split(i, *refs[:2], FFN_TM) if split else refs[0][...]
    (mod_ref, oac_ref, oal_ref, yf0_ref, yf1_ref, yb0_ref, yb1_ref, bonus_ref, g_ref, ofc_ref, ofl_ref,
     lng_ref, lnb_ref, avg_ref, nfg_ref, wo_ref, wi_ref, w2_ref) = refs[n_x:n_x + 18]
    tail = refs[n_x + 18:]
    is_ctx = i < N_CTX // FFN_TM
    o_attn = _tile_of_split(i, oac_ref, oal_ref, FFN_TM)
    o_fnet = _tile_of_split(i, ofc_ref, ofl_ref, FFN_TM)
    avg = avg_ref[...]
    y = jnp.where(_in_stream0(i, FFN_TM), yf0_ref[...] + yb0_ref[...], yf1_ref[...] + yb1_ref[...])
    yc = y - _dot_exact_rhs(y, avg)
    var = _dot_exact_rhs(yc * yc, avg)
    yn = yc * lax.rsqrt(var + GN_EPS) * lng_ref[...] + lnb_ref[...]
    o_rwkv = (yn + bonus_ref[...]) * g_ref[...]

    o = (_dg(o_attn.astype(BF16), wo_ref[0:ATTN_W, :], _NN)
         + _dg(o_rwkv.astype(BF16), wo_ref[ATTN_W:ATTN_W + RWKV_W, :], _NN)
         + _dg(o_fnet.astype(BF16), wo_ref[ATTN_W + RWKV_W:, :], _NN))
    g1 = mod_ref[:, 2 * D_MODEL:3 * D_MODEL]
    sh2 = mod_ref[:, 3 * D_MODEL:4 * D_MODEL]
    sc2 = mod_ref[:, 4 * D_MODEL:5 * D_MODEL]
    g2 = mod_ref[:, 5 * D_MODEL:6 * D_MODEL]
    x = x_in + g1 * o
    h = _rms(x, nfg_ref[...]) * (1.0 + sc2) + sh2
    gu = _dg(h.astype(BF16), wi_ref[...], _NN)
    gt = gu[:, 0:D_FF]
    act = gt * jax.nn.sigmoid(gt) * gu[:, D_FF:]
    x = x + g2 * _dg(act.astype(BF16), w2_ref[...], _NN)
    if not final:
        (o_ref,) = tail
        o_ref[...] = x
    else:
        fng_ref, oc_ref, ol_ref = tail
        y_out = _rms(x, fng_ref[...])

        @pl.when(is_ctx)
        def _():
            oc_ref[...] = y_out

        @pl.when(jnp.logical_not(is_ctx))
        def _():
            ol_ref[...] = y_out


def _in_stream0(m, tm):
    ctx_m = N_CTX // tm
    lat_ps = (N_TOK // tm - ctx_m) // N_STREAMS
    return (m < ctx_m // N_STREAMS) | ((m >= ctx_m) & (m < ctx_m + lat_ps))


def _stream_block(stream, tm):
    ctx_m = N_CTX // tm
    half = ctx_m // N_STREAMS
    lat_ps = (N_TOK // tm - ctx_m) // N_STREAMS
    if stream == 0:
        return lambda m: (jnp.where(m < half, m, jnp.where(m < ctx_m, half - 1,
                                    jnp.where(m < ctx_m + lat_ps, m - half, half + lat_ps - 1))), 0)
    return lambda m: (jnp.where(m < half, 0, jnp.where(m < ctx_m, m - half,
                                jnp.where(m < ctx_m + lat_ps, half - 1, m - half - lat_ps))), 0)


def _merge_ffn_call(xs, mods_l, oa_ctx, oa_lat, ys, bonus, g, of_ctx, of_lat, ln_g, ln_b, avg, nfg,
                    wo, wi, w2, l, final_g=None):
    final = final_g is not None
    split = isinstance(xs, tuple)
    xs = xs if split else (xs,)
    row = lambda i: (l, 0, 0)
    tile = lambda i: (i, 0)
    ctx_tile, lat_tile = _ctx_tile(FFN_TM), _lat_tile(FFN_TM)
    once = pl.Buffered(1)
    in_specs = _x_specs(split, FFN_TM) + [
        pl.BlockSpec((None, 1, 6 * D_MODEL), lambda i: (_mod_row(i, FFN_TM), 0, 0)),
        pl.BlockSpec((FFN_TM, ATTN_W), ctx_tile),
        pl.BlockSpec((FFN_TM, ATTN_W), lat_tile),
        pl.BlockSpec((FFN_TM, RWKV_W), _stream_block(0, FFN_TM)),
        pl.BlockSpec((FFN_TM, RWKV_W), _stream_block(1, FFN_TM)),
        pl.BlockSpec((FFN_TM, RWKV_W), _stream_block(0, FFN_TM)),
        pl.BlockSpec((FFN_TM, RWKV_W), _stream_block(1, FFN_TM)),
        pl.BlockSpec((FFN_TM, RWKV_W), tile),
        pl.BlockSpec((FFN_TM, RWKV_W), tile),
        pl.BlockSpec((FFN_TM, FNET_W), ctx_tile),
        pl.BlockSpec((FFN_TM, FNET_W), lat_tile),
        pl.BlockSpec((None, 1, RWKV_W), row),
        pl.BlockSpec((None, 1, RWKV_W), row),
        pl.BlockSpec((RWKV_W, RWKV_W), lambda i: (0, 0)),
        pl.BlockSpec((None, 1, D_MODEL), row),
        pl.BlockSpec((None, D_MODEL, D_MODEL), row, pipeline_mode=once),
        pl.BlockSpec((None, D_MODEL, 2 * D_FF), row, pipeline_mode=once),
        pl.BlockSpec((None, D_FF, D_MODEL), row, pipeline_mode=once),
    ]
    args = [*xs, mods_l, oa_ctx, oa_lat, *ys, bonus, g, of_ctx, of_lat, ln_g, ln_b, avg, nfg, wo, wi, w2]
    if final:
        in_specs.append(pl.BlockSpec((1, D_MODEL), lambda i: (0, 0)))
        args.append(final_g)
        out_specs = [pl.BlockSpec((FFN_TM, D_MODEL), ctx_tile), pl.BlockSpec((FFN_TM, D_MODEL), lat_tile)]
        out_shape = [jax.ShapeDtypeStruct((N_CTX, D_MODEL), F32), jax.ShapeDtypeStruct((N_LAT, D_MODEL), F32)]
    else:
        out_specs = pl.BlockSpec((FFN_TM, D_MODEL), tile)
        out_shape = jax.ShapeDtypeStruct((N_TOK, D_MODEL), F32)
    return pl.pallas_call(
        functools.partial(_merge_ffn_kernel, split, final),
        grid=(N_TOK // FFN_TM,),
        in_specs=in_specs,
        out_specs=out_specs,
        out_shape=out_shape,
        compiler_params=_cparams(("arbitrary",)),
        name="merge_ffn_final" if final else "merge_ffn",
    )(*args)


def _block_diag2(w):
    z = jnp.zeros_like(w[:, 0])
    return jnp.concatenate([jnp.concatenate([w[:, 0], z], axis=2), jnp.concatenate([z, w[:, 1]], axis=2)], axis=1)


def kernel(x_prompt, x_sample, cache_k, cache_v, state_rwkv, c, c_ctx, w_ada, b_ada, norm_mix_g, norm_ffn_g,
           w_in, w_out, attn_sink, rwkv_shift_w, rwkv_w0, rwkv_w_up, rwkv_a0, rwkv_a_up, rwkv_g_up,
           rwkv_k_k, rwkv_k_a, rwkv_r_k, rwkv_ln_g, rwkv_ln_b, ffn_w_in, ffn_w_out, norm_final_g):
    x = (x_prompt.reshape(N_CTX, D_MODEL), x_sample.reshape(N_LAT, D_MODEL))
    cvec = jnp.concatenate([c_ctx[None, :], c, jnp.zeros((8 - 1 - N_LAT_SEQ, D_MODEL), F32)], axis=0)
    mods = _mods_call(cvec, w_ada, b_ada).reshape(DEPTH, 8, 1, 6 * D_MODEL)

    w_in_b = w_in.astype(BF16)
    w_out_b = w_out.astype(BF16)
    ffn_in_b = ffn_w_in.astype(BF16)
    ffn_out_b = ffn_w_out.astype(BF16)
    wup_bd = _block_diag2(rwkv_w_up)
    aup_bd = _block_diag2(rwkv_a_up)
    row3 = lambda a: a.reshape(DEPTH, 1, -1)
    nmg, nfg = row3(norm_mix_g), row3(norm_ffn_g)
    k_k, k_a, r_k = row3(rwkv_k_k), row3(rwkv_k_a), row3(rwkv_r_k)
    ln_g, ln_b = row3(rwkv_ln_g), row3(rwkv_ln_b)
    kc = cache_k.reshape(N_LAT_SEQ, DEPTH, PAST_LEN, KV_W)
    vc = cache_v.reshape(N_LAT_SEQ, DEPTH, PAST_LEN, KV_W)
    s0 = state_rwkv.reshape(N_LAT_SEQ, DEPTH, 2, RWKV_W, HEAD_DIM)

    cos, sin = _rope_tables()
    consts = _scan_consts()
    bmask = jnp.asarray(_head_block_mask(RWKV_W))
    ones_bd = bmask.astype(BF16)
    avg_bd = (bmask * (1.0 / HEAD_DIM)).astype(BF16)
    rwkv_params = (rwkv_shift_w, wup_bd, aup_bd, rwkv_g_up, rwkv_w0, rwkv_a0, k_k, k_a, r_k, ones_bd)
    dft_ctx = _dft_tables(CTX_LEN)
    dft_lat = _dft_tables(LAT_LEN)

    new_k, new_v, new_s = [], [], []
    for l in range(DEPTH):
        q, k, v, u_f, r, vv, kn, g, bonus, lw, kd, b = _inproj_call(x, mods[l], nmg, w_in_b, rwkv_params, l)
        new_k.append(k[:N_CTX].reshape(N_CTX_SEQ, CTX_LEN, KV_W))
        new_v.append(v[:N_CTX].reshape(N_CTX_SEQ, CTX_LEN, KV_W))

        oa_ctx = _ctx_attn_call(q, k, v, attn_sink, l)
        oa_lat = _lat_attn_call(q, k, v, kc, vc, cos, sin, attn_sink, l)

        ys, sfins = _scan_call(r, vv, kn, lw, kd, b, s0, consts, l)
        new_s.append(sfins)

        of_ctx = _fnet_call(u_f, CTX_LEN, N_CTX_SEQ, 0, dft_ctx, "fnet_context")
        of_lat = _fnet_call(u_f, LAT_LEN, N_LAT_SEQ, N_CTX, dft_lat, "fnet_latent")

        last = l == DEPTH - 1
        x = _merge_ffn_call(x, mods[l], oa_ctx, oa_lat, ys, bonus, g, of_ctx, of_lat, ln_g, ln_b, avg_bd, nfg,
                            w_out_b, ffn_in_b, ffn_out_b, l,
                            final_g=norm_final_g.reshape(1, D_MODEL) if last else None)

    y_ctx, y_lat = x
    return (y_ctx.reshape(N_CTX_SEQ, CTX_LEN, D_MODEL), y_lat.reshape(N_LAT_SEQ, LAT_LEN, D_MODEL),
            jnp.stack(new_k, axis=1).reshape(N_CTX_SEQ, DEPTH, CTX_LEN, N_KV_HEADS, HEAD_DIM),
            jnp.stack(new_v, axis=1).reshape(N_CTX_SEQ, DEPTH, CTX_LEN, N_KV_HEADS, HEAD_DIM),
            _scan_final_states(new_s))
```

```python
import functools

import numpy as np
import jax
import jax.numpy as jnp
from jax import lax
from jax.experimental import pallas as pl
from jax.experimental.pallas import tpu as pltpu

F32 = jnp.float32
BF16 = jnp.bfloat16

D_MODEL = 1024
N_CTX_SEQ = 16
CTX_LEN = 256
DEPTH = 4
N_LAT_SEQ = 4
LAT_LEN = 2048
PAST_LEN = 512
GRID_W = 64
HEAD_DIM = 64
N_Q_HEADS = 8
N_KV_HEADS = 2
Q_PER_KV = N_Q_HEADS // N_KV_HEADS
ATTN_W = N_Q_HEADS * HEAD_DIM
KV_W = N_KV_HEADS * HEAD_DIM
BLOCK = 128
ATTN_SCALE = 0.125
ROPE_THETA = 10000.0
NEG = -1e30
RWKV_HEADS = 4
RWKV_W = RWKV_HEADS * HEAD_DIM
DECAY_RANK = 64
ICLR_RANK = 64
GATE_RANK = 128
DECAY_SCALE = 0.6065306597126334
GN_EPS = 64e-5
RWKV_IN_W = 3 * RWKV_W + GATE_RANK + 2 * DECAY_RANK + 2 * ICLR_RANK
FNET_W = 4 * HEAD_DIM
IN_W = ATTN_W + 2 * KV_W + RWKV_IN_W + FNET_W
D_FF = 2816
RMS_EPS = 1e-6

N_CTX = N_CTX_SEQ * CTX_LEN
N_LAT = N_LAT_SEQ * LAT_LEN
N_TOK = N_CTX + N_LAT
TM = 256
N_TILES = N_TOK // TM
CTX_TILES = N_CTX // TM
LAT_TILES_PER_SEQ = LAT_LEN // TM
CHUNK = 64
CHUNKS_PER_TILE = TM // CHUNK
HALO = 8
FFN_TM = 512
INPROJ_TM = 512
LAT_ATTN_QBLOCKS = 4
LAT_ATTN_GROUP = 8

VMEM_LIMIT = 56 * 1024 * 1024


def _cparams(sem):
    return pltpu.CompilerParams(dimension_semantics=sem, vmem_limit_bytes=VMEM_LIMIT)


def _mod_row(i, tm=TM):
    return jnp.where(i < N_CTX // tm, 0, 1 + (i - N_CTX // tm) // (LAT_LEN // tm))


def _split2(x):
    hi = x.astype(BF16)
    lo = (x - hi.astype(F32)).astype(BF16)
    return hi, lo


_NN = (((1,), (0,)), ((), ()))
_NT = (((1,), (1,)), ((), ()))


def _dg(a, b, dims):
    return lax.dot_general(a, b, dims, preferred_element_type=F32)


def _dot1(a, b, dims=_NN):
    return _dg(a.astype(BF16), b.astype(BF16), dims)


def _dot_exact_rhs(a, b_bf16):
    ah, al = _split2(a)
    return _dg(ah, b_bf16, _NN) + _dg(al, b_bf16, _NN)


def _dot_exact_lhs(a_bf16, b):
    bh, bl = _split2(b)
    return _dg(a_bf16, bh, _NN) + _dg(a_bf16, bl, _NN)


def _mods_kernel(c_ref, w_ref, b_ref, o_ref):
    c = c_ref[...]
    s = c * jax.nn.sigmoid(c)
    o_ref[...] = _dot1(s, w_ref[...]) + b_ref[...]


def _mods_call(cvec, w_ada, b_ada):
    tn = 1536
    return pl.pallas_call(
        _mods_kernel,
        grid=(DEPTH, 6 * D_MODEL // tn),
        in_specs=[
            pl.BlockSpec((8, D_MODEL), lambda l, j: (0, 0)),
            pl.BlockSpec((None, D_MODEL, tn), lambda l, j: (l, 0, j)),
            pl.BlockSpec((None, 1, tn), lambda l, j: (l, 0, j)),
        ],
        out_specs=pl.BlockSpec((None, 8, tn), lambda l, j: (l, 0, j)),
        out_shape=jax.ShapeDtypeStruct((DEPTH, 8, 6 * D_MODEL), F32),
        compiler_params=_cparams(("parallel", "parallel")),
        name="adaln_mods",
    )(cvec, w_ada, b_ada.reshape(DEPTH, 1, 6 * D_MODEL))


def _rms(x, g):
    return x * lax.rsqrt(jnp.mean(x * x, axis=-1, keepdims=True) + RMS_EPS) * g


def _tile_of_split(i, ctx_ref, lat_ref, tm=TM):
    return jnp.where(i < N_CTX // tm, ctx_ref[...], lat_ref[...])


def _ctx_tile(tm=TM):
    return lambda i: (jnp.minimum(i, N_CTX // tm - 1), 0)


def _lat_tile(tm=TM):
    return lambda i: (jnp.maximum(i - N_CTX // tm, 0), 0)


def _x_specs(split, tm=TM):
    if split:
        return [pl.BlockSpec((tm, D_MODEL), _ctx_tile(tm)), pl.BlockSpec((tm, D_MODEL), _lat_tile(tm))]
    return [pl.BlockSpec((tm, D_MODEL), lambda i: (i, 0))]


def _head_block_mask(n):
    r = np.arange(n)[:, None] // HEAD_DIM
    c = np.arange(n)[None, :] // HEAD_DIM
    return (r == c).astype(np.float32)


def _rwkv_token_terms(u, prev_row, next_row, seam, sw_ref, wup_ref, aup_ref, gup_ref, w0_ref, a0_ref,
                      kk_ref, ka_ref, rk_ref, ones_ref, r_out, v_out, kn_out, g_out, bonus_out, lw_out, kd_out, b_out):
    n = u.shape[0]
    row = lax.broadcasted_iota(jnp.int32, u.shape, 0)
    u_dn = jnp.where(row == 0, prev_row, pltpu.roll(u, 1, 0))
    u_up = jnp.where(row == n - 1, next_row, pltpu.roll(u, n - 1, 0))
    u_dn = jnp.where(seam & (row == n // 2), 0.0, u_dn)
    u_up = jnp.where(seam & (row == n // 2 - 1), 0.0, u_up)
    us = u_dn * sw_ref[0:1, :] + u * sw_ref[1:2, :] + u_up * sw_ref[2:3, :]

    r = us[:, 0:RWKV_W]
    k = us[:, RWKV_W:2 * RWKV_W]
    v = us[:, 2 * RWKV_W:3 * RWKV_W]
    o = 3 * RWKV_W
    gd = us[:, o:o + GATE_RANK]
    wd = us[:, o + GATE_RANK:o + GATE_RANK + 2 * DECAY_RANK]
    ad = us[:, o + GATE_RANK + 2 * DECAY_RANK:RWKV_IN_W]

    w_pre = _dot1(jnp.tanh(wd), wup_ref[...])
    a_pre = _dot1(ad, aup_ref[...])
    g_out[...] = _dot1(jax.nn.sigmoid(gd), gup_ref[...])

    ones_bd = ones_ref[...]
    kn = k * kk_ref[...]
    kn = kn * lax.rsqrt(_dot_exact_rhs(kn * kn, ones_bd) + 1e-12)
    r_out[...] = r
    v_out[...] = v
    kn_out[...] = kn
    bonus_out[...] = _dot_exact_rhs(r * k * rk_ref[...], ones_bd) * v
    ka = ka_ref[...]
    for d in range(2):
        sl = slice(d * RWKV_W, (d + 1) * RWKV_W)
        a_d = jax.nn.sigmoid(a0_ref[d:d + 1, :] + a_pre[:, sl])
        lw_out[d] = -DECAY_SCALE * jax.nn.sigmoid(w0_ref[d:d + 1, :] + w_pre[:, sl])
        kd_out[d] = k * (1.0 + (a_d - 1.0) * ka)
        b_out[d] = kn * a_d


RWKV_COL0 = ATTN_W + 2 * KV_W
N_RWKV_PARAMS = 10


def _inproj_kernel(split, *refs):
    n_x = 2 if split else 1
    i = pl.program_id(0)
    x = _tile_of_split(i, *refs[:2], INPROJ_TM) if split else refs[0][...]
    xp_ref, xn_ref, mod_ref, g_ref, w_ref = refs[n_x:n_x + 5]
    rwkv_params = refs[n_x + 5:n_x + 5 + N_RWKV_PARAMS]
    q_ref, k_ref, v_ref, uf_ref = refs[n_x + 5 + N_RWKV_PARAMS:n_x + 9 + N_RWKV_PARAMS]
    rwkv_outs = refs[n_x + 9 + N_RWKV_PARAMS:]

    shift = mod_ref[:, 0:D_MODEL]
    scale = mod_ref[:, D_MODEL:2 * D_MODEL]
    x_all = jnp.concatenate([x, xp_ref[...], xn_ref[...]], axis=0)
    h = (_rms(x_all, g_ref[...]) * (1.0 + scale) + shift).astype(BF16)
    u_all = _dg(h, w_ref[:, RWKV_COL0:RWKV_COL0 + RWKV_IN_W], _NN)
    o_qkv = _dg(h[:INPROJ_TM], w_ref[:, 0:RWKV_COL0], _NN)
    q_ref[...] = o_qkv[:, 0:ATTN_W]
    k_ref[...] = o_qkv[:, ATTN_W:ATTN_W + KV_W]
    v_ref[...] = o_qkv[:, ATTN_W + KV_W:RWKV_COL0]
    uf_ref[...] = _dg(h[:INPROJ_TM], w_ref[:, RWKV_COL0 + RWKV_IN_W:IN_W], _NN)

    ctx_tiles = N_CTX // INPROJ_TM
    per_seq = LAT_LEN // INPROJ_TM
    is_lat = i >= ctx_tiles
    j = (i - ctx_tiles) % per_seq
    has_prev = jnp.where(is_lat & (j != 0), 1.0, 0.0)
    has_next = jnp.where(is_lat & (j != per_seq - 1), 1.0, 0.0)
    prev_row = u_all[INPROJ_TM + HALO - 1:INPROJ_TM + HALO, :] * has_prev
    next_row = u_all[INPROJ_TM + HALO:INPROJ_TM + HALO + 1, :] * has_next
    seam = jnp.logical_not(is_lat) if INPROJ_TM == 2 * CTX_LEN else False
    _rwkv_token_terms(u_all[:INPROJ_TM], prev_row, next_row, seam, *rwkv_params, *rwkv_outs)


def _inproj_call(xs, mods_l, g, w_bf16, rwkv_params, l):
    split = isinstance(xs, tuple)
    xs = xs if split else (xs,)
    hb = INPROJ_TM // HALO
    ctx_tiles = N_CTX // INPROJ_TM
    if split:
        halo_src = xs[1]
        last = N_LAT // HALO - 1
        prev_blk = lambda i: (jnp.clip((i - ctx_tiles) * hb - 1, 0, last), 0)
        next_blk = lambda i: (jnp.clip((i - ctx_tiles + 1) * hb, 0, last), 0)
    else:
        halo_src = xs[0]
        last = N_TOK // HALO - 1
        prev_blk = lambda i: (jnp.maximum(i * hb - 1, 0), 0)
        next_blk = lambda i: (jnp.minimum((i + 1) * hb, last), 0)
    row = lambda i: (l, 0, 0)
    widths = (ATTN_W, KV_W, KV_W, FNET_W)
    tok = jax.ShapeDtypeStruct((N_TOK, RWKV_W), F32)
    tok2 = jax.ShapeDtypeStruct((2, N_TOK, RWKV_W), F32)
    return pl.pallas_call(
        functools.partial(_inproj_kernel, split),
        grid=(N_TOK // INPROJ_TM,),
        in_specs=_x_specs(split, INPROJ_TM) + [
            pl.BlockSpec((HALO, D_MODEL), prev_blk),
            pl.BlockSpec((HALO, D_MODEL), next_blk),
            pl.BlockSpec((None, 1, 6 * D_MODEL), lambda i: (_mod_row(i, INPROJ_TM), 0, 0)),
            pl.BlockSpec((None, 1, D_MODEL), row),
            pl.BlockSpec((None, D_MODEL, IN_W), row),
            pl.BlockSpec((None, 3, RWKV_IN_W), row),
            pl.BlockSpec((None, 2 * DECAY_RANK, 2 * RWKV_W), row),
            pl.BlockSpec((None, 2 * ICLR_RANK, 2 * RWKV_W), row),
            pl.BlockSpec((None, GATE_RANK, RWKV_W), row),
            pl.BlockSpec((None, 2, RWKV_W), row),
            pl.BlockSpec((None, 2, RWKV_W), row),
            pl.BlockSpec((None, 1, RWKV_W), row),
            pl.BlockSpec((None, 1, RWKV_W), row),
            pl.BlockSpec((None, 1, RWKV_W), row),
            pl.BlockSpec((RWKV_W, RWKV_W), lambda i: (0, 0)),
        ],
        out_specs=[pl.BlockSpec((INPROJ_TM, w), lambda i: (i, 0)) for w in widths]
        + [pl.BlockSpec((INPROJ_TM, RWKV_W), lambda i: (i, 0))] * 5
        + [pl.BlockSpec((2, INPROJ_TM, RWKV_W), lambda i: (0, i, 0))] * 3,
        out_shape=[jax.ShapeDtypeStruct((N_TOK, w), F32) for w in widths] + [tok] * 5 + [tok2] * 3,
        compiler_params=_cparams(("arbitrary",)),
        name="in_projection",
    )(*xs, halo_src, halo_src, mods_l, g, w_bf16, *rwkv_params)


def _ctx_attn_kernel(l, sink_ref, q_ref, k_ref, v_ref, o_ref):
    q = q_ref[...]
    k = k_ref[...]
    v = v_ref[...]
    half = CTX_LEN // 2
    pieces = [(h, r) for h in range(N_Q_HEADS) for r in range(2)]
    kb = k.astype(BF16)
    vb = v.astype(BF16)
    kv = lambda a, h: a[:, (h // Q_PER_KV) * HEAD_DIM:(h // Q_PER_KV + 1) * HEAD_DIM]
    qs = [(q[r * half:(r + 1) * half, h * HEAD_DIM:(h + 1) * HEAD_DIM] * ATTN_SCALE).astype(BF16) for h, r in pieces]
    s = [_dg(qs[n], kv(kb, h), _NT) for n, (h, r) in enumerate(pieces)]
    m = [jnp.maximum(jnp.max(s[n], axis=-1, keepdims=True), sink_ref[l, h]) for n, (h, r) in enumerate(pieces)]
    e = [jnp.exp(s[n] - m[n]) for n in range(len(pieces))]
    den = [jnp.sum(e[n], axis=-1, keepdims=True) + jnp.exp(sink_ref[l, h] - m[n]) for n, (h, r) in enumerate(pieces)]
    o = [_dg(e[n].astype(BF16), kv(vb, h), _NN) / den[n] for n, (h, r) in enumerate(pieces)]
    for n, (h, r) in enumerate(pieces):
        o_ref[r * half:(r + 1) * half, h * HEAD_DIM:(h + 1) * HEAD_DIM] = o[n]


def _ctx_attn_call(q, k, v, sink, l):
    return pl.pallas_call(
        functools.partial(_ctx_attn_kernel, l),
        grid=(N_CTX_SEQ,),
        in_specs=[
            pl.BlockSpec(memory_space=pltpu.SMEM),
            pl.BlockSpec((CTX_LEN, ATTN_W), lambda b: (b, 0)),
            pl.BlockSpec((CTX_LEN, KV_W), lambda b: (b, 0)),
            pl.BlockSpec((CTX_LEN, KV_W), lambda b: (b, 0)),
        ],
        out_specs=pl.BlockSpec((CTX_LEN, ATTN_W), lambda b: (b, 0)),
        out_shape=jax.ShapeDtypeStruct((N_CTX, ATTN_W), F32),
        compiler_params=_cparams(("parallel",)),
        name="context_attention",
    )(sink, q, k, v)


def _rope(x, cos, sin):
    lane = lax.broadcasted_iota(jnp.int32, x.shape, 1)
    partner = jnp.where((lane & 16) == 0, pltpu.roll(x, 128 - 16, 1), pltpu.roll(x, 16, 1))
    return x * cos + partner * sin


def _lat_attn_kernel(l, sink_ref, q_ref, k_ref, v_ref, kc_ref, vc_ref, cos_ref, sin_ref, o_ref):
    nq = LAT_ATTN_QBLOCKS
    nb = LAT_LEN // BLOCK
    n0 = pl.program_id(1) * nq
    r0 = pl.multiple_of(n0 * BLOCK, BLOCK)
    cos_q = cos_ref[pl.ds(r0, nq * BLOCK), :]
    sin_q = sin_ref[pl.ds(r0, nq * BLOCK), :]
    q = jnp.concatenate(
        [_rope(q_ref[:, j * 128:(j + 1) * 128], cos_q, sin_q) for j in range(ATTN_W // 128)], axis=1)

    kr, vr = [], []
    for dj in range(-1, nq + 1):
        rj = pl.multiple_of(jnp.clip(n0 + dj, 0, nb - 1) * BLOCK, BLOCK)
        kr.append(_rope(k_ref[pl.ds(rj, BLOCK), :], cos_ref[pl.ds(rj, BLOCK), :],
                        sin_ref[pl.ds(rj, BLOCK), :]).astype(BF16))
        vr.append(v_ref[pl.ds(rj, BLOCK), :].astype(BF16))
    kc = kc_ref[...].astype(BF16)
    vc = vc_ref[...].astype(BF16)
    k_all = [jnp.concatenate(kr[t:t + 3] + [kc], axis=0) for t in range(nq)]
    v_all = [jnp.concatenate(vr[t:t + 3] + [vc], axis=0) for t in range(nq)]

    kj = lax.broadcasted_iota(jnp.int32, (BLOCK, BLOCK), 1)
    qi = lax.broadcasted_iota(jnp.int32, (BLOCK, BLOCK), 0)
    keep_prev = [(kj >= qi) & (n0 + t > 0) for t in range(nq)]
    keep_next = [(kj <= qi) & (n0 + t < nb - 1) for t in range(nq)]
    kv_cols = lambda h: slice((h // Q_PER_KV) * HEAD_DIM, (h // Q_PER_KV + 1) * HEAD_DIM)

    all_pieces = [(t, h) for t in range(nq) for h in range(N_Q_HEADS)]
    for g0 in range(0, len(all_pieces), LAT_ATTN_GROUP):
        pieces = all_pieces[g0:g0 + LAT_ATTN_GROUP]
        rng = range(len(pieces))
        qs = [(q[t * BLOCK:(t + 1) * BLOCK, h * HEAD_DIM:(h + 1) * HEAD_DIM] * ATTN_SCALE).astype(BF16)
              for t, h in pieces]
        s = [_dg(qs[i], k_all[t][:, kv_cols(h)], _NT) for i, (t, h) in enumerate(pieces)]
        s = [jnp.concatenate([jnp.where(keep_prev[t], s[i][:, 0:BLOCK], NEG), s[i][:, BLOCK:2 * BLOCK],
                              jnp.where(keep_next[t], s[i][:, 2 * BLOCK:3 * BLOCK], NEG), s[i][:, 3 * BLOCK:]],
                             axis=1) for i, (t, h) in enumerate(pieces)]
        m = [jnp.maximum(jnp.max(s[i], axis=-1, keepdims=True), sink_ref[l, h]) for i, (t, h) in enumerate(pieces)]
        e = [jnp.exp(s[i] - m[i]) for i in rng]
        den = [jnp.sum(e[i], axis=-1, keepdims=True) + jnp.exp(sink_ref[l, h] - m[i])
               for i, (t, h) in enumerate(pieces)]
        o = [_dg(e[i].astype(BF16), v_all[t][:, kv_cols(h)], _NN) / den[i] for i, (t, h) in enumerate(pieces)]
        for i, (t, h) in enumerate(pieces):
            o_ref[t * BLOCK:(t + 1) * BLOCK, h * HEAD_DIM:(h + 1) * HEAD_DIM] = o[i]


def _lat_attn_call(q, k, v, kc, vc, cos, sin, sink, l):
    rows = LAT_ATTN_QBLOCKS * BLOCK
    steps = LAT_LEN // rows
    q_off = N_CTX // rows
    kv_off = N_CTX // LAT_LEN
    return pl.pallas_call(
        functools.partial(_lat_attn_kernel, l),
        grid=(N_LAT_SEQ, steps),
        in_specs=[
            pl.BlockSpec(memory_space=pltpu.SMEM),
            pl.BlockSpec((rows, ATTN_W), lambda b, n: (q_off + b * steps + n, 0)),
            pl.BlockSpec((LAT_LEN, KV_W), lambda b, n: (kv_off + b, 0)),
            pl.BlockSpec((LAT_LEN, KV_W), lambda b, n: (kv_off + b, 0)),
            pl.BlockSpec((None, None, PAST_LEN, KV_W), lambda b, n: (b, l, 0, 0)),
            pl.BlockSpec((None, None, PAST_LEN, KV_W), lambda b, n: (b, l, 0, 0)),
            pl.BlockSpec((LAT_LEN, 128), lambda b, n: (0, 0)),
            pl.BlockSpec((LAT_LEN, 128), lambda b, n: (0, 0)),
        ],
        out_specs=pl.BlockSpec((rows, ATTN_W), lambda b, n: (b * steps + n, 0)),
        out_shape=jax.ShapeDtypeStruct((N_LAT, ATTN_W), F32),
        compiler_params=_cparams(("parallel", "arbitrary")),
        name="latent_attention",
    )(sink, q, k, v, kc, vc, cos, sin)


def _rope_tables():
    t = np.arange(LAT_LEN)
    lane = np.arange(128)
    d = lane % HEAD_DIM
    pos = np.where(d[None, :] < HEAD_DIM // 2, (t // GRID_W)[:, None], (t % GRID_W)[:, None]).astype(np.float32)
    quarter = HEAD_DIM // 4
    inv = (np.float32(ROPE_THETA) ** (-(np.arange(quarter, dtype=np.float32)) / np.float32(quarter))).astype(np.float32)
    ang = pos * inv[d % quarter][None, :]
    sign = np.where((lane & 16) == 0, -1.0, 1.0)[None, :]
    return jnp.asarray(np.cos(ang), F32), jnp.asarray(np.sin(ang) * sign, F32)


PAIR_W = 2 * HEAD_DIM
INV_BASE = 8
INV_LEVELS = 3
N_PAIRS = RWKV_HEADS // 2
_TN = (((0,), (0,)), ((), ()))


def _pair_stack(x, pmask):
    return jnp.concatenate([x.astype(BF16)] * 2, axis=0) * pmask


def _hdot(m, x, pmask, dims):
    return jnp.concatenate(
        [_dg(m[:, p * PAIR_W:(p + 1) * PAIR_W], _pair_stack(x[:, p * PAIR_W:(p + 1) * PAIR_W], pmask), dims)
         for p in range(N_PAIRS)], axis=1)


def _scan_prepare(reverse, r, v, kn, lw, kd, b, cs, strict, incl, lvl, eye, pmask):
    linc = _dot_exact_lhs(cs, lw)
    end = 0 if reverse else CHUNK - 1
    ltot = jnp.concatenate(
        [jnp.broadcast_to(linc[c * CHUNK + end:c * CHUNK + end + 1, :], (CHUNK, RWKV_W))
         for c in range(CHUNKS_PER_TILE)], axis=0)
    v_all = v.astype(BF16)
    r_t = r * jnp.exp(linc)
    a_t = (-kn * jnp.exp(linc - lw)).astype(BF16)
    e_neg = jnp.exp(-linc)
    b_t = (b * e_neg).astype(BF16)
    k_t = (kd * e_neg).astype(BF16)
    e_rem = jnp.exp(ltot - linc)
    b_p = (b * e_rem).astype(BF16)
    k_p = (kd * e_rem).astype(BF16)
    dec = jnp.exp(ltot)

    chunks = range(CHUNKS_PER_TILE)
    sls = [slice(c * CHUNK, (c + 1) * CHUNK) for c in chunks]
    hd = lambda m, x, dims: _hdot(m, x, pmask, dims)
    ar = [jnp.concatenate([a_t[sl], r_t[sl].astype(BF16)], axis=0) for sl in sls]
    m_b = [hd(ar[c], b_t[sls[c]], _NT) for c in chunks]
    m_k = [hd(ar[c], k_t[sls[c]], _NT) for c in chunks]
    a_ab = [m_b[c][:CHUNK] * strict for c in chunks]
    p_rb = [(m_b[c][CHUNK:] * incl).astype(BF16) for c in chunks]
    ak_rk = [jnp.concatenate([m_k[c][:CHUNK] * strict, m_k[c][CHUNK:] * incl], axis=0).astype(BF16)
             for c in chunks]

    d8 = [a_ab[c] * lvl[0] for c in chunks]
    tinv = [eye + d8[c] for c in chunks]
    p = [d8[c].astype(BF16) for c in chunks]
    p = [hd(p[c], p[c], _NN).astype(BF16) for c in chunks]
    pt = [hd(jnp.concatenate([p[c], tinv[c].astype(BF16)], axis=0), p[c], _NN) for c in chunks]
    p = [pt[c][:CHUNK].astype(BF16) for c in chunks]
    tinv = [tinv[c] + pt[c][CHUNK:] for c in chunks]
    tinv = [tinv[c] + hd(tinv[c].astype(BF16), p[c], _NN) for c in chunks]
    for s in range(1, INV_LEVELS + 1):
        tb = [tinv[c].astype(BF16) for c in chunks]
        tn = [hd(tb[c], (a_ab[c] * lvl[s]).astype(BF16), _NN).astype(BF16) for c in chunks]
        tinv = [tinv[c] + hd(tn[c], tb[c], _NN) for c in chunks]
    tinv = [tinv[c].astype(BF16) for c in chunks]

    m_v = [hd(ak_rk[c], v_all[sls[c]], _NN) for c in chunks]
    w = [hd(tinv[c], a_t[sls[c]], _NN) for c in chunks]
    u0 = [hd(tinv[c], m_v[c][:CHUNK], _NN) for c in chunks]
    qp = [r_t[sls[c]] + hd(p_rb[c], w[c], _NN) for c in chunks]
    y0 = [hd(p_rb[c], u0[c], _NN) + m_v[c][CHUNK:] for c in chunks]
    pr = [slice(p * PAIR_W, (p + 1) * PAIR_W) for p in range(N_PAIRS)]
    pmask_f = pmask.astype(F32)
    bk = [jnp.concatenate([b_p[sl], k_p[sl]], axis=0) for sl in sls]
    uv = [jnp.concatenate([u0[c].astype(BF16), v_all[sls[c]]], axis=0) for c in chunks]
    wb = [[_dg(w[c][:, q].astype(BF16), b_p[sls[c]][:, q], _TN) * pmask_f for q in pr] for c in chunks]
    gs = [[_dg(uv[c][:, q], bk[c][:, q], _TN) * pmask_f for q in pr] for c in chunks]
    qpb = [qp[c].astype(BF16) for c in chunks]
    decs = [dec[c * CHUNK:c * CHUNK + 1, :] for c in chunks]
    return qpb, y0, wb, gs, decs


def _scan_chain_step(c, s, prep, y_ref):
    qpb, y0, wb, gs, decs = prep
    pr = [slice(p * PAIR_W, (p + 1) * PAIR_W) for p in range(N_PAIRS)]
    sb = [s[p].astype(BF16) for p in range(N_PAIRS)]
    y_ref[c * CHUNK:(c + 1) * CHUNK, :] = y0[c] + jnp.concatenate(
        [_dg(qpb[c][:, pr[p]], sb[p], _NT) for p in range(N_PAIRS)], axis=1)
    return [s[p] * decs[c][:, pr[p]] + (_dg(sb[p], wb[c][p].astype(BF16), _NN) + gs[c][p]) for p in range(N_PAIRS)]


N_STREAMS = 2
STREAM_TILES = N_TILES // N_STREAMS
STREAM_CTX = CTX_TILES // N_STREAMS
SCAN_LANES = tuple((rev, st) for rev in (False, True) for st in range(N_STREAMS))
N_LANE_IN = 7
N_SCAN_CONSTS = 6


def _stream_tile(stream, pos):
    lat0 = CTX_TILES + stream * (STREAM_TILES - STREAM_CTX)
    return jnp.where(pos < STREAM_CTX, stream * STREAM_CTX + pos, lat0 + pos - STREAM_CTX)


def _lane_pos(reverse, k):
    return (STREAM_TILES - 1 - k) if reverse else k


def _scan_kernel(*refs):
    n_lanes = len(SCAN_LANES)
    lane_in = [refs[n * N_LANE_IN:(n + 1) * N_LANE_IN] for n in range(n_lanes)]
    cs_ref, strict_ref, incl_ref, lvl_ref, eye_ref, pmask_ref = refs[n_lanes * N_LANE_IN:n_lanes * N_LANE_IN + N_SCAN_CONSTS]
    outs = refs[n_lanes * N_LANE_IN + N_SCAN_CONSTS:n_lanes * N_LANE_IN + N_SCAN_CONSTS + 2 * n_lanes]
    lane_out = [outs[2 * n:2 * n + 2] for n in range(n_lanes)]
    lane_scr = refs[n_lanes * N_LANE_IN + N_SCAN_CONSTS + 2 * n_lanes:]

    k = pl.program_id(0)
    pmask_f = pmask_ref[...]
    pmask = pmask_f.astype(BF16)
    eye = eye_ref[...]

    def load_state(s0_ref, scr, is_ctx):
        for p in range(N_PAIRS):
            carried = jnp.concatenate([s0_ref[p * PAIR_W:(p + 1) * PAIR_W, :]] * 2, axis=1) * pmask_f
            scr[p] = jnp.where(is_ctx, 0.0, carried)

    for n, (reverse, stream) in enumerate(SCAN_LANES):
        pos = _lane_pos(reverse, k)
        j = (pos - STREAM_CTX) % LAT_TILES_PER_SEQ
        first = (pos < STREAM_CTX) | (j == (LAT_TILES_PER_SEQ - 1 if reverse else 0))
        pl.when(first)(functools.partial(load_state, lane_in[n][6], lane_scr[n], pos < STREAM_CTX))

    preps = []
    for n, (reverse, stream) in enumerate(SCAN_LANES):
        d = 1 if reverse else 0
        r_ref, v_ref, kn_ref, lw_ref, kd_ref, b_ref, _ = lane_in[n]
        preps.append(_scan_prepare(reverse, r_ref[...], v_ref[...], kn_ref[...], lw_ref[...], kd_ref[...], b_ref[...],
                                   cs_ref[d], strict_ref[d], incl_ref[d], lvl_ref[d], eye, pmask))
    states = [[lane_scr[n][p] for p in range(N_PAIRS)] for n in range(n_lanes)]
    for c in range(CHUNKS_PER_TILE):
        for n, (reverse, stream) in enumerate(SCAN_LANES):
            cc = CHUNKS_PER_TILE - 1 - c if reverse else c
            states[n] = _scan_chain_step(cc, states[n], preps[n], lane_out[n][0])
    for n in range(n_lanes):
        for p in range(N_PAIRS):
            lane_scr[n][p] = states[n][p]
            for h in range(2):
                blk = slice(h * HEAD_DIM, (h + 1) * HEAD_DIM)
                lane_out[n][1][2 * p + h] = states[n][p][blk, blk]


def _scan_call(r, v, kn, lw, kd, b, s0, consts, l):
    cs, strict, incl, lvl, eye, pmask = consts
    lat_seq_of = lambda t: jnp.maximum(t - CTX_TILES, 0) // LAT_TILES_PER_SEQ
    const2 = lambda k: (0, 0)
    const3 = lambda k: (0, 0, 0)
    in_specs, out_specs, args = [], [], []
    for reverse, stream in SCAN_LANES:
        d = 1 if reverse else 0
        tile = functools.partial(lambda rv, st, k: _stream_tile(st, _lane_pos(rv, k)), reverse, stream)
        pos = functools.partial(_lane_pos, reverse)
        tok = pl.BlockSpec((TM, RWKV_W), functools.partial(lambda tl, k: (tl(k), 0), tile))
        tok_d = pl.BlockSpec((None, TM, RWKV_W), functools.partial(lambda tl, dd, k: (dd, tl(k), 0), tile, d))
        s0_spec = pl.BlockSpec((None, None, None, RWKV_W, HEAD_DIM),
                               functools.partial(lambda tl, dd, k: (lat_seq_of(tl(k)), l, dd, 0, 0), tile, d))
        in_specs += [tok, tok, tok, tok_d, tok_d, tok_d, s0_spec]
        args += [r, v, kn, lw, kd, b, s0]
        out_specs += [
            pl.BlockSpec((TM, RWKV_W), functools.partial(lambda ps, k: (ps(k), 0), pos)),
            pl.BlockSpec((None, RWKV_HEADS, HEAD_DIM, HEAD_DIM),
                         functools.partial(lambda ps, k: (jnp.minimum(ps(k), STREAM_CTX), 0, 0, 0), pos)),
        ]
    in_specs += [
        pl.BlockSpec((2, TM, TM), const3),
        pl.BlockSpec((2, CHUNK, RWKV_W), const3),
        pl.BlockSpec((2, CHUNK, RWKV_W), const3),
        pl.BlockSpec((2, INV_LEVELS + 1, CHUNK, RWKV_W), lambda k: (0, 0, 0, 0)),
        pl.BlockSpec((CHUNK, RWKV_W), const2),
        pl.BlockSpec((PAIR_W, PAIR_W), const2),
    ]
    args += [cs, strict, incl, lvl, eye, pmask]
    y = jax.ShapeDtypeStruct((STREAM_TILES * TM, RWKV_W), F32)
    sfin = jax.ShapeDtypeStruct((STREAM_CTX + 1, RWKV_HEADS, HEAD_DIM, HEAD_DIM), F32)
    state = pltpu.VMEM((N_PAIRS, PAIR_W, PAIR_W), F32)
    outs = pl.pallas_call(
        _scan_kernel,
        grid=(STREAM_TILES,),
        in_specs=in_specs,
        out_specs=out_specs,
        out_shape=[y, sfin] * len(SCAN_LANES),
        scratch_shapes=[state] * len(SCAN_LANES),
        compiler_params=_cparams(("arbitrary",)),
        name="rwkv_scan",
    )(*args)
    return outs[0::2], outs[1::2]


def _scan_final_states(sfins_per_layer):
    return jnp.stack(
        [jnp.stack([jnp.concatenate([sfins[d * N_STREAMS + st][:STREAM_CTX] for st in range(N_STREAMS)], axis=0)
                    for d in range(2)], axis=1)
         for sfins in sfins_per_layer], axis=1)


def _scan_consts():
    t = np.arange(TM)
    same_chunk = (t[:, None] // CHUNK) == (t[None, :] // CHUNK)
    cs_f = same_chunk & (t[None, :] <= t[:, None])
    cs_b = same_chunk & (t[None, :] >= t[:, None])
    cs = np.stack([cs_f, cs_b]).astype(np.float32)
    q = np.arange(CHUNK)[:, None]
    kcol = np.arange(RWKV_W)[None, :] % CHUNK
    strict = np.stack([kcol < q, kcol > q]).astype(np.float32)
    incl = np.stack([kcol <= q, kcol >= q]).astype(np.float32)
    eye = (kcol == q).astype(np.float32)
    lvl = []
    for lower in (True, False):
        row_blk, col_blk = (q, kcol) if lower else (kcol, q)
        masks = [(q // INV_BASE == kcol // INV_BASE) & ((kcol < q) if lower else (kcol > q))]
        for s in range(INV_LEVELS):
            size = INV_BASE << s
            masks.append((q // (2 * size) == kcol // (2 * size)) & ((row_blk // size) % 2 == 1)
                         & ((col_blk // size) % 2 == 0))
        lvl.append(np.stack(masks))
    lvl = np.stack(lvl).astype(np.float32)
    return (jnp.asarray(cs, BF16), jnp.asarray(strict), jnp.asarray(incl), jnp.asarray(lvl),
            jnp.asarray(eye), jnp.asarray(_head_block_mask(PAIR_W)))


FNET_TT = 512
FNET_IN_ROWS = N_CTX


def _fnet_kernel(seq_len, n_in, *refs):
    z_refs = refs[:n_in]
    ch_ref, cl_ref, th_ref, o_ref, z_scr = refs[n_in:]
    b = pl.program_id(1)

    @pl.when((pl.program_id(0) == 0) & (b == 0))
    def _():
        per_in = FNET_IN_ROWS // seq_len
        for n, z_ref in enumerate(z_refs):
            zh, zl = _split2(z_ref[...])
            for half in range(2):
                ch = ch_ref[half]
                cl = cl_ref[half]
                zc = (_dg(zh, ch, _NN) + (_dg(zh, cl, _NN) + _dg(zl, ch, _NN))).astype(BF16)
                for s in range(per_in):
                    z_scr[n * per_in + s, half * seq_len:(half + 1) * seq_len, :] = zc[s * seq_len:(s + 1) * seq_len]

    o_ref[...] = _dg(th_ref[...], z_scr[b], _NN)


def _fnet_call(u_f, seq_len, n_seq, row_off, tables, name):
    ch, cl, th = tables
    tt = min(FNET_TT, seq_len)
    nt = seq_len // tt
    n_in = n_seq * seq_len // FNET_IN_ROWS
    in_off = row_off // FNET_IN_ROWS
    z_specs = [pl.BlockSpec((FNET_IN_ROWS, FNET_W), functools.partial(lambda n, j, b: (in_off + n, 0), n))
               for n in range(n_in)]
    return pl.pallas_call(
        functools.partial(_fnet_kernel, seq_len, n_in),
        grid=(nt, n_seq),
        in_specs=z_specs + [
            pl.BlockSpec((2, FNET_W, FNET_W), lambda j, b: (0, 0, 0)),
            pl.BlockSpec((2, FNET_W, FNET_W), lambda j, b: (0, 0, 0)),
            pl.BlockSpec((tt, 2 * seq_len), lambda j, b: (j, 0)),
        ],
        out_specs=pl.BlockSpec((tt, FNET_W), lambda j, b: (b * nt + j, 0)),
        out_shape=jax.ShapeDtypeStruct((n_seq * seq_len, FNET_W), F32),
        scratch_shapes=[pltpu.VMEM((n_seq, 2 * seq_len, FNET_W), BF16)],
        compiler_params=_cparams(("arbitrary", "arbitrary")),
        name=name,
    )(*([u_f] * n_in), ch, cl, th)


DFT_SPLIT = 32


def _dft_tables(seq_len):
    c = np.arange(FNET_W)
    ang_c = 2.0 * np.pi * ((c[:, None] % HEAD_DIM) * (c[None, :] % HEAD_DIM) % HEAD_DIM) / HEAD_DIM
    scale = 1.0 / np.sqrt(float(seq_len) * HEAD_DIM)
    blk = _head_block_mask(FNET_W)
    ch, cl = _split2(jnp.asarray(np.stack([np.cos(ang_c) * blk * scale, np.sin(ang_c) * blk * scale]), F32))

    n1 = seq_len // DFT_SPLIT
    f = np.arange(seq_len)
    a1 = 2.0 * np.pi * ((np.arange(n1)[:, None] * f[None, :]) % n1) / n1
    a2 = 2.0 * np.pi * ((np.arange(DFT_SPLIT)[:, None] * f[None, :]) % seq_len) / seq_len
    c1, s1 = jnp.asarray(np.cos(a1), F32)[:, None, :], jnp.asarray(np.sin(a1), F32)[:, None, :]
    c2, s2 = jnp.asarray(np.cos(a2), F32)[None, :, :], jnp.asarray(np.sin(a2), F32)[None, :, :]
    cos_t = (c1 * c2 - s1 * s2).reshape(seq_len, seq_len)
    sin_t = (s1 * c2 + c1 * s2).reshape(seq_len, seq_len)
    th = jnp.concatenate([cos_t, -sin_t], axis=1).astype(BF16)
    return ch, cl, th


def _merge_ffn_kernel(split, final, *refs):
    i = pl.program_id(0)
    n_x = 2 if split else 1
    x_in = _tile_of_split(i, *refs[:2], FFN_TM) if split else refs[0][...]
    (mod_ref, oac_ref, oal_ref, yf0_ref, yf1_ref, yb0_ref, yb1_ref, bonus_ref, g_ref, ofc_ref, ofl_ref,
     lng_ref, lnb_ref, avg_ref, nfg_ref, wo_ref, wi_ref, w2_ref) = refs[n_x:n_x + 18]
    tail = refs[n_x + 18:]
    is_ctx = i < N_CTX // FFN_TM
    o_attn = _tile_of_split(i, oac_ref, oal_ref, FFN_TM)
    o_fnet = _tile_of_split(i, ofc_ref, ofl_ref, FFN_TM)
    avg = avg_ref[...]
    y = jnp.where(_in_stream0(i, FFN_TM), yf0_ref[...] + yb0_ref[...], yf1_ref[...] + yb1_ref[...])
    yc = y - _dot_exact_rhs(y, avg)
    var = _dot_exact_rhs(yc * yc, avg)
    yn = yc * lax.rsqrt(var + GN_EPS) * lng_ref[...] + lnb_ref[...]
    o_rwkv = (yn + bonus_ref[...]) * g_ref[...]

    o = (_dg(o_attn.astype(BF16), wo_ref[0:ATTN_W, :], _NN)
         + _dg(o_rwkv.astype(BF16), wo_ref[ATTN_W:ATTN_W + RWKV_W, :], _NN)
         + _dg(o_fnet.astype(BF16), wo_ref[ATTN_W + RWKV_W:, :], _NN))
    g1 = mod_ref[:, 2 * D_MODEL:3 * D_MODEL]
    sh2 = mod_ref[:, 3 * D_MODEL:4 * D_MODEL]
    sc2 = mod_ref[:, 4 * D_MODEL:5 * D_MODEL]
    g2 = mod_ref[:, 5 * D_MODEL:6 * D_MODEL]
    x = x_in + g1 * o
    h = _rms(x, nfg_ref[...]) * (1.0 + sc2) + sh2
    gu = _dg(h.astype(BF16), wi_ref[...], _NN)
    gt = gu[:, 0:D_FF]
    act = gt * jax.nn.sigmoid(gt) * gu[:, D_FF:]
    x = x + g2 * _dg(act.astype(BF16), w2_ref[...], _NN)
    if not final:
        (o_ref,) = tail
        o_ref[...] = x
    else:
        fng_ref, oc_ref, ol_ref = tail
        y_out = _rms(x, fng_ref[...])

        @pl.when(is_ctx)
        def _():
            oc_ref[...] = y_out

        @pl.when(jnp.logical_not(is_ctx))
        def _():
            ol_ref[...] = y_out


def _in_stream0(m, tm):
    ctx_m = N_CTX // tm
    lat_ps = (N_TOK // tm - ctx_m) // N_STREAMS
    return (m < ctx_m // N_STREAMS) | ((m >= ctx_m) & (m < ctx_m + lat_ps))


def _stream_block(stream, tm):
    ctx_m = N_CTX // tm
    half = ctx_m // N_STREAMS
    lat_ps = (N_TOK // tm - ctx_m) // N_STREAMS
    if stream == 0:
        return lambda m: (jnp.where(m < half, m, jnp.where(m < ctx_m, half - 1,
                                    jnp.where(m < ctx_m + lat_ps, m - half, half + lat_ps - 1))), 0)
    return lambda m: (jnp.where(m < half, 0, jnp.where(m < ctx_m, m - half,
                                jnp.where(m < ctx_m + lat_ps, half - 1, m - half - lat_ps))), 0)


def _merge_ffn_call(xs, mods_l, oa_ctx, oa_lat, ys, bonus, g, of_ctx, of_lat, ln_g, ln_b, avg, nfg,
                    wo, wi, w2, l, final_g=None):
    final = final_g is not None
    split = isinstance(xs, tuple)
    xs = xs if split else (xs,)
    row = lambda i: (l, 0, 0)
    tile = lambda i: (i, 0)
    ctx_tile, lat_tile = _ctx_tile(FFN_TM), _lat_tile(FFN_TM)
    once = pl.Buffered(1)
    in_specs = _x_specs(split, FFN_TM) + [
        pl.BlockSpec((None, 1, 6 * D_MODEL), lambda i: (_mod_row(i, FFN_TM), 0, 0)),
        pl.BlockSpec((FFN_TM, ATTN_W), ctx_tile),
        pl.BlockSpec((FFN_TM, ATTN_W), lat_tile),
        pl.BlockSpec((FFN_TM, RWKV_W), _stream_block(0, FFN_TM)),
        pl.BlockSpec((FFN_TM, RWKV_W), _stream_block(1, FFN_TM)),
        pl.BlockSpec((FFN_TM, RWKV_W), _stream_block(0, FFN_TM)),
        pl.BlockSpec((FFN_TM, RWKV_W), _stream_block(1, FFN_TM)),
        pl.BlockSpec((FFN_TM, RWKV_W), tile),
        pl.BlockSpec((FFN_TM, RWKV_W), tile),
        pl.BlockSpec((FFN_TM, FNET_W), ctx_tile),
        pl.BlockSpec((FFN_TM, FNET_W), lat_tile),
        pl.BlockSpec((None, 1, RWKV_W), row),
        pl.BlockSpec((None, 1, RWKV_W), row),
        pl.BlockSpec((RWKV_W, RWKV_W), lambda i: (0, 0)),
        pl.BlockSpec((None, 1, D_MODEL), row),
        pl.BlockSpec((None, D_MODEL, D_MODEL), row, pipeline_mode=once),
        pl.BlockSpec((None, D_MODEL, 2 * D_FF), row, pipeline_mode=once),
        pl.BlockSpec((None, D_FF, D_MODEL), row, pipeline_mode=once),
    ]
    args = [*xs, mods_l, oa_ctx, oa_lat, *ys, bonus, g, of_ctx, of_lat, ln_g, ln_b, avg, nfg, wo, wi, w2]
    if final:
        in_specs.append(pl.BlockSpec((1, D_MODEL), lambda i: (0, 0)))
        args.append(final_g)
        out_specs = [pl.BlockSpec((FFN_TM, D_MODEL), ctx_tile), pl.BlockSpec((FFN_TM, D_MODEL), lat_tile)]
        out_shape = [jax.ShapeDtypeStruct((N_CTX, D_MODEL), F32), jax.ShapeDtypeStruct((N_LAT, D_MODEL), F32)]
    else:
        out_specs = pl.BlockSpec((FFN_TM, D_MODEL), tile)
        out_shape = jax.ShapeDtypeStruct((N_TOK, D_MODEL), F32)
    return pl.pallas_call(
        functools.partial(_merge_ffn_kernel, split, final),
        grid=(N_TOK // FFN_TM,),
        in_specs=in_specs,
        out_specs=out_specs,
        out_shape=out_shape,
        compiler_params=_cparams(("arbitrary",)),
        name="merge_ffn_final" if final else "merge_ffn",
    )(*args)


def _block_diag2(w):
    z = jnp.zeros_like(w[:, 0])
    return jnp.concatenate([jnp.concatenate([w[:, 0], z], axis=2), jnp.concatenate([z, w[:, 1]], axis=2)], axis=1)


def kernel(x_prompt, x_sample, cache_k, cache_v, state_rwkv, c, c_ctx, w_ada, b_ada, norm_mix_g, norm_ffn_g,
           w_in, w_out, attn_sink, rwkv_shift_w, rwkv_w0, rwkv_w_up, rwkv_a0, rwkv_a_up, rwkv_g_up,
           rwkv_k_k, rwkv_k_a, rwkv_r_k, rwkv_ln_g, rwkv_ln_b, ffn_w_in, ffn_w_out, norm_final_g):
    x = (x_prompt.reshape(N_CTX, D_MODEL), x_sample.reshape(N_LAT, D_MODEL))
    cvec = jnp.concatenate([c_ctx[None, :], c, jnp.zeros((8 - 1 - N_LAT_SEQ, D_MODEL), F32)], axis=0)
    mods = _mods_call(cvec, w_ada, b_ada).reshape(DEPTH, 8, 1, 6 * D_MODEL)

    w_in_b = w_in.astype(BF16)
    w_out_b = w_out.astype(BF16)
    ffn_in_b = ffn_w_in.astype(BF16)
    ffn_out_b = ffn_w_out.astype(BF16)
    wup_bd = _block_diag2(rwkv_w_up)
    aup_bd = _block_diag2(rwkv_a_up)
    row3 = lambda a: a.reshape(DEPTH, 1, -1)
    nmg, nfg = row3(norm_mix_g), row3(norm_ffn_g)
    k_k, k_a, r_k = row3(rwkv_k_k), row3(rwkv_k_a), row3(rwkv_r_k)
    ln_g, ln_b = row3(rwkv_ln_g), row3(rwkv_ln_b)
    kc = cache_k.reshape(N_LAT_SEQ, DEPTH, PAST_LEN, KV_W)
    vc = cache_v.reshape(N_LAT_SEQ, DEPTH, PAST_LEN, KV_W)
    s0 = state_rwkv.reshape(N_LAT_SEQ, DEPTH, 2, RWKV_W, HEAD_DIM)

    cos, sin = _rope_tables()
    consts = _scan_consts()
    bmask = jnp.asarray(_head_block_mask(RWKV_W))
    ones_bd = bmask.astype(BF16)
    avg_bd = (bmask * (1.0 / HEAD_DIM)).astype(BF16)
    rwkv_params = (rwkv_shift_w, wup_bd, aup_bd, rwkv_g_up, rwkv_w0, rwkv_a0, k_k, k_a, r_k, ones_bd)
    dft_ctx = _dft_tables(CTX_LEN)
    dft_lat = _dft_tables(LAT_LEN)

    new_k, new_v, new_s = [], [], []
    for l in range(DEPTH):
        q, k, v, u_f, r, vv, kn, g, bonus, lw, kd, b = _inproj_call(x, mods[l], nmg, w_in_b, rwkv_params, l)
        new_k.append(k[:N_CTX].reshape(N_CTX_SEQ, CTX_LEN, KV_W))
        new_v.append(v[:N_CTX].reshape(N_CTX_SEQ, CTX_LEN, KV_W))

        oa_ctx = _ctx_attn_call(q, k, v, attn_sink, l)
        oa_lat = _lat_attn_call(q, k, v, kc, vc, cos, sin, attn_sink, l)

        ys, sfins = _scan_call(r, vv, kn, lw, kd, b, s0, consts, l)
        new_s.append(sfins)

        of_ctx = _fnet_call(u_f, CTX_LEN, N_CTX_SEQ, 0, dft_ctx, "fnet_context")
        of_lat = _fnet_call(u_f, LAT_LEN, N_LAT_SEQ, N_CTX, dft_lat, "fnet_latent")

        last = l == DEPTH - 1
        x = _merge_ffn_call(x, mods[l], oa_ctx, oa_lat, ys, bonus, g, of_ctx, of_lat, ln_g, ln_b, avg_bd, nfg,
                            w_out_b, ffn_in_b, ffn_out_b, l,
                            final_g=norm_final_g.reshape(1, D_MODEL) if last else None)

    y_ctx, y_lat = x
    return (y_ctx.reshape(N_CTX_SEQ, CTX_LEN, D_MODEL), y_lat.reshape(N_LAT_SEQ, LAT_LEN, D_MODEL),
            jnp.stack(new_k, axis=1).reshape(N_CTX_SEQ, DEPTH, CTX_LEN, N_KV_HEADS, HEAD_DIM),
            jnp.stack(new_v, axis=1).reshape(N_CTX_SEQ, DEPTH, CTX_LEN, N_KV_HEADS, HEAD_DIM),
            _scan_final_states(new_s))
```

```python
import functools

import numpy as np
import jax
import jax.numpy as jnp
from jax import lax
from jax.experimental import pallas as pl
from jax.experimental.pallas import tpu as pltpu

F32 = jnp.float32
BF16 = jnp.bfloat16

D_MODEL = 1024
N_CTX_SEQ = 16
CTX_LEN = 256
DEPTH = 4
N_LAT_SEQ = 4
LAT_LEN = 2048
PAST_LEN = 512
GRID_W = 64
HEAD_DIM = 64
N_Q_HEADS = 8
N_KV_HEADS = 2
Q_PER_KV = N_Q_HEADS // N_KV_HEADS
ATTN_W = N_Q_HEADS * HEAD_DIM
KV_W = N_KV_HEADS * HEAD_DIM
BLOCK = 128
ATTN_SCALE = 0.125
ROPE_THETA = 10000.0
NEG = -1e30
RWKV_HEADS = 4
RWKV_W = RWKV_HEADS * HEAD_DIM
DECAY_RANK = 64
ICLR_RANK = 64
GATE_RANK = 128
DECAY_SCALE = 0.6065306597126334
GN_EPS = 64e-5
RWKV_IN_W = 3 * RWKV_W + GATE_RANK + 2 * DECAY_RANK + 2 * ICLR_RANK
FNET_W = 4 * HEAD_DIM
IN_W = ATTN_W + 2 * KV_W + RWKV_IN_W + FNET_W
D_FF = 2816
RMS_EPS = 1e-6

N_CTX = N_CTX_SEQ * CTX_LEN
N_LAT = N_LAT_SEQ * LAT_LEN
N_TOK = N_CTX + N_LAT
TM = 256
N_TILES = N_TOK // TM
CTX_TILES = N_CTX // TM
LAT_TILES_PER_SEQ = LAT_LEN // TM
CHUNK = 64
CHUNKS_PER_TILE = TM // CHUNK
HALO = 8
FFN_TM = 512
INPROJ_TM = 512
LAT_ATTN_QBLOCKS = 4
LAT_ATTN_GROUP = 8

VMEM_LIMIT = 56 * 1024 * 1024


def _cparams(sem, fuse_inputs=None):
    return pltpu.CompilerParams(dimension_semantics=sem, vmem_limit_bytes=VMEM_LIMIT, allow_input_fusion=fuse_inputs)


def _mod_row(i, tm=TM):
    return jnp.where(i < N_CTX // tm, 0, 1 + (i - N_CTX // tm) // (LAT_LEN // tm))


def _split2(x):
    hi = x.astype(BF16)
    lo = (x - hi.astype(F32)).astype(BF16)
    return hi, lo


_NN = (((1,), (0,)), ((), ()))
_NT = (((1,), (1,)), ((), ()))


def _dg(a, b, dims):
    return lax.dot_general(a, b, dims, preferred_element_type=F32)


def _dot1(a, b, dims=_NN):
    return _dg(a.astype(BF16), b.astype(BF16), dims)


def _dot_exact_rhs(a, b_bf16):
    ah, al = _split2(a)
    return _dg(ah, b_bf16, _NN) + _dg(al, b_bf16, _NN)


def _dot_exact_lhs(a_bf16, b):
    bh, bl = _split2(b)
    return _dg(a_bf16, bh, _NN) + _dg(a_bf16, bl, _NN)


def _mods_kernel(c_ref, w_ref, b_ref, o_ref):
    c = c_ref[...]
    s = c * jax.nn.sigmoid(c)
    o_ref[...] = _dot1(s, w_ref[...]) + b_ref[...]


def _mods_call(cvec, w_ada, b_ada):
    tn = 1536
    return pl.pallas_call(
        _mods_kernel,
        grid=(DEPTH, 6 * D_MODEL // tn),
        in_specs=[
            pl.BlockSpec((8, D_MODEL), lambda l, j: (0, 0)),
            pl.BlockSpec((None, D_MODEL, tn), lambda l, j: (l, 0, j)),
            pl.BlockSpec((None, 1, tn), lambda l, j: (l, 0, j)),
        ],
        out_specs=pl.BlockSpec((None, 8, tn), lambda l, j: (l, 0, j)),
        out_shape=jax.ShapeDtypeStruct((DEPTH, 8, 6 * D_MODEL), F32),
        compiler_params=_cparams(("parallel", "parallel")),
        name="adaln_mods",
    )(cvec, w_ada, b_ada.reshape(DEPTH, 1, 6 * D_MODEL))


def _rms(x, g):
    return x * lax.rsqrt(jnp.mean(x * x, axis=-1, keepdims=True) + RMS_EPS) * g


def _tile_of_split(i, ctx_ref, lat_ref, tm=TM):
    return jnp.where(i < N_CTX // tm, ctx_ref[...], lat_ref[...])


def _ctx_tile(tm=TM):
    return lambda i: (jnp.minimum(i, N_CTX // tm - 1), 0)


def _lat_tile(tm=TM):
    return lambda i: (jnp.maximum(i - N_CTX // tm, 0), 0)


def _x_specs(split, tm=TM):
    if split:
        return [pl.BlockSpec((tm, D_MODEL), _ctx_tile(tm)), pl.BlockSpec((tm, D_MODEL), _lat_tile(tm))]
    return [pl.BlockSpec((tm, D_MODEL), lambda i: (i, 0))]


def _head_block_mask(n):
    r = np.arange(n)[:, None] // HEAD_DIM
    c = np.arange(n)[None, :] // HEAD_DIM
    return (r == c).astype(np.float32)


def _rwkv_token_terms(u, prev_row, next_row, seam, sw_ref, wup_ref, aup_ref, gup_ref, w0_ref, a0_ref,
                      kk_ref, ka_ref, rk_ref, ones_ref, r_out, v_out, kn_out, g_out, bonus_out, lw_out, kd_out, b_out):
    n = u.shape[0]
    row = lax.broadcasted_iota(jnp.int32, u.shape, 0)
    u_dn = jnp.where(row == 0, prev_row, pltpu.roll(u, 1, 0))
    u_up = jnp.where(row == n - 1, next_row, pltpu.roll(u, n - 1, 0))
    u_dn = jnp.where(seam & (row == n // 2), 0.0, u_dn)
    u_up = jnp.where(seam & (row == n // 2 - 1), 0.0, u_up)
    us = u_dn * sw_ref[0:1, :] + u * sw_ref[1:2, :] + u_up * sw_ref[2:3, :]

    r = us[:, 0:RWKV_W]
    k = us[:, RWKV_W:2 * RWKV_W]
    v = us[:, 2 * RWKV_W:3 * RWKV_W]
    o = 3 * RWKV_W
    gd = us[:, o:o + GATE_RANK]
    wd = us[:, o + GATE_RANK:o + GATE_RANK + 2 * DECAY_RANK]
    ad = us[:, o + GATE_RANK + 2 * DECAY_RANK:RWKV_IN_W]

    w_pre = _dot1(jnp.tanh(wd), wup_ref[...])
    a_pre = _dot1(ad, aup_ref[...])
    g_out[...] = _dot1(jax.nn.sigmoid(gd), gup_ref[...])

    ones_bd = ones_ref[...]
    kn = k * kk_ref[...]
    kn = kn * lax.rsqrt(_dot_exact_rhs(kn * kn, ones_bd) + 1e-12)
    r_out[...] = r
    v_out[...] = v
    kn_out[...] = kn
    bonus_out[...] = _dot_exact_rhs(r * k * rk_ref[...], ones_bd) * v
    ka = ka_ref[...]
    for d in range(2):
        sl = slice(d * RWKV_W, (d + 1) * RWKV_W)
        a_d = jax.nn.sigmoid(a0_ref[d:d + 1, :] + a_pre[:, sl])
        lw_out[d] = -DECAY_SCALE * jax.nn.sigmoid(w0_ref[d:d + 1, :] + w_pre[:, sl])
        kd_out[d] = k * (1.0 + (a_d - 1.0) * ka)
        b_out[d] = kn * a_d


RWKV_COL0 = ATTN_W + 2 * KV_W
N_RWKV_PARAMS = 10


def _inproj_kernel(split, *refs):
    n_x = 2 if split else 1
    i = pl.program_id(0)
    x = _tile_of_split(i, *refs[:2], INPROJ_TM) if split else refs[0][...]
    xp_ref, xn_ref, mod_ref, g_ref, w_ref = refs[n_x:n_x + 5]
    rwkv_params = refs[n_x + 5:n_x + 5 + N_RWKV_PARAMS]
    q_ref, k_ref, v_ref, uf_ref = refs[n_x + 5 + N_RWKV_PARAMS:n_x + 9 + N_RWKV_PARAMS]
    rwkv_outs = refs[n_x + 9 + N_RWKV_PARAMS:]

    shift = mod_ref[:, 0:D_MODEL]
    scale = mod_ref[:, D_MODEL:2 * D_MODEL]
    x_all = jnp.concatenate([x, xp_ref[...], xn_ref[...]], axis=0)
    h = (_rms(x_all, g_ref[...]) * (1.0 + scale) + shift).astype(BF16)
    u_all = _dg(h, w_ref[:, RWKV_COL0:RWKV_COL0 + RWKV_IN_W], _NN)
    o_qkv = _dg(h[:INPROJ_TM], w_ref[:, 0:RWKV_COL0], _NN)
    q_ref[...] = o_qkv[:, 0:ATTN_W]
    k_ref[...] = o_qkv[:, ATTN_W:ATTN_W + KV_W]
    v_ref[...] = o_qkv[:, ATTN_W + KV_W:RWKV_COL0]
    uf_ref[...] = _dg(h[:INPROJ_TM], w_ref[:, RWKV_COL0 + RWKV_IN_W:IN_W], _NN)

    ctx_tiles = N_CTX // INPROJ_TM
    per_seq = LAT_LEN // INPROJ_TM
    is_lat = i >= ctx_tiles
    j = (i - ctx_tiles) % per_seq
    has_prev = jnp.where(is_lat & (j != 0), 1.0, 0.0)
    has_next = jnp.where(is_lat & (j != per_seq - 1), 1.0, 0.0)
    prev_row = u_all[INPROJ_TM + HALO - 1:INPROJ_TM + HALO, :] * has_prev
    next_row = u_all[INPROJ_TM + HALO:INPROJ_TM + HALO + 1, :] * has_next
    seam = jnp.logical_not(is_lat) if INPROJ_TM == 2 * CTX_LEN else False
    _rwkv_token_terms(u_all[:INPROJ_TM], prev_row, next_row, seam, *rwkv_params, *rwkv_outs)


def _inproj_call(xs, mods_l, g, w_bf16, rwkv_params, l):
    split = isinstance(xs, tuple)
    xs = xs if split else (xs,)
    hb = INPROJ_TM // HALO
    ctx_tiles = N_CTX // INPROJ_TM
    if split:
        halo_src = xs[1]
        last = N_LAT // HALO - 1
        prev_blk = lambda i: (jnp.clip((i - ctx_tiles) * hb - 1, 0, last), 0)
        next_blk = lambda i: (jnp.clip((i - ctx_tiles + 1) * hb, 0, last), 0)
    else:
        halo_src = xs[0]
        last = N_TOK // HALO - 1
        prev_blk = lambda i: (jnp.maximum(i * hb - 1, 0), 0)
        next_blk = lambda i: (jnp.minimum((i + 1) * hb, last), 0)
    row = lambda i: (l, 0, 0)
    widths = (ATTN_W, KV_W, KV_W, FNET_W)
    tok = jax.ShapeDtypeStruct((N_TOK, RWKV_W), F32)
    tok2 = jax.ShapeDtypeStruct((2, N_TOK, RWKV_W), F32)
    return pl.pallas_call(
        functools.partial(_inproj_kernel, split),
        grid=(N_TOK // INPROJ_TM,),
        in_specs=_x_specs(split, INPROJ_TM) + [
            pl.BlockSpec((HALO, D_MODEL), prev_blk),
            pl.BlockSpec((HALO, D_MODEL), next_blk),
            pl.BlockSpec((None, 1, 6 * D_MODEL), lambda i: (_mod_row(i, INPROJ_TM), 0, 0)),
            pl.BlockSpec((None, 1, D_MODEL), row),
            pl.BlockSpec((None, D_MODEL, IN_W), row),
            pl.BlockSpec((None, 3, RWKV_IN_W), row),
            pl.BlockSpec((None, 2 * DECAY_RANK, 2 * RWKV_W), row),
            pl.BlockSpec((None, 2 * ICLR_RANK, 2 * RWKV_W), row),
            pl.BlockSpec((None, GATE_RANK, RWKV_W), row),
            pl.BlockSpec((None, 2, RWKV_W), row),
            pl.BlockSpec((None, 2, RWKV_W), row),
            pl.BlockSpec((None, 1, RWKV_W), row),
            pl.BlockSpec((None, 1, RWKV_W), row),
            pl.BlockSpec((None, 1, RWKV_W), row),
            pl.BlockSpec((RWKV_W, RWKV_W), lambda i: (0, 0)),
        ],
        out_specs=[pl.BlockSpec((INPROJ_TM, w), lambda i: (i, 0)) for w in widths]
        + [pl.BlockSpec((INPROJ_TM, RWKV_W), lambda i: (i, 0))] * 5
        + [pl.BlockSpec((2, INPROJ_TM, RWKV_W), lambda i: (0, i, 0))] * 3,
        out_shape=[jax.ShapeDtypeStruct((N_TOK, w), F32) for w in widths] + [tok] * 5 + [tok2] * 3,
        compiler_params=_cparams(("arbitrary",),
                                 fuse_inputs=[False] * (len(xs) + 4) + [True] + [False] * N_RWKV_PARAMS),
        name="in_projection",
    )(*xs, halo_src, halo_src, mods_l, g, w_bf16, *rwkv_params)


def _ctx_attn_kernel(l, sink_ref, q_ref, k_ref, v_ref, o_ref):
    q = q_ref[...]
    k = k_ref[...]
    v = v_ref[...]
    half = CTX_LEN // 2
    pieces = [(h, r) for h in range(N_Q_HEADS) for r in range(2)]
    kb = k.astype(BF16)
    vb = v.astype(BF16)
    kv = lambda a, h: a[:, (h // Q_PER_KV) * HEAD_DIM:(h // Q_PER_KV + 1) * HEAD_DIM]
    qs = [(q[r * half:(r + 1) * half, h * HEAD_DIM:(h + 1) * HEAD_DIM] * ATTN_SCALE).astype(BF16) for h, r in pieces]
    s = [_dg(qs[n], kv(kb, h), _NT) for n, (h, r) in enumerate(pieces)]
    m = [jnp.maximum(jnp.max(s[n], axis=-1, keepdims=True), sink_ref[l, h]) for n, (h, r) in enumerate(pieces)]
    e = [jnp.exp(s[n] - m[n]) for n in range(len(pieces))]
    den = [jnp.sum(e[n], axis=-1, keepdims=True) + jnp.exp(sink_ref[l, h] - m[n]) for n, (h, r) in enumerate(pieces)]
    o = [_dg(e[n].astype(BF16), kv(vb, h), _NN) / den[n] for n, (h, r) in enumerate(pieces)]
    for n, (h, r) in enumerate(pieces):
        o_ref[r * half:(r + 1) * half, h * HEAD_DIM:(h + 1) * HEAD_DIM] = o[n]


def _ctx_attn_call(q, k, v, sink, l):
    return pl.pallas_call(
        functools.partial(_ctx_attn_kernel, l),
        grid=(N_CTX_SEQ,),
        in_specs=[
            pl.BlockSpec(memory_space=pltpu.SMEM),
            pl.BlockSpec((CTX_LEN, ATTN_W), lambda b: (b, 0)),
            pl.BlockSpec((CTX_LEN, KV_W), lambda b: (b, 0)),
            pl.BlockSpec((CTX_LEN, KV_W), lambda b: (b, 0)),
        ],
        out_specs=pl.BlockSpec((CTX_LEN, ATTN_W), lambda b: (b, 0)),
        out_shape=jax.ShapeDtypeStruct((N_CTX, ATTN_W), F32),
        compiler_params=_cparams(("parallel",)),
        name="context_attention",
    )(sink, q, k, v)


def _rope(x, cos, sin):
    lane = lax.broadcasted_iota(jnp.int32, x.shape, 1)
    partner = jnp.where((lane & 16) == 0, pltpu.roll(x, 128 - 16, 1), pltpu.roll(x, 16, 1))
    return x * cos + partner * sin


def _lat_attn_kernel(l, sink_ref, q_ref, k_ref, v_ref, kc_ref, vc_ref, cos_ref, sin_ref, o_ref):
    nq = LAT_ATTN_QBLOCKS
    nb = LAT_LEN // BLOCK
    n0 = pl.program_id(1) * nq
    r0 = pl.multiple_of(n0 * BLOCK, BLOCK)
    cos_q = cos_ref[pl.ds(r0, nq * BLOCK), :]
    sin_q = sin_ref[pl.ds(r0, nq * BLOCK), :]
    q = jnp.concatenate(
        [_rope(q_ref[:, j * 128:(j + 1) * 128], cos_q, sin_q) for j in range(ATTN_W // 128)], axis=1)

    kr, vr = [], []
    for dj in range(-1, nq + 1):
        rj = pl.multiple_of(jnp.clip(n0 + dj, 0, nb - 1) * BLOCK, BLOCK)
        kr.append(_rope(k_ref[pl.ds(rj, BLOCK), :], cos_ref[pl.ds(rj, BLOCK), :],
                        sin_ref[pl.ds(rj, BLOCK), :]).astype(BF16))
        vr.append(v_ref[pl.ds(rj, BLOCK), :].astype(BF16))
    kc = kc_ref[...].astype(BF16)
    vc = vc_ref[...].astype(BF16)
    k_all = [jnp.concatenate(kr[t:t + 3] + [kc], axis=0) for t in range(nq)]
    v_all = [jnp.concatenate(vr[t:t + 3] + [vc], axis=0) for t in range(nq)]

    kj = lax.broadcasted_iota(jnp.int32, (BLOCK, BLOCK), 1)
    qi = lax.broadcasted_iota(jnp.int32, (BLOCK, BLOCK), 0)
    keep_prev = [(kj >= qi) & (n0 + t > 0) for t in range(nq)]
    keep_next = [(kj <= qi) & (n0 + t < nb - 1) for t in range(nq)]
    kv_cols = lambda h: slice((h // Q_PER_KV) * HEAD_DIM, (h // Q_PER_KV + 1) * HEAD_DIM)

    all_pieces = [(t, h) for t in range(nq) for h in range(N_Q_HEADS)]
    for g0 in range(0, len(all_pieces), LAT_ATTN_GROUP):
        pieces = all_pieces[g0:g0 + LAT_ATTN_GROUP]
        rng = range(len(pieces))
        qs = [(q[t * BLOCK:(t + 1) * BLOCK, h * HEAD_DIM:(h + 1) * HEAD_DIM] * ATTN_SCALE).astype(BF16)
              for t, h in pieces]
        s = [_dg(qs[i], k_all[t][:, kv_cols(h)], _NT) for i, (t, h) in enumerate(pieces)]
        s = [jnp.concatenate([jnp.where(keep_prev[t], s[i][:, 0:BLOCK], NEG), s[i][:, BLOCK:2 * BLOCK],
                              jnp.where(keep_next[t], s[i][:, 2 * BLOCK:3 * BLOCK], NEG), s[i][:, 3 * BLOCK:]],
                             axis=1) for i, (t, h) in enumerate(pieces)]
        m = [jnp.maximum(jnp.max(s[i], axis=-1, keepdims=True), sink_ref[l, h]) for i, (t, h) in enumerate(pieces)]
        e = [jnp.exp(s[i] - m[i]) for i in rng]
        den = [jnp.sum(e[i], axis=-1, keepdims=True) + jnp.exp(sink_ref[l, h] - m[i])
               for i, (t, h) in enumerate(pieces)]
        o = [_dg(e[i].astype(BF16), v_all[t][:, kv_cols(h)], _NN) / den[i] for i, (t, h) in enumerate(pieces)]
        for i, (t, h) in enumerate(pieces):
            o_ref[t * BLOCK:(t + 1) * BLOCK, h * HEAD_DIM:(h + 1) * HEAD_DIM] = o[i]


def _lat_attn_call(q, k, v, kc, vc, cos, sin, sink, l):
    rows = LAT_ATTN_QBLOCKS * BLOCK
    steps = LAT_LEN // rows
    q_off = N_CTX // rows
    kv_off = N_CTX // LAT_LEN
    return pl.pallas_call(
        functools.partial(_lat_attn_kernel, l),
        grid=(N_LAT_SEQ, steps),
        in_specs=[
            pl.BlockSpec(memory_space=pltpu.SMEM),
            pl.BlockSpec((rows, ATTN_W), lambda b, n: (q_off + b * steps + n, 0)),
            pl.BlockSpec((LAT_LEN, KV_W), lambda b, n: (kv_off + b, 0)),
            pl.BlockSpec((LAT_LEN, KV_W), lambda b, n: (kv_off + b, 0)),
            pl.BlockSpec((None, None, PAST_LEN, KV_W), lambda b, n: (b, l, 0, 0)),
            pl.BlockSpec((None, None, PAST_LEN, KV_W), lambda b, n: (b, l, 0, 0)),
            pl.BlockSpec((LAT_LEN, 128), lambda b, n: (0, 0)),
            pl.BlockSpec((LAT_LEN, 128), lambda b, n: (0, 0)),
        ],
        out_specs=pl.BlockSpec((rows, ATTN_W), lambda b, n: (b * steps + n, 0)),
        out_shape=jax.ShapeDtypeStruct((N_LAT, ATTN_W), F32),
        compiler_params=_cparams(("parallel", "arbitrary")),
        name="latent_attention",
    )(sink, q, k, v, kc, vc, cos, sin)


def _rope_tables():
    t = np.arange(LAT_LEN)
    lane = np.arange(128)
    d = lane % HEAD_DIM
    pos = np.where(d[None, :] < HEAD_DIM // 2, (t // GRID_W)[:, None], (t % GRID_W)[:, None]).astype(np.float32)
    quarter = HEAD_DIM // 4
    inv = (np.float32(ROPE_THETA) ** (-(np.arange(quarter, dtype=np.float32)) / np.float32(quarter))).astype(np.float32)
    ang = pos * inv[d % quarter][None, :]
    sign = np.where((lane & 16) == 0, -1.0, 1.0)[None, :]
    return jnp.asarray(np.cos(ang), F32), jnp.asarray(np.sin(ang) * sign, F32)


PAIR_W = 2 * HEAD_DIM
INV_BASE = 8
INV_LEVELS = 3
N_PAIRS = RWKV_HEADS // 2
_TN = (((0,), (0,)), ((), ()))


def _pair_stack(x, pmask):
    return jnp.concatenate([x.astype(BF16)] * 2, axis=0) * pmask


def _hdot(m, x, pmask, dims):
    return jnp.concatenate(
        [_dg(m[:, p * PAIR_W:(p + 1) * PAIR_W], _pair_stack(x[:, p * PAIR_W:(p + 1) * PAIR_W], pmask), dims)
         for p in range(N_PAIRS)], axis=1)


def _scan_prepare(reverse, r, v, kn, lw, kd, b, cs, strict, incl, lvl, eye, pmask):
    linc = _dot_exact_lhs(cs, lw)
    end = 0 if reverse else CHUNK - 1
    ltot = jnp.concatenate(
        [jnp.broadcast_to(linc[c * CHUNK + end:c * CHUNK + end + 1, :], (CHUNK, RWKV_W))
         for c in range(CHUNKS_PER_TILE)], axis=0)
    v_all = v.astype(BF16)
    r_t = r * jnp.exp(linc)
    a_t = (-kn * jnp.exp(linc - lw)).astype(BF16)
    e_neg = jnp.exp(-linc)
    b_t = (b * e_neg).astype(BF16)
    k_t = (kd * e_neg).astype(BF16)
    e_rem = jnp.exp(ltot - linc)
    b_p = (b * e_rem).astype(BF16)
    k_p = (kd * e_rem).astype(BF16)
    dec = jnp.exp(ltot)

    chunks = range(CHUNKS_PER_TILE)
    sls = [slice(c * CHUNK, (c + 1) * CHUNK) for c in chunks]
    hd = lambda m, x, dims: _hdot(m, x, pmask, dims)
    ar = [jnp.concatenate([a_t[sl], r_t[sl].astype(BF16)], axis=0) for sl in sls]
    m_b = [hd(ar[c], b_t[sls[c]], _NT) for c in chunks]
    m_k = [hd(ar[c], k_t[sls[c]], _NT) for c in chunks]
    a_ab = [m_b[c][:CHUNK] * strict for c in chunks]
    p_rb = [(m_b[c][CHUNK:] * incl).astype(BF16) for c in chunks]
    ak_rk = [jnp.concatenate([m_k[c][:CHUNK] * strict, m_k[c][CHUNK:] * incl], axis=0).astype(BF16)
             for c in chunks]

    d8 = [a_ab[c] * lvl[0] for c in chunks]
    tinv = [eye + d8[c] for c in chunks]
    p = [d8[c].astype(BF16) for c in chunks]
    p = [hd(p[c], p[c], _NN).astype(BF16) for c in chunks]
    pt = [hd(jnp.concatenate([p[c], tinv[c].astype(BF16)], axis=0), p[c], _NN) for c in chunks]
    p = [pt[c][:CHUNK].astype(BF16) for c in chunks]
    tinv = [tinv[c] + pt[c][CHUNK:] for c in chunks]
    tinv = [tinv[c] + hd(tinv[c].astype(BF16), p[c], _NN) for c in chunks]
    for s in range(1, INV_LEVELS + 1):
        tb = [tinv[c].astype(BF16) for c in chunks]
        tn = [hd(tb[c], (a_ab[c] * lvl[s]).astype(BF16), _NN).astype(BF16) for c in chunks]
        tinv = [tinv[c] + hd(tn[c], tb[c], _NN) for c in chunks]
    tinv = [tinv[c].astype(BF16) for c in chunks]

    m_v = [hd(ak_rk[c], v_all[sls[c]], _NN) for c in chunks]
    w = [hd(tinv[c], a_t[sls[c]], _NN) for c in chunks]
    u0 = [hd(tinv[c], m_v[c][:CHUNK], _NN) for c in chunks]
    qp = [r_t[sls[c]] + hd(p_rb[c], w[c], _NN) for c in chunks]
    y0 = [hd(p_rb[c], u0[c], _NN) + m_v[c][CHUNK:] for c in chunks]
    pr = [slice(p * PAIR_W, (p + 1) * PAIR_W) for p in range(N_PAIRS)]
    pmask_f = pmask.astype(F32)
    bk = [jnp.concatenate([b_p[sl], k_p[sl]], axis=0) for sl in sls]
    uv = [jnp.concatenate([u0[c].astype(BF16), v_all[sls[c]]], axis=0) for c in chunks]
    wb = [[_dg(w[c][:, q].astype(BF16), b_p[sls[c]][:, q], _TN) * pmask_f for q in pr] for c in chunks]
    gs = [[_dg(uv[c][:, q], bk[c][:, q], _TN) * pmask_f for q in pr] for c in chunks]
    qpb = [qp[c].astype(BF16) for c in chunks]
    decs = [dec[c * CHUNK:c * CHUNK + 1, :] for c in chunks]
    return qpb, y0, wb, gs, decs


def _scan_chain_step(c, s, prep, y_ref):
    qpb, y0, wb, gs, decs = prep
    pr = [slice(p * PAIR_W, (p + 1) * PAIR_W) for p in range(N_PAIRS)]
    sb = [s[p].astype(BF16) for p in range(N_PAIRS)]
    y_ref[c * CHUNK:(c + 1) * CHUNK, :] = y0[c] + jnp.concatenate(
        [_dg(qpb[c][:, pr[p]], sb[p], _NT) for p in range(N_PAIRS)], axis=1)
    return [s[p] * decs[c][:, pr[p]] + (_dg(sb[p], wb[c][p].astype(BF16), _NN) + gs[c][p]) for p in range(N_PAIRS)]


N_STREAMS = 2
STREAM_TILES = N_TILES // N_STREAMS
STREAM_CTX = CTX_TILES // N_STREAMS
SCAN_LANES = tuple((rev, st) for rev in (False, True) for st in range(N_STREAMS))
N_LANE_IN = 7
N_SCAN_CONSTS = 6


def _stream_tile(stream, pos):
    lat0 = CTX_TILES + stream * (STREAM_TILES - STREAM_CTX)
    return jnp.where(pos < STREAM_CTX, stream * STREAM_CTX + pos, lat0 + pos - STREAM_CTX)


def _lane_pos(reverse, k):
    return (STREAM_TILES - 1 - k) if reverse else k


def _scan_kernel(*refs):
    n_lanes = len(SCAN_LANES)
    lane_in = [refs[n * N_LANE_IN:(n + 1) * N_LANE_IN] for n in range(n_lanes)]
    cs_ref, strict_ref, incl_ref, lvl_ref, eye_ref, pmask_ref = refs[n_lanes * N_LANE_IN:n_lanes * N_LANE_IN + N_SCAN_CONSTS]
    outs = refs[n_lanes * N_LANE_IN + N_SCAN_CONSTS:n_lanes * N_LANE_IN + N_SCAN_CONSTS + 2 * n_lanes]
    lane_out = [outs[2 * n:2 * n + 2] for n in range(n_lanes)]
    lane_scr = refs[n_lanes * N_LANE_IN + N_SCAN_CONSTS + 2 * n_lanes:]

    k = pl.program_id(0)
    pmask_f = pmask_ref[...]
    pmask = pmask_f.astype(BF16)
    eye = eye_ref[...]

    def load_state(s0_ref, scr, is_ctx):
        for p in range(N_PAIRS):
            carried = jnp.concatenate([s0_ref[p * PAIR_W:(p + 1) * PAIR_W, :]] * 2, axis=1) * pmask_f
            scr[p] = jnp.where(is_ctx, 0.0, carried)

    for n, (reverse, stream) in enumerate(SCAN_LANES):
        pos = _lane_pos(reverse, k)
        j = (pos - STREAM_CTX) % LAT_TILES_PER_SEQ
        first = (pos < STREAM_CTX) | (j == (LAT_TILES_PER_SEQ - 1 if reverse else 0))
        pl.when(first)(functools.partial(load_state, lane_in[n][6], lane_scr[n], pos < STREAM_CTX))

    preps = []
    for n, (reverse, stream) in enumerate(SCAN_LANES):
        d = 1 if reverse else 0
        r_ref, v_ref, kn_ref, lw_ref, kd_ref, b_ref, _ = lane_in[n]
        preps.append(_scan_prepare(reverse, r_ref[...], v_ref[...], kn_ref[...], lw_ref[...], kd_ref[...], b_ref[...],
                                   cs_ref[d], strict_ref[d], incl_ref[d], lvl_ref[d], eye, pmask))
    states = [[lane_scr[n][p] for p in range(N_PAIRS)] for n in range(n_lanes)]
    for c in range(CHUNKS_PER_TILE):
        for n, (reverse, stream) in enumerate(SCAN_LANES):
            cc = CHUNKS_PER_TILE - 1 - c if reverse else c
            states[n] = _scan_chain_step(cc, states[n], preps[n], lane_out[n][0])
    for n in range(n_lanes):
        for p in range(N_PAIRS):
            lane_scr[n][p] = states[n][p]
            for h in range(2):
                blk = slice(h * HEAD_DIM, (h + 1) * HEAD_DIM)
                lane_out[n][1][2 * p + h] = states[n][p][blk, blk]


def _scan_call(r, v, kn, lw, kd, b, s0, consts, l):
    cs, strict, incl, lvl, eye, pmask = consts
    lat_seq_of = lambda t: jnp.maximum(t - CTX_TILES, 0) // LAT_TILES_PER_SEQ
    const2 = lambda k: (0, 0)
    const3 = lambda k: (0, 0, 0)
    in_specs, out_specs, args = [], [], []
    for reverse, stream in SCAN_LANES:
        d = 1 if reverse else 0
        tile = functools.partial(lambda rv, st, k: _stream_tile(st, _lane_pos(rv, k)), reverse, stream)
        pos = functools.partial(_lane_pos, reverse)
        tok = pl.BlockSpec((TM, RWKV_W), functools.partial(lambda tl, k: (tl(k), 0), tile))
        tok_d = pl.BlockSpec((None, TM, RWKV_W), functools.partial(lambda tl, dd, k: (dd, tl(k), 0), tile, d))
        s0_spec = pl.BlockSpec((None, None, None, RWKV_W, HEAD_DIM),
                               functools.partial(lambda tl, dd, k: (lat_seq_of(tl(k)), l, dd, 0, 0), tile, d))
        in_specs += [tok, tok, tok, tok_d, tok_d, tok_d, s0_spec]
        args += [r, v, kn, lw, kd, b, s0]
        out_specs += [
            pl.BlockSpec((TM, RWKV_W), functools.partial(lambda ps, k: (ps(k), 0), pos)),
            pl.BlockSpec((None, RWKV_HEADS, HEAD_DIM, HEAD_DIM),
                         functools.partial(lambda ps, k: (jnp.minimum(ps(k), STREAM_CTX), 0, 0, 0), pos)),
        ]
    in_specs += [
        pl.BlockSpec((2, TM, TM), const3),
        pl.BlockSpec((2, CHUNK, RWKV_W), const3),
        pl.BlockSpec((2, CHUNK, RWKV_W), const3),
        pl.BlockSpec((2, INV_LEVELS + 1, CHUNK, RWKV_W), lambda k: (0, 0, 0, 0)),
        pl.BlockSpec((CHUNK, RWKV_W), const2),
        pl.BlockSpec((PAIR_W, PAIR_W), const2),
    ]
    args += [cs, strict, incl, lvl, eye, pmask]
    y = jax.ShapeDtypeStruct((STREAM_TILES * TM, RWKV_W), F32)
    sfin = jax.ShapeDtypeStruct((STREAM_CTX + 1, RWKV_HEADS, HEAD_DIM, HEAD_DIM), F32)
    state = pltpu.VMEM((N_PAIRS, PAIR_W, PAIR_W), F32)
    outs = pl.pallas_call(
        _scan_kernel,
        grid=(STREAM_TILES,),
        in_specs=in_specs,
        out_specs=out_specs,
        out_shape=[y, sfin] * len(SCAN_LANES),
        scratch_shapes=[state] * len(SCAN_LANES),
        compiler_params=_cparams(("arbitrary",)),
        name="rwkv_scan",
    )(*args)
    return outs[0::2], outs[1::2]


def _scan_final_states(sfins_per_layer):
    return jnp.stack(
        [jnp.stack([jnp.concatenate([sfins[d * N_STREAMS + st][:STREAM_CTX] for st in range(N_STREAMS)], axis=0)
                    for d in range(2)], axis=1)
         for sfins in sfins_per_layer], axis=1)


def _scan_consts():
    t = np.arange(TM)
    same_chunk = (t[:, None] // CHUNK) == (t[None, :] // CHUNK)
    cs_f = same_chunk & (t[None, :] <= t[:, None])
    cs_b = same_chunk & (t[None, :] >= t[:, None])
    cs = np.stack([cs_f, cs_b]).astype(np.float32)
    q = np.arange(CHUNK)[:, None]
    kcol = np.arange(RWKV_W)[None, :] % CHUNK
    strict = np.stack([kcol < q, kcol > q]).astype(np.float32)
    incl = np.stack([kcol <= q, kcol >= q]).astype(np.float32)
    eye = (kcol == q).astype(np.float32)
    lvl = []
    for lower in (True, False):
        row_blk, col_blk = (q, kcol) if lower else (kcol, q)
        masks = [(q // INV_BASE == kcol // INV_BASE) & ((kcol < q) if lower else (kcol > q))]
        for s in range(INV_LEVELS):
            size = INV_BASE << s
            masks.append((q // (2 * size) == kcol // (2 * size)) & ((row_blk // size) % 2 == 1)
                         & ((col_blk // size) % 2 == 0))
        lvl.append(np.stack(masks))
    lvl = np.stack(lvl).astype(np.float32)
    return (jnp.asarray(cs, BF16), jnp.asarray(strict), jnp.asarray(incl), jnp.asarray(lvl),
            jnp.asarray(eye), jnp.asarray(_head_block_mask(PAIR_W)))


FNET_TT = 512
FNET_IN_ROWS = N_CTX


def _fnet_kernel(seq_len, n_in, *refs):
    z_refs = refs[:n_in]
    ch_ref, cl_ref, th_ref, o_ref, z_scr = refs[n_in:]
    b = pl.program_id(1)

    @pl.when((pl.program_id(0) == 0) & (b == 0))
    def _():
        per_in = FNET_IN_ROWS // seq_len
        for n, z_ref in enumerate(z_refs):
            zh, zl = _split2(z_ref[...])
            for half in range(2):
                ch = ch_ref[half]
                cl = cl_ref[half]
                zc = (_dg(zh, ch, _NN) + (_dg(zh, cl, _NN) + _dg(zl, ch, _NN))).astype(BF16)
                for s in range(per_in):
                    z_scr[n * per_in + s, half * seq_len:(half + 1) * seq_len, :] = zc[s * seq_len:(s + 1) * seq_len]

    o_ref[...] = _dg(th_ref[...], z_scr[b], _NN)


def _fnet_call(u_f, seq_len, n_seq, row_off, tables, name):
    ch, cl, th = tables
    tt = min(FNET_TT, seq_len)
    nt = seq_len // tt
    n_in = n_seq * seq_len // FNET_IN_ROWS
    in_off = row_off // FNET_IN_ROWS
    z_specs = [pl.BlockSpec((FNET_IN_ROWS, FNET_W), functools.partial(lambda n, j, b: (in_off + n, 0), n))
               for n in range(n_in)]
    return pl.pallas_call(
        functools.partial(_fnet_kernel, seq_len, n_in),
        grid=(nt, n_seq),
        in_specs=z_specs + [
            pl.BlockSpec((2, FNET_W, FNET_W), lambda j, b: (0, 0, 0)),
            pl.BlockSpec((2, FNET_W, FNET_W), lambda j, b: (0, 0, 0)),
            pl.BlockSpec((tt, 2 * seq_len), lambda j, b: (j, 0)),
        ],
        out_specs=pl.BlockSpec((tt, FNET_W), lambda j, b: (b * nt + j, 0)),
        out_shape=jax.ShapeDtypeStruct((n_seq * seq_len, FNET_W), F32),
        scratch_shapes=[pltpu.VMEM((n_seq, 2 * seq_len, FNET_W), BF16)],
        compiler_params=_cparams(("arbitrary", "arbitrary")),
        name=name,
    )(*([u_f] * n_in), ch, cl, th)


DFT_SPLIT = 32


def _dft_tables(seq_len):
    c = np.arange(FNET_W)
    ang_c = 2.0 * np.pi * ((c[:, None] % HEAD_DIM) * (c[None, :] % HEAD_DIM) % HEAD_DIM) / HEAD_DIM
    scale = 1.0 / np.sqrt(float(seq_len) * HEAD_DIM)
    blk = _head_block_mask(FNET_W)
    ch, cl = _split2(jnp.asarray(np.stack([np.cos(ang_c) * blk * scale, np.sin(ang_c) * blk * scale]), F32))

    n1 = seq_len // DFT_SPLIT
    f = np.arange(seq_len)
    a1 = 2.0 * np.pi * ((np.arange(n1)[:, None] * f[None, :]) % n1) / n1
    a2 = 2.0 * np.pi * ((np.arange(DFT_SPLIT)[:, None] * f[None, :]) % seq_len) / seq_len
    c1, s1 = jnp.asarray(np.cos(a1), F32)[:, None, :], jnp.asarray(np.sin(a1), F32)[:, None, :]
    c2, s2 = jnp.asarray(np.cos(a2), F32)[None, :, :], jnp.asarray(np.sin(a2), F32)[None, :, :]
    cos_t = (c1 * c2 - s1 * s2).reshape(seq_len, seq_len)
    sin_t = (s1 * c2 + c1 * s2).reshape(seq_len, seq_len)
    th = jnp.concatenate([cos_t, -sin_t], axis=1).astype(BF16)
    return ch, cl, th


def _merge_ffn_kernel(split, final, *refs):
    i = pl.program_id(0)
    n_x = 2 if split else 1
    x_in = _tile_of_split(i, *refs[:2], FFN_TM) if split else refs[0][...]
    (mod_ref, oac_ref, oal_ref, yf0_ref, yf1_ref, yb0_ref, yb1_ref, bonus_ref, g_ref, ofc_ref, ofl_ref,
     lng_ref, lnb_ref, avg_ref, nfg_ref, wo_ref, wi_ref, w2_ref) = refs[n_x:n_x + 18]
    tail = refs[n_x + 18:]
    is_ctx = i < N_CTX // FFN_TM
    o_attn = _tile_of_split(i, oac_ref, oal_ref, FFN_TM)
    o_fnet = _tile_of_split(i, ofc_ref, ofl_ref, FFN_TM)
    avg = avg_ref[...]
    y = jnp.where(_in_stream0(i, FFN_TM), yf0_ref[...] + yb0_ref[...], yf1_ref[...] + yb1_ref[...])
    yc = y - _dot_exact_rhs(y, avg)
    var = _dot_exact_rhs(yc * yc, avg)
    yn = yc * lax.rsqrt(var + GN_EPS) * lng_ref[...] + lnb_ref[...]
    o_rwkv = (yn + bonus_ref[...]) * g_ref[...]

    o = (_dg(o_attn.astype(BF16), wo_ref[0:ATTN_W, :], _NN)
         + _dg(o_rwkv.astype(BF16), wo_ref[ATTN_W:ATTN_W + RWKV_W, :], _NN)
         + _dg(o_fnet.astype(BF16), wo_ref[ATTN_W + RWKV_W:, :], _NN))
    g1 = mod_ref[:, 2 * D_MODEL:3 * D_MODEL]
    sh2 = mod_ref[:, 3 * D_MODEL:4 * D_MODEL]
    sc2 = mod_ref[:, 4 * D_MODEL:5 * D_MODEL]
    g2 = mod_ref[:, 5 * D_MODEL:6 * D_MODEL]
    x = x_in + g1 * o
    h = _rms(x, nfg_ref[...]) * (1.0 + sc2) + sh2
    gu = _dg(h.astype(BF16), wi_ref[...], _NN)
    gt = gu[:, 0:D_FF]
    act = gt * jax.nn.sigmoid(gt) * gu[:, D_FF:]
    x = x + g2 * _dg(act.astype(BF16), w2_ref[...], _NN)
    if not final:
        (o_ref,) = tail
        o_ref[...] = x
    else:
        fng_ref, oc_ref, ol_ref = tail
        y_out = _rms(x, fng_ref[...])

        @pl.when(is_ctx)
        def _():
            oc_ref[...] = y_out

        @pl.when(jnp.logical_not(is_ctx))
        def _():
            ol_ref[...] = y_out


def _in_stream0(m, tm):
    ctx_m = N_CTX // tm
    lat_ps = (N_TOK // tm - ctx_m) // N_STREAMS
    return (m < ctx_m // N_STREAMS) | ((m >= ctx_m) & (m < ctx_m + lat_ps))


def _stream_block(stream, tm):
    ctx_m = N_CTX // tm
    half = ctx_m // N_STREAMS
    lat_ps = (N_TOK // tm - ctx_m) // N_STREAMS
    if stream == 0:
        return lambda m: (jnp.where(m < half, m, jnp.where(m < ctx_m, half - 1,
                                    jnp.where(m < ctx_m + lat_ps, m - half, half + lat_ps - 1))), 0)
    return lambda m: (jnp.where(m < half, 0, jnp.where(m < ctx_m, m - half,
                                jnp.where(m < ctx_m + lat_ps, half - 1, m - half - lat_ps))), 0)


def _merge_ffn_call(xs, mods_l, oa_ctx, oa_lat, ys, bonus, g, of_ctx, of_lat, ln_g, ln_b, avg, nfg,
                    wo, wi, w2, l, final_g=None):
    final = final_g is not None
    split = isinstance(xs, tuple)
    xs = xs if split else (xs,)
    row = lambda i: (l, 0, 0)
    tile = lambda i: (i, 0)
    ctx_tile, lat_tile = _ctx_tile(FFN_TM), _lat_tile(FFN_TM)
    once = pl.Buffered(1)
    in_specs = _x_specs(split, FFN_TM) + [
        pl.BlockSpec((None, 1, 6 * D_MODEL), lambda i: (_mod_row(i, FFN_TM), 0, 0)),
        pl.BlockSpec((FFN_TM, ATTN_W), ctx_tile),
        pl.BlockSpec((FFN_TM, ATTN_W), lat_tile),
        pl.BlockSpec((FFN_TM, RWKV_W), _stream_block(0, FFN_TM)),
        pl.BlockSpec((FFN_TM, RWKV_W), _stream_block(1, FFN_TM)),
        pl.BlockSpec((FFN_TM, RWKV_W), _stream_block(0, FFN_TM)),
        pl.BlockSpec((FFN_TM, RWKV_W), _stream_block(1, FFN_TM)),
        pl.BlockSpec((FFN_TM, RWKV_W), tile),
        pl.BlockSpec((FFN_TM, RWKV_W), tile),
        pl.BlockSpec((FFN_TM, FNET_W), ctx_tile),
        pl.BlockSpec((FFN_TM, FNET_W), lat_tile),
        pl.BlockSpec((None, 1, RWKV_W), row),
        pl.BlockSpec((None, 1, RWKV_W), row),
        pl.BlockSpec((RWKV_W, RWKV_W), lambda i: (0, 0)),
        pl.BlockSpec((None, 1, D_MODEL), row),
        pl.BlockSpec((None, D_MODEL, D_MODEL), row, pipeline_mode=once),
        pl.BlockSpec((None, D_MODEL, 2 * D_FF), row, pipeline_mode=once),
        pl.BlockSpec((None, D_FF, D_MODEL), row, pipeline_mode=once),
    ]
    args = [*xs, mods_l, oa_ctx, oa_lat, *ys, bonus, g, of_ctx, of_lat, ln_g, ln_b, avg, nfg, wo, wi, w2]
    if final:
        in_specs.append(pl.BlockSpec((1, D_MODEL), lambda i: (0, 0)))
        args.append(final_g)
        out_specs = [pl.BlockSpec((FFN_TM, D_MODEL), ctx_tile), pl.BlockSpec((FFN_TM, D_MODEL), lat_tile)]
        out_shape = [jax.ShapeDtypeStruct((N_CTX, D_MODEL), F32), jax.ShapeDtypeStruct((N_LAT, D_MODEL), F32)]
    else:
        out_specs = pl.BlockSpec((FFN_TM, D_MODEL), tile)
        out_shape = jax.ShapeDtypeStruct((N_TOK, D_MODEL), F32)
    return pl.pallas_call(
        functools.partial(_merge_ffn_kernel, split, final),
        grid=(N_TOK // FFN_TM,),
        in_specs=in_specs,
        out_specs=out_specs,
        out_shape=out_shape,
        compiler_params=_cparams(("arbitrary",)),
        name="merge_ffn_final" if final else "merge_ffn",
    )(*args)


def _block_diag2(w):
    z = jnp.zeros_like(w[:, 0])
    return jnp.concatenate([jnp.concatenate([w[:, 0], z], axis=2), jnp.concatenate([z, w[:, 1]], axis=2)], axis=1)


def kernel(x_prompt, x_sample, cache_k, cache_v, state_rwkv, c, c_ctx, w_ada, b_ada, norm_mix_g, norm_ffn_g,
           w_in, w_out, attn_sink, rwkv_shift_w, rwkv_w0, rwkv_w_up, rwkv_a0, rwkv_a_up, rwkv_g_up,
           rwkv_k_k, rwkv_k_a, rwkv_r_k, rwkv_ln_g, rwkv_ln_b, ffn_w_in, ffn_w_out, norm_final_g):
    x = (x_prompt.reshape(N_CTX, D_MODEL), x_sample.reshape(N_LAT, D_MODEL))
    cvec = jnp.concatenate([c_ctx[None, :], c, jnp.zeros((8 - 1 - N_LAT_SEQ, D_MODEL), F32)], axis=0)
    mods = _mods_call(cvec, w_ada, b_ada).reshape(DEPTH, 8, 1, 6 * D_MODEL)

    w_in_b = w_in.astype(BF16)
    w_out_b = w_out.astype(BF16)
    ffn_in_b = ffn_w_in.astype(BF16)
    ffn_out_b = ffn_w_out.astype(BF16)
    wup_bd = _block_diag2(rwkv_w_up)
    aup_bd = _block_diag2(rwkv_a_up)
    row3 = lambda a: a.reshape(DEPTH, 1, -1)
    nmg, nfg = row3(norm_mix_g), row3(norm_ffn_g)
    k_k, k_a, r_k = row3(rwkv_k_k), row3(rwkv_k_a), row3(rwkv_r_k)
    ln_g, ln_b = row3(rwkv_ln_g), row3(rwkv_ln_b)
    kc = cache_k.reshape(N_LAT_SEQ, DEPTH, PAST_LEN, KV_W)
    vc = cache_v.reshape(N_LAT_SEQ, DEPTH, PAST_LEN, KV_W)
    s0 = state_rwkv.reshape(N_LAT_SEQ, DEPTH, 2, RWKV_W, HEAD_DIM)

    cos, sin = _rope_tables()
    consts = _scan_consts()
    bmask = jnp.asarray(_head_block_mask(RWKV_W))
    ones_bd = bmask.astype(BF16)
    avg_bd = (bmask * (1.0 / HEAD_DIM)).astype(BF16)
    rwkv_params = (rwkv_shift_w, wup_bd, aup_bd, rwkv_g_up, rwkv_w0, rwkv_a0, k_k, k_a, r_k, ones_bd)
    dft_ctx = _dft_tables(CTX_LEN)
    dft_lat = _dft_tables(LAT_LEN)

    new_k, new_v, new_s = [], [], []
    for l in range(DEPTH):
        q, k, v, u_f, r, vv, kn, g, bonus, lw, kd, b = _inproj_call(x, mods[l], nmg, w_in_b, rwkv_params, l)
        new_k.append(k[:N_CTX].reshape(N_CTX_SEQ, CTX_LEN, KV_W))
        new_v.append(v[:N_CTX].reshape(N_CTX_SEQ, CTX_LEN, KV_W))

        oa_ctx = _ctx_attn_call(q, k, v, attn_sink, l)
        oa_lat = _lat_attn_call(q, k, v, kc, vc, cos, sin, attn_sink, l)

        ys, sfins = _scan_call(r, vv, kn, lw, kd, b, s0, consts, l)
        new_s.append(sfins)

        of_ctx = _fnet_call(u_f, CTX_LEN, N_CTX_SEQ, 0, dft_ctx, "fnet_context")
        of_lat = _fnet_call(u_f, LAT_LEN, N_LAT_SEQ, N_CTX, dft_lat, "fnet_latent")

        last = l == DEPTH - 1
        x = _merge_ffn_call(x, mods[l], oa_ctx, oa_lat, ys, bonus, g, of_ctx, of_lat, ln_g, ln_b, avg_bd, nfg,
                            w_out_b, ffn_in_b, ffn_out_b, l,
                            final_g=norm_final_g.reshape(1, D_MODEL) if last else None)

    y_ctx, y_lat = x
    return (y_ctx.reshape(N_CTX_SEQ, CTX_LEN, D_MODEL), y_lat.reshape(N_LAT_SEQ, LAT_LEN, D_MODEL),
            jnp.stack(new_k, axis=1).reshape(N_CTX_SEQ, DEPTH, CTX_LEN, N_KV_HEADS, HEAD_DIM),
            jnp.stack(new_v, axis=1).reshape(N_CTX_SEQ, DEPTH, CTX_LEN, N_KV_HEADS, HEAD_DIM),
            _scan_final_states(new_s))
```
